```python
import math
import jax
import jax.numpy as jnp
from jax import lax
import numpy as np

D_MODEL = 1024
BATCH = 8
SEQ = 4096
DEPTH = 4

GRID_W = 64
CTX_LEN = 256
HEAD_DIM = 64
ROPE_BASE = 10000.0
LN_EPS = 1e-5

A_HEADS = D_MODEL // 256
A_KV_HEADS = A_HEADS // 2
A_GROUP = A_HEADS // A_KV_HEADS
A_WIDTH = A_HEADS * HEAD_DIM
A_KV_WIDTH = A_KV_HEADS * HEAD_DIM
WINDOW = 128

B_HEADS = 3 * D_MODEL // 512
B_QK_DIM = 32
B_V_DIM = 2 * B_QK_DIM
B_QK_WIDTH = B_HEADS * 2 * B_QK_DIM
B_WIDTH = B_HEADS * B_V_DIM
Q_BLOCK = 128

C_HEADS = 3 * D_MODEL // 512
C_WIDTH = C_HEADS * HEAD_DIM
C_GATE_WIDTH = 2 * 2 * C_HEADS
CHUNK = 64

D_MIX = A_WIDTH + B_WIDTH + C_WIDTH
D_IN = 2 * A_WIDTH + 2 * A_KV_WIDTH + 2 * B_QK_WIDTH + 2 * B_WIDTH + 5 * C_WIDTH + C_GATE_WIDTH

ALPHA = (2 * DEPTH) ** 0.25
BETA = (8 * DEPTH) ** -0.25

kernel_name = "hymba_style_diffusion_hybrid_block"


def layer_norm(x, g, b):
    xf = x.astype(jnp.float32)
    mu = xf.mean(-1, keepdims=True)
    var = jnp.mean(jnp.square(xf - mu), -1, keepdims=True)
    y = (xf - mu) * lax.rsqrt(var + LN_EPS)
    return (y * g.astype(jnp.float32) + b.astype(jnp.float32)).astype(x.dtype)


def rms_norm(x, g):
    xf = x.astype(jnp.float32)
    y = xf * lax.rsqrt(jnp.mean(jnp.square(xf), -1, keepdims=True) + LN_EPS)
    return (y * g.astype(jnp.float32)).astype(x.dtype)


def axial_rope_tables(n_tokens, dim):
    rows = n_tokens // GRID_W
    row = jnp.broadcast_to(jnp.arange(rows, dtype=jnp.float32)[:, None], (rows, GRID_W)).reshape(-1)
    col = jnp.broadcast_to(jnp.arange(GRID_W, dtype=jnp.float32)[None, :], (rows, GRID_W)).reshape(-1)
    n_freq = dim // 4
    inv = ROPE_BASE ** (-jnp.arange(n_freq, dtype=jnp.float32) / n_freq)
    ar = row[:, None] * inv
    ac = col[:, None] * inv
    ang = jnp.concatenate([ar, ar, ac, ac], -1)
    return jnp.cos(ang), jnp.sin(ang)


def apply_rope(x, tables):
    cos, sin = tables
    shp = (cos.shape[0],) + (1,) * (x.ndim - 3) + (cos.shape[1],)
    cos = cos.reshape(shp)
    sin = sin.reshape(shp)
    x1, x2, x3, x4 = jnp.split(x, 4, axis=-1)
    rot = jnp.concatenate([-x2, x1, -x4, x3], -1)
    return (x * cos + rot * sin).astype(x.dtype)


def sink_softmax(s, sink):
    m = jnp.maximum(s.max(-1, keepdims=True), sink)
    p = jnp.exp(s - m)
    return p / (p.sum(-1, keepdims=True) + jnp.exp(sink - m))


def window_attention(q, k, v, qc, kc, vc, sink, with_ctx):
    B, T = q.shape[:2]
    nb = T // WINDOW
    scale = HEAD_DIM ** -0.5
    sink = sink.astype(jnp.float32).reshape(A_KV_HEADS, A_GROUP)[:, :, None, None]
    qb = q.reshape(B, nb, WINDOW, A_KV_HEADS, A_GROUP, HEAD_DIM)
    pad = ((0, 0), (WINDOW, WINDOW), (0, 0), (0, 0))
    kp = jnp.pad(k, pad).reshape(B, nb + 2, WINDOW, A_KV_HEADS, HEAD_DIM)
    vp = jnp.pad(v, pad).reshape(B, nb + 2, WINDOW, A_KV_HEADS, HEAD_DIM)
    kb = jnp.concatenate([kp[:, :-2], kp[:, 1:-1], kp[:, 2:]], axis=2)
    vb = jnp.concatenate([vp[:, :-2], vp[:, 1:-1], vp[:, 2:]], axis=2)
    s_loc = jnp.einsum('bnqgrd,bnkgd->bngrqk', qb, kb, preferred_element_type=jnp.float32) * scale
    r = jnp.arange(WINDOW)[:, None]
    j = jnp.arange(3 * WINDOW)[None, :]
    rel = j - WINDOW - r
    kpos = jnp.arange(nb)[:, None, None] * WINDOW - WINDOW + j[None]
    mask = (jnp.abs(rel) <= WINDOW)[None] & (kpos >= 0) & (kpos < T)
    s_loc = jnp.where(mask[None, :, None, None], s_loc, -jnp.inf)
    s_ctx = jnp.einsum('bnqgrd,bkgd->bngrqk', qb, kc, preferred_element_type=jnp.float32) * scale
    p = sink_softmax(jnp.concatenate([s_loc, s_ctx], -1), sink)
    n_loc = 3 * WINDOW
    o = (jnp.einsum('bngrqk,bnkgd->bnqgrd', p[..., :n_loc].astype(v.dtype), vb)
         + jnp.einsum('bngrqk,bkgd->bnqgrd', p[..., n_loc:].astype(v.dtype), vc))
    y = o.reshape(B, T, A_WIDTH)
    yc = None
    if with_ctx:
        Cn = qc.shape[1]
        qcb = qc.reshape(B, Cn, A_KV_HEADS, A_GROUP, HEAD_DIM)
        sc = jnp.einsum('bqgrd,bkgd->bgrqk', qcb, kc, preferred_element_type=jnp.float32) * scale
        pc = sink_softmax(sc, sink)
        yc = jnp.einsum('bgrqk,bkgd->bqgrd', pc.astype(vc.dtype), vc).reshape(B, Cn, A_WIDTH)
    return y, yc


def diff_attention(q, k, v, qc, kc, vc, lam, lam_init, norm_g, with_ctx):
    B, T = q.shape[:2]
    scale = B_QK_DIM ** -0.5

    def attend(qblk, keys, vals):
        s = jnp.einsum('bqhmd,bkhmd->bhmqk', qblk, keys, preferred_element_type=jnp.float32) * scale
        p = jax.nn.softmax(s, axis=-1)
        pd = p[:, :, 0] - lam * p[:, :, 1]
        return jnp.einsum('bhqk,bkhv->bqhv', pd.astype(vals.dtype), vals)

    k_all = jnp.concatenate([k, kc], axis=1)
    v_all = jnp.concatenate([v, vc], axis=1)
    nb = T // Q_BLOCK
    q_blocks = jnp.moveaxis(q.reshape(B, nb, Q_BLOCK, B_HEADS, 2, B_QK_DIM), 1, 0)
    o = lax.map(lambda qb: attend(qb, k_all, v_all), q_blocks)
    o = jnp.moveaxis(o, 0, 1).reshape(B, T, B_HEADS, B_V_DIM)
    y = (rms_norm(o, norm_g) * (1.0 - lam_init)).reshape(B, T, B_WIDTH)
    yc = None
    if with_ctx:
        oc = attend(qc, kc, vc)
        yc = (rms_norm(oc, norm_g) * (1.0 - lam_init)).reshape(B, qc.shape[1], B_WIDTH)
    return y, yc


def mlstm_scan(q, k, v, i_pre, logf, state):
    B, H, T, d = q.shape
    nc = T // CHUNK

    def to_chunks(a):
        return jnp.moveaxis(a.reshape(a.shape[:2] + (nc, CHUNK) + a.shape[3:]), 2, 0)

    xs = (to_chunks(q), to_chunks(k), to_chunks(v), to_chunks(i_pre), to_chunks(logf))
    lower = jnp.tril(jnp.ones((CHUNK, CHUNK), dtype=bool))

    def step(carry, inp):
        Cm, n, m = carry
        qc, kc, vc, ic, fc = inp
        b = jnp.cumsum(fc, axis=-1)
        log_d = jnp.where(lower, b[..., :, None] - b[..., None, :] + ic[..., None, :], -jnp.inf)
        m_inter = b + m[..., None]
        m_t = jnp.maximum(m_inter, log_d.max(-1))
        dmat = jnp.exp(log_d - m_t[..., None])
        s = jnp.einsum('bhtd,bhsd->bhts', qc, kc) * dmat
        w_inter = jnp.exp(m_inter - m_t)
        num = w_inter[..., None] * jnp.einsum('bhvd,bhtd->bhtv', Cm, qc) + jnp.einsum('bhts,bhsv->bhtv', s, vc)
        den = w_inter * jnp.einsum('bhd,bhtd->bht', n, qc) + s.sum(-1)
        h = num / jnp.maximum(jnp.abs(den), jnp.exp(-m_t))[..., None]
        b_last = b[..., -1]
        log_w = b_last[..., None] - b + ic
        m_new = jnp.maximum(b_last + m, log_w.max(-1))
        w = jnp.exp(log_w - m_new[..., None])
        decay = jnp.exp(b_last + m - m_new)
        C_new = decay[..., None, None] * Cm + jnp.einsum('bhs,bhsv,bhsd->bhvd', w, vc, kc)
        n_new = decay[..., None] * n + jnp.einsum('bhs,bhsd->bhd', w, kc)
        return (C_new, n_new, m_new), h

    state, hs = lax.scan(step, state, xs)
    return jnp.moveaxis(hs, 0, 2).reshape(B, H, T, d), state


def mlstm_branch(q, k, v, o, gates, qc, kc, vc, oc, gates_c, i_bias, f_bias, norm_g):
    def prep(q, k, v, gates):
        bht = lambda a: jnp.transpose(a, (0, 2, 1, 3)).astype(jnp.float32)
        g = jnp.transpose(gates.astype(jnp.float32), (2, 3, 0, 4, 1))
        return bht(q), bht(k) * HEAD_DIM ** -0.5, bht(v), g

    ql, kl, vl, gl = prep(q, k, v, gates)
    qx, kx, vx, gx = prep(qc, kc, vc, gates_c)
    B, H, _, d = ql.shape
    zero = (jnp.zeros((B, H, d, d), jnp.float32), jnp.zeros((B, H, d), jnp.float32), jnp.zeros((B, H), jnp.float32))
    flip = lambda a: jnp.flip(a, axis=2)
    h_lat_dirs = []
    h_ctx_dirs = []
    for direction in range(2):
        ib = i_bias[direction].astype(jnp.float32)[:, None]
        fb = f_bias[direction].astype(jnp.float32)[:, None]
        lat = (ql, kl, vl, gl[direction, 0] + ib, jax.nn.log_sigmoid(gl[direction, 1] + fb))
        cx = (qx, kx, vx, gx[direction, 0] + ib, jax.nn.log_sigmoid(gx[direction, 1] + fb))
        if direction == 1:
            lat = tuple(flip(a) for a in lat)
            cx = tuple(flip(a) for a in cx)
        hx, st = mlstm_scan(*cx, zero)
        hl, _ = mlstm_scan(*lat, st)
        if direction == 1:
            hx, hl = flip(hx), flip(hl)
        h_lat_dirs.append(hl)
        h_ctx_dirs.append(hx)
    g = norm_g.reshape(C_HEADS, HEAD_DIM)

    def finish(h, o_gate):
        h = rms_norm(jnp.transpose(h, (0, 2, 1, 3)), g)
        return (h.reshape(h.shape[0], h.shape[1], C_WIDTH) * jax.nn.sigmoid(o_gate.astype(jnp.float32))).astype(q.dtype)

    return (finish(h_lat_dirs[0] + h_lat_dirs[1], o), finish(h_ctx_dirs[0] + h_ctx_dirs[1], oc))


def split_proj(p):
    sizes = (A_WIDTH, A_KV_WIDTH, A_KV_WIDTH, A_WIDTH,
             B_QK_WIDTH, B_QK_WIDTH, B_WIDTH, B_WIDTH,
             C_WIDTH, C_WIDTH, C_WIDTH, C_WIDTH, C_WIDTH, C_GATE_WIDTH)
    return jnp.split(p, np.cumsum(sizes)[:-1].tolist(), axis=-1)


def heads(t, *shape):
    return t.reshape(t.shape[:2] + shape)


def hybrid_layer(x, ctx, c, c_ctx, w_mod, b_mod, w_in, sink, lam_vec, lam_init, diff_g,
                 i_bias, f_bias, mlstm_g, w_out, ln_g, ln_b, rope_a, rope_b, with_ctx):
    mod = jax.nn.silu(c) @ w_mod + b_mod
    mod_c = jax.nn.silu(c_ctx) @ w_mod + b_mod
    shift, scale, gate = jnp.split(mod, 3, axis=-1)
    shift_c, scale_c, gate_c = jnp.split(mod_c, 3, axis=-1)
    h = x * (1.0 + scale[:, None]) + shift[:, None]
    hc = ctx * (1.0 + scale_c) + shift_c
    qa, ka, va, za, qb, kb, vb, zb, qm, km, vm, om, zm, gm = split_proj(h @ w_in)
    qa_c, ka_c, va_c, za_c, qb_c, kb_c, vb_c, zb_c, qm_c, km_c, vm_c, om_c, zm_c, gm_c = split_proj(hc @ w_in)

    ya, ya_c = window_attention(
        apply_rope(heads(qa, A_HEADS, HEAD_DIM), rope_a), apply_rope(heads(ka, A_KV_HEADS, HEAD_DIM), rope_a),
        heads(va, A_KV_HEADS, HEAD_DIM),
        heads(qa_c, A_HEADS, HEAD_DIM), heads(ka_c, A_KV_HEADS, HEAD_DIM), heads(va_c, A_KV_HEADS, HEAD_DIM),
        sink, with_ctx)

    lv = lam_vec.astype(jnp.float32)
    lam = jnp.exp(jnp.sum(lv[0] * lv[1])) - jnp.exp(jnp.sum(lv[2] * lv[3])) + lam_init
    yb, yb_c = diff_attention(
        apply_rope(heads(qb, B_HEADS, 2, B_QK_DIM), rope_b), apply_rope(heads(kb, B_HEADS, 2, B_QK_DIM), rope_b),
        heads(vb, B_HEADS, B_V_DIM),
        heads(qb_c, B_HEADS, 2, B_QK_DIM), heads(kb_c, B_HEADS, 2, B_QK_DIM), heads(vb_c, B_HEADS, B_V_DIM),
        lam, lam_init, diff_g, with_ctx)

    ym, ym_c = mlstm_branch(
        heads(qm, C_HEADS, HEAD_DIM), heads(km, C_HEADS, HEAD_DIM), heads(vm, C_HEADS, HEAD_DIM), om,
        heads(gm, 2, 2, C_HEADS),
        heads(qm_c, C_HEADS, HEAD_DIM), heads(km_c, C_HEADS, HEAD_DIM), heads(vm_c, C_HEADS, HEAD_DIM), om_c,
        heads(gm_c, 2, 2, C_HEADS),
        i_bias, f_bias, mlstm_g)

    y = jnp.concatenate([ya * jax.nn.silu(za), yb * jax.nn.silu(zb), ym * jax.nn.silu(zm)], -1) @ w_out
    x_new = layer_norm(ALPHA * x + gate[:, None] * y, ln_g, ln_b)
    ctx_new = None
    if with_ctx:
        yc = jnp.concatenate([ya_c * jax.nn.silu(za_c), yb_c * jax.nn.silu(zb_c), ym_c * jax.nn.silu(zm_c)], -1) @ w_out
        ctx_new = layer_norm(ALPHA * ctx + gate_c * yc, ln_g, ln_b)
    return x_new, ctx_new


def setup_inputs(seed: int = 0) -> dict:
    key = jax.random.key(seed)
    ks = jax.random.split(key, 16)

    def nrm(k, shape, s):
        return jax.random.normal(k, shape, jnp.float32) * s

    return {
        "x": nrm(ks[0], (BATCH, SEQ, D_MODEL), 1.0),
        "c": nrm(ks[1], (BATCH, D_MODEL), 1.0),
        "ctx": nrm(ks[2], (BATCH, CTX_LEN, D_MODEL), 1.0),
        "c_ctx": nrm(ks[3], (D_MODEL,), 1.0),
        "w_mod": nrm(ks[4], (DEPTH, D_MODEL, 3 * D_MODEL), 0.5 * D_MODEL ** -0.5),
        "b_mod": nrm(ks[5], (DEPTH, 3 * D_MODEL), 0.02),
        "w_in": nrm(ks[6], (DEPTH, D_MODEL, D_IN), D_MODEL ** -0.5),
        "attn_sink": nrm(ks[7], (DEPTH, A_HEADS), 0.5),
        "diff_lambda": nrm(ks[8], (DEPTH, 4, B_QK_DIM), 0.1),
        "diff_norm_g": 1.0 + nrm(ks[9], (DEPTH, B_V_DIM), 0.02),
        "mlstm_i_bias": nrm(ks[10], (DEPTH, 2, C_HEADS), 0.1),
        "mlstm_f_bias": jnp.linspace(3.0, 6.0, C_HEADS, dtype=jnp.float32) + nrm(ks[11], (DEPTH, 2, C_HEADS), 0.1),
        "mlstm_norm_g": 1.0 + nrm(ks[12], (DEPTH, C_WIDTH), 0.02),
        "w_out": nrm(ks[13], (DEPTH, D_MIX, D_MODEL), BETA * D_MIX ** -0.5),
        "ln_g": 1.0 + nrm(ks[14], (DEPTH, D_MODEL), 0.02),
        "ln_b": nrm(ks[15], (DEPTH, D_MODEL), 0.02),
    }


def reference(x, c, ctx, c_ctx, w_mod, b_mod, w_in, attn_sink, diff_lambda, diff_norm_g,
              mlstm_i_bias, mlstm_f_bias, mlstm_norm_g, w_out, ln_g, ln_b):
    T = x.shape[1]
    rope_a = axial_rope_tables(T, HEAD_DIM)
    rope_b = axial_rope_tables(T, B_QK_DIM)
    for l in range(DEPTH):
        lam_init = 0.8 - 0.6 * math.exp(-0.3 * l)
        x, ctx = hybrid_layer(x, ctx, c, c_ctx, w_mod[l], b_mod[l], w_in[l], attn_sink[l], diff_lambda[l],
                              lam_init, diff_norm_g[l], mlstm_i_bias[l], mlstm_f_bias[l], mlstm_norm_g[l],
                              w_out[l], ln_g[l], ln_b[l], rope_a, rope_b, with_ctx=(l < DEPTH - 1))
    return x
```

```python
import functools
import math

import numpy as np
import jax
import jax.numpy as jnp
from jax import lax
from jax.experimental import pallas as pl
from jax.experimental.pallas import tpu as pltpu

F32 = jnp.float32
BF16 = jnp.bfloat16

LANES = 128
GRID_W = 64
HEAD_DIM = 64
ROPE_BASE = 10000.0
LN_EPS = 1e-5
WINDOW = 128
B_QK_DIM = 32
MLSTM_CHUNK = 128
ROW_TILE = 256
VMEM_LIMIT = 56 * 1024 * 1024


def _dot(a, b):
    return jnp.dot(a, b, preferred_element_type=F32)


def _dot_nt(a, b):
    return lax.dot_general(a, b, (((1,), (1,)), ((), ())), preferred_element_type=F32)


def _dot_tn(a, b):
    return lax.dot_general(a, b, (((0,), (0,)), ((), ())), preferred_element_type=F32)


def _split_bf16(a):
    hi = a.astype(BF16)
    lo = (a - hi.astype(F32)).astype(BF16)
    return hi, lo


def _sigmoid(x):
    return 1.0 / (1.0 + jnp.exp(-x))


def _params(sem):
    return pltpu.CompilerParams(dimension_semantics=sem, vmem_limit_bytes=VMEM_LIMIT)


def _mod_kernel(c_ref, w_ref, b_ref, o_ref):
    c = c_ref[...]
    a = c * _sigmoid(c)
    a_hi, a_lo = _split_bf16(a)
    w_hi, w_lo = _split_bf16(w_ref[0])
    o_ref[0] = _dot(a_hi, w_hi) + _dot(a_lo, w_hi) + _dot(a_hi, w_lo) + b_ref[0]


def _modulation(cc, w_mod, b_mod):
    depth, d, n = w_mod.shape
    r = cc.shape[0]
    tn = 1024
    return pl.pallas_call(
        _mod_kernel,
        grid=(depth, n // tn),
        in_specs=[
            pl.BlockSpec((r, d), lambda l, j: (0, 0)),
            pl.BlockSpec((1, d, tn), lambda l, j: (l, 0, j)),
            pl.BlockSpec((1, 1, tn), lambda l, j: (l, 0, j)),
        ],
        out_specs=pl.BlockSpec((1, r, tn), lambda l, j: (l, 0, j)),
        out_shape=jax.ShapeDtypeStruct((depth, r, n), F32),
        compiler_params=_params(("arbitrary", "arbitrary")),
        name="modulation",
    )(cc, w_mod, b_mod.reshape(depth, 1, n))


_QA, _KA, _VA = (0, 512), (512, 640), (640, 768)
_QB, _KB, _VB = (768, 1152), (1152, 1536), (1536, 1920)
_QM, _KM, _VM = (1920, 2304), (2304, 2688), (2688, 3072)
_OM, _Z, _GM = (3072, 3456), (3456, 4480), (4480, 4608)
_W_COLS = 4608


def _rope(t, tab_ref, quarter):
    cos, sin_p, sin_m = tab_ref[0], tab_ref[1], tab_ref[2]
    outs = []
    for j in range(t.shape[1] // LANES):
        tj = t[:, j * LANES:(j + 1) * LANES]
        outs.append(tj * cos + pltpu.roll(tj, quarter, 1) * sin_p + pltpu.roll(tj, LANES - quarter, 1) * sin_m)
    return outs[0] if len(outs) == 1 else jnp.concatenate(outs, axis=1)


def _inproj_kernel(x_ref, mod_ref, w_ref, ra_ref, rb_ref,
                   qa_ref, ka_ref, va_ref, qb_ref, kb_ref, vb_ref,
                   qm_ref, km_ref, vm_ref, om_ref, g_ref, gm_ref):
    x = x_ref[0]
    shift = mod_ref[0, 0, 0:1, :]
    scale = mod_ref[0, 0, 1:2, :]
    h = (x * (1.0 + scale) + shift).astype(BF16)

    def proj(cols):
        return _dot(h, w_ref[:, cols[0]:cols[1]])

    qa_ref[0] = (_rope(proj(_QA), ra_ref, HEAD_DIM // 4) * (HEAD_DIM ** -0.5)).astype(BF16)
    ka_ref[0] = _rope(proj(_KA), ra_ref, HEAD_DIM // 4).astype(BF16)
    va_ref[0] = proj(_VA).astype(BF16)
    qb_ref[0] = (_rope(proj(_QB), rb_ref, B_QK_DIM // 4) * (B_QK_DIM ** -0.5)).astype(BF16)
    kb_ref[0] = _rope(proj(_KB), rb_ref, B_QK_DIM // 4).astype(BF16)
    vb_ref[0] = proj(_VB).astype(BF16)
    qm_ref[0] = proj(_QM).astype(BF16)
    km_ref[0] = (proj(_KM) * (HEAD_DIM ** -0.5)).astype(BF16)
    vm_ref[0] = proj(_VM).astype(BF16)
    om_ref[0] = _sigmoid(proj(_OM)).astype(BF16)
    z = proj(_Z)
    g_ref[0] = (z * _sigmoid(z)).astype(BF16)
    gm_ref[0] = proj(_GM)


def _in_projection(xa, modsel, w, rope_a, rope_b):
    b, s, d = xa.shape
    tm = ROW_TILE
    widths = [512, 128, 128, 384, 384, 384, 384, 384, 384, 384, 1024]
    row = lambda n: pl.BlockSpec((1, tm, n), lambda bi, i: (bi, i, 0))
    out_shape = [jax.ShapeDtypeStruct((b, s, n), BF16) for n in widths]
    out_shape.append(jax.ShapeDtypeStruct((b, s, LANES), F32))
    return pl.pallas_call(
        _inproj_kernel,
        grid=(b, s // tm),
        in_specs=[
            row(d),
            pl.BlockSpec((1, 1, 3, d), lambda bi, i: (bi, jnp.minimum(i, 1), 0, 0)),
            pl.BlockSpec((d, _W_COLS), lambda bi, i: (0, 0)),
            pl.BlockSpec((3, tm, LANES), lambda bi, i: (0, i, 0)),
            pl.BlockSpec((3, tm, LANES), lambda bi, i: (0, i, 0)),
        ],
        out_specs=[row(n) for n in widths] + [row(LANES)],
        out_shape=out_shape,
        compiler_params=_params(("arbitrary", "arbitrary")),
        name="in_projection",
    )(xa, modsel, w, rope_a, rope_b)


def _attn_a_kernel(sink_ref, q_ref, kp_ref, kc_ref, kn_ref, kx_ref, vp_ref, vc_ref, vn_ref, vx_ref,
                   o_ref, *, n_ctx_blocks, n_blocks):
    g = pl.program_id(1)
    i = pl.program_id(2)
    w = WINDOW
    q = q_ref[0]
    qs = jnp.concatenate([q[:, :LANES], q[:, LANES:]], axis=0)
    k_loc = jnp.concatenate([kp_ref[0], kc_ref[0], kn_ref[0]], axis=0)
    v_loc = jnp.concatenate([vp_ref[0], vc_ref[0], vn_ref[0]], axis=0)
    s_loc = _dot_nt(qs, k_loc)
    s_ctx = _dot_nt(qs, kx_ref[0])
    row = lax.broadcasted_iota(jnp.int32, s_loc.shape, 0)
    col = lax.broadcasted_iota(jnp.int32, s_loc.shape, 1)
    rel = col - w - (row & (w - 1))
    lo = jnp.where(i > n_ctx_blocks, 0, w)
    hi = jnp.where(i < n_blocks - 1, 3 * w, 2 * w)
    hi = jnp.where(i >= n_ctx_blocks, hi, 0)
    ok = (jnp.abs(rel) <= w) & (col >= lo) & (col < hi)
    s_loc = jnp.where(ok, s_loc, -jnp.inf)
    rows1 = lax.broadcasted_iota(jnp.int32, (2 * w, 1), 0)
    sink = jnp.where(rows1 < w, sink_ref[2 * g], sink_ref[2 * g + 1])
    m = jnp.maximum(jnp.maximum(jnp.max(s_loc, axis=1, keepdims=True),
                                jnp.max(s_ctx, axis=1, keepdims=True)), sink)
    p_loc = jnp.exp(s_loc - m)
    p_ctx = jnp.exp(s_ctx - m)
    den = (jnp.sum(p_loc, axis=1, keepdims=True) + jnp.sum(p_ctx, axis=1, keepdims=True)
           + jnp.exp(sink - m))
    o = (_dot(p_loc.astype(BF16), v_loc) + _dot(p_ctx.astype(BF16), vx_ref[0])) / den
    o0, o1 = o[:w], o[w:]
    lane = lax.broadcasted_iota(jnp.int32, (w, LANES), 1)
    gv = lane * 0 + g
    half = LANES // 2
    left = jnp.where(gv == 0, o0, pltpu.roll(o0, half, 1))
    right = jnp.where(gv == 0, pltpu.roll(o1, half, 1), o1)
    o_ref[0] = jnp.where(lane < half, left, right)


def _attn_a(sink, qa, ka, va, n_ctx):
    b, s, _ = qa.shape
    w = WINDOW
    nb = s // w
    ncb = n_ctx // w
    kv = lambda f: pl.BlockSpec((1, w, LANES), f)
    prev = lambda bi, g, i: (bi, jnp.maximum(i - 1, 0), 0)
    cur = lambda bi, g, i: (bi, i, 0)
    nxt = lambda bi, g, i: (bi, jnp.minimum(i + 1, nb - 1), 0)
    cx = pl.BlockSpec((1, n_ctx, LANES), lambda bi, g, i: (bi, 0, 0))
    return pl.pallas_call(
        functools.partial(_attn_a_kernel, n_ctx_blocks=ncb, n_blocks=nb),
        grid=(b, 2, nb),
        in_specs=[
            pl.BlockSpec(memory_space=pltpu.SMEM),
            pl.BlockSpec((1, w, 2 * LANES), lambda bi, g, i: (bi, i, g)),
            kv(prev), kv(cur), kv(nxt), cx,
            kv(prev), kv(cur), kv(nxt), cx,
        ],
        out_specs=pl.BlockSpec((1, w, LANES), lambda bi, g, i: (bi, i, g)),
        out_shape=jax.ShapeDtypeStruct((b, s, 2 * LANES), F32),
        compiler_params=_params(("arbitrary", "arbitrary", "arbitrary")),
        name="window_attention",
    )(sink, qa, ka, ka, ka, ka, va, va, va, va)


def _attn_b_kernel(li_ref, lam_ref, ng_ref, q_ref, k_ref, v_ref, o_ref, m_scr, l_scr, acc_scr,
                   *, n_ctx, n_lat, tk, tq):
    i = pl.program_id(2)
    q = q_ref[0]
    lane_q = lax.broadcasted_iota(jnp.int32, q.shape, 1)
    unit = lane_q >> 5
    zero = jnp.zeros_like(q)
    qs = jnp.concatenate([jnp.where(unit == u, q, zero) for u in range(4)], axis=0)

    s = _dot_nt(qs, k_ref[0, 0:n_ctx, :])
    m0 = jnp.max(s, axis=1, keepdims=True)
    p = jnp.exp(s - m0)
    m_scr[...] = m0
    l_scr[...] = jnp.sum(p, axis=1, keepdims=True)
    acc_scr[...] = _dot(p.astype(BF16), v_ref[0, 0:n_ctx, :])

    @pl.when(i >= n_ctx // tq)
    def _():
        def body(t, carry):
            off = pl.multiple_of(n_ctx + t * tk, LANES)
            s = _dot_nt(qs, k_ref[0, pl.ds(off, tk), :])
            m_old = m_scr[...]
            m_new = jnp.maximum(m_old, jnp.max(s, axis=1, keepdims=True))
            alpha = jnp.exp(m_old - m_new)
            p = jnp.exp(s - m_new)
            l_scr[...] = alpha * l_scr[...] + jnp.sum(p, axis=1, keepdims=True)
            acc_scr[...] = alpha * acc_scr[...] + _dot(p.astype(BF16), v_ref[0, pl.ds(off, tk), :])
            m_scr[...] = m_new
            return carry
        lax.fori_loop(0, n_lat // tk, body, 0)

    lam_init = li_ref[0]
    lv = lam_ref[...]
    lam = (jnp.exp(jnp.sum(lv[0:1] * lv[1:2], axis=1, keepdims=True))
           - jnp.exp(jnp.sum(lv[2:3] * lv[3:4], axis=1, keepdims=True)) + lam_init)
    o = acc_scr[...] / l_scr[...]
    o_a = o[0:tq] - lam * o[tq:2 * tq]
    o_b = o[2 * tq:3 * tq] - lam * o[3 * tq:4 * tq]
    lane = lax.broadcasted_iota(jnp.int32, (tq, LANES), 1)
    first = lane < LANES // 2
    od = jnp.where(first, o_a, o_b)
    sq = od * od
    ms_a = jnp.sum(jnp.where(first, sq, 0.0), axis=1, keepdims=True)
    ms_b = jnp.sum(jnp.where(first, 0.0, sq), axis=1, keepdims=True)
    ms = jnp.where(first, ms_a, ms_b) * (2.0 / LANES)
    o_ref[0] = od * lax.rsqrt(ms + LN_EPS) * ng_ref[...] * (1.0 - lam_init)


def _attn_b(lam_init, lam_vec, norm_g, qb, kb, vb, n_ctx):
    b, s, width = qb.shape
    tq = 128
    n_lat = s - n_ctx
    tk = min(1024, n_lat)
    pairs = width // LANES
    kvspec = pl.BlockSpec((1, s, LANES), lambda bi, p, i: (bi, 0, p))
    return pl.pallas_call(
        functools.partial(_attn_b_kernel, n_ctx=n_ctx, n_lat=n_lat, tk=tk, tq=tq),
        grid=(b, pairs, s // tq),
        in_specs=[
            pl.BlockSpec(memory_space=pltpu.SMEM),
            pl.BlockSpec((4, B_QK_DIM), lambda bi, p, i: (0, 0)),
            pl.BlockSpec((1, LANES), lambda bi, p, i: (0, 0)),
            pl.BlockSpec((1, tq, LANES), lambda bi, p, i: (bi, i, p)),
            kvspec, kvspec,
        ],
        out_specs=pl.BlockSpec((1, tq, LANES), lambda bi, p, i: (bi, i, p)),
        out_shape=jax.ShapeDtypeStruct((b, s, width), F32),
        scratch_shapes=[pltpu.VMEM((4 * tq, 1), F32), pltpu.VMEM((4 * tq, 1), F32),
                        pltpu.VMEM((4 * tq, LANES), F32)],
        compiler_params=_params(("arbitrary", "arbitrary", "arbitrary")),
        name="diff_attention",
    )(lam_init, lam_vec, norm_g, qb, kb, vb)


def _log_sigmoid(x):
    return jnp.minimum(x, 0.0) - jnp.log(1.0 + jnp.exp(-jnp.abs(x)))


def _mlstm_kernel(bc_ref, br_ref, ng_ref, q_ref, k_ref, v_ref, gc_ref, gr_ref, og_ref, o_ref,
                  hf_scr, hb_scr, st_scr, *, n_ctx_chunks, n_chunks):
    L = MLSTM_CHUNK
    half = LANES // 2
    r_i = lax.broadcasted_iota(jnp.int32, (L, L), 0)
    c_i = lax.broadcasted_iota(jnp.int32, (L, L), 1)
    lower = r_i >= c_i
    upper = r_i <= c_i
    t_low = jnp.where(lower, 1.0, 0.0).astype(BF16)
    t_up = jnp.where(upper, 1.0, 0.0).astype(BF16)
    lane = lax.broadcasted_iota(jnp.int32, (L, LANES), 1)
    first = lane < half
    ones_ext = jnp.where(lane < 2, 1.0, 0.0).astype(BF16)
    sel_a = jnp.where(first, 1.0, 0.0).astype(BF16)
    sel_b = jnp.where(first, 0.0, 1.0).astype(BF16)
    sr = lax.broadcasted_iota(jnp.int32, (LANES, 2 * LANES), 0)
    sc = lax.broadcasted_iota(jnp.int32, (LANES, 2 * LANES), 1)
    top = sr < half
    keep = ((top & ((sc < half) | (sc == LANES)))
            | ((sr >= half) & (((sc >= half) & (sc < LANES)) | (sc == LANES + 1))))
    st_scr[...] = jnp.zeros_like(st_scr)
    bias_c = bc_ref[0]
    bias_r = br_ref[0]

    def chunk_step(direction, c, m_prev):
        r0 = pl.multiple_of(c * L, L)
        qc = q_ref[0, pl.ds(r0, L), :]
        kc = k_ref[0, pl.ds(r0, L), :]
        vc = v_ref[0, pl.ds(r0, L), :]
        v_ext = jnp.concatenate([vc, ones_ext], axis=1)
        gcol = gc_ref[0, 0, pl.ds(r0, L), :] + bias_c
        grow = gr_ref[0, 0, c] + bias_r
        fcol = _log_sigmoid(gcol)
        frow = _log_sigmoid(grow)
        fc_hi, fc_lo = _split_bf16(fcol)
        fr_hi, fr_lo = _split_bf16(frow)
        if direction == 0:
            bcol_all = _dot(t_low, fc_hi) + _dot(t_low, fc_lo)
            brow_all = _dot(fr_hi, t_up) + _dot(fr_lo, t_up)
            tri = lower
        else:
            bcol_all = _dot(t_up, fc_hi) + _dot(t_up, fc_lo)
            brow_all = _dot(fr_hi, t_low) + _dot(fr_lo, t_low)
            tri = upper
        qs = jnp.concatenate([qc * sel_a, qc * sel_b], axis=0)
        s_all = _dot_nt(qs, kc)
        state = st_scr[direction]
        r_mat = _dot(qc, state.astype(BF16))
        hs, ws, decays, m_news = [], [], [], []
        for a in range(2):
            ci = 2 * direction + a
            cf = 4 + ci
            bcol = bcol_all[:, cf:cf + 1]
            brow = brow_all[cf:cf + 1, :]
            icol = gcol[:, ci:ci + 1]
            irow = grow[ci:ci + 1, :]
            b_last = jnp.sum(frow[cf:cf + 1, :], axis=1, keepdims=True)
            log_d = jnp.where(tri, bcol - brow + irow, -jnp.inf)
            m_inter = bcol + m_prev[a]
            m_t = jnp.maximum(m_inter, jnp.max(log_d, axis=1, keepdims=True))
            dmat = jnp.exp(log_d - m_t)
            w_inter = jnp.exp(m_inter - m_t)
            s_a = s_all[a * L:(a + 1) * L] * dmat
            p_mat = _dot(s_a.astype(BF16), v_ext)
            num = w_inter * r_mat[:, :LANES] + p_mat[:, :LANES]
            den = w_inter * r_mat[:, LANES + a:LANES + a + 1] + p_mat[:, LANES:LANES + 1]
            hs.append(num / jnp.maximum(jnp.abs(den), jnp.exp(-m_t)))
            log_w = b_last - bcol + icol
            m_new = jnp.maximum(b_last + m_prev[a], jnp.max(log_w, axis=0, keepdims=True))
            ws.append(jnp.exp(log_w - m_new))
            decays.append(jnp.exp(b_last + m_prev[a] - m_new))
            m_news.append(m_new)
        h_pair = jnp.where(first, hs[0], hs[1])
        if direction == 0:
            hf_scr[pl.ds(r0, L), :] = h_pair
        else:
            hb_scr[pl.ds(r0, L), :] = h_pair
        kw = (kc.astype(F32) * jnp.where(first, ws[0], ws[1])).astype(BF16)
        upd = _dot_tn(kw, v_ext)
        st_scr[direction] = jnp.where(top, decays[0], decays[1]) * state + jnp.where(keep, upd, 0.0)
        return m_news

    def body(j, carry):
        mf, mb = carry
        cb = jnp.where(j < n_ctx_chunks, n_ctx_chunks - 1 - j, n_chunks - 1 + n_ctx_chunks - j)
        mf = chunk_step(0, j, mf)
        mb = chunk_step(1, cb, mb)
        return (tuple(mf), tuple(mb))

    z11 = jnp.zeros((1, 1), F32)
    lax.fori_loop(0, n_chunks, body, ((z11, z11), (z11, z11)))

    def finish(c, carry):
        r0 = pl.multiple_of(c * L, L)
        h = hf_scr[pl.ds(r0, L), :] + hb_scr[pl.ds(r0, L), :]
        sq = h * h
        ms_a = jnp.sum(jnp.where(first, sq, 0.0), axis=1, keepdims=True)
        ms_b = jnp.sum(jnp.where(first, 0.0, sq), axis=1, keepdims=True)
        ms = jnp.where(first, ms_a, ms_b) * (1.0 / half)
        y = h * lax.rsqrt(ms + LN_EPS) * ng_ref[...]
        o_ref[0, pl.ds(r0, L), :] = y * og_ref[0, pl.ds(r0, L), :].astype(F32)
        return carry
    lax.fori_loop(0, n_chunks, finish, 0)


def _mlstm(bias_c, bias_r, norm_g, qm, km, vm, gc, gr, og, n_ctx):
    b, s, width = qm.shape
    L = MLSTM_CHUNK
    pairs = width // LANES
    nc = s // L
    seq = pl.BlockSpec((1, s, LANES), lambda bi, p: (bi, 0, p))
    return pl.pallas_call(
        functools.partial(_mlstm_kernel, n_ctx_chunks=n_ctx // L, n_chunks=nc),
        grid=(b, pairs),
        in_specs=[
            pl.BlockSpec((1, 1, 8), lambda bi, p: (p, 0, 0)),
            pl.BlockSpec((1, 8, 1), lambda bi, p: (p, 0, 0)),
            pl.BlockSpec((1, LANES), lambda bi, p: (0, p)),
            seq, seq, seq,
            pl.BlockSpec((1, 1, s, 8), lambda bi, p: (bi, p, 0, 0)),
            pl.BlockSpec((1, 1, nc, 8, L), lambda bi, p: (bi, p, 0, 0, 0)),
            seq,
        ],
        out_specs=seq,
        out_shape=jax.ShapeDtypeStruct((b, s, width), F32),
        scratch_shapes=[pltpu.VMEM((s, LANES), F32), pltpu.VMEM((s, LANES), F32),
                        pltpu.VMEM((2, LANES, 2 * LANES), F32)],
        compiler_params=_params(("arbitrary", "arbitrary")),
        name="mlstm",
    )(bias_c, bias_r, norm_g, qm, km, vm, gc, gr, og)


def _out_kernel(x_ref, mod_ref, ya_ref, yb_ref, ym_ref, g_ref, w_ref, lng_ref, lnb_ref, o_ref, *, alpha):
    x = x_ref[0]
    gate = mod_ref[0, 0, 2:3, :]
    g = g_ref[0].astype(F32)
    na, nb_ = ya_ref.shape[2], yb_ref.shape[2]
    mix_a = (ya_ref[0] * g[:, :na]).astype(BF16)
    mix_b = (yb_ref[0] * g[:, na:na + nb_]).astype(BF16)
    mix_m = (ym_ref[0] * g[:, na + nb_:]).astype(BF16)
    y = (_dot(mix_a, w_ref[0:na, :]) + _dot(mix_b, w_ref[na:na + nb_, :])
         + _dot(mix_m, w_ref[na + nb_:, :]))
    r = alpha * x + gate * y
    mu = jnp.mean(r, axis=1, keepdims=True)
    d = r - mu
    var = jnp.mean(d * d, axis=1, keepdims=True)
    o_ref[0] = d * lax.rsqrt(var + LN_EPS) * lng_ref[...] + lnb_ref[...]


def _out_projection(xa, modsel, ya, yb, ym, g, w, ln_g, ln_b, alpha):
    b, s, d = xa.shape
    tm = ROW_TILE
    row = lambda n: pl.BlockSpec((1, tm, n), lambda bi, i: (bi, i, 0))
    vec = pl.BlockSpec((1, d), lambda bi, i: (0, 0))
    return pl.pallas_call(
        functools.partial(_out_kernel, alpha=alpha),
        grid=(b, s // tm),
        in_specs=[
            row(d),
            pl.BlockSpec((1, 1, 3, d), lambda bi, i: (bi, jnp.minimum(i, 1), 0, 0)),
            row(ya.shape[2]), row(yb.shape[2]), row(ym.shape[2]), row(d),
            pl.BlockSpec(w.shape, lambda bi, i: (0, 0)),
            vec, vec,
        ],
        out_specs=row(d),
        out_shape=jax.ShapeDtypeStruct((b, s, d), F32),
        compiler_params=_params(("arbitrary", "arbitrary")),
        name="out_projection",
    )(xa, modsel, ya, yb, ym, g, w, ln_g, ln_b)


def _in_weight_columns(d_in):
    z = d_in
    src = {}
    off = 0
    for name, n in (("qa", 256), ("ka", 128), ("va", 128), ("za", 256), ("qb", 384), ("kb", 384), ("vb", 384),
                    ("zb", 384), ("qm", 384), ("km", 384), ("vm", 384), ("om", 384), ("zm", 384), ("gm", 24)):
        src[name] = np.arange(off, off + n)
        off += n
    assert off == d_in
    cols = []
    for hd in range(4):
        blk = np.full(LANES, z)
        gq = hd // 2
        blk[64 * gq:64 * gq + 64] = src["qa"][64 * hd:64 * hd + 64]
        cols.append(blk)
    for name in ("ka", "va", "qb", "kb", "vb", "qm", "km", "vm", "om", "za", "zb", "zm"):
        cols.append(src[name])
    cols.append(np.concatenate([src["gm"], np.full(LANES - 24, z)]))
    cols = np.concatenate(cols)
    assert cols.shape[0] == _W_COLS
    return cols


def _rope_table(n_tokens, n_ctx, dim):
    rows = n_tokens // GRID_W
    rowp = jnp.broadcast_to(jnp.arange(rows, dtype=F32)[:, None], (rows, GRID_W)).reshape(-1)
    colp = jnp.broadcast_to(jnp.arange(GRID_W, dtype=F32)[None, :], (rows, GRID_W)).reshape(-1)
    n_freq = dim // 4
    inv = ROPE_BASE ** (-jnp.arange(n_freq, dtype=F32) / n_freq)
    ar = rowp[:, None] * inv
    ac = colp[:, None] * inv
    ang = jnp.concatenate([ar, ar, ac, ac], -1)
    cos, sin = jnp.cos(ang), jnp.sin(ang)
    odd = (jnp.arange(dim) // n_freq) % 2 == 1
    sin_p = jnp.where(odd, sin, 0.0)
    sin_m = jnp.where(odd, 0.0, -sin)
    tab = jnp.stack([cos, sin_p, sin_m])
    ident = jnp.stack([jnp.ones((n_ctx, dim), F32), jnp.zeros((n_ctx, dim), F32), jnp.zeros((n_ctx, dim), F32)])
    tab = jnp.concatenate([ident, tab], axis=1)
    return jnp.tile(tab, (1, 1, LANES // dim))


def kernel(x, c, ctx, c_ctx, w_mod, b_mod, w_in, attn_sink, diff_lambda, diff_norm_g, mlstm_i_bias,
           mlstm_f_bias, mlstm_norm_g, w_out, ln_g, ln_b):
    b, t, d = x.shape
    n_ctx = ctx.shape[1]
    depth = w_mod.shape[0]
    d_in = w_in.shape[2]
    s = n_ctx + t
    L = MLSTM_CHUNK
    nc = s // L
    alpha = (2 * depth) ** 0.25

    xa = jnp.concatenate([ctx, x], axis=1)
    rope_a = _rope_table(t, n_ctx, HEAD_DIM)
    rope_b = _rope_table(t, n_ctx, B_QK_DIM)

    rows = -(-(b + 1) // 8) * 8
    cc = jnp.concatenate([c, c_ctx[None, :], jnp.zeros((rows - b - 1, d), F32)], axis=0)
    mod = _modulation(cc, w_mod, b_mod).reshape(depth, rows, 3, d)

    cols = _in_weight_columns(d_in)
    w_in_p = jnp.take(jnp.concatenate([w_in, jnp.zeros((depth, d, 1), F32)], axis=2), cols, axis=2).astype(BF16)
    w_out_b = w_out.astype(BF16)
    norm_b = jnp.tile(diff_norm_g, (1, LANES // diff_norm_g.shape[1]))[:, None, :]

    n_pairs = mlstm_i_bias.shape[2] // 2
    heads = mlstm_i_bias.shape[2]
    gate_cols = np.array([[di * 2 * heads + gi * heads + 2 * p + a for gi in range(2) for di in range(2) for a in range(2)]
                          for p in range(n_pairs)])

    for l in range(depth):
        lam_init = 0.8 - 0.6 * math.exp(-0.3 * l)
        modsel = jnp.stack([jnp.broadcast_to(mod[l, b], (b, 3, d)), mod[l, :b]], axis=1)
        (qa, ka, va, qb, kb, vb, qm, km, vm, og, g, gm) = _in_projection(xa, modsel, w_in_p[l], rope_a, rope_b)

        ya = _attn_a(attn_sink[l], qa, ka, va, n_ctx)
        yb = _attn_b(jnp.full((1,), lam_init, F32), diff_lambda[l], norm_b[l], qb, kb, vb, n_ctx)

        gsel = jnp.transpose(gm[:, :, gate_cols], (0, 2, 1, 3))
        gr = jnp.swapaxes(gsel.reshape(b, n_pairs, nc, L, 8), 3, 4)
        bias = jnp.stack([mlstm_i_bias[l], mlstm_f_bias[l]], axis=1).reshape(-1)[gate_cols]
        ym = _mlstm(bias[:, None, :], bias[:, :, None], mlstm_norm_g[l][None, :], qm, km, vm, gsel, gr, og, n_ctx)

        xa = _out_projection(xa, modsel, ya, yb, ym, g, w_out_b[l], ln_g[l][None, :], ln_b[l][None, :], alpha)
    return xa[:, n_ctx:, :]
```

```python
import functools
import math

import numpy as np
import jax
import jax.numpy as jnp
from jax import lax
from jax.experimental import pallas as pl
from jax.experimental.pallas import tpu as pltpu

F32 = jnp.float32
BF16 = jnp.bfloat16

LANES = 128
GRID_W = 64
HEAD_DIM = 64
ROPE_BASE = 10000.0
LN_EPS = 1e-5
WINDOW = 128
B_QK_DIM = 32
LOG2E = math.log2(math.e)
MLSTM_CHUNK = 128
ROW_TILE = 256
VMEM_LIMIT = 56 * 1024 * 1024


def _dot(a, b):
    return jnp.dot(a, b, preferred_element_type=F32)


def _dot_nt(a, b):
    return lax.dot_general(a, b, (((1,), (1,)), ((), ())), preferred_element_type=F32)


def _dot_tn(a, b):
    return lax.dot_general(a, b, (((0,), (0,)), ((), ())), preferred_element_type=F32)


def _split_bf16(a):
    hi = a.astype(BF16)
    lo = (a - hi.astype(F32)).astype(BF16)
    return hi, lo


def _sigmoid(x):
    return 1.0 / (1.0 + jnp.exp(-x))


def _params(sem):
    return pltpu.CompilerParams(dimension_semantics=sem, vmem_limit_bytes=VMEM_LIMIT)


def _mod_kernel(c_ref, w_ref, b_ref, o_ref):
    c = c_ref[...]
    a = c * _sigmoid(c)
    a_hi, a_lo = _split_bf16(a)
    w_hi, w_lo = _split_bf16(w_ref[0])
    o_ref[0] = _dot(a_hi, w_hi) + _dot(a_lo, w_hi) + _dot(a_hi, w_lo) + b_ref[0]


def _modulation(cc, w_mod, b_mod):
    depth, d, n = w_mod.shape
    r = cc.shape[0]
    tn = 1024
    return pl.pallas_call(
        _mod_kernel,
        grid=(depth, n // tn),
        in_specs=[
            pl.BlockSpec((r, d), lambda l, j: (0, 0)),
            pl.BlockSpec((1, d, tn), lambda l, j: (l, 0, j)),
            pl.BlockSpec((1, 1, tn), lambda l, j: (l, 0, j)),
        ],
        out_specs=pl.BlockSpec((1, r, tn), lambda l, j: (l, 0, j)),
        out_shape=jax.ShapeDtypeStruct((depth, r, n), F32),
        compiler_params=_params(("arbitrary", "arbitrary")),
        name="modulation",
    )(cc, w_mod, b_mod.reshape(depth, 1, n))


_QA, _KA, _VA = (0, 512), (512, 640), (640, 768)
_QB, _KB, _VB = (768, 1152), (1152, 1536), (1536, 1920)
_QM, _KM, _VM = (1920, 2304), (2304, 2688), (2688, 3072)
_OM, _Z, _GM = (3072, 3456), (3456, 4480), (4480, 4608)
_W_COLS = 4608


def _rope(t, tab_ref, quarter):
    cos, sin_p, sin_m = tab_ref[0], tab_ref[1], tab_ref[2]
    outs = []
    for j in range(t.shape[1] // LANES):
        tj = t[:, j * LANES:(j + 1) * LANES]
        outs.append(tj * cos + pltpu.roll(tj, quarter, 1) * sin_p + pltpu.roll(tj, LANES - quarter, 1) * sin_m)
    return outs[0] if len(outs) == 1 else jnp.concatenate(outs, axis=1)


def _inproj_kernel(x_ref, mod_ref, w_ref, ra_ref, rb_ref,
                   qa_ref, ka_ref, va_ref, qb_ref, kb_ref, vb_ref,
                   qm_ref, km_ref, vm_ref, om_ref, g_ref, gm_ref):
    x = x_ref[0]
    shift = mod_ref[0, 0, 0:1, :]
    scale = mod_ref[0, 0, 1:2, :]
    h = (x * (1.0 + scale) + shift).astype(BF16)

    def proj(cols):
        return _dot(h, w_ref[:, cols[0]:cols[1]])

    qa_ref[0] = (_rope(proj(_QA), ra_ref, HEAD_DIM // 4) * (HEAD_DIM ** -0.5)).astype(BF16)
    ka_ref[0] = _rope(proj(_KA), ra_ref, HEAD_DIM // 4).astype(BF16)
    va_ref[0] = proj(_VA).astype(BF16)
    qb_t = (_rope(proj(_QB), rb_ref, B_QK_DIM // 4) * (B_QK_DIM ** -0.5 * LOG2E)).T.astype(BF16)
    vb_t = proj(_VB).T.astype(BF16)
    for p in range(qb_t.shape[0] // LANES):
        qb_ref[0, p, 0] = qb_t[p * LANES:(p + 1) * LANES]
        vb_ref[0, p, 0] = vb_t[p * LANES:(p + 1) * LANES]
    kb_ref[0] = _rope(proj(_KB), rb_ref, B_QK_DIM // 4).astype(BF16)
    qm_ref[0] = proj(_QM).astype(BF16)
    km_ref[0] = (proj(_KM) * (HEAD_DIM ** -0.5)).astype(BF16)
    vm_ref[0] = proj(_VM).astype(BF16)
    om_ref[0] = _sigmoid(proj(_OM)).astype(BF16)
    z = proj(_Z)
    g_ref[0] = (z * _sigmoid(z)).astype(BF16)
    gm_ref[0] = proj(_GM)


def _in_projection(xa, modsel, w, rope_a, rope_b):
    b, s, d = xa.shape
    tm = ROW_TILE
    widths = [512, 128, 128, None, 384, None, 384, 384, 384, 384, 1024]
    n_pairs = (_QB[1] - _QB[0]) // LANES
    tposed = pl.BlockSpec((1, n_pairs, 1, LANES, tm), lambda bi, i: (bi, 0, i, 0, 0))
    row = lambda n: tposed if n is None else pl.BlockSpec((1, tm, n), lambda bi, i: (bi, i, 0))
    out_shape = [jax.ShapeDtypeStruct((b, n_pairs, s // tm, LANES, tm) if n is None else (b, s, n), BF16)
                 for n in widths]
    out_shape.append(jax.ShapeDtypeStruct((b, s, LANES), F32))
    return pl.pallas_call(
        _inproj_kernel,
        grid=(b, s // tm),
        in_specs=[
            row(d),
            pl.BlockSpec((1, 1, 3, d), lambda bi, i: (bi, jnp.minimum(i, 1), 0, 0)),
            pl.BlockSpec((d, _W_COLS), lambda bi, i: (0, 0)),
            pl.BlockSpec((3, tm, LANES), lambda bi, i: (0, i, 0)),
            pl.BlockSpec((3, tm, LANES), lambda bi, i: (0, i, 0)),
        ],
        out_specs=[row(n) for n in widths] + [row(LANES)],
        out_shape=out_shape,
        compiler_params=_params(("arbitrary", "arbitrary")),
        name="in_projection",
    )(xa, modsel, w, rope_a, rope_b)


def _attn_a_kernel(sink_ref, q_ref, k_ref, v_ref, o_ref, *, n_ctx, n_blocks):
    w = WINDOW
    s_len = n_blocks * w
    half = LANES // 2
    rows1 = lax.broadcasted_iota(jnp.int32, (4 * w, 1), 0)
    sink = jnp.where(rows1 < w, sink_ref[0],
                     jnp.where(rows1 < 2 * w, sink_ref[1], jnp.where(rows1 < 3 * w, sink_ref[2], sink_ref[3])))
    row = lax.broadcasted_iota(jnp.int32, (4 * w, 3 * w), 0) & (w - 1)
    col = lax.broadcasted_iota(jnp.int32, (4 * w, 3 * w), 1)
    lane = lax.broadcasted_iota(jnp.int32, (w, LANES), 1)
    k_ctx = k_ref[0, 0:n_ctx, :]
    v_ctx = v_ref[0, 0:n_ctx, :]

    def body(i, carry):
        r0 = pl.multiple_of(i * w, w)
        q = q_ref[0, pl.ds(r0, w), :]
        qs = jnp.concatenate([q[:, j * LANES:(j + 1) * LANES] for j in range(4)], axis=0)
        start = pl.multiple_of(jnp.clip(r0 - w, 0, s_len - 3 * w), w)
        s_loc = _dot_nt(qs, k_ref[0, pl.ds(start, 3 * w), :])
        s_ctx = _dot_nt(qs, k_ctx)
        kpos = col + start
        rel = kpos - r0 - row
        ok = (jnp.abs(rel) <= w) & (kpos >= jnp.where(r0 >= n_ctx, n_ctx, s_len))
        s_loc = jnp.where(ok, s_loc, -jnp.inf)
        m = jnp.maximum(jnp.maximum(jnp.max(s_loc, axis=1, keepdims=True),
                                    jnp.max(s_ctx, axis=1, keepdims=True)), sink)
        p_loc = jnp.exp(s_loc - m)
        p_ctx = jnp.exp(s_ctx - m)
        den = (jnp.sum(p_loc, axis=1, keepdims=True) + jnp.sum(p_ctx, axis=1, keepdims=True)
               + jnp.exp(sink - m))
        o = (_dot(p_loc.astype(BF16), v_ref[0, pl.ds(start, 3 * w), :])
             + _dot(p_ctx.astype(BF16), v_ctx)) / den
        left = jnp.where(lane < half, o[0:w], pltpu.roll(o[w:2 * w], half, 1))
        right = jnp.where(lane < half, pltpu.roll(o[2 * w:3 * w], half, 1), o[3 * w:])
        o_ref[0, pl.ds(r0, w), :] = jnp.concatenate([left, right], axis=1)
        return carry

    lax.fori_loop(0, n_blocks, body, 0, unroll=2)


def _attn_a(sink, qa, ka, va, n_ctx):
    b, s, _ = qa.shape
    seq = lambda n: pl.BlockSpec((1, s, n), lambda bi: (bi, 0, 0))
    return pl.pallas_call(
        functools.partial(_attn_a_kernel, n_ctx=n_ctx, n_blocks=s // WINDOW),
        grid=(b,),
        in_specs=[pl.BlockSpec(memory_space=pltpu.SMEM), seq(4 * LANES), seq(LANES), seq(LANES)],
        out_specs=seq(2 * LANES),
        out_shape=jax.ShapeDtypeStruct((b, s, 2 * LANES), F32),
        compiler_params=_params(("arbitrary",)),
        name="window_attention",
    )(sink, qa, ka, va)


def _attn_b_kernel(li_ref, lam_ref, ng_ref, qt_ref, k_ref, vt_ref, o_ref, sa_scr, sb_scr, acc_scr,
                   *, n_ctx, n_lat, tk, tq):
    i = pl.program_id(2)
    nt = n_lat // tk
    w = vt_ref.shape[-1]
    qt = qt_ref[0, 0, 0]
    unit = lax.broadcasted_iota(jnp.int32, qt.shape, 0) >> 5
    zero = jnp.zeros_like(qt)
    qst = jnp.concatenate([jnp.where(unit == u, qt, zero) for u in range(4)], axis=1)

    def scores(off, n):
        return _dot(k_ref[0, pl.ds(off, n), :], qst)

    def values(p, j0, n):
        pb = p.astype(BF16)
        out = _dot(vt_ref[0, 0, j0], pb[0:w])
        for c in range(1, n // w):
            out = out + _dot(vt_ref[0, 0, j0 + c], pb[c * w:(c + 1) * w])
        return out

    s = scores(0, n_ctx)
    m = jnp.max(s, axis=0, keepdims=True)
    p = jnp.exp2(s - m)
    l = jnp.sum(p, axis=0, keepdims=True)
    acc_scr[...] = values(p, 0, n_ctx)

    def fetch(t, s_scr):
        off = pl.multiple_of(n_ctx + t * tk, LANES)
        s_new = scores(off, tk)
        s_scr[...] = s_new
        return jnp.max(s_new, axis=0, keepdims=True)

    def consume(s_scr, mt, m, l, t):
        m_new = jnp.maximum(m, mt)
        alpha = jnp.exp2(m - m_new)
        p = jnp.exp2(s_scr[...] - m_new)
        l = alpha * l + jnp.sum(p, axis=0, keepdims=True)
        acc_scr[...] = alpha * acc_scr[...] + values(p, n_ctx // w + t * (tk // w), tk)
        return m_new, l

    def body(tt, carry):
        m, l, mt_a = carry
        t0 = 2 * tt
        mt_b = fetch(t0 + 1, sb_scr)
        m, l = consume(sa_scr, mt_a, m, l, t0)
        mt_a = fetch(jnp.minimum(t0 + 2, nt - 1), sa_scr)
        m, l = consume(sb_scr, mt_b, m, l, t0 + 1)
        return m, l, mt_a

    mt0 = fetch(0, sa_scr)
    n_iter = jnp.where(i >= n_ctx // tq, nt // 2, 0)
    m, l, _ = lax.fori_loop(0, n_iter, body, (m, l, mt0))

    lam_init = li_ref[0]
    lv = lam_ref[...]
    lam = (jnp.exp(jnp.sum(lv[0:1] * lv[1:2], axis=1, keepdims=True))
           - jnp.exp(jnp.sum(lv[2:3] * lv[3:4], axis=1, keepdims=True)) + lam_init)
    o = acc_scr[...] * (1.0 / l)
    h = LANES // 2
    o_a = o[0:h, 0:tq] - lam * o[0:h, tq:2 * tq]
    o_b = o[h:, 2 * tq:3 * tq] - lam * o[h:, 3 * tq:]

    def norm(x):
        ms = jnp.mean(x * x, axis=0, keepdims=True)
        return x * lax.rsqrt(ms + LN_EPS) * ng_ref[...]

    y_t = jnp.concatenate([norm(o_a), norm(o_b)], axis=0) * (1.0 - lam_init)
    o_ref[0] = y_t.T


def _attn_b(lam_init, lam_vec, norm_g, qbt, kb, vbt, n_ctx):
    b, pairs, n_row_tiles, _, tq = qbt.shape
    s = n_row_tiles * tq
    n_lat = s - n_ctx
    tk = min(512, n_lat // 2)
    assert n_lat % (2 * tk) == 0 and tk % tq == 0 and n_ctx % tq == 0
    return pl.pallas_call(
        functools.partial(_attn_b_kernel, n_ctx=n_ctx, n_lat=n_lat, tk=tk, tq=tq),
        grid=(b, pairs, n_row_tiles),
        in_specs=[
            pl.BlockSpec(memory_space=pltpu.SMEM),
            pl.BlockSpec((4, B_QK_DIM), lambda bi, p, i: (0, 0)),
            pl.BlockSpec((LANES // 2, 1), lambda bi, p, i: (0, 0)),
            pl.BlockSpec((1, 1, 1, LANES, tq), lambda bi, p, i: (bi, p, i, 0, 0)),
            pl.BlockSpec((1, s, LANES), lambda bi, p, i: (bi, 0, p)),
            pl.BlockSpec((1, 1, n_row_tiles, LANES, tq), lambda bi, p, i: (bi, p, 0, 0, 0)),
        ],
        out_specs=pl.BlockSpec((1, tq, LANES), lambda bi, p, i: (bi, i, p)),
        out_shape=jax.ShapeDtypeStruct((b, s, pairs * LANES), F32),
        scratch_shapes=[pltpu.VMEM((tk, 4 * tq), F32), pltpu.VMEM((tk, 4 * tq), F32),
                        pltpu.VMEM((LANES, 4 * tq), F32)],
        compiler_params=_params(("arbitrary", "arbitrary", "arbitrary")),
        name="diff_attention",
    )(lam_init, lam_vec, norm_g, qbt, kb, vbt)


def _log_sigmoid(x):
    return jnp.minimum(x, 0.0) - jnp.log(1.0 + jnp.exp(-jnp.abs(x)))


def _mlstm_kernel(bc_ref, br_ref, ng_ref, q_ref, k_ref, v_ref, gc_ref, gr_ref, og_ref, o_ref,
                  hf_scr, hb_scr, st_scr, *, n_ctx_chunks, n_chunks):
    L = MLSTM_CHUNK
    half = LANES // 2
    r_i = lax.broadcasted_iota(jnp.int32, (L, L), 0)
    c_i = lax.broadcasted_iota(jnp.int32, (L, L), 1)
    lower = r_i >= c_i
    upper = r_i <= c_i
    t_low = jnp.where(lower, 1.0, 0.0).astype(BF16)
    t_up = jnp.where(upper, 1.0, 0.0).astype(BF16)
    lane = lax.broadcasted_iota(jnp.int32, (L, LANES), 1)
    first = lane < half
    ones_ext = jnp.where(lane < 2, 1.0, 0.0).astype(BF16)
    sel_a = jnp.where(first, 1.0, 0.0).astype(BF16)
    sel_b = jnp.where(first, 0.0, 1.0).astype(BF16)
    sr = lax.broadcasted_iota(jnp.int32, (LANES, 2 * LANES), 0)
    sc = lax.broadcasted_iota(jnp.int32, (LANES, 2 * LANES), 1)
    top = sr < half
    keep = ((top & ((sc < half) | (sc == LANES)))
            | ((sr >= half) & (((sc >= half) & (sc < LANES)) | (sc == LANES + 1))))
    st_scr[...] = jnp.zeros_like(st_scr)
    bias_c = bc_ref[0]
    bias_r = br_ref[0]

    def chunk_step(direction, c, m_prev):
        r0 = pl.multiple_of(c * L, L)
        qc = q_ref[0, pl.ds(r0, L), :]
        kc = k_ref[0, pl.ds(r0, L), :]
        vc = v_ref[0, pl.ds(r0, L), :]
        v_ext = jnp.concatenate([vc, ones_ext], axis=1)
        gcol = gc_ref[0, 0, pl.ds(r0, L), :] + bias_c
        grow = gr_ref[0, 0, c] + bias_r
        fcol = _log_sigmoid(gcol)
        frow = _log_sigmoid(grow)
        fc_hi, fc_lo = _split_bf16(fcol)
        fr_hi, fr_lo = _split_bf16(frow)
        if direction == 0:
            bcol_all = _dot(t_low, fc_hi) + _dot(t_low, fc_lo)
            brow_all = _dot(fr_hi, t_up) + _dot(fr_lo, t_up)
            tri = lower
        else:
            bcol_all = _dot(t_up, fc_hi) + _dot(t_up, fc_lo)
            brow_all = _dot(fr_hi, t_low) + _dot(fr_lo, t_low)
            tri = upper
        qs = jnp.concatenate([qc * sel_a, qc * sel_b], axis=0)
        s_all = _dot_nt(qs, kc)
        state = st_scr[direction]
        r_mat = _dot(qc, state.astype(BF16))
        hs, ws, decays, m_news = [], [], [], []
        for a in range(2):
            ci = 2 * direction + a
            cf = 4 + ci
            bcol = bcol_all[:, cf:cf + 1]
            brow = brow_all[cf:cf + 1, :]
            icol = gcol[:, ci:ci + 1]
            irow = grow[ci:ci + 1, :]
            b_last = jnp.sum(frow[cf:cf + 1, :], axis=1, keepdims=True)
            log_d = jnp.where(tri, bcol - brow + irow, -jnp.inf)
            m_inter = bcol + m_prev[a]
            m_t = jnp.maximum(m_inter, jnp.max(log_d, axis=1, keepdims=True))
            dmat = jnp.exp(log_d - m_t)
            w_inter = jnp.exp(m_inter - m_t)
            s_a = s_all[a * L:(a + 1) * L] * dmat
            p_mat = _dot(s_a.astype(BF16), v_ext)
            num = w_inter * r_mat[:, :LANES] + p_mat[:, :LANES]
            den = w_inter * r_mat[:, LANES + a:LANES + a + 1] + p_mat[:, LANES:LANES + 1]
            hs.append(num / jnp.maximum(jnp.abs(den), jnp.exp(-m_t)))
            log_w = b_last - bcol + icol
            m_new = jnp.maximum(b_last + m_prev[a], jnp.max(log_w, axis=0, keepdims=True))
            ws.append(jnp.exp(log_w - m_new))
            decays.append(jnp.exp(b_last + m_prev[a] - m_new))
            m_news.append(m_new)
        h_pair = jnp.where(first, hs[0], hs[1])
        if direction == 0:
            hf_scr[pl.ds(r0, L), :] = h_pair
        else:
            hb_scr[pl.ds(r0, L), :] = h_pair
        kw = (kc.astype(F32) * jnp.where(first, ws[0], ws[1])).astype(BF16)
        upd = _dot_tn(kw, v_ext)
        st_scr[direction] = jnp.where(top, decays[0], decays[1]) * state + jnp.where(keep, upd, 0.0)
        return m_news

    def body(j, carry):
        mf, mb = carry
        cb = jnp.where(j < n_ctx_chunks, n_ctx_chunks - 1 - j, n_chunks - 1 + n_ctx_chunks - j)
        mf = chunk_step(0, j, mf)
        mb = chunk_step(1, cb, mb)
        return (tuple(mf), tuple(mb))

    z11 = jnp.zeros((1, 1), F32)
    lax.fori_loop(0, n_chunks, body, ((z11, z11), (z11, z11)))

    def finish(c, carry):
        r0 = pl.multiple_of(c * L, L)
        h = hf_scr[pl.ds(r0, L), :] + hb_scr[pl.ds(r0, L), :]
        sq = h * h
        ms_a = jnp.sum(jnp.where(first, sq, 0.0), axis=1, keepdims=True)
        ms_b = jnp.sum(jnp.where(first, 0.0, sq), axis=1, keepdims=True)
        ms = jnp.where(first, ms_a, ms_b) * (1.0 / half)
        y = h * lax.rsqrt(ms + LN_EPS) * ng_ref[...]
        o_ref[0, pl.ds(r0, L), :] = y * og_ref[0, pl.ds(r0, L), :].astype(F32)
        return carry
    lax.fori_loop(0, n_chunks, finish, 0)


def _mlstm(bias_c, bias_r, norm_g, qm, km, vm, gc, gr, og, n_ctx):
    b, s, width = qm.shape
    L = MLSTM_CHUNK
    pairs = width // LANES
    nc = s // L
    seq = pl.BlockSpec((1, s, LANES), lambda bi, p: (bi, 0, p))
    return pl.pallas_call(
        functools.partial(_mlstm_kernel, n_ctx_chunks=n_ctx // L, n_chunks=nc),
        grid=(b, pairs),
        in_specs=[
            pl.BlockSpec((1, 1, 8), lambda bi, p: (p, 0, 0)),
            pl.BlockSpec((1, 8, 1), lambda bi, p: (p, 0, 0)),
            pl.BlockSpec((1, LANES), lambda bi, p: (0, p)),
            seq, seq, seq,
            pl.BlockSpec((1, 1, s, 8), lambda bi, p: (bi, p, 0, 0)),
            pl.BlockSpec((1, 1, nc, 8, L), lambda bi, p: (bi, p, 0, 0, 0)),
            seq,
        ],
        out_specs=seq,
        out_shape=jax.ShapeDtypeStruct((b, s, width), F32),
        scratch_shapes=[pltpu.VMEM((s, LANES), F32), pltpu.VMEM((s, LANES), F32),
                        pltpu.VMEM((2, LANES, 2 * LANES), F32)],
        compiler_params=_params(("arbitrary", "arbitrary")),
        name="mlstm",
    )(bias_c, bias_r, norm_g, qm, km, vm, gc, gr, og)


def _out_kernel(x_ref, mod_ref, ya_ref, yb_ref, ym_ref, g_ref, w_ref, lng_ref, lnb_ref, o_ref, *, alpha):
    x = x_ref[0]
    gate = mod_ref[0, 0, 2:3, :]
    g = g_ref[0].astype(F32)
    na, nb_ = ya_ref.shape[2], yb_ref.shape[2]
    mix_a = (ya_ref[0] * g[:, :na]).astype(BF16)
    mix_b = (yb_ref[0] * g[:, na:na + nb_]).astype(BF16)
    mix_m = (ym_ref[0] * g[:, na + nb_:]).astype(BF16)
    y = (_dot(mix_a, w_ref[0:na, :]) + _dot(mix_b, w_ref[na:na + nb_, :])
         + _dot(mix_m, w_ref[na + nb_:, :]))
    r = alpha * x + gate * y
    mu = jnp.mean(r, axis=1, keepdims=True)
    d = r - mu
    var = jnp.mean(d * d, axis=1, keepdims=True)
    o_ref[0] = d * lax.rsqrt(var + LN_EPS) * lng_ref[...] + lnb_ref[...]


def _out_projection(xa, modsel, ya, yb, ym, g, w, ln_g, ln_b, alpha):
    b, s, d = xa.shape
    tm = ROW_TILE
    row = lambda n: pl.BlockSpec((1, tm, n), lambda bi, i: (bi, i, 0))
    vec = pl.BlockSpec((1, d), lambda bi, i: (0, 0))
    return pl.pallas_call(
        functools.partial(_out_kernel, alpha=alpha),
        grid=(b, s // tm),
        in_specs=[
            row(d),
            pl.BlockSpec((1, 1, 3, d), lambda bi, i: (bi, jnp.minimum(i, 1), 0, 0)),
            row(ya.shape[2]), row(yb.shape[2]), row(ym.shape[2]), row(d),
            pl.BlockSpec(w.shape, lambda bi, i: (0, 0)),
            vec, vec,
        ],
        out_specs=row(d),
        out_shape=jax.ShapeDtypeStruct((b, s, d), F32),
        compiler_params=_params(("arbitrary", "arbitrary")),
        name="out_projection",
    )(xa, modsel, ya, yb, ym, g, w, ln_g, ln_b)


def _in_weight_columns(d_in):
    z = d_in
    src = {}
    off = 0
    for name, n in (("qa", 256), ("ka", 128), ("va", 128), ("za", 256), ("qb", 384), ("kb", 384), ("vb", 384),
                    ("zb", 384), ("qm", 384), ("km", 384), ("vm", 384), ("om", 384), ("zm", 384), ("gm", 24)):
        src[name] = np.arange(off, off + n)
        off += n
    assert off == d_in
    cols = []
    for hd in range(4):
        blk = np.full(LANES, z)
        gq = hd // 2
        blk[64 * gq:64 * gq + 64] = src["qa"][64 * hd:64 * hd + 64]
        cols.append(blk)
    for name in ("ka", "va", "qb", "kb", "vb", "qm", "km", "vm", "om", "za", "zb", "zm"):
        cols.append(src[name])
    cols.append(np.concatenate([src["gm"], np.full(LANES - 24, z)]))
    cols = np.concatenate(cols)
    assert cols.shape[0] == _W_COLS
    return cols


def _rope_table(n_tokens, n_ctx, dim):
    rows = n_tokens // GRID_W
    rowp = jnp.broadcast_to(jnp.arange(rows, dtype=F32)[:, None], (rows, GRID_W)).reshape(-1)
    colp = jnp.broadcast_to(jnp.arange(GRID_W, dtype=F32)[None, :], (rows, GRID_W)).reshape(-1)
    n_freq = dim // 4
    inv = ROPE_BASE ** (-jnp.arange(n_freq, dtype=F32) / n_freq)
    ar = rowp[:, None] * inv
    ac = colp[:, None] * inv
    ang = jnp.concatenate([ar, ar, ac, ac], -1)
    cos, sin = jnp.cos(ang), jnp.sin(ang)
    odd = (jnp.arange(dim) // n_freq) % 2 == 1
    sin_p = jnp.where(odd, sin, 0.0)
    sin_m = jnp.where(odd, 0.0, -sin)
    tab = jnp.stack([cos, sin_p, sin_m])
    ident = jnp.stack([jnp.ones((n_ctx, dim), F32), jnp.zeros((n_ctx, dim), F32), jnp.zeros((n_ctx, dim), F32)])
    tab = jnp.concatenate([ident, tab], axis=1)
    return jnp.tile(tab, (1, 1, LANES // dim))


def kernel(x, c, ctx, c_ctx, w_mod, b_mod, w_in, attn_sink, diff_lambda, diff_norm_g, mlstm_i_bias,
           mlstm_f_bias, mlstm_norm_g, w_out, ln_g, ln_b):
    b, t, d = x.shape
    n_ctx = ctx.shape[1]
    depth = w_mod.shape[0]
    d_in = w_in.shape[2]
    s = n_ctx + t
    L = MLSTM_CHUNK
    nc = s // L
    alpha = (2 * depth) ** 0.25

    xa = jnp.concatenate([ctx, x], axis=1)
    rope_a = _rope_table(t, n_ctx, HEAD_DIM)
    rope_b = _rope_table(t, n_ctx, B_QK_DIM)

    rows = -(-(b + 1) // 8) * 8
    cc = jnp.concatenate([c, c_ctx[None, :], jnp.zeros((rows - b - 1, d), F32)], axis=0)
    mod = _modulation(cc, w_mod, b_mod).reshape(depth, rows, 3, d)

    cols = _in_weight_columns(d_in)
    w_in_p = jnp.take(jnp.concatenate([w_in, jnp.zeros((depth, d, 1), F32)], axis=2), cols, axis=2).astype(BF16)
    w_out_b = w_out.astype(BF16)
    norm_b = diff_norm_g[:, :, None]

    n_pairs = mlstm_i_bias.shape[2] // 2
    heads = mlstm_i_bias.shape[2]
    gate_cols = np.array([[di * 2 * heads + gi * heads + 2 * p + a for gi in range(2) for di in range(2) for a in range(2)]
                          for p in range(n_pairs)])

    for l in range(depth):
        lam_init = 0.8 - 0.6 * math.exp(-0.3 * l)
        modsel = jnp.stack([jnp.broadcast_to(mod[l, b], (b, 3, d)), mod[l, :b]], axis=1)
        (qa, ka, va, qb, kb, vb, qm, km, vm, og, g, gm) = _in_projection(xa, modsel, w_in_p[l], rope_a, rope_b)

        ya = _attn_a(attn_sink[l], qa, ka, va, n_ctx)
        yb = _attn_b(jnp.full((1,), lam_init, F32), diff_lambda[l], norm_b[l], qb, kb, vb, n_ctx)

        gsel = jnp.transpose(gm[:, :, gate_cols], (0, 2, 1, 3))
        gr = jnp.swapaxes(gsel.reshape(b, n_pairs, nc, L, 8), 3, 4)
        bias = jnp.stack([mlstm_i_bias[l], mlstm_f_bias[l]], axis=1).reshape(-1)[gate_cols]
        ym = _mlstm(bias[:, None, :], bias[:, :, None], mlstm_norm_g[l][None, :], qm, km, vm, gsel, gr, og, n_ctx)

        xa = _out_projection(xa, modsel, ya, yb, ym, g, w_out_b[l], ln_g[l][None, :], ln_b[l][None, :], alpha)
    return xa[:, n_ctx:, :]
```

```python
import functools
import math

import numpy as np
import jax
import jax.numpy as jnp
from jax import lax
from jax.experimental import pallas as pl
from jax.experimental.pallas import tpu as pltpu

F32 = jnp.float32
BF16 = jnp.bfloat16

LANES = 128
GRID_W = 64
HEAD_DIM = 64
ROPE_BASE = 10000.0
LN_EPS = 1e-5
WINDOW = 128
B_QK_DIM = 32
LOG2E = math.log2(math.e)
VB_ROWS = HEAD_DIM + 16
MLSTM_CHUNK = 128
ROW_TILE = 256
VMEM_LIMIT = 56 * 1024 * 1024


def _dot(a, b):
    return jnp.dot(a, b, preferred_element_type=F32)


def _dot_nt(a, b):
    return lax.dot_general(a, b, (((1,), (1,)), ((), ())), preferred_element_type=F32)


def _dot_tn(a, b):
    return lax.dot_general(a, b, (((0,), (0,)), ((), ())), preferred_element_type=F32)


def _split_bf16(a):
    hi = a.astype(BF16)
    lo = (a - hi.astype(F32)).astype(BF16)
    return hi, lo


def _sigmoid(x):
    return 1.0 / (1.0 + jnp.exp(-x))


def _params(sem, flags=None):
    return pltpu.CompilerParams(dimension_semantics=sem, vmem_limit_bytes=VMEM_LIMIT, flags=flags)


def _mod_kernel(c_ref, w_ref, b_ref, o_ref):
    c = c_ref[...]
    a = c * _sigmoid(c)
    a_hi, a_lo = _split_bf16(a)
    w_hi, w_lo = _split_bf16(w_ref[0])
    o_ref[0] = _dot(a_hi, w_hi) + _dot(a_lo, w_hi) + _dot(a_hi, w_lo) + b_ref[0]


def _modulation(cc, w_mod, b_mod):
    depth, d, n = w_mod.shape
    r = cc.shape[0]
    tn = 1024
    return pl.pallas_call(
        _mod_kernel,
        grid=(depth, n // tn),
        in_specs=[
            pl.BlockSpec((r, d), lambda l, j: (0, 0)),
            pl.BlockSpec((1, d, tn), lambda l, j: (l, 0, j)),
            pl.BlockSpec((1, 1, tn), lambda l, j: (l, 0, j)),
        ],
        out_specs=pl.BlockSpec((1, r, tn), lambda l, j: (l, 0, j)),
        out_shape=jax.ShapeDtypeStruct((depth, r, n), F32),
        compiler_params=_params(("arbitrary", "arbitrary")),
        name="modulation",
    )(cc, w_mod, b_mod.reshape(depth, 1, n))


_QA, _KA, _VA = (0, 512), (512, 640), (640, 768)
_QB, _KB, _VB = (768, 1152), (1152, 1536), (1536, 1920)
_QM, _KM, _VM = (1920, 2304), (2304, 2688), (2688, 3072)
_OM, _Z, _GM = (3072, 3456), (3456, 4480), (4480, 4608)
_W_COLS = 4608


def _rope(t, tab_ref, quarter):
    cos, sin_p, sin_m = tab_ref[0], tab_ref[1], tab_ref[2]
    outs = []
    for j in range(t.shape[1] // LANES):
        tj = t[:, j * LANES:(j + 1) * LANES]
        outs.append(tj * cos + pltpu.roll(tj, quarter, 1) * sin_p + pltpu.roll(tj, LANES - quarter, 1) * sin_m)
    return outs[0] if len(outs) == 1 else jnp.concatenate(outs, axis=1)


def _inproj_kernel(x_ref, mod_ref, w_ref, ra_ref, rb_ref,
                   qa_ref, ka_ref, va_ref, qb_ref, kb_ref, vb_ref,
                   qm_ref, km_ref, vm_ref, om_ref, g_ref, gm_ref):
    x = x_ref[0]
    shift = mod_ref[0, 0, 0:1, :]
    scale = mod_ref[0, 0, 1:2, :]
    h = (x * (1.0 + scale) + shift).astype(BF16)

    def proj(cols):
        return _dot(h, w_ref[:, cols[0]:cols[1]])

    qa_ref[0] = (_rope(proj(_QA), ra_ref, HEAD_DIM // 4) * (HEAD_DIM ** -0.5)).astype(BF16)
    ka_ref[0] = _rope(proj(_KA), ra_ref, HEAD_DIM // 4).astype(BF16)
    va_ref[0] = proj(_VA).astype(BF16)
    qb_t = (_rope(proj(_QB), rb_ref, B_QK_DIM // 4) * (B_QK_DIM ** -0.5 * LOG2E)).T.astype(BF16)
    for p in range(qb_t.shape[0] // LANES):
        qb_ref[0, p, 0] = qb_t[p * LANES:(p + 1) * LANES]
    vb_t = proj(_VB).T
    ones_rows = jnp.where(lax.broadcasted_iota(jnp.int32, (VB_ROWS - HEAD_DIM, vb_t.shape[1]), 0) == 0, 1.0, 0.0)
    for hd in range(vb_t.shape[0] // HEAD_DIM):
        vb_ref[0, hd, 0] = jnp.concatenate([vb_t[hd * HEAD_DIM:(hd + 1) * HEAD_DIM], ones_rows], axis=0).astype(BF16)
    kb_ref[0] = _rope(proj(_KB), rb_ref, B_QK_DIM // 4).astype(BF16)
    km_ref[0] = (proj(_KM) * (HEAD_DIM ** -0.5)).astype(BF16)
    qm_t = proj(_QM).T.astype(BF16)
    om_t = _sigmoid(proj(_OM)).T.astype(BF16)
    vm_t = proj(_VM).T
    L = MLSTM_CHUNK
    ones_rows_c = jnp.where(lax.broadcasted_iota(jnp.int32, (VB_ROWS - HEAD_DIM, L), 0) == 0, 1.0, 0.0)
    for c in range(qm_t.shape[1] // L):
        cols = slice(c * L, (c + 1) * L)
        for p in range(qm_t.shape[0] // LANES):
            qm_ref[0, p, c] = qm_t[p * LANES:(p + 1) * LANES, cols]
            om_ref[0, p, c] = om_t[p * LANES:(p + 1) * LANES, cols]
        for hd in range(vm_t.shape[0] // HEAD_DIM):
            vm_ref[0, hd, c] = jnp.concatenate(
                [vm_t[hd * HEAD_DIM:(hd + 1) * HEAD_DIM, cols], ones_rows_c], axis=0).astype(BF16)
    z = proj(_Z)
    g_ref[0] = (z * _sigmoid(z)).astype(BF16)
    gm_ref[0] = proj(_GM)


def _in_projection(xa, modsel, w, rope_a, rope_b):
    b, s, d = xa.shape
    tm = ROW_TILE
    n_pairs = (_QB[1] - _QB[0]) // LANES
    n_heads = (_VB[1] - _VB[0]) // HEAD_DIM
    L = MLSTM_CHUNK
    widths = [512, 128, 128, (n_pairs, LANES, tm), 384, (n_heads, VB_ROWS, tm),
              (n_pairs, LANES, L), 384, (n_heads, VB_ROWS, L), (n_pairs, LANES, L), 1024]
    tposed = lambda g, r, tt: pl.BlockSpec((1, g, tm // tt, r, tt), lambda bi, i: (bi, 0, i, 0, 0))
    row = lambda n: tposed(*n) if isinstance(n, tuple) else pl.BlockSpec((1, tm, n), lambda bi, i: (bi, i, 0))
    out_shape = [jax.ShapeDtypeStruct((b, n[0], s // n[2], n[1], n[2]) if isinstance(n, tuple) else (b, s, n), BF16)
                 for n in widths]
    out_shape.append(jax.ShapeDtypeStruct((b, s, LANES), F32))
    return pl.pallas_call(
        _inproj_kernel,
        grid=(b, s // tm),
        in_specs=[
            row(d),
            pl.BlockSpec((1, 1, 3, d), lambda bi, i: (bi, jnp.minimum(i, 1), 0, 0)),
            pl.BlockSpec((d, _W_COLS), lambda bi, i: (0, 0)),
            pl.BlockSpec((3, tm, LANES), lambda bi, i: (0, i, 0)),
            pl.BlockSpec((3, tm, LANES), lambda bi, i: (0, i, 0)),
        ],
        out_specs=[row(n) for n in widths] + [row(LANES)],
        out_shape=out_shape,
        compiler_params=_params(("arbitrary", "arbitrary")),
        name="in_projection",
    )(xa, modsel, w, rope_a, rope_b)


def _attn_a_kernel(sink_ref, q_ref, k_ref, v_ref, o_ref, *, n_ctx, n_blocks, block0):
    w = WINDOW
    s_len = n_blocks * w
    half = LANES // 2
    rows1 = lax.broadcasted_iota(jnp.int32, (4 * w, 1), 0)
    sink = jnp.where(rows1 < w, sink_ref[0],
                     jnp.where(rows1 < 2 * w, sink_ref[1], jnp.where(rows1 < 3 * w, sink_ref[2], sink_ref[3])))
    row = lax.broadcasted_iota(jnp.int32, (4 * w, 3 * w), 0) & (w - 1)
    col = lax.broadcasted_iota(jnp.int32, (4 * w, 3 * w), 1)
    lane = lax.broadcasted_iota(jnp.int32, (w, LANES), 1)
    k_ctx = k_ref[0, 0:n_ctx, :]
    v_ctx = v_ref[0, 0:n_ctx, :]

    def body(i, carry):
        r0 = pl.multiple_of(i * w, w)
        q = q_ref[0, pl.ds(r0, w), :]
        qs = jnp.concatenate([q[:, j * LANES:(j + 1) * LANES] for j in range(4)], axis=0)
        start = pl.multiple_of(jnp.clip(r0 - w, 0, s_len - 3 * w), w)
        s_loc = _dot_nt(qs, k_ref[0, pl.ds(start, 3 * w), :])
        s_ctx = _dot_nt(qs, k_ctx)
        kpos = col + start
        rel = kpos - r0 - row
        ok = (jnp.abs(rel) <= w) & (kpos >= jnp.where(r0 >= n_ctx, n_ctx, s_len))
        s_loc = jnp.where(ok, s_loc, -jnp.inf)
        m = jnp.maximum(jnp.maximum(jnp.max(s_loc, axis=1, keepdims=True),
                                    jnp.max(s_ctx, axis=1, keepdims=True)), sink)
        p_loc = jnp.exp(s_loc - m)
        p_ctx = jnp.exp(s_ctx - m)
        den = (jnp.sum(p_loc, axis=1, keepdims=True) + jnp.sum(p_ctx, axis=1, keepdims=True)
               + jnp.exp(sink - m))
        o = (_dot(p_loc.astype(BF16), v_ref[0, pl.ds(start, 3 * w), :])
             + _dot(p_ctx.astype(BF16), v_ctx)) / den
        left = jnp.where(lane < half, o[0:w], pltpu.roll(o[w:2 * w], half, 1))
        right = jnp.where(lane < half, pltpu.roll(o[2 * w:3 * w], half, 1), o[3 * w:])
        o_ref[0, pl.ds(r0, w), :] = jnp.concatenate([left, right], axis=1)
        return carry

    lax.fori_loop(block0, n_blocks, body, 0, unroll=2)


def _attn_a(sink, qa, ka, va, n_ctx, with_ctx):
    b, s, _ = qa.shape
    seq = lambda n: pl.BlockSpec((1, s, n), lambda bi: (bi, 0, 0))
    return pl.pallas_call(
        functools.partial(_attn_a_kernel, n_ctx=n_ctx, n_blocks=s // WINDOW,
                          block0=0 if with_ctx else n_ctx // WINDOW),
        grid=(b,),
        in_specs=[pl.BlockSpec(memory_space=pltpu.SMEM), seq(4 * LANES), seq(LANES), seq(LANES)],
        out_specs=seq(2 * LANES),
        out_shape=jax.ShapeDtypeStruct((b, s, 2 * LANES), F32),
        compiler_params=_params(("arbitrary",)),
        name="window_attention",
    )(sink, qa, ka, va)


def _attn_b_kernel(li_ref, lam_ref, ng_ref, qt_ref, k_ref, vt_ref, *rest, nt, tq):
    o_ref, qs_scr, sa_scr, sb_scr, acc_scr = rest[-5:]
    tk = vt_ref.shape[-1]
    hv = HEAD_DIM
    qt = qt_ref[0, 0, 0]
    unit = lax.broadcasted_iota(jnp.int32, qt.shape, 0) >> 5
    zero = jnp.zeros_like(qt)
    qs_scr[...] = jnp.concatenate([jnp.where(unit == u, qt, zero) for u in range(4)], axis=1)
    acc_scr[...] = jnp.zeros_like(acc_scr)

    def fetch(t, s_scr):
        off = pl.multiple_of(t * tk, tk)
        s_new = _dot(k_ref[0, pl.ds(off, tk), :], qs_scr[...])
        s_scr[...] = s_new
        return jnp.max(s_new, axis=0, keepdims=True)

    def consume(s_scr, mt, m, t):
        m_new = jnp.maximum(m, mt)
        alpha = jnp.exp2(m - m_new)
        p = jnp.exp2(s_scr[...] - m_new).astype(BF16)
        for hd in range(2):
            cols = slice(hd * 2 * tq, (hd + 1) * 2 * tq)
            acc_scr[hd] = alpha[:, cols] * acc_scr[hd] + _dot(vt_ref[0, hd, t], p[:, cols])
        return m_new

    def body(tt, carry):
        m, mt_a = carry
        t0 = 2 * tt
        mt_b = fetch(t0 + 1, sb_scr)
        m = consume(sa_scr, mt_a, m, t0)
        mt_a = fetch(t0 + 2, sa_scr)
        m = consume(sb_scr, mt_b, m, t0 + 1)
        return m, mt_a

    m = jnp.full((1, 4 * tq), -jnp.inf, F32)
    mt_a = fetch(0, sa_scr)
    m, mt_a = lax.fori_loop(0, (nt - 1) // 2, body, (m, mt_a))
    if nt % 2 == 0:
        mt_b = fetch(nt - 1, sb_scr)
        m = consume(sa_scr, mt_a, m, nt - 2)
        consume(sb_scr, mt_b, m, nt - 1)
    else:
        consume(sa_scr, mt_a, m, nt - 1)

    lam_init = li_ref[0]
    lv = lam_ref[...]
    lam = (jnp.exp(jnp.sum(lv[0:1] * lv[1:2], axis=1, keepdims=True))
           - jnp.exp(jnp.sum(lv[2:3] * lv[3:4], axis=1, keepdims=True)) + lam_init)

    def head_out(hd):
        acc = acc_scr[hd]
        o = acc[0:hv] * (1.0 / acc[hv:hv + 1])
        od = o[:, 0:tq] - lam * o[:, tq:]
        ms = jnp.mean(od * od, axis=0, keepdims=True)
        return od * lax.rsqrt(ms + LN_EPS) * ng_ref[...]

    y_t = jnp.concatenate([head_out(0), head_out(1)], axis=0) * (1.0 - lam_init)
    o_ref[0] = y_t.T


def _attn_b_call(lam_init, lam_vec, norm_g, qbt, kb, vbt, out_buf, q0, nq, nt):
    b, pairs, n_tiles, _, tq = qbt.shape
    vrows, tk = vbt.shape[-2:]
    s = n_tiles * tq
    in_specs = [
        pl.BlockSpec(memory_space=pltpu.SMEM),
        pl.BlockSpec((4, B_QK_DIM), lambda bi, p, i: (0, 0)),
        pl.BlockSpec((HEAD_DIM, 1), lambda bi, p, i: (0, 0)),
        pl.BlockSpec((1, 1, 1, LANES, tq), lambda bi, p, i: (bi, p, i + q0, 0, 0)),
        pl.BlockSpec((1, nt * tk, LANES), lambda bi, p, i: (bi, 0, p)),
        pl.BlockSpec((1, 2, nt, vrows, tk), lambda bi, p, i: (bi, p, 0, 0, 0)),
    ]
    args = [lam_init, lam_vec, norm_g, qbt, kb, vbt]
    aliases = {}
    if out_buf is not None:
        in_specs.append(pl.BlockSpec(memory_space=pl.ANY))
        args.append(out_buf)
        aliases = {len(args) - 1: 0}
    return pl.pallas_call(
        functools.partial(_attn_b_kernel, nt=nt, tq=tq),
        grid=(b, pairs, nq),
        in_specs=in_specs,
        out_specs=pl.BlockSpec((1, tq, LANES), lambda bi, p, i: (bi, i + q0, p)),
        out_shape=jax.ShapeDtypeStruct((b, s, pairs * LANES), F32),
        scratch_shapes=[pltpu.VMEM((LANES, 4 * tq), BF16), pltpu.VMEM((tk, 4 * tq), F32),
                        pltpu.VMEM((tk, 4 * tq), F32), pltpu.VMEM((2, vrows, 2 * tq), F32)],
        input_output_aliases=aliases,
        compiler_params=_params(("arbitrary", "arbitrary", "arbitrary")),
        name="diff_attention",
    )(*args)


def _attn_b(lam_init, lam_vec, norm_g, qbt, kb, vbt, n_ctx, with_ctx):
    n_tiles, tq = qbt.shape[2], qbt.shape[4]
    tk = vbt.shape[-1]
    yb = _attn_b_call(lam_init, lam_vec, norm_g, qbt, kb, vbt, None, n_ctx // tq, n_tiles - n_ctx // tq, n_tiles)
    if with_ctx:
        yb = _attn_b_call(lam_init, lam_vec, norm_g, qbt, kb, vbt, yb, 0, n_ctx // tq, n_ctx // tk)
    return yb


def _log_sigmoid(x):
    return jnp.minimum(x, 0.0) - jnp.log(1.0 + jnp.exp(-jnp.abs(x)))


def _mlstm_kernel(bir_ref, bic_ref, bfr_ref, bfc_ref, ng_ref, qt_ref, k_ref, vt_ref, gic_ref, gfc_ref, gir_ref,
                  gfr_ref, ogt_ref,
                  o_ref, hf_scr, hb_scr, st_scr, ir_scr, br_scr, gc_scr, *, n_ctx_chunks, n_chunks):
    L = MLSTM_CHUNK
    hv = HEAD_DIM
    r_i = lax.broadcasted_iota(jnp.int32, (L, L), 0)
    c_i = lax.broadcasted_iota(jnp.int32, (L, L), 1)
    upper = r_i <= c_i
    lower = r_i >= c_i
    t_up = jnp.where(upper, 1.0, 0.0).astype(BF16)
    t_low = jnp.where(lower, 1.0, 0.0).astype(BF16)
    row_q = lax.broadcasted_iota(jnp.int32, (LANES, L), 0)
    sel = [jnp.where(row_q < hv, 1.0, 0.0).astype(BF16), jnp.where(row_q < hv, 0.0, 1.0).astype(BF16)]
    fwd_r = lax.broadcasted_iota(jnp.int32, (8, L), 0) < 2
    fwd_c = lax.broadcasted_iota(jnp.int32, (L, 8), 1) < 2
    st_scr[...] = jnp.zeros_like(st_scr)

    def gates(c, carry):
        r0 = pl.multiple_of(c * L, L)
        frow = _log_sigmoid(gfr_ref[0, 0, c] + bfr_ref[0])
        hi, lo = _split_bf16(frow)
        ir_scr[c] = gir_ref[0, 0, c] + bir_ref[0]
        br_scr[c] = jnp.where(fwd_r, _dot(hi, t_up) + _dot(lo, t_up), _dot(hi, t_low) + _dot(lo, t_low))
        fcol = _log_sigmoid(gfc_ref[0, 0, pl.ds(r0, L), :] + bfc_ref[0])
        hi, lo = _split_bf16(fcol)
        bcol = jnp.where(fwd_c, _dot(t_low, hi) + _dot(t_low, lo), _dot(t_up, hi) + _dot(t_up, lo))
        gc_scr[c] = gic_ref[0, 0, pl.ds(r0, L), :] + bic_ref[0] - bcol
        return carry
    lax.fori_loop(0, n_chunks, gates, 0, unroll=2)

    def chunk_step(direction, c, m_prev):
        r0 = pl.multiple_of(c * L, L)
        kc = k_ref[0, pl.ds(r0, L), :]
        qtc = qt_ref[0, 0, c]
        irow8, brow8, gcol8 = ir_scr[c], br_scr[c], gc_scr[c]
        tri = upper if direction == 0 else lower
        h_scr = hf_scr if direction == 0 else hb_scr
        m_news = []
        for a in range(2):
            ci = 2 * direction + a
            b_row, i_row, g_col = brow8[ci:ci + 1], irow8[ci:ci + 1], gcol8[:, ci:ci + 1]
            qtm = qtc * sel[a]
            s_t = _dot(kc, qtm)
            log_d = jnp.where(tri, g_col + b_row, -jnp.inf)
            m_in = b_row + m_prev[a]
            m_t = jnp.maximum(m_in, jnp.max(log_d, axis=0, keepdims=True))
            w_in = jnp.exp(m_in - m_t)
            sd = (s_t * jnp.exp(log_d - m_t)).astype(BF16)
            qw = (qtm.astype(F32) * w_in).astype(BF16)
            vt = vt_ref[0, a, c]
            state = st_scr[ci]
            numden = _dot(state.astype(BF16), qw) + _dot(vt, sd)
            h_scr[c, a * hv:(a + 1) * hv, :] = (
                numden[0:hv] / jnp.maximum(jnp.abs(numden[hv:hv + 1]), jnp.exp(-m_t)))
            b_last = jnp.min(b_row, axis=1, keepdims=True)
            log_w = b_last - b_row + i_row
            m_new = jnp.maximum(b_last + m_prev[a], jnp.max(log_w, axis=1, keepdims=True))
            vw = (vt.astype(F32) * jnp.exp(log_w - m_new)).astype(BF16)
            st_scr[ci] = jnp.exp(b_last + m_prev[a] - m_new) * state + _dot(vw, kc)
            m_news.append(m_new)
        return tuple(m_news)

    def body(j, carry):
        mf, mb = carry
        cb = jnp.where(j < n_ctx_chunks, n_ctx_chunks - 1 - j, n_chunks - 1 + n_ctx_chunks - j)
        return chunk_step(0, j, mf), chunk_step(1, cb, mb)

    z11 = jnp.zeros((1, 1), F32)
    lax.fori_loop(0, n_chunks, body, ((z11, z11), (z11, z11)), unroll=2)

    def finish(c, carry):
        r0 = pl.multiple_of(c * L, L)
        h = hf_scr[c] + hb_scr[c]
        outs = []
        for a in range(2):
            ha = h[a * hv:(a + 1) * hv]
            outs.append(ha * lax.rsqrt(jnp.mean(ha * ha, axis=0, keepdims=True) + LN_EPS))
        y_t = jnp.concatenate(outs, axis=0) * ng_ref[0] * ogt_ref[0, 0, c].astype(F32)
        o_ref[0, pl.ds(r0, L), :] = y_t.T
        return carry
    lax.fori_loop(0, n_chunks, finish, 0, unroll=2)


def _mlstm(bias_i, bias_f, norm_g, qmt, km, vmt, gi, gf, ogt, n_ctx):
    b, pairs, nc, _, L = qmt.shape
    s = nc * L
    vrows = vmt.shape[3]
    col = lambda a: jnp.swapaxes(a.reshape(b, pairs, nc, L, 8), 3, 4)
    seq = pl.BlockSpec((1, s, LANES), lambda bi, p: (bi, 0, p))
    pair_tiles = pl.BlockSpec((1, 1, nc, LANES, L), lambda bi, p: (bi, p, 0, 0, 0))
    gate_c = pl.BlockSpec((1, 1, s, 8), lambda bi, p: (bi, p, 0, 0))
    gate_r = pl.BlockSpec((1, 1, nc, 8, L), lambda bi, p: (bi, p, 0, 0, 0))
    bias_r = pl.BlockSpec((1, 8, 1), lambda bi, p: (p, 0, 0))
    bias_c = pl.BlockSpec((1, 1, 8), lambda bi, p: (p, 0, 0))
    return pl.pallas_call(
        functools.partial(_mlstm_kernel, n_ctx_chunks=n_ctx // L, n_chunks=nc),
        grid=(b, pairs),
        in_specs=[
            bias_r, bias_c, bias_r, bias_c,
            pl.BlockSpec((1, LANES, 1), lambda bi, p: (p, 0, 0)),
            pair_tiles, seq,
            pl.BlockSpec((1, 2, nc, vrows, L), lambda bi, p: (bi, p, 0, 0, 0)),
            gate_c, gate_c, gate_r, gate_r,
            pair_tiles,
        ],
        out_specs=seq,
        out_shape=jax.ShapeDtypeStruct((b, s, pairs * LANES), F32),
        scratch_shapes=[pltpu.VMEM((nc, LANES, L), F32), pltpu.VMEM((nc, LANES, L), F32),
                        pltpu.VMEM((4, vrows, LANES), F32),
                        pltpu.VMEM((nc, 8, L), F32), pltpu.VMEM((nc, 8, L), F32), pltpu.VMEM((nc, L, 8), F32)],
        compiler_params=_params(("arbitrary", "arbitrary")),
        name="mlstm",
    )(bias_i[:, :, None], bias_i[:, None, :], bias_f[:, :, None], bias_f[:, None, :], norm_g[:, :, None],
      qmt, km, vmt, gi, gf, col(gi), col(gf), ogt)


def _out_kernel(x_ref, mod_ref, ya_ref, yb_ref, ym_ref, g_ref, w_ref, lng_ref, lnb_ref, o_ref, *, alpha):
    x = x_ref[0]
    gate = mod_ref[0, 0, 2:3, :]
    g = g_ref[0].astype(F32)
    na, nb_ = ya_ref.shape[2], yb_ref.shape[2]
    mix_a = (ya_ref[0] * g[:, :na]).astype(BF16)
    mix_b = (yb_ref[0] * g[:, na:na + nb_]).astype(BF16)
    mix_m = (ym_ref[0] * g[:, na + nb_:]).astype(BF16)
    y = (_dot(mix_a, w_ref[0:na, :]) + _dot(mix_b, w_ref[na:na + nb_, :])
         + _dot(mix_m, w_ref[na + nb_:, :]))
    r = alpha * x + gate * y
    mu = jnp.mean(r, axis=1, keepdims=True)
    d = r - mu
    var = jnp.mean(d * d, axis=1, keepdims=True)
    o_ref[0] = d * lax.rsqrt(var + LN_EPS) * lng_ref[...] + lnb_ref[...]


def _out_projection(xa, modsel, ya, yb, ym, g, w, ln_g, ln_b, alpha, row0):
    b, s, d = xa.shape
    tm = ROW_TILE
    row = lambda n: pl.BlockSpec((1, tm, n), lambda bi, i: (bi, i + row0, 0))
    vec = pl.BlockSpec((1, d), lambda bi, i: (0, 0))
    return pl.pallas_call(
        functools.partial(_out_kernel, alpha=alpha),
        grid=(b, s // tm - row0),
        in_specs=[
            row(d),
            pl.BlockSpec((1, 1, 3, d), lambda bi, i: (bi, jnp.minimum(i + row0, 1), 0, 0)),
            row(ya.shape[2]), row(yb.shape[2]), row(ym.shape[2]), row(d),
            pl.BlockSpec(w.shape, lambda bi, i: (0, 0)),
            vec, vec,
        ],
        out_specs=pl.BlockSpec((1, tm, d), lambda bi, i: (bi, i, 0)),
        out_shape=jax.ShapeDtypeStruct((b, s - row0 * tm, d), F32),
        compiler_params=_params(("arbitrary", "arbitrary")),
        name="out_projection",
    )(xa, modsel, ya, yb, ym, g, w, ln_g, ln_b)


def _in_weight_columns(d_in):
    z = d_in
    src = {}
    off = 0
    for name, n in (("qa", 256), ("ka", 128), ("va", 128), ("za", 256), ("qb", 384), ("kb", 384), ("vb", 384),
                    ("zb", 384), ("qm", 384), ("km", 384), ("vm", 384), ("om", 384), ("zm", 384), ("gm", 24)):
        src[name] = np.arange(off, off + n)
        off += n
    assert off == d_in
    cols = []
    for hd in range(4):
        blk = np.full(LANES, z)
        gq = hd // 2
        blk[64 * gq:64 * gq + 64] = src["qa"][64 * hd:64 * hd + 64]
        cols.append(blk)
    for name in ("ka", "va", "qb", "kb", "vb", "qm", "km", "vm", "om", "za", "zb", "zm"):
        cols.append(src[name])
    cols.append(np.concatenate([src["gm"], np.full(LANES - 24, z)]))
    cols = np.concatenate(cols)
    assert cols.shape[0] == _W_COLS
    return cols


def _rope_table(n_tokens, n_ctx, dim):
    rows = n_tokens // GRID_W
    rowp = jnp.broadcast_to(jnp.arange(rows, dtype=F32)[:, None], (rows, GRID_W)).reshape(-1)
    colp = jnp.broadcast_to(jnp.arange(GRID_W, dtype=F32)[None, :], (rows, GRID_W)).reshape(-1)
    n_freq = dim // 4
    inv = ROPE_BASE ** (-jnp.arange(n_freq, dtype=F32) / n_freq)
    ar = rowp[:, None] * inv
    ac = colp[:, None] * inv
    ang = jnp.concatenate([ar, ar, ac, ac], -1)
    cos, sin = jnp.cos(ang), jnp.sin(ang)
    odd = (jnp.arange(dim) // n_freq) % 2 == 1
    sin_p = jnp.where(odd, sin, 0.0)
    sin_m = jnp.where(odd, 0.0, -sin)
    tab = jnp.stack([cos, sin_p, sin_m])
    ident = jnp.stack([jnp.ones((n_ctx, dim), F32), jnp.zeros((n_ctx, dim), F32), jnp.zeros((n_ctx, dim), F32)])
    tab = jnp.concatenate([ident, tab], axis=1)
    return jnp.tile(tab, (1, 1, LANES // dim))


def kernel(x, c, ctx, c_ctx, w_mod, b_mod, w_in, attn_sink, diff_lambda, diff_norm_g, mlstm_i_bias,
           mlstm_f_bias, mlstm_norm_g, w_out, ln_g, ln_b):
    b, t, d = x.shape
    n_ctx = ctx.shape[1]
    depth = w_mod.shape[0]
    d_in = w_in.shape[2]
    s = n_ctx + t
    L = MLSTM_CHUNK
    nc = s // L
    alpha = (2 * depth) ** 0.25

    xa = jnp.concatenate([ctx, x], axis=1)
    rope_a = _rope_table(t, n_ctx, HEAD_DIM)
    rope_b = _rope_table(t, n_ctx, B_QK_DIM)

    rows = -(-(b + 1) // 8) * 8
    cc = jnp.concatenate([c, c_ctx[None, :], jnp.zeros((rows - b - 1, d), F32)], axis=0)
    mod = _modulation(cc, w_mod, b_mod).reshape(depth, rows, 3, d)

    cols = _in_weight_columns(d_in)
    w_in_p = jnp.take(jnp.concatenate([w_in, jnp.zeros((depth, d, 1), F32)], axis=2), cols, axis=2).astype(BF16)
    w_out_b = w_out.astype(BF16)
    norm_b = diff_norm_g[:, :, None]

    n_pairs = mlstm_i_bias.shape[2] // 2
    heads = mlstm_i_bias.shape[2]
    chains = [(di, a) for di in range(2) for a in range(2)]
    zero_col = LANES - 1
    gate_cols = [np.array([[di * 2 * heads + gi * heads + 2 * p + a for di, a in chains] + [zero_col] * 4
                           for p in range(n_pairs)]) for gi in range(2)]
    bias_idx = np.array([[di * heads + 2 * p + a for di, a in chains] for p in range(n_pairs)])
    pad8 = lambda v: jnp.pad(v.reshape(-1)[bias_idx], ((0, 0), (0, 4)))

    for l in range(depth):
        with_ctx = l < depth - 1
        lam_init = 0.8 - 0.6 * math.exp(-0.3 * l)
        modsel = jnp.stack([jnp.broadcast_to(mod[l, b], (b, 3, d)), mod[l, :b]], axis=1)
        (qa, ka, va, qb, kb, vb, qm, km, vm, og, g, gm) = _in_projection(xa, modsel, w_in_p[l], rope_a, rope_b)

        ya = _attn_a(attn_sink[l], qa, ka, va, n_ctx, with_ctx)
        yb = _attn_b(jnp.full((1,), lam_init, F32), diff_lambda[l], norm_b[l], qb, kb, vb, n_ctx, with_ctx)

        gi, gf = (jnp.transpose(gm[:, :, cols], (0, 2, 1, 3)) for cols in gate_cols)
        ym = _mlstm(pad8(mlstm_i_bias[l]), pad8(mlstm_f_bias[l]), mlstm_norm_g[l].reshape(n_pairs, LANES),
                    qm, km, vm, gi, gf, og, n_ctx)

        xa = _out_projection(xa, modsel, ya, yb, ym, g, w_out_b[l], ln_g[l][None, :], ln_b[l][None, :], alpha,
                             0 if with_ctx else n_ctx // ROW_TILE)
    return xa
```

```python
import functools
import math

import numpy as np
import jax
import jax.numpy as jnp
from jax import lax
from jax.experimental import pallas as pl
from jax.experimental.pallas import tpu as pltpu

F32 = jnp.float32
BF16 = jnp.bfloat16

LANES = 128
GRID_W = 64
HEAD_DIM = 64
ROPE_BASE = 10000.0
LN_EPS = 1e-5
WINDOW = 128
B_QK_DIM = 32
LOG2E = math.log2(math.e)
VB_ROWS = HEAD_DIM + 16
MLSTM_CHUNK = 128
ROW_TILE = 256
VMEM_LIMIT = 56 * 1024 * 1024


def _dot(a, b):
    return jnp.dot(a, b, preferred_element_type=F32)


def _dot_nt(a, b):
    return lax.dot_general(a, b, (((1,), (1,)), ((), ())), preferred_element_type=F32)


def _split_bf16(a):
    hi = a.astype(BF16)
    lo = (a - hi.astype(F32)).astype(BF16)
    return hi, lo


def _sigmoid(x):
    return 1.0 / (1.0 + jnp.exp(-x))


def _params(sem):
    return pltpu.CompilerParams(dimension_semantics=sem, vmem_limit_bytes=VMEM_LIMIT)


def _mod_kernel(c_ref, w_ref, b_ref, o_ref):
    c = c_ref[...]
    a = c * _sigmoid(c)
    a_hi, a_lo = _split_bf16(a)
    w_hi, w_lo = _split_bf16(w_ref[0])
    o_ref[0] = _dot(a_hi, w_hi) + _dot(a_lo, w_hi) + _dot(a_hi, w_lo) + b_ref[0]


def _modulation(cc, w_mod, b_mod):
    depth, d, n = w_mod.shape
    r = cc.shape[0]
    tn = 1024
    return pl.pallas_call(
        _mod_kernel,
        grid=(depth, n // tn),
        in_specs=[
            pl.BlockSpec((r, d), lambda l, j: (0, 0)),
            pl.BlockSpec((1, d, tn), lambda l, j: (l, 0, j)),
            pl.BlockSpec((1, 1, tn), lambda l, j: (l, 0, j)),
        ],
        out_specs=pl.BlockSpec((1, r, tn), lambda l, j: (l, 0, j)),
        out_shape=jax.ShapeDtypeStruct((depth, r, n), F32),
        compiler_params=_params(("arbitrary", "arbitrary")),
        name="modulation",
    )(cc, w_mod, b_mod.reshape(depth, 1, n))


_QA, _KA, _VA = (0, 512), (512, 640), (640, 768)
_QB, _KB, _VB = (768, 1152), (1152, 1536), (1536, 1920)
_QM, _KM, _VM = (1920, 2304), (2304, 2688), (2688, 3072)
_OM, _Z, _GM = (3072, 3456), (3456, 4480), (4480, 4608)
_W_COLS = 4608


def _rope(t, tab_ref, quarter):
    cos, sin_p, sin_m = tab_ref[0], tab_ref[1], tab_ref[2]
    outs = []
    for j in range(t.shape[1] // LANES):
        tj = t[:, j * LANES:(j + 1) * LANES]
        outs.append(tj * cos + pltpu.roll(tj, quarter, 1) * sin_p + pltpu.roll(tj, LANES - quarter, 1) * sin_m)
    return outs[0] if len(outs) == 1 else jnp.concatenate(outs, axis=1)


def _inproj_kernel(x_ref, mod_ref, w_ref, ra_ref, rb_ref,
                   qa_ref, ka_ref, va_ref, qb_ref, kb_ref, vb_ref,
                   qm_ref, km_ref, vm_ref, om_ref, g_ref, gm_ref):
    x = x_ref[0]
    shift = mod_ref[0, 0, 0:1, :]
    scale = mod_ref[0, 0, 1:2, :]
    h = (x * (1.0 + scale) + shift).astype(BF16)

    def proj(cols):
        return _dot(h, w_ref[:, cols[0]:cols[1]])

    qa_ref[0] = (_rope(proj(_QA), ra_ref, HEAD_DIM // 4) * (HEAD_DIM ** -0.5)).astype(BF16)
    ka_ref[0] = _rope(proj(_KA), ra_ref, HEAD_DIM // 4).astype(BF16)
    va_ref[0] = proj(_VA).astype(BF16)
    qb_t = (_rope(proj(_QB), rb_ref, B_QK_DIM // 4) * (B_QK_DIM ** -0.5 * LOG2E)).T.astype(BF16)
    for p in range(qb_t.shape[0] // LANES):
        qb_ref[0, p, 0] = qb_t[p * LANES:(p + 1) * LANES]
    vb_t = proj(_VB).T
    ones_rows = jnp.where(lax.broadcasted_iota(jnp.int32, (VB_ROWS - HEAD_DIM, vb_t.shape[1]), 0) == 0, 1.0, 0.0)
    for hd in range(vb_t.shape[0] // HEAD_DIM):
        vb_ref[0, hd, 0] = jnp.concatenate([vb_t[hd * HEAD_DIM:(hd + 1) * HEAD_DIM], ones_rows], axis=0).astype(BF16)
    kb_ref[0] = _rope(proj(_KB), rb_ref, B_QK_DIM // 4).astype(BF16)
    km_ref[0] = (proj(_KM) * (HEAD_DIM ** -0.5)).astype(BF16)
    qm_t = proj(_QM).T.astype(BF16)
    om_t = _sigmoid(proj(_OM)).T.astype(BF16)
    vm_t = proj(_VM).T
    L = MLSTM_CHUNK
    ones_rows_c = jnp.where(lax.broadcasted_iota(jnp.int32, (VB_ROWS - HEAD_DIM, L), 0) == 0, 1.0, 0.0)
    for c in range(qm_t.shape[1] // L):
        cols = slice(c * L, (c + 1) * L)
        for p in range(qm_t.shape[0] // LANES):
            qm_ref[0, p, c] = qm_t[p * LANES:(p + 1) * LANES, cols]
            om_ref[0, p, c] = om_t[p * LANES:(p + 1) * LANES, cols]
        for hd in range(vm_t.shape[0] // HEAD_DIM):
            vm_ref[0, hd, c] = jnp.concatenate(
                [vm_t[hd * HEAD_DIM:(hd + 1) * HEAD_DIM, cols], ones_rows_c], axis=0).astype(BF16)
    z = proj(_Z)
    g_ref[0] = (z * _sigmoid(z)).astype(BF16)
    gm_ref[0] = proj(_GM)


def _in_projection(xa, modsel, w, rope_a, rope_b):
    b, s, d = xa.shape
    tm = ROW_TILE
    n_pairs = (_QB[1] - _QB[0]) // LANES
    n_heads = (_VB[1] - _VB[0]) // HEAD_DIM
    L = MLSTM_CHUNK
    widths = [512, 128, 128, (n_pairs, LANES, tm), 384, (n_heads, VB_ROWS, tm),
              (n_pairs, LANES, L), 384, (n_heads, VB_ROWS, L), (n_pairs, LANES, L), 1024]
    tposed = lambda g, r, tt: pl.BlockSpec((1, g, tm // tt, r, tt), lambda bi, i: (bi, 0, i, 0, 0))
    row = lambda n: tposed(*n) if isinstance(n, tuple) else pl.BlockSpec((1, tm, n), lambda bi, i: (bi, i, 0))
    out_shape = [jax.ShapeDtypeStruct((b, n[0], s // n[2], n[1], n[2]) if isinstance(n, tuple) else (b, s, n), BF16)
                 for n in widths]
    out_shape.append(jax.ShapeDtypeStruct((b, s, LANES), F32))
    return pl.pallas_call(
        _inproj_kernel,
        grid=(b, s // tm),
        in_specs=[
            row(d),
            pl.BlockSpec((1, 1, 3, d), lambda bi, i: (bi, jnp.minimum(i, 1), 0, 0)),
            pl.BlockSpec((d, _W_COLS), lambda bi, i: (0, 0)),
            pl.BlockSpec((3, tm, LANES), lambda bi, i: (0, i, 0)),
            pl.BlockSpec((3, tm, LANES), lambda bi, i: (0, i, 0)),
        ],
        out_specs=[row(n) for n in widths] + [row(LANES)],
        out_shape=out_shape,
        compiler_params=_params(("arbitrary", "arbitrary")),
        name="in_projection",
    )(xa, modsel, w, rope_a, rope_b)


def _attn_a_kernel(sink_ref, q_ref, k_ref, v_ref, o_ref, *, n_ctx, n_blocks, block0):
    w = WINDOW
    s_len = n_blocks * w
    half = LANES // 2
    rows1 = lax.broadcasted_iota(jnp.int32, (4 * w, 1), 0)
    sink = jnp.where(rows1 < w, sink_ref[0],
                     jnp.where(rows1 < 2 * w, sink_ref[1], jnp.where(rows1 < 3 * w, sink_ref[2], sink_ref[3])))
    row = lax.broadcasted_iota(jnp.int32, (4 * w, 3 * w), 0) & (w - 1)
    col = lax.broadcasted_iota(jnp.int32, (4 * w, 3 * w), 1)
    lane = lax.broadcasted_iota(jnp.int32, (w, LANES), 1)
    k_ctx = k_ref[0, 0:n_ctx, :]
    v_ctx = v_ref[0, 0:n_ctx, :]

    def body(i, carry):
        r0 = pl.multiple_of(i * w, w)
        q = q_ref[0, pl.ds(r0, w), :]
        qs = jnp.concatenate([q[:, j * LANES:(j + 1) * LANES] for j in range(4)], axis=0)
        start = pl.multiple_of(jnp.clip(r0 - w, 0, s_len - 3 * w), w)
        s_loc = _dot_nt(qs, k_ref[0, pl.ds(start, 3 * w), :])
        s_ctx = _dot_nt(qs, k_ctx)
        kpos = col + start
        rel = kpos - r0 - row
        ok = (jnp.abs(rel) <= w) & (kpos >= jnp.where(r0 >= n_ctx, n_ctx, s_len))
        s_loc = jnp.where(ok, s_loc, -jnp.inf)
        m = jnp.maximum(jnp.maximum(jnp.max(s_loc, axis=1, keepdims=True),
                                    jnp.max(s_ctx, axis=1, keepdims=True)), sink)
        p_loc = jnp.exp(s_loc - m)
        p_ctx = jnp.exp(s_ctx - m)
        den = (jnp.sum(p_loc, axis=1, keepdims=True) + jnp.sum(p_ctx, axis=1, keepdims=True)
               + jnp.exp(sink - m))
        o = (_dot(p_loc.astype(BF16), v_ref[0, pl.ds(start, 3 * w), :])
             + _dot(p_ctx.astype(BF16), v_ctx)) / den
        left = jnp.where(lane < half, o[0:w], pltpu.roll(o[w:2 * w], half, 1))
        right = jnp.where(lane < half, pltpu.roll(o[2 * w:3 * w], half, 1), o[3 * w:])
        o_ref[0, pl.ds(r0, w), :] = jnp.concatenate([left, right], axis=1)
        return carry

    lax.fori_loop(block0, n_blocks, body, 0, unroll=2)


def _attn_a(sink, qa, ka, va, n_ctx, with_ctx):
    b, s, _ = qa.shape
    seq = lambda n: pl.BlockSpec((1, s, n), lambda bi: (bi, 0, 0))
    return pl.pallas_call(
        functools.partial(_attn_a_kernel, n_ctx=n_ctx, n_blocks=s // WINDOW,
                          block0=0 if with_ctx else n_ctx // WINDOW),
        grid=(b,),
        in_specs=[pl.BlockSpec(memory_space=pltpu.SMEM), seq(4 * LANES), seq(LANES), seq(LANES)],
        out_specs=seq(2 * LANES),
        out_shape=jax.ShapeDtypeStruct((b, s, 2 * LANES), F32),
        compiler_params=_params(("arbitrary",)),
        name="window_attention",
    )(sink, qa, ka, va)


def _attn_b_kernel(li_ref, lam_ref, ng_ref, qt_ref, *refs, nt, tq, chained):
    k_ref, vt_ref = refs[1:3] if chained else refs[0:2]
    o_ref, qs_scr, sa_scr, sb_scr, sc_scr, mt_scr, acc_scr = refs[-7:]
    i = pl.program_id(2)
    tk = vt_ref.shape[-1]
    hv = HEAD_DIM
    cur = i % 2 if chained else 0

    def mask_queries(q_ref, slot):
        qt = q_ref[0, 0, 0]
        unit = lax.broadcasted_iota(jnp.int32, qt.shape, 0) >> 5
        zero = jnp.zeros_like(qt)
        qs_scr[slot] = jnp.concatenate([jnp.where(unit == u, qt, zero) for u in range(4)], axis=1)

    def fetch(t, s_scr, slot):
        off = pl.multiple_of(t * tk, tk)
        s_new = _dot(k_ref[0, pl.ds(off, tk), :], qs_scr[slot])
        s_scr[...] = s_new
        return jnp.max(s_new, axis=0, keepdims=True)

    def consume(s_scr, mt, m, t):
        m_new = jnp.maximum(m, mt)
        alpha = jnp.exp2(m - m_new)
        p = jnp.exp2(s_scr[...] - m_new).astype(BF16)
        for hd in range(2):
            cols = slice(hd * 2 * tq, (hd + 1) * 2 * tq)
            acc_scr[hd] = alpha[:, cols] * acc_scr[hd] + _dot(vt_ref[0, hd, t], p[:, cols])
        return m_new

    def first_tile():
        mask_queries(qt_ref, cur)
        mt_scr[...] = fetch(0, sc_scr, cur)

    def successor_first_tile():
        if chained:
            mask_queries(refs[0], 1 - cur)
            mt_scr[...] = fetch(0, sc_scr, 1 - cur)

    if chained:
        pl.when(i == 0)(first_tile)
    else:
        first_tile()
    acc_scr[...] = jnp.zeros_like(acc_scr)
    m = jnp.full((1, 4 * tq), -jnp.inf, F32)
    mt_c = mt_scr[...]

    def body(tt, carry):
        m, mt_a = carry
        t0 = 2 * tt + 1
        mt_b = fetch(t0 + 1, sb_scr, cur)
        m = consume(sa_scr, mt_a, m, t0)
        mt_a = fetch(t0 + 2, sa_scr, cur)
        m = consume(sb_scr, mt_b, m, t0 + 1)
        return m, mt_a

    if nt == 1:
        successor_first_tile()
        consume(sc_scr, mt_c, m, 0)
    else:
        mt_a = fetch(1, sa_scr, cur)
        m = consume(sc_scr, mt_c, m, 0)
        m, mt_a = lax.fori_loop(0, (nt - 2) // 2, body, (m, mt_a))
        if nt % 2 == 1:
            mt_b = fetch(nt - 1, sb_scr, cur)
            m = consume(sa_scr, mt_a, m, nt - 2)
            successor_first_tile()
            consume(sb_scr, mt_b, m, nt - 1)
        else:
            successor_first_tile()
            consume(sa_scr, mt_a, m, nt - 1)

    lam_init = li_ref[0]
    lv = lam_ref[...]
    lam = (jnp.exp(jnp.sum(lv[0:1] * lv[1:2], axis=1, keepdims=True))
           - jnp.exp(jnp.sum(lv[2:3] * lv[3:4], axis=1, keepdims=True)) + lam_init)

    def head_out(hd):
        acc = acc_scr[hd]
        o = acc[0:hv] * (1.0 / acc[hv:hv + 1])
        od = o[:, 0:tq] - lam * o[:, tq:]
        ms = jnp.mean(od * od, axis=0, keepdims=True)
        return od * lax.rsqrt(ms + LN_EPS) * ng_ref[...]

    y_t = jnp.concatenate([head_out(0), head_out(1)], axis=0) * (1.0 - lam_init)
    o_ref[0] = y_t.T


def _attn_b_call(lam_init, lam_vec, norm_g, qbt, kb, vbt, out_buf, q0, nq, nt):
    b, pairs, n_tiles, _, tq = qbt.shape
    vrows, tk = vbt.shape[-2:]
    s = n_tiles * tq
    chained = nq > 1
    q_spec = lambda step: pl.BlockSpec((1, 1, 1, LANES, tq),
                                       lambda bi, p, i: (bi, p, jnp.minimum(i + step, nq - 1) + q0, 0, 0))
    in_specs = [
        pl.BlockSpec(memory_space=pltpu.SMEM),
        pl.BlockSpec((4, B_QK_DIM), lambda bi, p, i: (0, 0)),
        pl.BlockSpec((HEAD_DIM, 1), lambda bi, p, i: (0, 0)),
        q_spec(0),
    ]
    args = [lam_init, lam_vec, norm_g, qbt]
    if chained:
        in_specs.append(q_spec(1))
        args.append(qbt)
    in_specs += [pl.BlockSpec((1, nt * tk, LANES), lambda bi, p, i: (bi, 0, p)),
                 pl.BlockSpec((1, 2, nt, vrows, tk), lambda bi, p, i: (bi, p, 0, 0, 0))]
    args += [kb, vbt]
    aliases = {}
    if out_buf is not None:
        in_specs.append(pl.BlockSpec(memory_space=pl.ANY))
        args.append(out_buf)
        aliases = {len(args) - 1: 0}
    return pl.pallas_call(
        functools.partial(_attn_b_kernel, nt=nt, tq=tq, chained=chained),
        grid=(b, pairs, nq),
        in_specs=in_specs,
        out_specs=pl.BlockSpec((1, tq, LANES), lambda bi, p, i: (bi, i + q0, p)),
        out_shape=jax.ShapeDtypeStruct((b, s, pairs * LANES), F32),
        scratch_shapes=[pltpu.VMEM((2, LANES, 4 * tq), BF16), pltpu.VMEM((tk, 4 * tq), F32),
                        pltpu.VMEM((tk, 4 * tq), F32), pltpu.VMEM((tk, 4 * tq), F32),
                        pltpu.VMEM((1, 4 * tq), F32), pltpu.VMEM((2, vrows, 2 * tq), F32)],
        input_output_aliases=aliases,
        compiler_params=_params(("arbitrary", "arbitrary", "arbitrary")),
        name="diff_attention",
    )(*args)


def _attn_b(lam_init, lam_vec, norm_g, qbt, kb, vbt, n_ctx, with_ctx):
    n_tiles, tq = qbt.shape[2], qbt.shape[4]
    tk = vbt.shape[-1]
    yb = _attn_b_call(lam_init, lam_vec, norm_g, qbt, kb, vbt, None, n_ctx // tq, n_tiles - n_ctx // tq, n_tiles)
    if with_ctx:
        yb = _attn_b_call(lam_init, lam_vec, norm_g, qbt, kb, vbt, yb, 0, n_ctx // tq, n_ctx // tk)
    return yb


def _log_sigmoid(x):
    return jnp.minimum(x, 0.0) - jnp.log(1.0 + jnp.exp(-jnp.abs(x)))


def _mlstm_kernel(bir_ref, bic_ref, bfr_ref, bfc_ref, ng_ref, qt_ref, k_ref, vt_ref, gic_ref, gfc_ref, gir_ref,
                  gfr_ref, ogt_ref,
                  o_ref, hf_scr, hb_scr, st_scr, ir_scr, br_scr, gc_scr, *, n_ctx_chunks, n_chunks):
    L = MLSTM_CHUNK
    hv = HEAD_DIM
    r_i = lax.broadcasted_iota(jnp.int32, (L, L), 0)
    c_i = lax.broadcasted_iota(jnp.int32, (L, L), 1)
    upper = r_i <= c_i
    lower = r_i >= c_i
    t_up = jnp.where(upper, 1.0, 0.0).astype(BF16)
    t_low = jnp.where(lower, 1.0, 0.0).astype(BF16)
    row_q = lax.broadcasted_iota(jnp.int32, (LANES, L), 0)
    sel = [jnp.where(row_q < hv, 1.0, 0.0).astype(BF16), jnp.where(row_q < hv, 0.0, 1.0).astype(BF16)]
    fwd_r = lax.broadcasted_iota(jnp.int32, (8, L), 0) < 2
    fwd_c = lax.broadcasted_iota(jnp.int32, (L, 8), 1) < 2
    st_scr[...] = jnp.zeros_like(st_scr)

    def gates(c, carry):
        r0 = pl.multiple_of(c * L, L)
        frow = _log_sigmoid(gfr_ref[0, 0, c] + bfr_ref[0])
        hi, lo = _split_bf16(frow)
        ir_scr[c] = gir_ref[0, 0, c] + bir_ref[0]
        br_scr[c] = jnp.where(fwd_r, _dot(hi, t_up) + _dot(lo, t_up), _dot(hi, t_low) + _dot(lo, t_low))
        fcol = _log_sigmoid(gfc_ref[0, 0, pl.ds(r0, L), :] + bfc_ref[0])
        hi, lo = _split_bf16(fcol)
        bcol = jnp.where(fwd_c, _dot(t_low, hi) + _dot(t_low, lo), _dot(t_up, hi) + _dot(t_up, lo))
        gc_scr[c] = gic_ref[0, 0, pl.ds(r0, L), :] + bic_ref[0] - bcol
        return carry
    lax.fori_loop(0, n_chunks, gates, 0, unroll=2)

    def chunk_step(direction, c, m_prev):
        r0 = pl.multiple_of(c * L, L)
        kc = k_ref[0, pl.ds(r0, L), :]
        qtc = qt_ref[0, 0, c]
        irow8, brow8, gcol8 = ir_scr[c], br_scr[c], gc_scr[c]
        tri = upper if direction == 0 else lower
        h_scr = hf_scr if direction == 0 else hb_scr
        m_news = []
        for a in range(2):
            ci = 2 * direction + a
            b_row, i_row, g_col = brow8[ci:ci + 1], irow8[ci:ci + 1], gcol8[:, ci:ci + 1]
            qtm = qtc * sel[a]
            s_t = _dot(kc, qtm)
            log_d = jnp.where(tri, g_col + b_row, -jnp.inf)
            m_in = b_row + m_prev[a]
            m_t = jnp.maximum(m_in, jnp.max(log_d, axis=0, keepdims=True))
            w_in = jnp.exp(m_in - m_t)
            sd = (s_t * jnp.exp(log_d - m_t)).astype(BF16)
            qw = (qtm.astype(F32) * w_in).astype(BF16)
            vt = vt_ref[0, a, c]
            state = st_scr[ci]
            numden = _dot(state.astype(BF16), qw) + _dot(vt, sd)
            h_scr[c, a * hv:(a + 1) * hv, :] = (
                numden[0:hv] / jnp.maximum(jnp.abs(numden[hv:hv + 1]), jnp.exp(-m_t)))
            b_last = jnp.min(b_row, axis=1, keepdims=True)
            log_w = b_last - b_row + i_row
            m_new = jnp.maximum(b_last + m_prev[a], jnp.max(log_w, axis=1, keepdims=True))
            vw = (vt.astype(F32) * jnp.exp(log_w - m_new)).astype(BF16)
            st_scr[ci] = jnp.exp(b_last + m_prev[a] - m_new) * state + _dot(vw, kc)
            m_news.append(m_new)
        return tuple(m_news)

    def body(j, carry):
        mf, mb = carry
        cb = jnp.where(j < n_ctx_chunks, n_ctx_chunks - 1 - j, n_chunks - 1 + n_ctx_chunks - j)
        return chunk_step(0, j, mf), chunk_step(1, cb, mb)

    z11 = jnp.zeros((1, 1), F32)
    lax.fori_loop(0, n_chunks, body, ((z11, z11), (z11, z11)), unroll=2)

    def finish(c, carry):
        r0 = pl.multiple_of(c * L, L)
        h = hf_scr[c] + hb_scr[c]
        outs = []
        for a in range(2):
            ha = h[a * hv:(a + 1) * hv]
            outs.append(ha * lax.rsqrt(jnp.mean(ha * ha, axis=0, keepdims=True) + LN_EPS))
        y_t = jnp.concatenate(outs, axis=0) * ng_ref[0] * ogt_ref[0, 0, c].astype(F32)
        o_ref[0, pl.ds(r0, L), :] = y_t.T
        return carry
    lax.fori_loop(0, n_chunks, finish, 0, unroll=2)


def _mlstm(bias_i, bias_f, norm_g, qmt, km, vmt, gi, gf, ogt, n_ctx):
    b, pairs, nc, _, L = qmt.shape
    s = nc * L
    vrows = vmt.shape[3]
    col = lambda a: jnp.swapaxes(a.reshape(b, pairs, nc, L, 8), 3, 4)
    seq = pl.BlockSpec((1, s, LANES), lambda bi, p: (bi, 0, p))
    pair_tiles = pl.BlockSpec((1, 1, nc, LANES, L), lambda bi, p: (bi, p, 0, 0, 0))
    gate_c = pl.BlockSpec((1, 1, s, 8), lambda bi, p: (bi, p, 0, 0))
    gate_r = pl.BlockSpec((1, 1, nc, 8, L), lambda bi, p: (bi, p, 0, 0, 0))
    bias_r = pl.BlockSpec((1, 8, 1), lambda bi, p: (p, 0, 0))
    bias_c = pl.BlockSpec((1, 1, 8), lambda bi, p: (p, 0, 0))
    return pl.pallas_call(
        functools.partial(_mlstm_kernel, n_ctx_chunks=n_ctx // L, n_chunks=nc),
        grid=(b, pairs),
        in_specs=[
            bias_r, bias_c, bias_r, bias_c,
            pl.BlockSpec((1, LANES, 1), lambda bi, p: (p, 0, 0)),
            pair_tiles, seq,
            pl.BlockSpec((1, 2, nc, vrows, L), lambda bi, p: (bi, p, 0, 0, 0)),
            gate_c, gate_c, gate_r, gate_r,
            pair_tiles,
        ],
        out_specs=seq,
        out_shape=jax.ShapeDtypeStruct((b, s, pairs * LANES), F32),
        scratch_shapes=[pltpu.VMEM((nc, LANES, L), F32), pltpu.VMEM((nc, LANES, L), F32),
                        pltpu.VMEM((4, vrows, LANES), F32),
                        pltpu.VMEM((nc, 8, L), F32), pltpu.VMEM((nc, 8, L), F32), pltpu.VMEM((nc, L, 8), F32)],
        compiler_params=_params(("arbitrary", "arbitrary")),
        name="mlstm",
    )(bias_i[:, :, None], bias_i[:, None, :], bias_f[:, :, None], bias_f[:, None, :], norm_g[:, :, None],
      qmt, km, vmt, gi, gf, col(gi), col(gf), ogt)


def _out_kernel(x_ref, mod_ref, ya_ref, yb_ref, ym_ref, g_ref, w_ref, lng_ref, lnb_ref, o_ref, *, alpha):
    x = x_ref[0]
    gate = mod_ref[0, 0, 2:3, :]
    g = g_ref[0].astype(F32)
    na, nb_ = ya_ref.shape[2], yb_ref.shape[2]
    mix_a = (ya_ref[0] * g[:, :na]).astype(BF16)
    mix_b = (yb_ref[0] * g[:, na:na + nb_]).astype(BF16)
    mix_m = (ym_ref[0] * g[:, na + nb_:]).astype(BF16)
    y = (_dot(mix_a, w_ref[0:na, :]) + _dot(mix_b, w_ref[na:na + nb_, :])
         + _dot(mix_m, w_ref[na + nb_:, :]))
    r = alpha * x + gate * y
    mu = jnp.mean(r, axis=1, keepdims=True)
    d = r - mu
    var = jnp.mean(d * d, axis=1, keepdims=True)
    o_ref[0] = d * lax.rsqrt(var + LN_EPS) * lng_ref[...] + lnb_ref[...]


def _out_projection(xa, modsel, ya, yb, ym, g, w, ln_g, ln_b, alpha, row0):
    b, s, d = xa.shape
    tm = ROW_TILE
    row = lambda n: pl.BlockSpec((1, tm, n), lambda bi, i: (bi, i + row0, 0))
    vec = pl.BlockSpec((1, d), lambda bi, i: (0, 0))
    return pl.pallas_call(
        functools.partial(_out_kernel, alpha=alpha),
        grid=(b, s // tm - row0),
        in_specs=[
            row(d),
            pl.BlockSpec((1, 1, 3, d), lambda bi, i: (bi, jnp.minimum(i + row0, 1), 0, 0)),
            row(ya.shape[2]), row(yb.shape[2]), row(ym.shape[2]), row(d),
            pl.BlockSpec(w.shape, lambda bi, i: (0, 0)),
            vec, vec,
        ],
        out_specs=pl.BlockSpec((1, tm, d), lambda bi, i: (bi, i, 0)),
        out_shape=jax.ShapeDtypeStruct((b, s - row0 * tm, d), F32),
        compiler_params=_params(("arbitrary", "arbitrary")),
        name="out_projection",
    )(xa, modsel, ya, yb, ym, g, w, ln_g, ln_b)


def _in_weight_columns(d_in):
    z = d_in
    src = {}
    off = 0
    for name, n in (("qa", 256), ("ka", 128), ("va", 128), ("za", 256), ("qb", 384), ("kb", 384), ("vb", 384),
                    ("zb", 384), ("qm", 384), ("km", 384), ("vm", 384), ("om", 384), ("zm", 384), ("gm", 24)):
        src[name] = np.arange(off, off + n)
        off += n
    assert off == d_in
    cols = []
    for hd in range(4):
        blk = np.full(LANES, z)
        gq = hd // 2
        blk[64 * gq:64 * gq + 64] = src["qa"][64 * hd:64 * hd + 64]
        cols.append(blk)
    for name in ("ka", "va", "qb", "kb", "vb", "qm", "km", "vm", "om", "za", "zb", "zm"):
        cols.append(src[name])
    cols.append(np.concatenate([src["gm"], np.full(LANES - 24, z)]))
    cols = np.concatenate(cols)
    assert cols.shape[0] == _W_COLS
    return cols


def _rope_table(n_tokens, n_ctx, dim):
    f32 = np.float32
    rows = n_tokens // GRID_W
    rowp = np.broadcast_to(np.arange(rows, dtype=f32)[:, None], (rows, GRID_W)).reshape(-1)
    colp = np.broadcast_to(np.arange(GRID_W, dtype=f32)[None, :], (rows, GRID_W)).reshape(-1)
    n_freq = dim // 4
    inv = np.power(f32(ROPE_BASE), -np.arange(n_freq, dtype=f32) / f32(n_freq)).astype(f32)
    ar = rowp[:, None] * inv
    ac = colp[:, None] * inv
    ang = np.concatenate([ar, ar, ac, ac], -1).astype(f32)
    cos, sin = np.cos(ang), np.sin(ang)
    odd = (np.arange(dim) // n_freq) % 2 == 1
    sin_p = np.where(odd, sin, f32(0))
    sin_m = np.where(odd, f32(0), -sin)
    tab = np.stack([cos, sin_p, sin_m])
    ident = np.stack([np.ones((n_ctx, dim), f32), np.zeros((n_ctx, dim), f32), np.zeros((n_ctx, dim), f32)])
    tab = np.concatenate([ident, tab], axis=1)
    return jnp.asarray(np.tile(tab, (1, 1, LANES // dim)).astype(f32))


def kernel(x, c, ctx, c_ctx, w_mod, b_mod, w_in, attn_sink, diff_lambda, diff_norm_g, mlstm_i_bias,
           mlstm_f_bias, mlstm_norm_g, w_out, ln_g, ln_b):
    b, t, d = x.shape
    n_ctx = ctx.shape[1]
    depth = w_mod.shape[0]
    d_in = w_in.shape[2]
    alpha = (2 * depth) ** 0.25

    xa = jnp.concatenate([ctx, x], axis=1)
    rope_a = _rope_table(t, n_ctx, HEAD_DIM)
    rope_b = _rope_table(t, n_ctx, B_QK_DIM)

    rows = -(-(b + 1) // 8) * 8
    cc = jnp.concatenate([c, c_ctx[None, :], jnp.zeros((rows - b - 1, d), F32)], axis=0)
    mod = _modulation(cc, w_mod, b_mod).reshape(depth, rows, 3, d)

    cols = _in_weight_columns(d_in)
    w_in_p = jnp.take(jnp.concatenate([w_in, jnp.zeros((depth, d, 1), F32)], axis=2), cols, axis=2).astype(BF16)
    w_out_b = w_out.astype(BF16)
    norm_b = diff_norm_g[:, :, None]

    n_pairs = mlstm_i_bias.shape[2] // 2
    heads = mlstm_i_bias.shape[2]
    chains = [(di, a) for di in range(2) for a in range(2)]
    zero_col = LANES - 1
    gate_cols = [np.array([[di * 2 * heads + gi * heads + 2 * p + a for di, a in chains] + [zero_col] * 4
                           for p in range(n_pairs)]) for gi in range(2)]
    bias_idx = np.array([[di * heads + 2 * p + a for di, a in chains] for p in range(n_pairs)])
    pad8 = lambda v: jnp.pad(v.reshape(-1)[bias_idx], ((0, 0), (0, 4)))

    for l in range(depth):
        with_ctx = l < depth - 1
        lam_init = 0.8 - 0.6 * math.exp(-0.3 * l)
        modsel = jnp.stack([jnp.broadcast_to(mod[l, b], (b, 3, d)), mod[l, :b]], axis=1)
        (qa, ka, va, qb, kb, vb, qm, km, vm, og, g, gm) = _in_projection(xa, modsel, w_in_p[l], rope_a, rope_b)

        ya = _attn_a(attn_sink[l], qa, ka, va, n_ctx, with_ctx)
        yb = _attn_b(jnp.full((1,), lam_init, F32), diff_lambda[l], norm_b[l], qb, kb, vb, n_ctx, with_ctx)

        gi, gf = (jnp.transpose(gm[:, :, cols], (0, 2, 1, 3)) for cols in gate_cols)
        ym = _mlstm(pad8(mlstm_i_bias[l]), pad8(mlstm_f_bias[l]), mlstm_norm_g[l].reshape(n_pairs, LANES),
                    qm, km, vm, gi, gf, og, n_ctx)

        xa = _out_projection(xa, modsel, ya, yb, ym, g, w_out_b[l], ln_g[l][None, :], ln_b[l][None, :], alpha,
                             0 if with_ctx else n_ctx // ROW_TILE)
    return xa
```

```python
import functools
import math

import numpy as np
import jax
import jax.numpy as jnp
from jax import lax
from jax.experimental import pallas as pl
from jax.experimental.pallas import tpu as pltpu

F32 = jnp.float32
BF16 = jnp.bfloat16

LANES = 128
GRID_W = 64
HEAD_DIM = 64
ROPE_BASE = 10000.0
LN_EPS = 1e-5
WINDOW = 128
B_QK_DIM = 32
LOG2E = math.log2(math.e)
VB_ROWS = HEAD_DIM + 16
MLSTM_CHUNK = 128
ROW_TILE = 256
VMEM_LIMIT = 56 * 1024 * 1024


def _dot(a, b):
    return jnp.dot(a, b, preferred_element_type=F32)


def _dot_nt(a, b):
    return lax.dot_general(a, b, (((1,), (1,)), ((), ())), preferred_element_type=F32)


def _split_bf16(a):
    hi = a.astype(BF16)
    lo = (a - hi.astype(F32)).astype(BF16)
    return hi, lo


def _sigmoid(x):
    return 1.0 / (1.0 + jnp.exp(-x))


def _params(sem):
    return pltpu.CompilerParams(dimension_semantics=sem, vmem_limit_bytes=VMEM_LIMIT)


def _mod_kernel(c_ref, w_ref, b_ref, o_ref):
    c = c_ref[...]
    a = c * _sigmoid(c)
    a_hi, a_lo = _split_bf16(a)
    w_hi, w_lo = _split_bf16(w_ref[0])
    o_ref[0] = _dot(a_hi, w_hi) + _dot(a_lo, w_hi) + _dot(a_hi, w_lo) + b_ref[0]


def _modulation(cc, w_mod, b_mod):
    depth, d, n = w_mod.shape
    r = cc.shape[0]
    tn = 1024
    return pl.pallas_call(
        _mod_kernel,
        grid=(depth, n // tn),
        in_specs=[
            pl.BlockSpec((r, d), lambda l, j: (0, 0)),
            pl.BlockSpec((1, d, tn), lambda l, j: (l, 0, j)),
            pl.BlockSpec((1, 1, tn), lambda l, j: (l, 0, j)),
        ],
        out_specs=pl.BlockSpec((1, r, tn), lambda l, j: (l, 0, j)),
        out_shape=jax.ShapeDtypeStruct((depth, r, n), F32),
        compiler_params=_params(("arbitrary", "arbitrary")),
        name="modulation",
    )(cc, w_mod, b_mod.reshape(depth, 1, n))


_QA, _KA, _VA = (0, 512), (512, 640), (640, 768)
_QB, _KB, _VB = (768, 1152), (1152, 1536), (1536, 1920)
_QM, _KM, _VM = (1920, 2304), (2304, 2688), (2688, 3072)
_OM, _Z, _GM = (3072, 3456), (3456, 4480), (4480, 4608)
_W_COLS = 4608


def _rope(t, tab_ref, quarter):
    cos, sin_p, sin_m = tab_ref[0], tab_ref[1], tab_ref[2]
    outs = []
    for j in range(t.shape[1] // LANES):
        tj = t[:, j * LANES:(j + 1) * LANES]
        outs.append(tj * cos + pltpu.roll(tj, quarter, 1) * sin_p + pltpu.roll(tj, LANES - quarter, 1) * sin_m)
    return outs[0] if len(outs) == 1 else jnp.concatenate(outs, axis=1)


def _inproj_kernel(x_ref, mod_ref, w_ref, ra_ref, rb_ref,
                   qa_ref, ka_ref, va_ref, qb_ref, kb_ref, vb_ref,
                   qm_ref, km_ref, vm_ref, om_ref, g_ref, gic_ref, gfc_ref, gir_ref, gfr_ref):
    x = x_ref[0]
    shift = mod_ref[0, 0, 0:1, :]
    scale = mod_ref[0, 0, 1:2, :]
    h = (x * (1.0 + scale) + shift).astype(BF16)

    def proj(cols):
        return _dot(h, w_ref[:, cols[0]:cols[1]])

    qa_ref[0] = (_rope(proj(_QA), ra_ref, HEAD_DIM // 4) * (HEAD_DIM ** -0.5)).astype(BF16)
    ka_ref[0] = _rope(proj(_KA), ra_ref, HEAD_DIM // 4).astype(BF16)
    va_ref[0] = proj(_VA).astype(BF16)
    qb_t = (_rope(proj(_QB), rb_ref, B_QK_DIM // 4) * (B_QK_DIM ** -0.5 * LOG2E)).T.astype(BF16)
    for p in range(qb_t.shape[0] // LANES):
        qb_ref[0, p, 0] = qb_t[p * LANES:(p + 1) * LANES]
    vb_t = proj(_VB).T
    ones_rows = jnp.where(lax.broadcasted_iota(jnp.int32, (VB_ROWS - HEAD_DIM, vb_t.shape[1]), 0) == 0, 1.0, 0.0)
    for hd in range(vb_t.shape[0] // HEAD_DIM):
        vb_ref[0, hd, 0] = jnp.concatenate([vb_t[hd * HEAD_DIM:(hd + 1) * HEAD_DIM], ones_rows], axis=0).astype(BF16)
    kb_ref[0] = _rope(proj(_KB), rb_ref, B_QK_DIM // 4).astype(BF16)
    km_ref[0] = (proj(_KM) * (HEAD_DIM ** -0.5)).astype(BF16)
    qm_t = proj(_QM).T.astype(BF16)
    om_t = _sigmoid(proj(_OM)).T.astype(BF16)
    vm_t = proj(_VM).T
    L = MLSTM_CHUNK
    ones_rows_c = jnp.where(lax.broadcasted_iota(jnp.int32, (VB_ROWS - HEAD_DIM, L), 0) == 0, 1.0, 0.0)
    for c in range(qm_t.shape[1] // L):
        cols = slice(c * L, (c + 1) * L)
        for p in range(qm_t.shape[0] // LANES):
            qm_ref[0, p, c] = qm_t[p * LANES:(p + 1) * LANES, cols]
            om_ref[0, p, c] = om_t[p * LANES:(p + 1) * LANES, cols]
        for hd in range(vm_t.shape[0] // HEAD_DIM):
            vm_ref[0, hd, c] = jnp.concatenate(
                [vm_t[hd * HEAD_DIM:(hd + 1) * HEAD_DIM, cols], ones_rows_c], axis=0).astype(BF16)
    z = proj(_Z)
    g_ref[0] = (z * _sigmoid(z)).astype(BF16)
    gm = proj(_GM)
    gm_t = gm.T
    for p in range(gic_ref.shape[1]):
        lo = 16 * p
        gic_ref[0, p] = gm[:, lo:lo + 8]
        gfc_ref[0, p] = gm[:, lo + 8:lo + 16]
        for c in range(gm_t.shape[1] // L):
            gir_ref[0, p, c] = gm_t[lo:lo + 8, c * L:(c + 1) * L]
            gfr_ref[0, p, c] = gm_t[lo + 8:lo + 16, c * L:(c + 1) * L]


def _in_projection(xa, modsel, w, rope_a, rope_b):
    b, s, d = xa.shape
    tm = ROW_TILE
    n_pairs = (_QB[1] - _QB[0]) // LANES
    n_heads = (_VB[1] - _VB[0]) // HEAD_DIM
    L = MLSTM_CHUNK
    widths = [512, 128, 128, (n_pairs, LANES, tm), 384, (n_heads, VB_ROWS, tm),
              (n_pairs, LANES, L), 384, (n_heads, VB_ROWS, L), (n_pairs, LANES, L), 1024]
    tposed = lambda g, r, tt: pl.BlockSpec((1, g, tm // tt, r, tt), lambda bi, i: (bi, 0, i, 0, 0))
    row = lambda n: tposed(*n) if isinstance(n, tuple) else pl.BlockSpec((1, tm, n), lambda bi, i: (bi, i, 0))
    out_shape = [jax.ShapeDtypeStruct((b, n[0], s // n[2], n[1], n[2]) if isinstance(n, tuple) else (b, s, n), BF16)
                 for n in widths]
    gate_c = pl.BlockSpec((1, n_pairs, tm, 8), lambda bi, i: (bi, 0, i, 0))
    gate_r = pl.BlockSpec((1, n_pairs, tm // L, 8, L), lambda bi, i: (bi, 0, i, 0, 0))
    out_shape += [jax.ShapeDtypeStruct((b, n_pairs, s, 8), F32)] * 2
    out_shape += [jax.ShapeDtypeStruct((b, n_pairs, s // L, 8, L), F32)] * 2
    return pl.pallas_call(
        _inproj_kernel,
        grid=(b, s // tm),
        in_specs=[
            row(d),
            pl.BlockSpec((1, 1, 3, d), lambda bi, i: (bi, jnp.minimum(i, 1), 0, 0)),
            pl.BlockSpec((d, _W_COLS), lambda bi, i: (0, 0)),
            pl.BlockSpec((3, tm, LANES), lambda bi, i: (0, i, 0)),
            pl.BlockSpec((3, tm, LANES), lambda bi, i: (0, i, 0)),
        ],
        out_specs=[row(n) for n in widths] + [gate_c, gate_c, gate_r, gate_r],
        out_shape=out_shape,
        compiler_params=_params(("arbitrary", "arbitrary")),
        name="in_projection",
    )(xa, modsel, w, rope_a, rope_b)


def _attn_a_kernel(sink_ref, q_ref, k_ref, v_ref, o_ref, *, n_ctx, n_blocks, block0):
    w = WINDOW
    s_len = n_blocks * w
    half = LANES // 2
    rows1 = lax.broadcasted_iota(jnp.int32, (4 * w, 1), 0)
    sink = jnp.where(rows1 < w, sink_ref[0],
                     jnp.where(rows1 < 2 * w, sink_ref[1], jnp.where(rows1 < 3 * w, sink_ref[2], sink_ref[3])))
    row = lax.broadcasted_iota(jnp.int32, (4 * w, 3 * w), 0) & (w - 1)
    col = lax.broadcasted_iota(jnp.int32, (4 * w, 3 * w), 1)
    lane = lax.broadcasted_iota(jnp.int32, (w, LANES), 1)
    k_ctx = k_ref[0, 0:n_ctx, :]
    v_ctx = v_ref[0, 0:n_ctx, :]

    def body(i, carry):
        r0 = pl.multiple_of(i * w, w)
        q = q_ref[0, pl.ds(r0, w), :]
        qs = jnp.concatenate([q[:, j * LANES:(j + 1) * LANES] for j in range(4)], axis=0)
        start = pl.multiple_of(jnp.clip(r0 - w, 0, s_len - 3 * w), w)
        s_loc = _dot_nt(qs, k_ref[0, pl.ds(start, 3 * w), :])
        s_ctx = _dot_nt(qs, k_ctx)
        kpos = col + start
        rel = kpos - r0 - row
        ok = (jnp.abs(rel) <= w) & (kpos >= jnp.where(r0 >= n_ctx, n_ctx, s_len))
        s_loc = jnp.where(ok, s_loc, -jnp.inf)
        m = jnp.maximum(jnp.maximum(jnp.max(s_loc, axis=1, keepdims=True),
                                    jnp.max(s_ctx, axis=1, keepdims=True)), sink)
        p_loc = jnp.exp(s_loc - m)
        p_ctx = jnp.exp(s_ctx - m)
        den = (jnp.sum(p_loc, axis=1, keepdims=True) + jnp.sum(p_ctx, axis=1, keepdims=True)
               + jnp.exp(sink - m))
        o = (_dot(p_loc.astype(BF16), v_ref[0, pl.ds(start, 3 * w), :])
             + _dot(p_ctx.astype(BF16), v_ctx)) / den
        left = jnp.where(lane < half, o[0:w], pltpu.roll(o[w:2 * w], half, 1))
        right = jnp.where(lane < half, pltpu.roll(o[2 * w:3 * w], half, 1), o[3 * w:])
        o_ref[0, pl.ds(r0, w), :] = jnp.concatenate([left, right], axis=1).astype(o_ref.dtype)
        return carry

    lax.fori_loop(block0, n_blocks, body, 0, unroll=2)


def _attn_a(sink, qa, ka, va, n_ctx, with_ctx):
    b, s, _ = qa.shape
    seq = lambda n: pl.BlockSpec((1, s, n), lambda bi: (bi, 0, 0))
    return pl.pallas_call(
        functools.partial(_attn_a_kernel, n_ctx=n_ctx, n_blocks=s // WINDOW,
                          block0=0 if with_ctx else n_ctx // WINDOW),
        grid=(b,),
        in_specs=[pl.BlockSpec(memory_space=pltpu.SMEM), seq(4 * LANES), seq(LANES), seq(LANES)],
        out_specs=seq(2 * LANES),
        out_shape=jax.ShapeDtypeStruct((b, s, 2 * LANES), BF16),
        compiler_params=_params(("arbitrary",)),
        name="window_attention",
    )(sink, qa, ka, va)


def _attn_b_kernel(li_ref, lam_ref, ng_ref, qt_ref, *refs, nt, tq, chained):
    k_ref, vt_ref = refs[1:3] if chained else refs[0:2]
    o_ref, qs_scr, sa_scr, sb_scr, sc_scr, mt_scr, acc_scr = refs[-7:]
    i = pl.program_id(2)
    tk = vt_ref.shape[-1]
    hv = HEAD_DIM
    cur = i % 2 if chained else 0

    def mask_queries(q_ref, slot):
        qt = q_ref[0, 0, 0]
        unit = lax.broadcasted_iota(jnp.int32, qt.shape, 0) >> 5
        zero = jnp.zeros_like(qt)
        qs_scr[slot] = jnp.concatenate([jnp.where(unit == u, qt, zero) for u in range(4)], axis=1)

    def fetch(t, s_scr, slot):
        off = pl.multiple_of(t * tk, tk)
        s_new = _dot(k_ref[0, pl.ds(off, tk), :], qs_scr[slot])
        s_scr[...] = s_new
        return jnp.max(s_new, axis=0, keepdims=True)

    def consume(s_scr, mt, m, t):
        m_new = jnp.maximum(m, mt)
        alpha = jnp.exp2(m - m_new)
        for hd in range(2):
            cols = slice(hd * 2 * tq, (hd + 1) * 2 * tq)
            p = jnp.exp2(s_scr[:, cols] - m_new[:, cols]).astype(BF16)
            acc_scr[hd] = alpha[:, cols] * acc_scr[hd] + _dot(vt_ref[0, hd, t], p)
        return m_new

    def first_tile():
        mask_queries(qt_ref, cur)
        mt_scr[...] = fetch(0, sc_scr, cur)

    def successor_first_tile():
        if chained:
            mask_queries(refs[0], 1 - cur)
            mt_scr[...] = fetch(0, sc_scr, 1 - cur)

    if chained:
        pl.when(i == 0)(first_tile)
    else:
        first_tile()
    acc_scr[...] = jnp.zeros_like(acc_scr)
    m = jnp.full((1, 4 * tq), -jnp.inf, F32)
    mt_c = mt_scr[...]

    def body(tt, carry):
        m, mt_a = carry
        t0 = 2 * tt + 1
        mt_b = fetch(t0 + 1, sb_scr, cur)
        m = consume(sa_scr, mt_a, m, t0)
        mt_a = fetch(t0 + 2, sa_scr, cur)
        m = consume(sb_scr, mt_b, m, t0 + 1)
        return m, mt_a

    if nt == 1:
        successor_first_tile()
        consume(sc_scr, mt_c, m, 0)
    else:
        mt_a = fetch(1, sa_scr, cur)
        m = consume(sc_scr, mt_c, m, 0)
        m, mt_a = lax.fori_loop(0, (nt - 2) // 2, body, (m, mt_a))
        if nt % 2 == 1:
            mt_b = fetch(nt - 1, sb_scr, cur)
            m = consume(sa_scr, mt_a, m, nt - 2)
            successor_first_tile()
            consume(sb_scr, mt_b, m, nt - 1)
        else:
            successor_first_tile()
            consume(sa_scr, mt_a, m, nt - 1)

    lam_init = li_ref[0]
    lv = lam_ref[...]
    lam = (jnp.exp(jnp.sum(lv[0:1] * lv[1:2], axis=1, keepdims=True))
           - jnp.exp(jnp.sum(lv[2:3] * lv[3:4], axis=1, keepdims=True)) + lam_init)

    def head_out(hd):
        acc = acc_scr[hd]
        o = acc[0:hv] * (1.0 / acc[hv:hv + 1])
        od = o[:, 0:tq] - lam * o[:, tq:]
        ms = jnp.mean(od * od, axis=0, keepdims=True)
        return od * lax.rsqrt(ms + LN_EPS) * ng_ref[...]

    y_t = jnp.concatenate([head_out(0), head_out(1)], axis=0) * (1.0 - lam_init)
    o_ref[0] = y_t.T.astype(o_ref.dtype)


def _attn_b_call(lam_init, lam_vec, norm_g, qbt, kb, vbt, out_buf, q0, nq, nt):
    b, pairs, n_tiles, _, tq = qbt.shape
    vrows, tk = vbt.shape[-2:]
    s = n_tiles * tq
    chained = nq > 1
    q_spec = lambda step: pl.BlockSpec((1, 1, 1, LANES, tq),
                                       lambda bi, p, i: (bi, p, jnp.minimum(i + step, nq - 1) + q0, 0, 0))
    in_specs = [
        pl.BlockSpec(memory_space=pltpu.SMEM),
        pl.BlockSpec((4, B_QK_DIM), lambda bi, p, i: (0, 0)),
        pl.BlockSpec((HEAD_DIM, 1), lambda bi, p, i: (0, 0)),
        q_spec(0),
    ]
    args = [lam_init, lam_vec, norm_g, qbt]
    if chained:
        in_specs.append(q_spec(1))
        args.append(qbt)
    in_specs += [pl.BlockSpec((1, nt * tk, LANES), lambda bi, p, i: (bi, 0, p)),
                 pl.BlockSpec((1, 2, nt, vrows, tk), lambda bi, p, i: (bi, p, 0, 0, 0))]
    args += [kb, vbt]
    aliases = {}
    if out_buf is not None:
        in_specs.append(pl.BlockSpec(memory_space=pl.ANY))
        args.append(out_buf)
        aliases = {len(args) - 1: 0}
    return pl.pallas_call(
        functools.partial(_attn_b_kernel, nt=nt, tq=tq, chained=chained),
        grid=(b, pairs, nq),
        in_specs=in_specs,
        out_specs=pl.BlockSpec((1, tq, LANES), lambda bi, p, i: (bi, i + q0, p)),
        out_shape=jax.ShapeDtypeStruct((b, s, pairs * LANES), BF16),
        scratch_shapes=[pltpu.VMEM((2, LANES, 4 * tq), BF16), pltpu.VMEM((tk, 4 * tq), F32),
                        pltpu.VMEM((tk, 4 * tq), F32), pltpu.VMEM((tk, 4 * tq), F32),
                        pltpu.VMEM((1, 4 * tq), F32), pltpu.VMEM((2, vrows, 2 * tq), F32)],
        input_output_aliases=aliases,
        compiler_params=_params(("arbitrary", "arbitrary", "arbitrary")),
        name="diff_attention",
    )(*args)


def _attn_b(lam_init, lam_vec, norm_g, qbt, kb, vbt, n_ctx, with_ctx):
    n_tiles, tq = qbt.shape[2], qbt.shape[4]
    tk = vbt.shape[-1]
    yb = _attn_b_call(lam_init, lam_vec, norm_g, qbt, kb, vbt, None, n_ctx // tq, n_tiles - n_ctx // tq, n_tiles)
    if with_ctx:
        yb = _attn_b_call(lam_init, lam_vec, norm_g, qbt, kb, vbt, yb, 0, n_ctx // tq, n_ctx // tk)
    return yb


def _log_sigmoid(x):
    return jnp.minimum(x, 0.0) - jnp.log(1.0 + jnp.exp(-jnp.abs(x)))


def _mlstm_kernel(bir_ref, bic_ref, bfr_ref, bfc_ref, ng_ref, qt_ref, k_ref, vt_ref, gic_ref, gfc_ref, gir_ref,
                  gfr_ref, ogt_ref,
                  o_ref, hf_scr, hb_scr, st_scr, ir_scr, br_scr, gc_scr, *, n_ctx_chunks, n_chunks):
    L = MLSTM_CHUNK
    hv = HEAD_DIM
    r_i = lax.broadcasted_iota(jnp.int32, (L, L), 0)
    c_i = lax.broadcasted_iota(jnp.int32, (L, L), 1)
    upper = r_i <= c_i
    lower = r_i >= c_i
    t_up = jnp.where(upper, 1.0, 0.0).astype(BF16)
    t_low = jnp.where(lower, 1.0, 0.0).astype(BF16)
    row_q = lax.broadcasted_iota(jnp.int32, (LANES, L), 0)
    sel = [jnp.where(row_q < hv, 1.0, 0.0).astype(BF16), jnp.where(row_q < hv, 0.0, 1.0).astype(BF16)]
    fwd_r = lax.broadcasted_iota(jnp.int32, (8, L), 0) < 2
    fwd_c = lax.broadcasted_iota(jnp.int32, (L, 8), 1) < 2
    st_scr[...] = jnp.zeros_like(st_scr)

    def gates(c, carry):
        r0 = pl.multiple_of(c * L, L)
        frow = _log_sigmoid(gfr_ref[0, 0, c] + bfr_ref[0])
        hi, lo = _split_bf16(frow)
        ir_scr[c] = gir_ref[0, 0, c] + bir_ref[0]
        br_scr[c] = jnp.where(fwd_r, _dot(hi, t_up) + _dot(lo, t_up), _dot(hi, t_low) + _dot(lo, t_low))
        fcol = _log_sigmoid(gfc_ref[0, 0, pl.ds(r0, L), :] + bfc_ref[0])
        hi, lo = _split_bf16(fcol)
        bcol = jnp.where(fwd_c, _dot(t_low, hi) + _dot(t_low, lo), _dot(t_up, hi) + _dot(t_up, lo))
        gc_scr[c] = gic_ref[0, 0, pl.ds(r0, L), :] + bic_ref[0] - bcol
        return carry
    lax.fori_loop(0, n_chunks, gates, 0, unroll=2)

    def chunk_step(direction, c, m_prev):
        r0 = pl.multiple_of(c * L, L)
        kc = k_ref[0, pl.ds(r0, L), :]
        qtc = qt_ref[0, 0, c]
        irow8, brow8, gcol8 = ir_scr[c], br_scr[c], gc_scr[c]
        tri = upper if direction == 0 else lower
        h_scr = hf_scr if direction == 0 else hb_scr
        m_news = []
        for a in range(2):
            ci = 2 * direction + a
            b_row, i_row, g_col = brow8[ci:ci + 1], irow8[ci:ci + 1], gcol8[:, ci:ci + 1]
            qtm = qtc * sel[a]
            s_t = _dot(kc, qtm)
            log_d = jnp.where(tri, g_col + b_row, -jnp.inf)
            m_in = b_row + m_prev[a]
            m_t = jnp.maximum(m_in, jnp.max(log_d, axis=0, keepdims=True))
            w_in = jnp.exp(m_in - m_t)
            sd = (s_t * jnp.exp(log_d - m_t)).astype(BF16)
            qw = (qtm.astype(F32) * w_in).astype(BF16)
            vt = vt_ref[0, a, c]
            state = st_scr[ci]
            numden = _dot(state.astype(BF16), qw) + _dot(vt, sd)
            h_scr[c, a * hv:(a + 1) * hv, :] = (
                numden[0:hv] / jnp.maximum(jnp.abs(numden[hv:hv + 1]), jnp.exp(-m_t)))
            b_last = jnp.min(b_row, axis=1, keepdims=True)
            log_w = b_last - b_row + i_row
            m_new = jnp.maximum(b_last + m_prev[a], jnp.max(log_w, axis=1, keepdims=True))
            vw = (vt.astype(F32) * jnp.exp(log_w - m_new)).astype(BF16)
            st_scr[ci] = jnp.exp(b_last + m_prev[a] - m_new) * state + _dot(vw, kc)
            m_news.append(m_new)
        return tuple(m_news)

    def body(j, carry):
        mf, mb = carry
        cb = jnp.where(j < n_ctx_chunks, n_ctx_chunks - 1 - j, n_chunks - 1 + n_ctx_chunks - j)
        return chunk_step(0, j, mf), chunk_step(1, cb, mb)

    z11 = jnp.zeros((1, 1), F32)
    lax.fori_loop(0, n_chunks, body, ((z11, z11), (z11, z11)), unroll=4)

    def finish(c, carry):
        r0 = pl.multiple_of(c * L, L)
        h = hf_scr[c] + hb_scr[c]
        outs = []
        for a in range(2):
            ha = h[a * hv:(a + 1) * hv]
            outs.append(ha * lax.rsqrt(jnp.mean(ha * ha, axis=0, keepdims=True) + LN_EPS))
        y_t = jnp.concatenate(outs, axis=0) * ng_ref[0] * ogt_ref[0, 0, c].astype(F32)
        o_ref[0, pl.ds(r0, L), :] = y_t.T.astype(o_ref.dtype)
        return carry
    lax.fori_loop(0, n_chunks, finish, 0, unroll=2)


def _mlstm(bias_i, bias_f, norm_g, qmt, km, vmt, gic, gfc, gir, gfr, ogt, n_ctx):
    b, pairs, nc, _, L = qmt.shape
    s = nc * L
    vrows = vmt.shape[3]
    seq = pl.BlockSpec((1, s, LANES), lambda bi, p: (bi, 0, p))
    pair_tiles = pl.BlockSpec((1, 1, nc, LANES, L), lambda bi, p: (bi, p, 0, 0, 0))
    gate_c = pl.BlockSpec((1, 1, s, 8), lambda bi, p: (bi, p, 0, 0))
    gate_r = pl.BlockSpec((1, 1, nc, 8, L), lambda bi, p: (bi, p, 0, 0, 0))
    bias_r = pl.BlockSpec((1, 8, 1), lambda bi, p: (p, 0, 0))
    bias_c = pl.BlockSpec((1, 1, 8), lambda bi, p: (p, 0, 0))
    return pl.pallas_call(
        functools.partial(_mlstm_kernel, n_ctx_chunks=n_ctx // L, n_chunks=nc),
        grid=(b, pairs),
        in_specs=[
            bias_r, bias_c, bias_r, bias_c,
            pl.BlockSpec((1, LANES, 1), lambda bi, p: (p, 0, 0)),
            pair_tiles, seq,
            pl.BlockSpec((1, 2, nc, vrows, L), lambda bi, p: (bi, p, 0, 0, 0)),
            gate_c, gate_c, gate_r, gate_r,
            pair_tiles,
        ],
        out_specs=seq,
        out_shape=jax.ShapeDtypeStruct((b, s, pairs * LANES), BF16),
        scratch_shapes=[pltpu.VMEM((nc, LANES, L), F32), pltpu.VMEM((nc, LANES, L), F32),
                        pltpu.VMEM((4, vrows, LANES), F32),
                        pltpu.VMEM((nc, 8, L), F32), pltpu.VMEM((nc, 8, L), F32), pltpu.VMEM((nc, L, 8), F32)],
        compiler_params=_params(("arbitrary", "arbitrary")),
        name="mlstm",
    )(bias_i[:, :, None], bias_i[:, None, :], bias_f[:, :, None], bias_f[:, None, :], norm_g[:, :, None],
      qmt, km, vmt, gic, gfc, gir, gfr, ogt)


def _out_kernel(x_ref, mod_ref, ya_ref, yb_ref, ym_ref, g_ref, w_ref, lng_ref, lnb_ref, o_ref, *, alpha):
    x = x_ref[0]
    gate = mod_ref[0, 0, 2:3, :]
    g = g_ref[0].astype(F32)
    na, nb_ = ya_ref.shape[2], yb_ref.shape[2]
    mix_a = (ya_ref[0].astype(F32) * g[:, :na]).astype(BF16)
    mix_b = (yb_ref[0].astype(F32) * g[:, na:na + nb_]).astype(BF16)
    mix_m = (ym_ref[0].astype(F32) * g[:, na + nb_:]).astype(BF16)
    y = (_dot(mix_a, w_ref[0:na, :]) + _dot(mix_b, w_ref[na:na + nb_, :])
         + _dot(mix_m, w_ref[na + nb_:, :]))
    r = alpha * x + gate * y
    mu = jnp.mean(r, axis=1, keepdims=True)
    d = r - mu
    var = jnp.mean(d * d, axis=1, keepdims=True)
    o_ref[0] = d * lax.rsqrt(var + LN_EPS) * lng_ref[...] + lnb_ref[...]


def _out_projection(xa, modsel, ya, yb, ym, g, w, ln_g, ln_b, alpha, row0):
    b, s, d = xa.shape
    tm = ROW_TILE
    row = lambda n: pl.BlockSpec((1, tm, n), lambda bi, i: (bi, i + row0, 0))
    vec = pl.BlockSpec((1, d), lambda bi, i: (0, 0))
    return pl.pallas_call(
        functools.partial(_out_kernel, alpha=alpha),
        grid=(b, s // tm - row0),
        in_specs=[
            row(d),
            pl.BlockSpec((1, 1, 3, d), lambda bi, i: (bi, jnp.minimum(i + row0, 1), 0, 0)),
            row(ya.shape[2]), row(yb.shape[2]), row(ym.shape[2]), row(d),
            pl.BlockSpec(w.shape, lambda bi, i: (0, 0)),
            vec, vec,
        ],
        out_specs=pl.BlockSpec((1, tm, d), lambda bi, i: (bi, i, 0)),
        out_shape=jax.ShapeDtypeStruct((b, s - row0 * tm, d), F32),
        compiler_params=_params(("arbitrary", "arbitrary")),
        name="out_projection",
    )(xa, modsel, ya, yb, ym, g, w, ln_g, ln_b)


def _in_weight_columns(d_in):
    z = d_in
    src = {}
    off = 0
    for name, n in (("qa", 256), ("ka", 128), ("va", 128), ("za", 256), ("qb", 384), ("kb", 384), ("vb", 384),
                    ("zb", 384), ("qm", 384), ("km", 384), ("vm", 384), ("om", 384), ("zm", 384), ("gm", 24)):
        src[name] = np.arange(off, off + n)
        off += n
    assert off == d_in
    cols = []
    for hd in range(4):
        blk = np.full(LANES, z)
        gq = hd // 2
        blk[64 * gq:64 * gq + 64] = src["qa"][64 * hd:64 * hd + 64]
        cols.append(blk)
    for name in ("ka", "va", "qb", "kb", "vb", "qm", "km", "vm", "om", "za", "zb", "zm"):
        cols.append(src[name])
    heads = 6
    for p in range(heads // 2):
        for gate in range(2):
            cols.append(np.array([src["gm"][di * 2 * heads + gate * heads + 2 * p + a]
                                  for di in range(2) for a in range(2)] + [z] * 4))
    cols.append(np.full(LANES - 16 * (heads // 2), z))
    cols = np.concatenate(cols)
    assert cols.shape[0] == _W_COLS
    return cols


def _permute_columns(w, cols):
    n = w.shape[-1]
    pieces, start = [], 0
    breaks = np.flatnonzero(np.diff(cols) != 1) + 1
    for stop in list(breaks) + [len(cols)]:
        run = cols[start:stop]
        if len(run) >= 32:
            pieces.append(lax.slice_in_dim(w, int(run[0]), int(run[-1]) + 1, axis=w.ndim - 1))
        elif pieces and isinstance(pieces[-1], list):
            pieces[-1].extend(run)
        else:
            pieces.append(list(run))
        start = stop
    lo = min(min(p) for p in pieces if isinstance(p, list))
    tail = jnp.concatenate([lax.slice_in_dim(w, lo, n, axis=w.ndim - 1), jnp.zeros(w.shape[:-1] + (1,), w.dtype)], -1)
    pieces = [jnp.take(tail, np.array(p) - lo, axis=-1) if isinstance(p, list) else p for p in pieces]
    return jnp.concatenate(pieces, axis=-1)


def _rope_table(n_tokens, n_ctx, dim):
    f32 = np.float32
    rows = n_tokens // GRID_W
    rowp = np.broadcast_to(np.arange(rows, dtype=f32)[:, None], (rows, GRID_W)).reshape(-1)
    colp = np.broadcast_to(np.arange(GRID_W, dtype=f32)[None, :], (rows, GRID_W)).reshape(-1)
    n_freq = dim // 4
    inv = np.power(f32(ROPE_BASE), -np.arange(n_freq, dtype=f32) / f32(n_freq)).astype(f32)
    ar = rowp[:, None] * inv
    ac = colp[:, None] * inv
    ang = np.concatenate([ar, ar, ac, ac], -1).astype(f32)
    cos, sin = np.cos(ang), np.sin(ang)
    odd = (np.arange(dim) // n_freq) % 2 == 1
    sin_p = np.where(odd, sin, f32(0))
    sin_m = np.where(odd, f32(0), -sin)
    tab = np.stack([cos, sin_p, sin_m])
    ident = np.stack([np.ones((n_ctx, dim), f32), np.zeros((n_ctx, dim), f32), np.zeros((n_ctx, dim), f32)])
    tab = np.concatenate([ident, tab], axis=1)
    return jnp.asarray(np.tile(tab, (1, 1, LANES // dim)).astype(f32))


def kernel(x, c, ctx, c_ctx, w_mod, b_mod, w_in, attn_sink, diff_lambda, diff_norm_g, mlstm_i_bias,
           mlstm_f_bias, mlstm_norm_g, w_out, ln_g, ln_b):
    b, t, d = x.shape
    n_ctx = ctx.shape[1]
    depth = w_mod.shape[0]
    d_in = w_in.shape[2]
    alpha = (2 * depth) ** 0.25

    xa = jnp.concatenate([ctx, x], axis=1)
    rope_a = _rope_table(t, n_ctx, HEAD_DIM)
    rope_b = _rope_table(t, n_ctx, B_QK_DIM)

    rows = -(-(b + 1) // 8) * 8
    cc = jnp.concatenate([c, c_ctx[None, :], jnp.zeros((rows - b - 1, d), F32)], axis=0)
    mod = _modulation(cc, w_mod, b_mod).reshape(depth, rows, 3, d)

    w_in_p = _permute_columns(w_in, _in_weight_columns(d_in)).astype(BF16)
    w_out_b = w_out.astype(BF16)
    norm_b = diff_norm_g[:, :, None]

    n_pairs = mlstm_i_bias.shape[2] // 2
    heads = mlstm_i_bias.shape[2]
    chains = [(di, a) for di in range(2) for a in range(2)]
    bias_idx = np.array([[di * heads + 2 * p + a for di, a in chains] for p in range(n_pairs)])
    pad8 = lambda v: jnp.pad(v.reshape(-1)[bias_idx], ((0, 0), (0, 4)))

    for l in range(depth):
        with_ctx = l < depth - 1
        lam_init = 0.8 - 0.6 * math.exp(-0.3 * l)
        modsel = jnp.stack([jnp.broadcast_to(mod[l, b], (b, 3, d)), mod[l, :b]], axis=1)
        (qa, ka, va, qb, kb, vb, qm, km, vm, og, g, gic, gfc, gir, gfr) = _in_projection(
            xa, modsel, w_in_p[l], rope_a, rope_b)

        ya = _attn_a(attn_sink[l], qa, ka, va, n_ctx, with_ctx)
        yb = _attn_b(jnp.full((1,), lam_init, F32), diff_lambda[l], norm_b[l], qb, kb, vb, n_ctx, with_ctx)

        ym = _mlstm(pad8(mlstm_i_bias[l]), pad8(mlstm_f_bias[l]), mlstm_norm_g[l].reshape(n_pairs, LANES),
                    qm, km, vm, gic, gfc, gir, gfr, og, n_ctx)

        xa = _out_projection(xa, modsel, ya, yb, ym, g, w_out_b[l], ln_g[l][None, :], ln_b[l][None, :], alpha,
                             0 if with_ctx else n_ctx // ROW_TILE)
    return xa
```

```python
import functools
import math

import numpy as np
import jax
import jax.numpy as jnp
from jax import lax
from jax.experimental import pallas as pl
from jax.experimental.pallas import tpu as pltpu

F32 = jnp.float32
BF16 = jnp.bfloat16

LANES = 128
GRID_W = 64
HEAD_DIM = 64
ROPE_BASE = 10000.0
LN_EPS = 1e-5
WINDOW = 128
B_QK_DIM = 32
LOG2E = math.log2(math.e)
VB_ROWS = HEAD_DIM + 16
MLSTM_CHUNK = 128
ROW_TILE = 256
VMEM_LIMIT = 56 * 1024 * 1024


def _dot(a, b):
    return jnp.dot(a, b, preferred_element_type=F32)


def _dot_nt(a, b):
    return lax.dot_general(a, b, (((1,), (1,)), ((), ())), preferred_element_type=F32)


def _split_bf16(a):
    hi = a.astype(BF16)
    lo = (a - hi.astype(F32)).astype(BF16)
    return hi, lo


def _sigmoid(x):
    return 1.0 / (1.0 + jnp.exp(-x))


def _params(sem):
    return pltpu.CompilerParams(dimension_semantics=sem, vmem_limit_bytes=VMEM_LIMIT)


def _mod_kernel(c_ref, w_ref, b_ref, o_ref):
    c = c_ref[...]
    a = c * _sigmoid(c)
    a_hi, a_lo = _split_bf16(a)
    w_hi, w_lo = _split_bf16(w_ref[0])
    o_ref[0] = _dot(a_hi, w_hi) + _dot(a_lo, w_hi) + _dot(a_hi, w_lo) + b_ref[0]


def _modulation(cc, w_mod, b_mod):
    depth, d, n = w_mod.shape
    r = cc.shape[0]
    tn = 1024
    return pl.pallas_call(
        _mod_kernel,
        grid=(depth, n // tn),
        in_specs=[
            pl.BlockSpec((r, d), lambda l, j: (0, 0)),
            pl.BlockSpec((1, d, tn), lambda l, j: (l, 0, j)),
            pl.BlockSpec((1, 1, tn), lambda l, j: (l, 0, j)),
        ],
        out_specs=pl.BlockSpec((1, r, tn), lambda l, j: (l, 0, j)),
        out_shape=jax.ShapeDtypeStruct((depth, r, n), F32),
        compiler_params=_params(("arbitrary", "arbitrary")),
        name="modulation",
    )(cc, w_mod, b_mod.reshape(depth, 1, n))


_QA, _KA, _VA = (0, 512), (512, 640), (640, 768)
_QB, _KB, _VB = (768, 1152), (1152, 1536), (1536, 1920)
_QM, _KM, _VM = (1920, 2304), (2304, 2688), (2688, 3072)
_OM, _Z, _GM = (3072, 3456), (3456, 4480), (4480, 4608)
_W_COLS = 4608


def _rope(t, tab_ref, quarter):
    cos, sin_p, sin_m = tab_ref[0], tab_ref[1], tab_ref[2]
    outs = []
    for j in range(t.shape[1] // LANES):
        tj = t[:, j * LANES:(j + 1) * LANES]
        outs.append(tj * cos + pltpu.roll(tj, quarter, 1) * sin_p + pltpu.roll(tj, LANES - quarter, 1) * sin_m)
    return outs[0] if len(outs) == 1 else jnp.concatenate(outs, axis=1)


def _inproj_kernel(x_ref, mod_ref, w_ref, ra_ref, rb_ref,
                   qa_ref, ka_ref, va_ref, qb_ref, kb_ref, vb_ref,
                   qm_ref, km_ref, vm_ref, om_ref, g_ref, gic_ref, gfc_ref, gir_ref, gfr_ref):
    x = x_ref[0]
    shift = mod_ref[0, 0, 0:1, :]
    scale = mod_ref[0, 0, 1:2, :]
    h = (x * (1.0 + scale) + shift).astype(BF16)

    def proj(cols):
        return _dot(h, w_ref[:, cols[0]:cols[1]])

    qa_ref[0] = (_rope(proj(_QA), ra_ref, HEAD_DIM // 4) * (HEAD_DIM ** -0.5)).astype(BF16)
    ka_ref[0] = _rope(proj(_KA), ra_ref, HEAD_DIM // 4).astype(BF16)
    va_ref[0] = proj(_VA).astype(BF16)
    qb_t = (_rope(proj(_QB), rb_ref, B_QK_DIM // 4) * (B_QK_DIM ** -0.5 * LOG2E)).T.astype(BF16)
    for p in range(qb_t.shape[0] // LANES):
        qb_ref[0, p, 0] = qb_t[p * LANES:(p + 1) * LANES]
    vb_t = proj(_VB).T
    ones_rows = jnp.where(lax.broadcasted_iota(jnp.int32, (VB_ROWS - HEAD_DIM, vb_t.shape[1]), 0) == 0, 1.0, 0.0)
    for hd in range(vb_t.shape[0] // HEAD_DIM):
        vb_ref[0, hd, 0] = jnp.concatenate([vb_t[hd * HEAD_DIM:(hd + 1) * HEAD_DIM], ones_rows], axis=0).astype(BF16)
    kb_ref[0] = _rope(proj(_KB), rb_ref, B_QK_DIM // 4).astype(BF16)
    km_ref[0] = (proj(_KM) * (HEAD_DIM ** -0.5)).astype(BF16)
    qm_t = proj(_QM).T.astype(BF16)
    om_t = _sigmoid(proj(_OM)).T.astype(BF16)
    vm_t = proj(_VM).T
    L = MLSTM_CHUNK
    ones_rows_c = jnp.where(lax.broadcasted_iota(jnp.int32, (VB_ROWS - HEAD_DIM, L), 0) == 0, 1.0, 0.0)
    for c in range(qm_t.shape[1] // L):
        cols = slice(c * L, (c + 1) * L)
        for p in range(qm_t.shape[0] // LANES):
            qm_ref[0, p, c] = qm_t[p * LANES:(p + 1) * LANES, cols]
            om_ref[0, p, c] = om_t[p * LANES:(p + 1) * LANES, cols]
        for hd in range(vm_t.shape[0] // HEAD_DIM):
            vm_ref[0, hd, c] = jnp.concatenate(
                [vm_t[hd * HEAD_DIM:(hd + 1) * HEAD_DIM, cols], ones_rows_c], axis=0).astype(BF16)
    z = proj(_Z)
    g_ref[0] = (z * _sigmoid(z)).astype(BF16)
    gm = proj(_GM)
    gm_t = gm.T
    for p in range(gic_ref.shape[1]):
        lo = 16 * p
        gic_ref[0, p] = gm[:, lo:lo + 8]
        gfc_ref[0, p] = gm[:, lo + 8:lo + 16]
        for c in range(gm_t.shape[1] // L):
            gir_ref[0, p, c] = gm_t[lo:lo + 8, c * L:(c + 1) * L]
            gfr_ref[0, p, c] = gm_t[lo + 8:lo + 16, c * L:(c + 1) * L]


def _in_projection(xa, modsel, w, rope_a, rope_b):
    b, s, d = xa.shape
    tm = ROW_TILE
    n_pairs = (_QB[1] - _QB[0]) // LANES
    n_heads = (_VB[1] - _VB[0]) // HEAD_DIM
    L = MLSTM_CHUNK
    widths = [512, 128, 128, (n_pairs, LANES, tm), 384, (n_heads, VB_ROWS, tm),
              (n_pairs, LANES, L), 384, (n_heads, VB_ROWS, L), (n_pairs, LANES, L), 1024]
    tposed = lambda g, r, tt: pl.BlockSpec((1, g, tm // tt, r, tt), lambda bi, i: (bi, 0, i, 0, 0))
    row = lambda n: tposed(*n) if isinstance(n, tuple) else pl.BlockSpec((1, tm, n), lambda bi, i: (bi, i, 0))
    out_shape = [jax.ShapeDtypeStruct((b, n[0], s // n[2], n[1], n[2]) if isinstance(n, tuple) else (b, s, n), BF16)
                 for n in widths]
    gate_c = pl.BlockSpec((1, n_pairs, tm, 8), lambda bi, i: (bi, 0, i, 0))
    gate_r = pl.BlockSpec((1, n_pairs, tm // L, 8, L), lambda bi, i: (bi, 0, i, 0, 0))
    out_shape += [jax.ShapeDtypeStruct((b, n_pairs, s, 8), F32)] * 2
    out_shape += [jax.ShapeDtypeStruct((b, n_pairs, s // L, 8, L), F32)] * 2
    return pl.pallas_call(
        _inproj_kernel,
        grid=(b, s // tm),
        in_specs=[
            row(d),
            pl.BlockSpec((1, 1, 3, d), lambda bi, i: (bi, jnp.minimum(i, 1), 0, 0)),
            pl.BlockSpec((d, _W_COLS), lambda bi, i: (0, 0)),
            pl.BlockSpec((3, tm, LANES), lambda bi, i: (0, i, 0)),
            pl.BlockSpec((3, tm, LANES), lambda bi, i: (0, i, 0)),
        ],
        out_specs=[row(n) for n in widths] + [gate_c, gate_c, gate_r, gate_r],
        out_shape=out_shape,
        compiler_params=_params(("arbitrary", "arbitrary")),
        name="in_projection",
    )(xa, modsel, w, rope_a, rope_b)


def _attn_a_kernel(sink_ref, q_ref, k_ref, v_ref, o_ref, *, n_ctx, n_blocks, block0):
    w = WINDOW
    s_len = n_blocks * w
    half = LANES // 2
    rows1 = lax.broadcasted_iota(jnp.int32, (4 * w, 1), 0)
    sink = jnp.where(rows1 < w, sink_ref[0],
                     jnp.where(rows1 < 2 * w, sink_ref[1], jnp.where(rows1 < 3 * w, sink_ref[2], sink_ref[3])))
    row = lax.broadcasted_iota(jnp.int32, (4 * w, 3 * w), 0) & (w - 1)
    col = lax.broadcasted_iota(jnp.int32, (4 * w, 3 * w), 1)
    lane = lax.broadcasted_iota(jnp.int32, (w, LANES), 1)
    k_ctx = k_ref[0, 0:n_ctx, :]
    v_ctx = v_ref[0, 0:n_ctx, :]

    def body(i, carry):
        r0 = pl.multiple_of(i * w, w)
        q = q_ref[0, pl.ds(r0, w), :]
        qs = jnp.concatenate([q[:, j * LANES:(j + 1) * LANES] for j in range(4)], axis=0)
        start = pl.multiple_of(jnp.clip(r0 - w, 0, s_len - 3 * w), w)
        s_loc = _dot_nt(qs, k_ref[0, pl.ds(start, 3 * w), :])
        s_ctx = _dot_nt(qs, k_ctx)
        kpos = col + start
        rel = kpos - r0 - row
        ok = (jnp.abs(rel) <= w) & (kpos >= jnp.where(r0 >= n_ctx, n_ctx, s_len))
        s_loc = jnp.where(ok, s_loc, -jnp.inf)
        m = jnp.maximum(jnp.maximum(jnp.max(s_loc, axis=1, keepdims=True),
                                    jnp.max(s_ctx, axis=1, keepdims=True)), sink)
        p_loc = jnp.exp(s_loc - m)
        p_ctx = jnp.exp(s_ctx - m)
        den = (jnp.sum(p_loc, axis=1, keepdims=True) + jnp.sum(p_ctx, axis=1, keepdims=True)
               + jnp.exp(sink - m))
        o = (_dot(p_loc.astype(BF16), v_ref[0, pl.ds(start, 3 * w), :])
             + _dot(p_ctx.astype(BF16), v_ctx)) / den
        left = jnp.where(lane < half, o[0:w], pltpu.roll(o[w:2 * w], half, 1))
        right = jnp.where(lane < half, pltpu.roll(o[2 * w:3 * w], half, 1), o[3 * w:])
        o_ref[0, pl.ds(r0, w), :] = jnp.concatenate([left, right], axis=1).astype(o_ref.dtype)
        return carry

    lax.fori_loop(block0, n_blocks, body, 0, unroll=2)


def _attn_a(sink, qa, ka, va, n_ctx, with_ctx):
    b, s, _ = qa.shape
    seq = lambda n: pl.BlockSpec((1, s, n), lambda bi: (bi, 0, 0))
    return pl.pallas_call(
        functools.partial(_attn_a_kernel, n_ctx=n_ctx, n_blocks=s // WINDOW,
                          block0=0 if with_ctx else n_ctx // WINDOW),
        grid=(b,),
        in_specs=[pl.BlockSpec(memory_space=pltpu.SMEM), seq(4 * LANES), seq(LANES), seq(LANES)],
        out_specs=seq(2 * LANES),
        out_shape=jax.ShapeDtypeStruct((b, s, 2 * LANES), BF16),
        compiler_params=_params(("arbitrary",)),
        name="window_attention",
    )(sink, qa, ka, va)


def _attn_b_kernel(li_ref, lam_ref, ng_ref, qt_ref, k_ref, vt_ref, o_ref,
                   qs_scr, sa_scr, sb_scr, sc_scr, mt_scr, acc_scr, *, nt, q0, nq):
    tq, tk = qt_ref.shape[-1], vt_ref.shape[-1]
    hv = HEAD_DIM
    lam_init = li_ref[0]
    lv = lam_ref[...]
    lam = (jnp.exp(jnp.sum(lv[0:1] * lv[1:2], axis=1, keepdims=True))
           - jnp.exp(jnp.sum(lv[2:3] * lv[3:4], axis=1, keepdims=True)) + lam_init)

    def first_tile(i, slot):
        qt = qt_ref[0, 0, q0 + i]
        unit = lax.broadcasted_iota(jnp.int32, qt.shape, 0) >> 5
        zero = jnp.zeros_like(qt)
        qs_scr[slot] = jnp.concatenate([jnp.where(unit == u, qt, zero) for u in range(4)], axis=1)
        mt_scr[...] = fetch(0, sc_scr, slot)

    def fetch(t, s_scr, slot):
        off = pl.multiple_of(t * tk, tk)
        s_new = _dot(k_ref[0, pl.ds(off, tk), :], qs_scr[slot])
        s_scr[...] = s_new
        return jnp.max(s_new, axis=0, keepdims=True)

    def consume(s_scr, mt, m, t):
        m_new = jnp.maximum(m, mt)
        alpha = jnp.exp2(m - m_new)
        p = jnp.exp2(s_scr[...] - m_new).astype(BF16)
        for hd in range(2):
            cols = slice(hd * 2 * tq, (hd + 1) * 2 * tq)
            acc_scr[hd] = alpha[:, cols] * acc_scr[hd] + _dot(vt_ref[0, hd, t], p[:, cols])
        return m_new

    def head_out(hd):
        acc = acc_scr[hd]
        o = acc[0:hv] * (1.0 / acc[hv:hv + 1])
        od = o[:, 0:tq] - lam * o[:, tq:]
        ms = jnp.mean(od * od, axis=0, keepdims=True)
        return od * lax.rsqrt(ms + LN_EPS) * ng_ref[...]

    def query_tile(i, carry):
        cur = i % 2

        def successor():
            if nq > 1:
                first_tile(jnp.minimum(i + 1, nq - 1), 1 - cur)

        acc_scr[...] = jnp.zeros_like(acc_scr)
        m = jnp.full((1, 4 * tq), -jnp.inf, F32)
        mt_c = mt_scr[...]

        def body(tt, carry):
            m, mt_a = carry
            t0 = 2 * tt + 1
            mt_b = fetch(t0 + 1, sb_scr, cur)
            m = consume(sa_scr, mt_a, m, t0)
            mt_a = fetch(t0 + 2, sa_scr, cur)
            m = consume(sb_scr, mt_b, m, t0 + 1)
            return m, mt_a

        if nt == 1:
            successor()
            consume(sc_scr, mt_c, m, 0)
        else:
            mt_a = fetch(1, sa_scr, cur)
            m = consume(sc_scr, mt_c, m, 0)
            m, mt_a = lax.fori_loop(0, (nt - 2) // 2, body, (m, mt_a))
            if nt % 2 == 1:
                mt_b = fetch(nt - 1, sb_scr, cur)
                m = consume(sa_scr, mt_a, m, nt - 2)
                successor()
                consume(sb_scr, mt_b, m, nt - 1)
            else:
                successor()
                consume(sa_scr, mt_a, m, nt - 1)

        y_t = jnp.concatenate([head_out(0), head_out(1)], axis=0) * (1.0 - lam_init)
        o_ref[0, pl.ds(pl.multiple_of(i * tq, tq), tq), :] = y_t.T.astype(o_ref.dtype)
        return carry

    first_tile(0, 0)
    lax.fori_loop(0, nq, query_tile, 0)


def _attn_b_call(lam_init, lam_vec, norm_g, qbt, kb, vbt, q0, nq, nt):
    b, pairs, n_tiles, _, tq = qbt.shape
    vrows, tk = vbt.shape[-2:]
    return pl.pallas_call(
        functools.partial(_attn_b_kernel, nt=nt, q0=q0, nq=nq),
        grid=(b, pairs),
        in_specs=[
            pl.BlockSpec(memory_space=pltpu.SMEM),
            pl.BlockSpec((4, B_QK_DIM), lambda bi, p: (0, 0)),
            pl.BlockSpec((HEAD_DIM, 1), lambda bi, p: (0, 0)),
            pl.BlockSpec((1, 1, n_tiles, LANES, tq), lambda bi, p: (bi, p, 0, 0, 0)),
            pl.BlockSpec((1, nt * tk, LANES), lambda bi, p: (bi, 0, p)),
            pl.BlockSpec((1, 2, nt, vrows, tk), lambda bi, p: (bi, p, 0, 0, 0)),
        ],
        out_specs=pl.BlockSpec((1, nq * tq, LANES), lambda bi, p: (bi, 0, p)),
        out_shape=jax.ShapeDtypeStruct((b, nq * tq, pairs * LANES), BF16),
        scratch_shapes=[pltpu.VMEM((2, LANES, 4 * tq), BF16), pltpu.VMEM((tk, 4 * tq), F32),
                        pltpu.VMEM((tk, 4 * tq), F32), pltpu.VMEM((tk, 4 * tq), F32),
                        pltpu.VMEM((1, 4 * tq), F32), pltpu.VMEM((2, vrows, 2 * tq), F32)],
        compiler_params=_params(("arbitrary", "arbitrary")),
        name="diff_attention",
    )(lam_init, lam_vec, norm_g, qbt, kb, vbt)


def _attn_b(lam_init, lam_vec, norm_g, qbt, kb, vbt, n_ctx, with_ctx):
    n_tiles, tq = qbt.shape[2], qbt.shape[4]
    tk = vbt.shape[-1]
    n_ctx_tiles = n_ctx // tq
    y_lat = _attn_b_call(lam_init, lam_vec, norm_g, qbt, kb, vbt, n_ctx_tiles, n_tiles - n_ctx_tiles, n_tiles)
    y_ctx = _attn_b_call(lam_init, lam_vec, norm_g, qbt, kb, vbt, 0, n_ctx_tiles, n_ctx // tk) if with_ctx else None
    return y_lat, y_ctx


def _log_sigmoid(x):
    return jnp.minimum(x, 0.0) - jnp.log(1.0 + jnp.exp(-jnp.abs(x)))


def _mlstm_kernel(bir_ref, bic_ref, bfr_ref, bfc_ref, ng_ref, qt_ref, k_ref, vt_ref, gic_ref, gfc_ref, gir_ref,
                  gfr_ref, ogt_ref,
                  o_ref, hf_scr, hb_scr, st_scr, ir_scr, br_scr, gc_scr, *, n_ctx_chunks, n_chunks):
    L = MLSTM_CHUNK
    hv = HEAD_DIM
    r_i = lax.broadcasted_iota(jnp.int32, (L, L), 0)
    c_i = lax.broadcasted_iota(jnp.int32, (L, L), 1)
    upper = r_i <= c_i
    lower = r_i >= c_i
    t_up = jnp.where(upper, 1.0, 0.0).astype(BF16)
    t_low = jnp.where(lower, 1.0, 0.0).astype(BF16)
    row_q = lax.broadcasted_iota(jnp.int32, (LANES, L), 0)
    sel = [jnp.where(row_q < hv, 1.0, 0.0).astype(BF16), jnp.where(row_q < hv, 0.0, 1.0).astype(BF16)]
    fwd_r = lax.broadcasted_iota(jnp.int32, (8, L), 0) < 2
    fwd_c = lax.broadcasted_iota(jnp.int32, (L, 8), 1) < 2
    st_scr[...] = jnp.zeros_like(st_scr)

    def gates(c, carry):
        r0 = pl.multiple_of(c * L, L)
        frow = _log_sigmoid(gfr_ref[0, 0, c] + bfr_ref[0])
        hi, lo = _split_bf16(frow)
        ir_scr[c] = gir_ref[0, 0, c] + bir_ref[0]
        br_scr[c] = jnp.where(fwd_r, _dot(hi, t_up) + _dot(lo, t_up), _dot(hi, t_low) + _dot(lo, t_low))
        fcol = _log_sigmoid(gfc_ref[0, 0, pl.ds(r0, L), :] + bfc_ref[0])
        hi, lo = _split_bf16(fcol)
        bcol = jnp.where(fwd_c, _dot(t_low, hi) + _dot(t_low, lo), _dot(t_up, hi) + _dot(t_up, lo))
        gc_scr[c] = gic_ref[0, 0, pl.ds(r0, L), :] + bic_ref[0] - bcol
        return carry
    lax.fori_loop(0, n_chunks, gates, 0, unroll=2)

    def chunk_step(direction, c, m_prev):
        r0 = pl.multiple_of(c * L, L)
        kc = k_ref[0, pl.ds(r0, L), :]
        qtc = qt_ref[0, 0, c]
        irow8, brow8, gcol8 = ir_scr[c], br_scr[c], gc_scr[c]
        tri = upper if direction == 0 else lower
        h_scr = hf_scr if direction == 0 else hb_scr
        m_news = []
        for a in range(2):
            ci = 2 * direction + a
            b_row, i_row, g_col = brow8[ci:ci + 1], irow8[ci:ci + 1], gcol8[:, ci:ci + 1]
            qtm = qtc * sel[a]
            s_t = _dot(kc, qtm)
            log_d = jnp.where(tri, g_col + b_row, -jnp.inf)
            m_in = b_row + m_prev[a]
            m_t = jnp.maximum(m_in, jnp.max(log_d, axis=0, keepdims=True))
            w_in = jnp.exp(m_in - m_t)
            sd = (s_t * jnp.exp(log_d - m_t)).astype(BF16)
            qw = (qtm.astype(F32) * w_in).astype(BF16)
            vt = vt_ref[0, a, c]
            state = st_scr[ci]
            numden = _dot(state.astype(BF16), qw) + _dot(vt, sd)
            h_scr[c, a * hv:(a + 1) * hv, :] = (
                numden[0:hv] / jnp.maximum(jnp.abs(numden[hv:hv + 1]), jnp.exp(-m_t)))
            b_last = jnp.min(b_row, axis=1, keepdims=True)
            log_w = b_last - b_row + i_row
            m_new = jnp.maximum(b_last + m_prev[a], jnp.max(log_w, axis=1, keepdims=True))
            vw = (vt.astype(F32) * jnp.exp(log_w - m_new)).astype(BF16)
            st_scr[ci] = jnp.exp(b_last + m_prev[a] - m_new) * state + _dot(vw, kc)
            m_news.append(m_new)
        return tuple(m_news)

    def body(j, carry):
        mf, mb = carry
        cb = jnp.where(j < n_ctx_chunks, n_ctx_chunks - 1 - j, n_chunks - 1 + n_ctx_chunks - j)
        return chunk_step(0, j, mf), chunk_step(1, cb, mb)

    z11 = jnp.zeros((1, 1), F32)
    lax.fori_loop(0, n_chunks, body, ((z11, z11), (z11, z11)), unroll=4)

    def finish(c, carry):
        r0 = pl.multiple_of(c * L, L)
        h = hf_scr[c] + hb_scr[c]
        outs = []
        for a in range(2):
            ha = h[a * hv:(a + 1) * hv]
            outs.append(ha * lax.rsqrt(jnp.mean(ha * ha, axis=0, keepdims=True) + LN_EPS))
        y_t = jnp.concatenate(outs, axis=0) * ng_ref[0] * ogt_ref[0, 0, c].astype(F32)
        o_ref[0, pl.ds(r0, L), :] = y_t.T.astype(o_ref.dtype)
        return carry
    lax.fori_loop(0, n_chunks, finish, 0, unroll=2)


def _mlstm(bias_i, bias_f, norm_g, qmt, km, vmt, gic, gfc, gir, gfr, ogt, n_ctx):
    b, pairs, nc, _, L = qmt.shape
    s = nc * L
    vrows = vmt.shape[3]
    seq = pl.BlockSpec((1, s, LANES), lambda bi, p: (bi, 0, p))
    pair_tiles = pl.BlockSpec((1, 1, nc, LANES, L), lambda bi, p: (bi, p, 0, 0, 0))
    gate_c = pl.BlockSpec((1, 1, s, 8), lambda bi, p: (bi, p, 0, 0))
    gate_r = pl.BlockSpec((1, 1, nc, 8, L), lambda bi, p: (bi, p, 0, 0, 0))
    bias_r = pl.BlockSpec((1, 8, 1), lambda bi, p: (p, 0, 0))
    bias_c = pl.BlockSpec((1, 1, 8), lambda bi, p: (p, 0, 0))
    return pl.pallas_call(
        functools.partial(_mlstm_kernel, n_ctx_chunks=n_ctx // L, n_chunks=nc),
        grid=(b, pairs),
        in_specs=[
            bias_r, bias_c, bias_r, bias_c,
            pl.BlockSpec((1, LANES, 1), lambda bi, p: (p, 0, 0)),
            pair_tiles, seq,
            pl.BlockSpec((1, 2, nc, vrows, L), lambda bi, p: (bi, p, 0, 0, 0)),
            gate_c, gate_c, gate_r, gate_r,
            pair_tiles,
        ],
        out_specs=seq,
        out_shape=jax.ShapeDtypeStruct((b, s, pairs * LANES), BF16),
        scratch_shapes=[pltpu.VMEM((nc, LANES, L), F32), pltpu.VMEM((nc, LANES, L), F32),
                        pltpu.VMEM((4, vrows, LANES), F32),
                        pltpu.VMEM((nc, 8, L), F32), pltpu.VMEM((nc, 8, L), F32), pltpu.VMEM((nc, L, 8), F32)],
        compiler_params=_params(("arbitrary", "arbitrary")),
        name="mlstm",
    )(bias_i[:, :, None], bias_i[:, None, :], bias_f[:, :, None], bias_f[:, None, :], norm_g[:, :, None],
      qmt, km, vmt, gic, gfc, gir, gfr, ogt)


def _out_kernel(x_ref, mod_ref, ya_ref, yb_ref, *refs, alpha, n_ctx_tiles):
    ym_ref, g_ref, w_ref, lng_ref, lnb_ref, o_ref = refs[-6:]
    x = x_ref[0]
    gate = mod_ref[0, 0, 2:3, :]
    g = g_ref[0].astype(F32)
    na, nb_ = ya_ref.shape[2], yb_ref.shape[2]
    yb = yb_ref[0]
    if len(refs) == 7:
        yb = jnp.where(pl.program_id(1) < n_ctx_tiles, refs[0][0], yb)
    mix_a = (ya_ref[0].astype(F32) * g[:, :na]).astype(BF16)
    mix_b = (yb.astype(F32) * g[:, na:na + nb_]).astype(BF16)
    mix_m = (ym_ref[0].astype(F32) * g[:, na + nb_:]).astype(BF16)
    y = (_dot(mix_a, w_ref[0:na, :]) + _dot(mix_b, w_ref[na:na + nb_, :])
         + _dot(mix_m, w_ref[na + nb_:, :]))
    r = alpha * x + gate * y
    mu = jnp.mean(r, axis=1, keepdims=True)
    d = r - mu
    var = jnp.mean(d * d, axis=1, keepdims=True)
    o_ref[0] = d * lax.rsqrt(var + LN_EPS) * lng_ref[...] + lnb_ref[...]


def _out_projection(xa, modsel, ya, yb_lat, yb_ctx, ym, g, w, ln_g, ln_b, alpha, n_ctx_tiles):
    b, s, d = xa.shape
    tm = ROW_TILE
    row0 = 0 if yb_ctx is not None else n_ctx_tiles
    row = lambda n: pl.BlockSpec((1, tm, n), lambda bi, i: (bi, i + row0, 0))
    vec = pl.BlockSpec((1, d), lambda bi, i: (0, 0))
    nb_ = yb_lat.shape[2]
    in_specs = [
        row(d),
        pl.BlockSpec((1, 1, 3, d), lambda bi, i: (bi, jnp.minimum(i + row0, 1), 0, 0)),
        row(ya.shape[2]),
        pl.BlockSpec((1, tm, nb_), lambda bi, i: (bi, jnp.maximum(i + row0 - n_ctx_tiles, 0), 0)),
    ]
    args = [xa, modsel, ya, yb_lat]
    if yb_ctx is not None:
        in_specs.append(pl.BlockSpec((1, tm, nb_), lambda bi, i: (bi, jnp.minimum(i, n_ctx_tiles - 1), 0)))
        args.append(yb_ctx)
    in_specs += [row(ym.shape[2]), row(d), pl.BlockSpec(w.shape, lambda bi, i: (0, 0)), vec, vec]
    args += [ym, g, w, ln_g, ln_b]
    return pl.pallas_call(
        functools.partial(_out_kernel, alpha=alpha, n_ctx_tiles=n_ctx_tiles),
        grid=(b, s // tm - row0),
        in_specs=in_specs,
        out_specs=pl.BlockSpec((1, tm, d), lambda bi, i: (bi, i, 0)),
        out_shape=jax.ShapeDtypeStruct((b, s - row0 * tm, d), F32),
        compiler_params=_params(("arbitrary", "arbitrary")),
        name="out_projection",
    )(*args)


def _in_weight_columns(d_in):
    z = d_in
    src = {}
    off = 0
    for name, n in (("qa", 256), ("ka", 128), ("va", 128), ("za", 256), ("qb", 384), ("kb", 384), ("vb", 384),
                    ("zb", 384), ("qm", 384), ("km", 384), ("vm", 384), ("om", 384), ("zm", 384), ("gm", 24)):
        src[name] = np.arange(off, off + n)
        off += n
    assert off == d_in
    cols = []
    for hd in range(4):
        blk = np.full(LANES, z)
        gq = hd // 2
        blk[64 * gq:64 * gq + 64] = src["qa"][64 * hd:64 * hd + 64]
        cols.append(blk)
    for name in ("ka", "va", "qb", "kb", "vb", "qm", "km", "vm", "om", "za", "zb", "zm"):
        cols.append(src[name])
    heads = 6
    for p in range(heads // 2):
        for gate in range(2):
            cols.append(np.array([src["gm"][di * 2 * heads + gate * heads + 2 * p + a]
                                  for di in range(2) for a in range(2)] + [z] * 4))
    cols.append(np.full(LANES - 16 * (heads // 2), z))
    cols = np.concatenate(cols)
    assert cols.shape[0] == _W_COLS
    return cols


def _permute_columns(w, cols):
    n = w.shape[-1]
    pieces, start = [], 0
    breaks = np.flatnonzero(np.diff(cols) != 1) + 1
    for stop in list(breaks) + [len(cols)]:
        run = cols[start:stop]
        if len(run) >= 32:
            pieces.append(lax.slice_in_dim(w, int(run[0]), int(run[-1]) + 1, axis=w.ndim - 1))
        elif pieces and isinstance(pieces[-1], list):
            pieces[-1].extend(run)
        else:
            pieces.append(list(run))
        start = stop
    lo = min(min(p) for p in pieces if isinstance(p, list))
    tail = jnp.concatenate([lax.slice_in_dim(w, lo, n, axis=w.ndim - 1), jnp.zeros(w.shape[:-1] + (1,), w.dtype)], -1)
    pieces = [jnp.take(tail, np.array(p) - lo, axis=-1) if isinstance(p, list) else p for p in pieces]
    return jnp.concatenate(pieces, axis=-1)


def _rope_table(n_tokens, n_ctx, dim):
    f32 = np.float32
    rows = n_tokens // GRID_W
    rowp = np.broadcast_to(np.arange(rows, dtype=f32)[:, None], (rows, GRID_W)).reshape(-1)
    colp = np.broadcast_to(np.arange(GRID_W, dtype=f32)[None, :], (rows, GRID_W)).reshape(-1)
    n_freq = dim // 4
    inv = np.power(f32(ROPE_BASE), -np.arange(n_freq, dtype=f32) / f32(n_freq)).astype(f32)
    ar = rowp[:, None] * inv
    ac = colp[:, None] * inv
    ang = np.concatenate([ar, ar, ac, ac], -1).astype(f32)
    cos, sin = np.cos(ang), np.sin(ang)
    odd = (np.arange(dim) // n_freq) % 2 == 1
    sin_p = np.where(odd, sin, f32(0))
    sin_m = np.where(odd, f32(0), -sin)
    tab = np.stack([cos, sin_p, sin_m])
    ident = np.stack([np.ones((n_ctx, dim), f32), np.zeros((n_ctx, dim), f32), np.zeros((n_ctx, dim), f32)])
    tab = np.concatenate([ident, tab], axis=1)
    return jnp.asarray(np.tile(tab, (1, 1, LANES // dim)).astype(f32))


def kernel(x, c, ctx, c_ctx, w_mod, b_mod, w_in, attn_sink, diff_lambda, diff_norm_g, mlstm_i_bias,
           mlstm_f_bias, mlstm_norm_g, w_out, ln_g, ln_b):
    b, t, d = x.shape
    n_ctx = ctx.shape[1]
    depth = w_mod.shape[0]
    d_in = w_in.shape[2]
    alpha = (2 * depth) ** 0.25

    xa = jnp.concatenate([ctx, x], axis=1)
    rope_a = _rope_table(t, n_ctx, HEAD_DIM)
    rope_b = _rope_table(t, n_ctx, B_QK_DIM)

    rows = -(-(b + 1) // 8) * 8
    cc = jnp.concatenate([c, c_ctx[None, :], jnp.zeros((rows - b - 1, d), F32)], axis=0)
    mod = _modulation(cc, w_mod, b_mod).reshape(depth, rows, 3, d)

    w_in_p = _permute_columns(w_in, _in_weight_columns(d_in)).astype(BF16)
    w_out_b = w_out.astype(BF16)
    norm_b = diff_norm_g[:, :, None]

    n_pairs = mlstm_i_bias.shape[2] // 2
    heads = mlstm_i_bias.shape[2]
    chains = [(di, a) for di in range(2) for a in range(2)]
    bias_idx = np.array([[di * heads + 2 * p + a for di, a in chains] for p in range(n_pairs)])
    pad8 = lambda v: jnp.pad(v.reshape(-1)[bias_idx], ((0, 0), (0, 4)))

    for l in range(depth):
        with_ctx = l < depth - 1
        lam_init = 0.8 - 0.6 * math.exp(-0.3 * l)
        modsel = jnp.stack([jnp.broadcast_to(mod[l, b], (b, 3, d)), mod[l, :b]], axis=1)
        (qa, ka, va, qb, kb, vb, qm, km, vm, og, g, gic, gfc, gir, gfr) = _in_projection(
            xa, modsel, w_in_p[l], rope_a, rope_b)

        ya = _attn_a(attn_sink[l], qa, ka, va, n_ctx, with_ctx)
        yb, yb_ctx = _attn_b(jnp.full((1,), lam_init, F32), diff_lambda[l], norm_b[l], qb, kb, vb, n_ctx, with_ctx)

        ym = _mlstm(pad8(mlstm_i_bias[l]), pad8(mlstm_f_bias[l]), mlstm_norm_g[l].reshape(n_pairs, LANES),
                    qm, km, vm, gic, gfc, gir, gfr, og, n_ctx)

        xa = _out_projection(xa, modsel, ya, yb, yb_ctx, ym, g, w_out_b[l], ln_g[l][None, :], ln_b[l][None, :],
                             alpha, n_ctx // ROW_TILE)
    return xa
```

```python
import functools
import math

import numpy as np
import jax
import jax.numpy as jnp
from jax import lax
from jax.experimental import pallas as pl
from jax.experimental.pallas import tpu as pltpu

F32 = jnp.float32
BF16 = jnp.bfloat16

LANES = 128
GRID_W = 64
HEAD_DIM = 64
ROPE_BASE = 10000.0
LN_EPS = 1e-5
WINDOW = 128
B_QK_DIM = 32
LOG2E = math.log2(math.e)
VB_ROWS = HEAD_DIM + 16
MLSTM_CHUNK = 128
GATE_PAD = 8
MOD_COL_TILE = 1024
B_KEY_GROUP = 2
ROW_TILE = 256
VMEM_LIMIT = 56 * 1024 * 1024


def _dot(a, b):
    return jnp.dot(a, b, preferred_element_type=F32)


def _dot_nt(a, b):
    return lax.dot_general(a, b, (((1,), (1,)), ((), ())), preferred_element_type=F32)


def _split_bf16(a):
    hi = a.astype(BF16)
    lo = (a - hi.astype(F32)).astype(BF16)
    return hi, lo


def _sigmoid(x):
    return 1.0 / (1.0 + jnp.exp(-x))


def _params(sem):
    return pltpu.CompilerParams(dimension_semantics=sem, vmem_limit_bytes=VMEM_LIMIT)


def _mod_kernel(c_ref, w_ref, b_ref, o_ref):
    c = c_ref[...]
    a = c * _sigmoid(c)
    a_hi, a_lo = _split_bf16(a)
    w_hi, w_lo = _split_bf16(w_ref[0])
    o_ref[0] = _dot(a_hi, w_hi) + _dot(a_lo, w_hi) + _dot(a_hi, w_lo) + b_ref[0]


def _modulation(cc, w_mod, b_mod):
    depth, d, n = w_mod.shape
    r = cc.shape[0]
    tn = MOD_COL_TILE
    return pl.pallas_call(
        _mod_kernel,
        grid=(depth, n // tn),
        in_specs=[
            pl.BlockSpec((r, d), lambda l, j: (0, 0)),
            pl.BlockSpec((1, d, tn), lambda l, j: (l, 0, j)),
            pl.BlockSpec((1, 1, tn), lambda l, j: (l, 0, j)),
        ],
        out_specs=pl.BlockSpec((1, r, tn), lambda l, j: (l, 0, j)),
        out_shape=jax.ShapeDtypeStruct((depth, r, n), F32),
        compiler_params=_params(("arbitrary", "arbitrary")),
        name="modulation",
    )(cc, w_mod, b_mod.reshape(depth, 1, n))


_QA, _KA, _VA = (0, 512), (512, 640), (640, 768)
_QB, _KB, _VB = (768, 1152), (1152, 1536), (1536, 1920)
_QM, _KM, _VM = (1920, 2304), (2304, 2688), (2688, 3072)
_OM, _Z, _GM = (3072, 3456), (3456, 4480), (4480, 4608)
_W_COLS = 4608


def _rope(t, tab_ref, quarter):
    cos, sin_p, sin_m = tab_ref[0], tab_ref[1], tab_ref[2]
    outs = []
    for j in range(t.shape[1] // LANES):
        tj = t[:, j * LANES:(j + 1) * LANES]
        outs.append(tj * cos + pltpu.roll(tj, quarter, 1) * sin_p + pltpu.roll(tj, LANES - quarter, 1) * sin_m)
    return outs[0] if len(outs) == 1 else jnp.concatenate(outs, axis=1)


def _inproj_kernel(x_ref, mod_ref, w_ref, ra_ref, rb_ref,
                   qa_ref, ka_ref, va_ref, qb_ref, kb_ref, vb_ref,
                   qm_ref, km_ref, vm_ref, om_ref, g_ref, gic_ref, gfc_ref, gir_ref, gfr_ref):
    x = x_ref[0]
    shift = mod_ref[0, 0, 0:1, :]
    scale = mod_ref[0, 0, 1:2, :]
    h = (x * (1.0 + scale) + shift).astype(BF16)

    def proj(cols):
        return _dot(h, w_ref[:, cols[0]:cols[1]])

    qa_ref[0] = (_rope(proj(_QA), ra_ref, HEAD_DIM // 4) * (HEAD_DIM ** -0.5)).astype(BF16)
    ka_ref[0] = _rope(proj(_KA), ra_ref, HEAD_DIM // 4).astype(BF16)
    va_ref[0] = proj(_VA).astype(BF16)
    qb_t = (_rope(proj(_QB), rb_ref, B_QK_DIM // 4) * (B_QK_DIM ** -0.5 * LOG2E)).T.astype(BF16)
    for p in range(qb_t.shape[0] // LANES):
        qb_ref[0, p, 0] = qb_t[p * LANES:(p + 1) * LANES]
    vb_t = proj(_VB).T
    ones_rows = jnp.where(lax.broadcasted_iota(jnp.int32, (VB_ROWS - HEAD_DIM, vb_t.shape[1]), 0) == 0, 1.0, 0.0)
    for hd in range(vb_t.shape[0] // HEAD_DIM):
        vb_ref[0, hd, 0] = jnp.concatenate([vb_t[hd * HEAD_DIM:(hd + 1) * HEAD_DIM], ones_rows], axis=0).astype(BF16)
    kb_ref[0] = _rope(proj(_KB), rb_ref, B_QK_DIM // 4).astype(BF16)
    km_ref[0] = (proj(_KM) * (HEAD_DIM ** -0.5)).astype(BF16)
    qm_t = proj(_QM).T.astype(BF16)
    om_t = _sigmoid(proj(_OM)).T.astype(BF16)
    vm_t = proj(_VM).T
    L = MLSTM_CHUNK
    ones_rows_c = jnp.where(lax.broadcasted_iota(jnp.int32, (VB_ROWS - HEAD_DIM, L), 0) == 0, 1.0, 0.0)
    for c in range(qm_t.shape[1] // L):
        cols = slice(c * L, (c + 1) * L)
        for p in range(qm_t.shape[0] // LANES):
            qm_ref[0, p, c] = qm_t[p * LANES:(p + 1) * LANES, cols]
            om_ref[0, p, c] = om_t[p * LANES:(p + 1) * LANES, cols]
        for hd in range(vm_t.shape[0] // HEAD_DIM):
            vm_ref[0, hd, c] = jnp.concatenate(
                [vm_t[hd * HEAD_DIM:(hd + 1) * HEAD_DIM, cols], ones_rows_c], axis=0).astype(BF16)
    z = proj(_Z)
    g_ref[0] = (z * _sigmoid(z)).astype(BF16)
    gm = proj(_GM)
    gm_t = gm.T
    for p in range(gic_ref.shape[1]):
        lo, hi = 2 * GATE_PAD * p, 2 * GATE_PAD * p + GATE_PAD
        gic_ref[0, p] = gm[:, lo:hi]
        gfc_ref[0, p] = gm[:, hi:hi + GATE_PAD]
        for c in range(gm_t.shape[1] // L):
            gir_ref[0, p, c] = gm_t[lo:hi, c * L:(c + 1) * L]
            gfr_ref[0, p, c] = gm_t[hi:hi + GATE_PAD, c * L:(c + 1) * L]


def _in_projection(xa, modsel, w, rope_a, rope_b):
    b, s, d = xa.shape
    tm = ROW_TILE
    n_pairs = (_QB[1] - _QB[0]) // LANES
    n_heads = (_VB[1] - _VB[0]) // HEAD_DIM
    L = MLSTM_CHUNK
    widths = [512, 128, 128, (n_pairs, LANES, tm), 384, (n_heads, VB_ROWS, tm),
              (n_pairs, LANES, L), 384, (n_heads, VB_ROWS, L), (n_pairs, LANES, L), 1024]
    tposed = lambda g, r, tt: pl.BlockSpec((1, g, tm // tt, r, tt), lambda bi, i: (bi, 0, i, 0, 0))
    row = lambda n: tposed(*n) if isinstance(n, tuple) else pl.BlockSpec((1, tm, n), lambda bi, i: (bi, i, 0))
    out_shape = [jax.ShapeDtypeStruct((b, n[0], s // n[2], n[1], n[2]) if isinstance(n, tuple) else (b, s, n), BF16)
                 for n in widths]
    gate_c = pl.BlockSpec((1, n_pairs, tm, GATE_PAD), lambda bi, i: (bi, 0, i, 0))
    gate_r = pl.BlockSpec((1, n_pairs, tm // L, GATE_PAD, L), lambda bi, i: (bi, 0, i, 0, 0))
    out_shape += [jax.ShapeDtypeStruct((b, n_pairs, s, GATE_PAD), F32)] * 2
    out_shape += [jax.ShapeDtypeStruct((b, n_pairs, s // L, GATE_PAD, L), F32)] * 2
    return pl.pallas_call(
        _inproj_kernel,
        grid=(b, s // tm),
        in_specs=[
            row(d),
            pl.BlockSpec((1, 1, 3, d), lambda bi, i: (bi, jnp.minimum(i, 1), 0, 0)),
            pl.BlockSpec((d, _W_COLS), lambda bi, i: (0, 0)),
            pl.BlockSpec((3, tm, LANES), lambda bi, i: (0, i, 0)),
            pl.BlockSpec((3, tm, LANES), lambda bi, i: (0, i, 0)),
        ],
        out_specs=[row(n) for n in widths] + [gate_c, gate_c, gate_r, gate_r],
        out_shape=out_shape,
        compiler_params=_params(("arbitrary", "arbitrary")),
        name="in_projection",
    )(xa, modsel, w, rope_a, rope_b)


def _attn_a_kernel(sink_ref, q_ref, k_ref, v_ref, o_ref, *, n_ctx, n_blocks):
    w = WINDOW
    s_len = n_blocks * w
    half = LANES // 2
    rows1 = lax.broadcasted_iota(jnp.int32, (4 * w, 1), 0)
    sink = jnp.where(rows1 < w, sink_ref[0],
                     jnp.where(rows1 < 2 * w, sink_ref[1], jnp.where(rows1 < 3 * w, sink_ref[2], sink_ref[3])))
    row = lax.broadcasted_iota(jnp.int32, (4 * w, 3 * w), 0) & (w - 1)
    col = lax.broadcasted_iota(jnp.int32, (4 * w, 3 * w), 1)
    lane = lax.broadcasted_iota(jnp.int32, (w, LANES), 1)
    k_ctx = k_ref[0, 0:n_ctx, :]
    v_ctx = v_ref[0, 0:n_ctx, :]

    def body(i, carry):
        r0 = pl.multiple_of(i * w, w)
        q = q_ref[0, pl.ds(r0, w), :]
        qs = jnp.concatenate([q[:, j * LANES:(j + 1) * LANES] for j in range(4)], axis=0)
        start = pl.multiple_of(jnp.clip(r0 - w, 0, s_len - 3 * w), w)
        s_loc = _dot_nt(qs, k_ref[0, pl.ds(start, 3 * w), :])
        s_ctx = _dot_nt(qs, k_ctx)
        kpos = col + start
        rel = kpos - r0 - row
        ok = (jnp.abs(rel) <= w) & (kpos >= jnp.where(r0 >= n_ctx, n_ctx, s_len))
        s_loc = jnp.where(ok, s_loc, -jnp.inf)
        m = jnp.maximum(jnp.maximum(jnp.max(s_loc, axis=1, keepdims=True),
                                    jnp.max(s_ctx, axis=1, keepdims=True)), sink)
        p_loc = jnp.exp(s_loc - m)
        p_ctx = jnp.exp(s_ctx - m)
        den = (jnp.sum(p_loc, axis=1, keepdims=True) + jnp.sum(p_ctx, axis=1, keepdims=True)
               + jnp.exp(sink - m))
        o = (_dot(p_loc.astype(BF16), v_ref[0, pl.ds(start, 3 * w), :])
             + _dot(p_ctx.astype(BF16), v_ctx)) / den
        left = jnp.where(lane < half, o[0:w], pltpu.roll(o[w:2 * w], half, 1))
        right = jnp.where(lane < half, pltpu.roll(o[2 * w:3 * w], half, 1), o[3 * w:])
        o_ref[0, pl.ds(r0, w), :] = jnp.concatenate([left, right], axis=1).astype(o_ref.dtype)
        return carry

    lax.fori_loop(0, n_blocks, body, 0, unroll=2)


def _attn_a(sink, qa, ka, va, n_ctx):
    b, s, _ = qa.shape
    seq = lambda n: pl.BlockSpec((1, s, n), lambda bi: (bi, 0, 0))
    return pl.pallas_call(
        functools.partial(_attn_a_kernel, n_ctx=n_ctx, n_blocks=s // WINDOW),
        grid=(b,),
        in_specs=[pl.BlockSpec(memory_space=pltpu.SMEM), seq(4 * LANES), seq(LANES), seq(LANES)],
        out_specs=seq(2 * LANES),
        out_shape=jax.ShapeDtypeStruct((b, s, 2 * LANES), BF16),
        compiler_params=_params(("arbitrary",)),
        name="window_attention",
    )(sink, qa, ka, va)


def _attn_b_kernel(li_ref, lam_ref, ng_ref, qt_ref, k_ref, vt_ref, o_ref,
                   qs_scr, sa_scr, sb_scr, sc_scr, mt_scr, acc_scr, *, nt, q0, nq, kg):
    tq, tk = qt_ref.shape[-1], vt_ref.shape[-1]
    hv = HEAD_DIM
    lam_init = li_ref[0]
    lv = lam_ref[...]
    lam = (jnp.exp(jnp.sum(lv[0:1] * lv[1:2], axis=1, keepdims=True))
           - jnp.exp(jnp.sum(lv[2:3] * lv[3:4], axis=1, keepdims=True)) + lam_init)

    def first_tile(i, slot):
        qt = qt_ref[0, 0, q0 + i]
        unit = lax.broadcasted_iota(jnp.int32, qt.shape, 0) // B_QK_DIM
        zero = jnp.zeros_like(qt)
        qs_scr[slot] = jnp.concatenate([jnp.where(unit == u, qt, zero) for u in range(4)], axis=1)
        mt_scr[...] = fetch(0, sc_scr, slot)

    def fetch(t, s_scr, slot, n=1):
        off = pl.multiple_of(t * tk, tk)
        s_new = _dot(k_ref[0, pl.ds(off, n * tk), :], qs_scr[slot])
        s_scr[...] = s_new
        return jnp.max(s_new, axis=0, keepdims=True)

    def consume(s_scr, mt, m, t, n=1):
        m_new = jnp.maximum(m, mt)
        alpha = jnp.exp2(m - m_new)
        p = jnp.exp2(s_scr[...] - m_new).astype(BF16)
        for hd in range(2):
            cols = slice(hd * 2 * tq, (hd + 1) * 2 * tq)
            pv = _dot(vt_ref[0, hd, t], p[0:tk, cols])
            for c in range(1, n):
                pv = pv + _dot(vt_ref[0, hd, t + c], p[c * tk:(c + 1) * tk, cols])
            acc_scr[hd] = alpha[:, cols] * acc_scr[hd] + pv
        return m_new

    def head_out(hd):
        acc = acc_scr[hd]
        o = acc[0:hv] * (1.0 / acc[hv:hv + 1])
        od = o[:, 0:tq] - lam * o[:, tq:]
        ms = jnp.mean(od * od, axis=0, keepdims=True)
        return od * lax.rsqrt(ms + LN_EPS) * ng_ref[...]

    def query_tile(i, carry):
        cur = i % 2

        def successor():
            if nq > 1:
                first_tile(jnp.minimum(i + 1, nq - 1), 1 - cur)

        acc_scr[...] = jnp.zeros_like(acc_scr)
        m = jnp.full((1, 4 * tq), -jnp.inf, F32)
        mt_c = mt_scr[...]

        nb = (nt - 1) // kg
        big = lambda j: 1 + kg * j

        def body(tt, carry):
            m, mt_a = carry
            j0 = 2 * tt
            mt_b = fetch(big(j0 + 1), sb_scr, cur, kg)
            m = consume(sa_scr, mt_a, m, big(j0), kg)
            mt_a = fetch(big(j0 + 2), sa_scr, cur, kg)
            m = consume(sb_scr, mt_b, m, big(j0 + 1), kg)
            return m, mt_a

        if nb == 0:
            successor()
            consume(sc_scr, mt_c, m, 0)
        else:
            mt_a = fetch(big(0), sa_scr, cur, kg)
            m = consume(sc_scr, mt_c, m, 0)
            m, mt_a = lax.fori_loop(0, (nb - 1) // 2, body, (m, mt_a))
            if nb % 2 == 0:
                mt_b = fetch(big(nb - 1), sb_scr, cur, kg)
                m = consume(sa_scr, mt_a, m, big(nb - 2), kg)
                successor()
                consume(sb_scr, mt_b, m, big(nb - 1), kg)
            else:
                successor()
                consume(sa_scr, mt_a, m, big(nb - 1), kg)

        y_t = jnp.concatenate([head_out(0), head_out(1)], axis=0) * (1.0 - lam_init)
        o_ref[0, pl.ds(pl.multiple_of(i * tq, tq), tq), :] = y_t.T.astype(o_ref.dtype)
        return carry

    first_tile(0, 0)
    lax.fori_loop(0, nq, query_tile, 0)


def _attn_b_call(lam_init, lam_vec, norm_g, qbt, kb, vbt, q0, nq, nt):
    b, pairs, n_tiles, _, tq = qbt.shape
    vrows, tk = vbt.shape[-2:]
    kg = B_KEY_GROUP if (nt - 1) % B_KEY_GROUP == 0 else 1
    return pl.pallas_call(
        functools.partial(_attn_b_kernel, nt=nt, q0=q0, nq=nq, kg=kg),
        grid=(b, pairs),
        in_specs=[
            pl.BlockSpec(memory_space=pltpu.SMEM),
            pl.BlockSpec((4, B_QK_DIM), lambda bi, p: (0, 0)),
            pl.BlockSpec((HEAD_DIM, 1), lambda bi, p: (0, 0)),
            pl.BlockSpec((1, 1, n_tiles, LANES, tq), lambda bi, p: (bi, p, 0, 0, 0)),
            pl.BlockSpec((1, nt * tk, LANES), lambda bi, p: (bi, 0, p)),
            pl.BlockSpec((1, 2, nt, vrows, tk), lambda bi, p: (bi, p, 0, 0, 0)),
        ],
        out_specs=pl.BlockSpec((1, nq * tq, LANES), lambda bi, p: (bi, 0, p)),
        out_shape=jax.ShapeDtypeStruct((b, nq * tq, pairs * LANES), BF16),
        scratch_shapes=[pltpu.VMEM((2, LANES, 4 * tq), BF16), pltpu.VMEM((kg * tk, 4 * tq), F32),
                        pltpu.VMEM((kg * tk, 4 * tq), F32), pltpu.VMEM((tk, 4 * tq), F32),
                        pltpu.VMEM((1, 4 * tq), F32), pltpu.VMEM((2, vrows, 2 * tq), F32)],
        compiler_params=_params(("arbitrary", "arbitrary")),
        name="diff_attention",
    )(lam_init, lam_vec, norm_g, qbt, kb, vbt)


def _attn_b(lam_init, lam_vec, norm_g, qbt, kb, vbt, n_ctx, with_ctx):
    n_tiles, tq = qbt.shape[2], qbt.shape[4]
    tk = vbt.shape[-1]
    n_ctx_tiles = n_ctx // tq
    y_lat = _attn_b_call(lam_init, lam_vec, norm_g, qbt, kb, vbt, n_ctx_tiles, n_tiles - n_ctx_tiles, n_tiles)
    y_ctx = _attn_b_call(lam_init, lam_vec, norm_g, qbt, kb, vbt, 0, n_ctx_tiles, n_ctx // tk) if with_ctx else None
    return y_lat, y_ctx


def _log_sigmoid(x):
    return jnp.minimum(x, 0.0) - jnp.log(1.0 + jnp.exp(-jnp.abs(x)))


def _mlstm_kernel(bir_ref, bic_ref, bfr_ref, bfc_ref, ng_ref, qt_ref, k_ref, vt_ref, gic_ref, gfc_ref, gir_ref,
                  gfr_ref, ogt_ref,
                  o_ref, hf_scr, hb_scr, st_scr, ir_scr, br_scr, gc_scr, *, n_ctx_chunks, n_chunks):
    L = MLSTM_CHUNK
    hv = HEAD_DIM
    r_i = lax.broadcasted_iota(jnp.int32, (L, L), 0)
    c_i = lax.broadcasted_iota(jnp.int32, (L, L), 1)
    upper = r_i <= c_i
    lower = r_i >= c_i
    t_up = jnp.where(upper, 1.0, 0.0).astype(BF16)
    t_low = jnp.where(lower, 1.0, 0.0).astype(BF16)
    row_q = lax.broadcasted_iota(jnp.int32, (LANES, L), 0)
    sel = [jnp.where(row_q < hv, 1.0, 0.0).astype(BF16), jnp.where(row_q < hv, 0.0, 1.0).astype(BF16)]
    fwd_r = lax.broadcasted_iota(jnp.int32, (GATE_PAD, L), 0) < 2
    fwd_c = lax.broadcasted_iota(jnp.int32, (L, GATE_PAD), 1) < 2
    st_scr[...] = jnp.zeros_like(st_scr)

    def gates(c, carry):
        r0 = pl.multiple_of(c * L, L)
        frow = _log_sigmoid(gfr_ref[0, 0, c] + bfr_ref[0])
        hi, lo = _split_bf16(frow)
        ir_scr[c] = gir_ref[0, 0, c] + bir_ref[0]
        br_scr[c] = jnp.where(fwd_r, _dot(hi, t_up) + _dot(lo, t_up), _dot(hi, t_low) + _dot(lo, t_low))
        fcol = _log_sigmoid(gfc_ref[0, 0, pl.ds(r0, L), :] + bfc_ref[0])
        hi, lo = _split_bf16(fcol)
        bcol = jnp.where(fwd_c, _dot(t_low, hi) + _dot(t_low, lo), _dot(t_up, hi) + _dot(t_up, lo))
        gc_scr[c] = gic_ref[0, 0, pl.ds(r0, L), :] + bic_ref[0] - bcol
        return carry
    lax.fori_loop(0, n_chunks, gates, 0, unroll=2)

    def chunk_step(direction, c, m_prev):
        r0 = pl.multiple_of(c * L, L)
        kc = k_ref[0, pl.ds(r0, L), :]
        qtc = qt_ref[0, 0, c]
        irow8, brow8, gcol8 = ir_scr[c], br_scr[c], gc_scr[c]
        tri = upper if direction == 0 else lower
        h_scr = hf_scr if direction == 0 else hb_scr
        m_news = []
        for a in range(2):
            ci = 2 * direction + a
            b_row, i_row, g_col = brow8[ci:ci + 1], irow8[ci:ci + 1], gcol8[:, ci:ci + 1]
            qtm = qtc * sel[a]
            s_t = _dot(kc, qtm)
            log_d = jnp.where(tri, g_col + b_row, -jnp.inf)
            m_in = b_row + m_prev[a]
            m_t = jnp.maximum(m_in, jnp.max(log_d, axis=0, keepdims=True))
            w_in = jnp.exp(m_in - m_t)
            sd = (s_t * jnp.exp(log_d - m_t)).astype(BF16)
            qw = (qtm.astype(F32) * w_in).astype(BF16)
            vt = vt_ref[0, a, c]
            state = st_scr[ci]
            numden = _dot(state.astype(BF16), qw) + _dot(vt, sd)
            h_scr[c, a * hv:(a + 1) * hv, :] = (
                numden[0:hv] / jnp.maximum(jnp.abs(numden[hv:hv + 1]), jnp.exp(-m_t)))
            b_last = jnp.min(b_row, axis=1, keepdims=True)
            log_w = b_last - b_row + i_row
            m_new = jnp.maximum(b_last + m_prev[a], jnp.max(log_w, axis=1, keepdims=True))
            vw = (vt.astype(F32) * jnp.exp(log_w - m_new)).astype(BF16)
            st_scr[ci] = jnp.exp(b_last + m_prev[a] - m_new) * state + _dot(vw, kc)
            m_news.append(m_new)
        return tuple(m_news)

    def body(j, carry):
        mf, mb = carry
        cb = jnp.where(j < n_ctx_chunks, n_ctx_chunks - 1 - j, n_chunks - 1 + n_ctx_chunks - j)
        return chunk_step(0, j, mf), chunk_step(1, cb, mb)

    z11 = jnp.zeros((1, 1), F32)
    lax.fori_loop(0, n_chunks, body, ((z11, z11), (z11, z11)), unroll=4)

    def finish(c, carry):
        r0 = pl.multiple_of(c * L, L)
        h = hf_scr[c] + hb_scr[c]
        outs = []
        for a in range(2):
            ha = h[a * hv:(a + 1) * hv]
            outs.append(ha * lax.rsqrt(jnp.mean(ha * ha, axis=0, keepdims=True) + LN_EPS))
        y_t = jnp.concatenate(outs, axis=0) * ng_ref[0] * ogt_ref[0, 0, c].astype(F32)
        o_ref[0, pl.ds(r0, L), :] = y_t.T.astype(o_ref.dtype)
        return carry
    lax.fori_loop(0, n_chunks, finish, 0, unroll=2)


def _mlstm(bias_i, bias_f, norm_g, qmt, km, vmt, gic, gfc, gir, gfr, ogt, n_ctx):
    b, pairs, nc, _, L = qmt.shape
    s = nc * L
    vrows = vmt.shape[3]
    seq = pl.BlockSpec((1, s, LANES), lambda bi, p: (bi, 0, p))
    pair_tiles = pl.BlockSpec((1, 1, nc, LANES, L), lambda bi, p: (bi, p, 0, 0, 0))
    gate_c = pl.BlockSpec((1, 1, s, GATE_PAD), lambda bi, p: (bi, p, 0, 0))
    gate_r = pl.BlockSpec((1, 1, nc, GATE_PAD, L), lambda bi, p: (bi, p, 0, 0, 0))
    bias_r = pl.BlockSpec((1, GATE_PAD, 1), lambda bi, p: (p, 0, 0))
    bias_c = pl.BlockSpec((1, 1, GATE_PAD), lambda bi, p: (p, 0, 0))
    return pl.pallas_call(
        functools.partial(_mlstm_kernel, n_ctx_chunks=n_ctx // L, n_chunks=nc),
        grid=(b, pairs),
        in_specs=[
            bias_r, bias_c, bias_r, bias_c,
            pl.BlockSpec((1, LANES, 1), lambda bi, p: (p, 0, 0)),
            pair_tiles, seq,
            pl.BlockSpec((1, 2, nc, vrows, L), lambda bi, p: (bi, p, 0, 0, 0)),
            gate_c, gate_c, gate_r, gate_r,
            pair_tiles,
        ],
        out_specs=seq,
        out_shape=jax.ShapeDtypeStruct((b, s, pairs * LANES), BF16),
        scratch_shapes=[pltpu.VMEM((nc, LANES, L), F32), pltpu.VMEM((nc, LANES, L), F32),
                        pltpu.VMEM((4, vrows, LANES), F32),
                        pltpu.VMEM((nc, GATE_PAD, L), F32), pltpu.VMEM((nc, GATE_PAD, L), F32),
                        pltpu.VMEM((nc, L, GATE_PAD), F32)],
        compiler_params=_params(("arbitrary", "arbitrary")),
        name="mlstm",
    )(bias_i[:, :, None], bias_i[:, None, :], bias_f[:, :, None], bias_f[:, None, :], norm_g[:, :, None],
      qmt, km, vmt, gic, gfc, gir, gfr, ogt)


def _out_kernel(x_ref, mod_ref, ya_ref, yb_ref, *refs, alpha, n_ctx_tiles):
    ym_ref, g_ref, w_ref, lng_ref, lnb_ref, o_ref = refs[-6:]
    x = x_ref[0]
    gate = mod_ref[0, 0, 2:3, :]
    g = g_ref[0].astype(F32)
    na, nb_ = ya_ref.shape[2], yb_ref.shape[2]
    yb = yb_ref[0]
    if len(refs) == 7:
        yb = jnp.where(pl.program_id(1) < n_ctx_tiles, refs[0][0], yb)
    mix_a = (ya_ref[0].astype(F32) * g[:, :na]).astype(BF16)
    mix_b = (yb.astype(F32) * g[:, na:na + nb_]).astype(BF16)
    mix_m = (ym_ref[0].astype(F32) * g[:, na + nb_:]).astype(BF16)
    y = (_dot(mix_a, w_ref[0:na, :]) + _dot(mix_b, w_ref[na:na + nb_, :])
         + _dot(mix_m, w_ref[na + nb_:, :]))
    r = alpha * x + gate * y
    mu = jnp.mean(r, axis=1, keepdims=True)
    d = r - mu
    var = jnp.mean(d * d, axis=1, keepdims=True)
    o_ref[0] = d * lax.rsqrt(var + LN_EPS) * lng_ref[...] + lnb_ref[...]


def _out_projection(xa, modsel, ya, yb_lat, yb_ctx, ym, g, w, ln_g, ln_b, alpha, n_ctx_tiles):
    b, s, d = xa.shape
    tm = ROW_TILE
    row0 = 0 if yb_ctx is not None else n_ctx_tiles
    row = lambda n: pl.BlockSpec((1, tm, n), lambda bi, i: (bi, i + row0, 0))
    vec = pl.BlockSpec((1, d), lambda bi, i: (0, 0))
    nb_ = yb_lat.shape[2]
    in_specs = [
        row(d),
        pl.BlockSpec((1, 1, 3, d), lambda bi, i: (bi, jnp.minimum(i + row0, 1), 0, 0)),
        row(ya.shape[2]),
        pl.BlockSpec((1, tm, nb_), lambda bi, i: (bi, jnp.maximum(i + row0 - n_ctx_tiles, 0), 0)),
    ]
    args = [xa, modsel, ya, yb_lat]
    if yb_ctx is not None:
        in_specs.append(pl.BlockSpec((1, tm, nb_), lambda bi, i: (bi, jnp.minimum(i, n_ctx_tiles - 1), 0)))
        args.append(yb_ctx)
    in_specs += [row(ym.shape[2]), row(d), pl.BlockSpec(w.shape, lambda bi, i: (0, 0)), vec, vec]
    args += [ym, g, w, ln_g, ln_b]
    return pl.pallas_call(
        functools.partial(_out_kernel, alpha=alpha, n_ctx_tiles=n_ctx_tiles),
        grid=(b, s // tm - row0),
        in_specs=in_specs,
        out_specs=pl.BlockSpec((1, tm, d), lambda bi, i: (bi, i, 0)),
        out_shape=jax.ShapeDtypeStruct((b, s - row0 * tm, d), F32),
        compiler_params=_params(("arbitrary", "arbitrary")),
        name="out_projection",
    )(*args)


def _in_weight_columns(d_in):
    z = d_in
    src = {}
    off = 0
    for name, n in (("qa", 256), ("ka", 128), ("va", 128), ("za", 256), ("qb", 384), ("kb", 384), ("vb", 384),
                    ("zb", 384), ("qm", 384), ("km", 384), ("vm", 384), ("om", 384), ("zm", 384), ("gm", 24)):
        src[name] = np.arange(off, off + n)
        off += n
    assert off == d_in
    cols = []
    for hd in range(4):
        blk = np.full(LANES, z)
        gq = hd // 2
        blk[64 * gq:64 * gq + 64] = src["qa"][64 * hd:64 * hd + 64]
        cols.append(blk)
    for name in ("ka", "va", "qb", "kb", "vb", "qm", "km", "vm", "om", "za", "zb", "zm"):
        cols.append(src[name])
    heads = len(src["gm"]) // 4
    for p in range(heads // 2):
        for gate in range(2):
            cols.append(np.array([src["gm"][di * 2 * heads + gate * heads + 2 * p + a]
                                  for di in range(2) for a in range(2)] + [z] * (GATE_PAD - 4)))
    cols.append(np.full(LANES - 2 * GATE_PAD * (heads // 2), z))
    cols = np.concatenate(cols)
    assert cols.shape[0] == _W_COLS
    return cols


def _permute_columns(w, cols):
    n = w.shape[-1]
    pieces, start = [], 0
    breaks = np.flatnonzero(np.diff(cols) != 1) + 1
    for stop in list(breaks) + [len(cols)]:
        run = cols[start:stop]
        if len(run) >= 32:
            pieces.append(lax.slice_in_dim(w, int(run[0]), int(run[-1]) + 1, axis=w.ndim - 1))
        elif pieces and isinstance(pieces[-1], list):
            pieces[-1].extend(run)
        else:
            pieces.append(list(run))
        start = stop
    lo = min(min(p) for p in pieces if isinstance(p, list))
    tail = jnp.concatenate([lax.slice_in_dim(w, lo, n, axis=w.ndim - 1), jnp.zeros(w.shape[:-1] + (1,), w.dtype)], -1)
    pieces = [jnp.take(tail, np.array(p) - lo, axis=-1) if isinstance(p, list) else p for p in pieces]
    return jnp.concatenate(pieces, axis=-1)


def _rope_table(n_tokens, n_ctx, dim):
    f32 = np.float32
    rows = n_tokens // GRID_W
    rowp = np.broadcast_to(np.arange(rows, dtype=f32)[:, None], (rows, GRID_W)).reshape(-1)
    colp = np.broadcast_to(np.arange(GRID_W, dtype=f32)[None, :], (rows, GRID_W)).reshape(-1)
    n_freq = dim // 4
    inv = np.power(f32(ROPE_BASE), -np.arange(n_freq, dtype=f32) / f32(n_freq)).astype(f32)
    ar = rowp[:, None] * inv
    ac = colp[:, None] * inv
    ang = np.concatenate([ar, ar, ac, ac], -1).astype(f32)
    cos, sin = np.cos(ang), np.sin(ang)
    odd = (np.arange(dim) // n_freq) % 2 == 1
    sin_p = np.where(odd, sin, f32(0))
    sin_m = np.where(odd, f32(0), -sin)
    tab = np.stack([cos, sin_p, sin_m])
    ident = np.stack([np.ones((n_ctx, dim), f32), np.zeros((n_ctx, dim), f32), np.zeros((n_ctx, dim), f32)])
    tab = np.concatenate([ident, tab], axis=1)
    return jnp.asarray(np.tile(tab, (1, 1, LANES // dim)).astype(f32))


def kernel(x, c, ctx, c_ctx, w_mod, b_mod, w_in, attn_sink, diff_lambda, diff_norm_g, mlstm_i_bias,
           mlstm_f_bias, mlstm_norm_g, w_out, ln_g, ln_b):
    b, t, d = x.shape
    n_ctx = ctx.shape[1]
    depth = w_mod.shape[0]
    d_in = w_in.shape[2]
    alpha = (2 * depth) ** 0.25

    xa = jnp.concatenate([ctx, x], axis=1)
    rope_a = _rope_table(t, n_ctx, HEAD_DIM)
    rope_b = _rope_table(t, n_ctx, B_QK_DIM)

    rows = -(-(b + 1) // 8) * 8
    cc = jnp.concatenate([c, c_ctx[None, :], jnp.zeros((rows - b - 1, d), F32)], axis=0)
    mod = _modulation(cc, w_mod, b_mod).reshape(depth, rows, 3, d)

    w_in_p = _permute_columns(w_in, _in_weight_columns(d_in)).astype(BF16)
    w_out_b = w_out.astype(BF16)
    norm_b = diff_norm_g[:, :, None]

    n_pairs = mlstm_i_bias.shape[2] // 2
    heads = mlstm_i_bias.shape[2]
    chains = [(di, a) for di in range(2) for a in range(2)]
    bias_idx = np.array([[di * heads + 2 * p + a for di, a in chains] for p in range(n_pairs)])
    pad8 = lambda v: jnp.pad(v.reshape(-1)[bias_idx], ((0, 0), (0, GATE_PAD - 4)))

    for l in range(depth):
        with_ctx = l < depth - 1
        lam_init = 0.8 - 0.6 * math.exp(-0.3 * l)
        modsel = jnp.stack([jnp.broadcast_to(mod[l, b], (b, 3, d)), mod[l, :b]], axis=1)
        (qa, ka, va, qb, kb, vb, qm, km, vm, og, g, gic, gfc, gir, gfr) = _in_projection(
            xa, modsel, w_in_p[l], rope_a, rope_b)

        ya = _attn_a(attn_sink[l], qa, ka, va, n_ctx)
        yb, yb_ctx = _attn_b(jnp.full((1,), lam_init, F32), diff_lambda[l], norm_b[l], qb, kb, vb, n_ctx, with_ctx)

        ym = _mlstm(pad8(mlstm_i_bias[l]), pad8(mlstm_f_bias[l]), mlstm_norm_g[l].reshape(n_pairs, LANES),
                    qm, km, vm, gic, gfc, gir, gfr, og, n_ctx)

        xa = _out_projection(xa, modsel, ya, yb, yb_ctx, ym, g, w_out_b[l], ln_g[l][None, :], ln_b[l][None, :],
                             alpha, n_ctx // ROW_TILE)
    return xa
```

```python
import functools
import math

import numpy as np
import jax
import jax.numpy as jnp
from jax import lax
from jax.experimental import pallas as pl
from jax.experimental.pallas import tpu as pltpu

F32 = jnp.float32
BF16 = jnp.bfloat16

LANES = 128
GRID_W = 64
HEAD_DIM = 64
ROPE_BASE = 10000.0
LN_EPS = 1e-5
WINDOW = 128
B_QK_DIM = 32
LOG2E = math.log2(math.e)
VB_ROWS = HEAD_DIM + 16
MLSTM_CHUNK = 128
GATE_PAD = 8
MOD_COL_TILE = 1024
B_KEY_GROUP = 2
ROW_TILE = 256
VMEM_LIMIT = 56 * 1024 * 1024


def _dot(a, b):
    return jnp.dot(a, b, preferred_element_type=F32)


def _dot_nt(a, b):
    return lax.dot_general(a, b, (((1,), (1,)), ((), ())), preferred_element_type=F32)


def _split_bf16(a):
    hi = a.astype(BF16)
    lo = (a - hi.astype(F32)).astype(BF16)
    return hi, lo


def _sigmoid(x):
    return 1.0 / (1.0 + jnp.exp(-x))


def _params(sem):
    return pltpu.CompilerParams(dimension_semantics=sem, vmem_limit_bytes=VMEM_LIMIT)


def _mod_kernel(c_ref, w_ref, b_ref, o_ref):
    c = c_ref[...]
    a = c * _sigmoid(c)
    a_hi, a_lo = _split_bf16(a)
    w_hi, w_lo = _split_bf16(w_ref[0])
    o_ref[0] = _dot(a_hi, w_hi) + _dot(a_lo, w_hi) + _dot(a_hi, w_lo) + b_ref[0]


def _modulation(cc, w_mod, b_mod):
    depth, d, n = w_mod.shape
    r = cc.shape[0]
    tn = MOD_COL_TILE
    return pl.pallas_call(
        _mod_kernel,
        grid=(depth, n // tn),
        in_specs=[
            pl.BlockSpec((r, d), lambda l, j: (0, 0)),
            pl.BlockSpec((1, d, tn), lambda l, j: (l, 0, j)),
            pl.BlockSpec((1, 1, tn), lambda l, j: (l, 0, j)),
        ],
        out_specs=pl.BlockSpec((1, r, tn), lambda l, j: (l, 0, j)),
        out_shape=jax.ShapeDtypeStruct((depth, r, n), F32),
        compiler_params=_params(("arbitrary", "arbitrary")),
        name="modulation",
    )(cc, w_mod, b_mod.reshape(depth, 1, n))


_QA, _KA, _VA = (0, 512), (512, 640), (640, 768)
_QB, _KB, _VB = (768, 1152), (1152, 1536), (1536, 1920)
_QM, _KM, _VM = (1920, 2304), (2304, 2688), (2688, 3072)
_OM, _Z, _GM = (3072, 3456), (3456, 4480), (4480, 4608)
_W_COLS = 4608
_W_GROUPS = ((0, 768), (768, 1536), (1536, 2304), (2304, 3072), (3072, 4608))


def _rope(t, tab_ref, quarter):
    cos, sin_p, sin_m = tab_ref[0], tab_ref[1], tab_ref[2]
    outs = []
    for j in range(t.shape[1] // LANES):
        tj = t[:, j * LANES:(j + 1) * LANES]
        outs.append(tj * cos + pltpu.roll(tj, quarter, 1) * sin_p + pltpu.roll(tj, LANES - quarter, 1) * sin_m)
    return outs[0] if len(outs) == 1 else jnp.concatenate(outs, axis=1)


def _inproj_kernel(x_ref, mod_ref, w_ref, ra_ref, rb_ref,
                   qa_ref, ka_ref, va_ref, qb_ref, kb_ref, vb_ref,
                   qm_ref, km_ref, vm_ref, om_ref, g_ref, gic_ref, gfc_ref, gir_ref, gfr_ref):
    x = x_ref[0]
    shift = mod_ref[0, 0, 0:1, :]
    scale = mod_ref[0, 0, 1:2, :]
    h = (x * (1.0 + scale) + shift).astype(BF16)

    group_dots = {}

    def proj(cols):
        lo, hi = next(g for g in _W_GROUPS if g[0] <= cols[0] and cols[1] <= g[1])
        if lo not in group_dots:
            group_dots[lo] = _dot(h, w_ref[:, lo:hi])
        return group_dots[lo][:, cols[0] - lo:cols[1] - lo]

    qa_ref[0] = (_rope(proj(_QA), ra_ref, HEAD_DIM // 4) * (HEAD_DIM ** -0.5)).astype(BF16)
    ka_ref[0] = _rope(proj(_KA), ra_ref, HEAD_DIM // 4).astype(BF16)
    va_ref[0] = proj(_VA).astype(BF16)
    qb_t = (_rope(proj(_QB), rb_ref, B_QK_DIM // 4) * (B_QK_DIM ** -0.5 * LOG2E)).T.astype(BF16)
    for p in range(qb_t.shape[0] // LANES):
        qb_ref[0, p, 0] = qb_t[p * LANES:(p + 1) * LANES]
    vb_t = proj(_VB).T
    ones_rows = jnp.where(lax.broadcasted_iota(jnp.int32, (VB_ROWS - HEAD_DIM, vb_t.shape[1]), 0) == 0, 1.0, 0.0)
    for hd in range(vb_t.shape[0] // HEAD_DIM):
        vb_ref[0, hd, 0] = jnp.concatenate([vb_t[hd * HEAD_DIM:(hd + 1) * HEAD_DIM], ones_rows], axis=0).astype(BF16)
    kb_ref[0] = _rope(proj(_KB), rb_ref, B_QK_DIM // 4).astype(BF16)
    km_ref[0] = (proj(_KM) * (HEAD_DIM ** -0.5)).astype(BF16)
    qm_t = proj(_QM).T.astype(BF16)
    om_t = _sigmoid(proj(_OM)).T.astype(BF16)
    vm_t = proj(_VM).T
    L = MLSTM_CHUNK
    ones_rows_c = jnp.where(lax.broadcasted_iota(jnp.int32, (VB_ROWS - HEAD_DIM, L), 0) == 0, 1.0, 0.0)
    for c in range(qm_t.shape[1] // L):
        cols = slice(c * L, (c + 1) * L)
        for p in range(qm_t.shape[0] // LANES):
            qm_ref[0, p, c] = qm_t[p * LANES:(p + 1) * LANES, cols]
            om_ref[0, p, c] = om_t[p * LANES:(p + 1) * LANES, cols]
        for hd in range(vm_t.shape[0] // HEAD_DIM):
            vm_ref[0, hd, c] = jnp.concatenate(
                [vm_t[hd * HEAD_DIM:(hd + 1) * HEAD_DIM, cols], ones_rows_c], axis=0).astype(BF16)
    z = proj(_Z)
    g_ref[0] = (z * _sigmoid(z)).astype(BF16)
    gm = proj(_GM)
    gm_t = gm.T
    for p in range(gic_ref.shape[1]):
        lo, hi = 2 * GATE_PAD * p, 2 * GATE_PAD * p + GATE_PAD
        gic_ref[0, p] = gm[:, lo:hi]
        gfc_ref[0, p] = gm[:, hi:hi + GATE_PAD]
        for c in range(gm_t.shape[1] // L):
            gir_ref[0, p, c] = gm_t[lo:hi, c * L:(c + 1) * L]
            gfr_ref[0, p, c] = gm_t[hi:hi + GATE_PAD, c * L:(c + 1) * L]


def _in_projection(xa, modsel, w, rope_a, rope_b):
    b, s, d = xa.shape
    tm = ROW_TILE
    n_pairs = (_QB[1] - _QB[0]) // LANES
    n_heads = (_VB[1] - _VB[0]) // HEAD_DIM
    L = MLSTM_CHUNK
    widths = [512, 128, 128, (n_pairs, LANES, tm), 384, (n_heads, VB_ROWS, tm),
              (n_pairs, LANES, L), 384, (n_heads, VB_ROWS, L), (n_pairs, LANES, L), 1024]
    tposed = lambda g, r, tt: pl.BlockSpec((1, g, tm // tt, r, tt), lambda bi, i: (bi, 0, i, 0, 0))
    row = lambda n: tposed(*n) if isinstance(n, tuple) else pl.BlockSpec((1, tm, n), lambda bi, i: (bi, i, 0))
    out_shape = [jax.ShapeDtypeStruct((b, n[0], s // n[2], n[1], n[2]) if isinstance(n, tuple) else (b, s, n), BF16)
                 for n in widths]
    gate_c = pl.BlockSpec((1, n_pairs, tm, GATE_PAD), lambda bi, i: (bi, 0, i, 0))
    gate_r = pl.BlockSpec((1, n_pairs, tm // L, GATE_PAD, L), lambda bi, i: (bi, 0, i, 0, 0))
    out_shape += [jax.ShapeDtypeStruct((b, n_pairs, s, GATE_PAD), F32)] * 2
    out_shape += [jax.ShapeDtypeStruct((b, n_pairs, s // L, GATE_PAD, L), F32)] * 2
    return pl.pallas_call(
        _inproj_kernel,
        grid=(b, s // tm),
        in_specs=[
            row(d),
            pl.BlockSpec((1, 1, 3, d), lambda bi, i: (bi, jnp.minimum(i, 1), 0, 0)),
            pl.BlockSpec((d, _W_COLS), lambda bi, i: (0, 0)),
            pl.BlockSpec((3, tm, LANES), lambda bi, i: (0, i, 0)),
            pl.BlockSpec((3, tm, LANES), lambda bi, i: (0, i, 0)),
        ],
        out_specs=[row(n) for n in widths] + [gate_c, gate_c, gate_r, gate_r],
        out_shape=out_shape,
        compiler_params=_params(("arbitrary", "arbitrary")),
        name="in_projection",
    )(xa, modsel, w, rope_a, rope_b)


def _attn_a_kernel(sink_ref, q_ref, k_ref, v_ref, o_ref, *, n_ctx, n_blocks):
    w = WINDOW
    s_len = n_blocks * w
    half = LANES // 2
    rows1 = lax.broadcasted_iota(jnp.int32, (4 * w, 1), 0)
    sink = jnp.where(rows1 < w, sink_ref[0],
                     jnp.where(rows1 < 2 * w, sink_ref[1], jnp.where(rows1 < 3 * w, sink_ref[2], sink_ref[3])))
    row = lax.broadcasted_iota(jnp.int32, (4 * w, 3 * w), 0) & (w - 1)
    col = lax.broadcasted_iota(jnp.int32, (4 * w, 3 * w), 1)
    lane = lax.broadcasted_iota(jnp.int32, (w, LANES), 1)
    k_ctx = k_ref[0, 0:n_ctx, :]
    v_ctx = v_ref[0, 0:n_ctx, :]

    def body(i, carry):
        r0 = pl.multiple_of(i * w, w)
        q = q_ref[0, pl.ds(r0, w), :]
        qs = jnp.concatenate([q[:, j * LANES:(j + 1) * LANES] for j in range(4)], axis=0)
        start = pl.multiple_of(jnp.clip(r0 - w, 0, s_len - 3 * w), w)
        s_loc = _dot_nt(qs, k_ref[0, pl.ds(start, 3 * w), :])
        s_ctx = _dot_nt(qs, k_ctx)
        kpos = col + start
        rel = kpos - r0 - row
        ok = (jnp.abs(rel) <= w) & (kpos >= jnp.where(r0 >= n_ctx, n_ctx, s_len))
        s_loc = jnp.where(ok, s_loc, -jnp.inf)
        m = jnp.maximum(jnp.maximum(jnp.max(s_loc, axis=1, keepdims=True),
                                    jnp.max(s_ctx, axis=1, keepdims=True)), sink)
        p_loc = jnp.exp(s_loc - m)
        p_ctx = jnp.exp(s_ctx - m)
        den = (jnp.sum(p_loc, axis=1, keepdims=True) + jnp.sum(p_ctx, axis=1, keepdims=True)
               + jnp.exp(sink - m))
        o = (_dot(p_loc.astype(BF16), v_ref[0, pl.ds(start, 3 * w), :])
             + _dot(p_ctx.astype(BF16), v_ctx)) / den
        left = jnp.where(lane < half, o[0:w], pltpu.roll(o[w:2 * w], half, 1))
        right = jnp.where(lane < half, pltpu.roll(o[2 * w:3 * w], half, 1), o[3 * w:])
        o_ref[0, pl.ds(r0, w), :] = jnp.concatenate([left, right], axis=1).astype(o_ref.dtype)
        return carry

    lax.fori_loop(0, n_blocks, body, 0, unroll=2)


def _attn_a(sink, qa, ka, va, n_ctx):
    b, s, _ = qa.shape
    seq = lambda n: pl.BlockSpec((1, s, n), lambda bi: (bi, 0, 0))
    return pl.pallas_call(
        functools.partial(_attn_a_kernel, n_ctx=n_ctx, n_blocks=s // WINDOW),
        grid=(b,),
        in_specs=[pl.BlockSpec(memory_space=pltpu.SMEM), seq(4 * LANES), seq(LANES), seq(LANES)],
        out_specs=seq(2 * LANES),
        out_shape=jax.ShapeDtypeStruct((b, s, 2 * LANES), BF16),
        compiler_params=_params(("arbitrary",)),
        name="window_attention",
    )(sink, qa, ka, va)


def _attn_b_kernel(li_ref, lam_ref, ng_ref, qt_ref, k_ref, vt_ref, o_ref,
                   qs_scr, sa_scr, sb_scr, sc_scr, mt_scr, acc_scr, *, nt, q0, nq, kg):
    tq, tk = qt_ref.shape[-1], vt_ref.shape[-1]
    hv = HEAD_DIM
    lam_init = li_ref[0]
    lv = lam_ref[...]
    lam = (jnp.exp(jnp.sum(lv[0:1] * lv[1:2], axis=1, keepdims=True))
           - jnp.exp(jnp.sum(lv[2:3] * lv[3:4], axis=1, keepdims=True)) + lam_init)

    def first_tile(i, slot):
        qt = qt_ref[0, 0, q0 + i]
        unit = lax.broadcasted_iota(jnp.int32, qt.shape, 0) // B_QK_DIM
        zero = jnp.zeros_like(qt)
        qs_scr[slot] = jnp.concatenate([jnp.where(unit == u, qt, zero) for u in range(4)], axis=1)
        mt_scr[...] = fetch(0, sc_scr, slot)

    def fetch(t, s_scr, slot, n=1):
        off = pl.multiple_of(t * tk, tk)
        s_new = _dot(k_ref[0, pl.ds(off, n * tk), :], qs_scr[slot])
        s_scr[...] = s_new
        return jnp.max(s_new, axis=0, keepdims=True)

    def consume(s_scr, mt, m, t, n=1):
        m_new = jnp.maximum(m, mt)
        alpha = jnp.exp2(m - m_new)
        p = jnp.exp2(s_scr[...] - m_new).astype(BF16)
        for hd in range(2):
            cols = slice(hd * 2 * tq, (hd + 1) * 2 * tq)
            pv = _dot(vt_ref[0, hd, t], p[0:tk, cols])
            for c in range(1, n):
                pv = pv + _dot(vt_ref[0, hd, t + c], p[c * tk:(c + 1) * tk, cols])
            acc_scr[hd] = alpha[:, cols] * acc_scr[hd] + pv
        return m_new

    def head_out(hd):
        acc = acc_scr[hd]
        o = acc[0:hv] * (1.0 / acc[hv:hv + 1])
        od = o[:, 0:tq] - lam * o[:, tq:]
        ms = jnp.mean(od * od, axis=0, keepdims=True)
        return od * lax.rsqrt(ms + LN_EPS) * ng_ref[...]

    def query_tile(i, carry):
        cur = i % 2

        def successor():
            if nq > 1:
                first_tile(jnp.minimum(i + 1, nq - 1), 1 - cur)

        acc_scr[...] = jnp.zeros_like(acc_scr)
        m = jnp.full((1, 4 * tq), -jnp.inf, F32)
        mt_c = mt_scr[...]

        nb = (nt - 1) // kg
        big = lambda j: 1 + kg * j

        def body(tt, carry):
            m, mt_a = carry
            j0 = 2 * tt
            mt_b = fetch(big(j0 + 1), sb_scr, cur, kg)
            m = consume(sa_scr, mt_a, m, big(j0), kg)
            mt_a = fetch(big(j0 + 2), sa_scr, cur, kg)
            m = consume(sb_scr, mt_b, m, big(j0 + 1), kg)
            return m, mt_a

        if nb == 0:
            successor()
            consume(sc_scr, mt_c, m, 0)
        else:
            mt_a = fetch(big(0), sa_scr, cur, kg)
            m = consume(sc_scr, mt_c, m, 0)
            m, mt_a = lax.fori_loop(0, (nb - 1) // 2, body, (m, mt_a))
            if nb % 2 == 0:
                mt_b = fetch(big(nb - 1), sb_scr, cur, kg)
                m = consume(sa_scr, mt_a, m, big(nb - 2), kg)
                successor()
                consume(sb_scr, mt_b, m, big(nb - 1), kg)
            else:
                successor()
                consume(sa_scr, mt_a, m, big(nb - 1), kg)

        y_t = jnp.concatenate([head_out(0), head_out(1)], axis=0) * (1.0 - lam_init)
        o_ref[0, pl.ds(pl.multiple_of(i * tq, tq), tq), :] = y_t.T.astype(o_ref.dtype)
        return carry

    first_tile(0, 0)
    lax.fori_loop(0, nq, query_tile, 0)


def _attn_b_call(lam_init, lam_vec, norm_g, qbt, kb, vbt, q0, nq, nt):
    b, pairs, n_tiles, _, tq = qbt.shape
    vrows, tk = vbt.shape[-2:]
    kg = B_KEY_GROUP if (nt - 1) % B_KEY_GROUP == 0 else 1
    return pl.pallas_call(
        functools.partial(_attn_b_kernel, nt=nt, q0=q0, nq=nq, kg=kg),
        grid=(b, pairs),
        in_specs=[
            pl.BlockSpec(memory_space=pltpu.SMEM),
            pl.BlockSpec((4, B_QK_DIM), lambda bi, p: (0, 0)),
            pl.BlockSpec((HEAD_DIM, 1), lambda bi, p: (0, 0)),
            pl.BlockSpec((1, 1, n_tiles, LANES, tq), lambda bi, p: (bi, p, 0, 0, 0)),
            pl.BlockSpec((1, nt * tk, LANES), lambda bi, p: (bi, 0, p)),
            pl.BlockSpec((1, 2, nt, vrows, tk), lambda bi, p: (bi, p, 0, 0, 0)),
        ],
        out_specs=pl.BlockSpec((1, nq * tq, LANES), lambda bi, p: (bi, 0, p)),
        out_shape=jax.ShapeDtypeStruct((b, nq * tq, pairs * LANES), BF16),
        scratch_shapes=[pltpu.VMEM((2, LANES, 4 * tq), BF16), pltpu.VMEM((kg * tk, 4 * tq), F32),
                        pltpu.VMEM((kg * tk, 4 * tq), F32), pltpu.VMEM((tk, 4 * tq), F32),
                        pltpu.VMEM((1, 4 * tq), F32), pltpu.VMEM((2, vrows, 2 * tq), F32)],
        compiler_params=_params(("arbitrary", "arbitrary")),
        name="diff_attention",
    )(lam_init, lam_vec, norm_g, qbt, kb, vbt)


def _attn_b(lam_init, lam_vec, norm_g, qbt, kb, vbt, n_ctx, with_ctx):
    n_tiles, tq = qbt.shape[2], qbt.shape[4]
    tk = vbt.shape[-1]
    n_ctx_tiles = n_ctx // tq
    y_lat = _attn_b_call(lam_init, lam_vec, norm_g, qbt, kb, vbt, n_ctx_tiles, n_tiles - n_ctx_tiles, n_tiles)
    y_ctx = _attn_b_call(lam_init, lam_vec, norm_g, qbt, kb, vbt, 0, n_ctx_tiles, n_ctx // tk) if with_ctx else None
    return y_lat, y_ctx


def _log_sigmoid(x):
    return jnp.minimum(x, 0.0) - jnp.log(1.0 + jnp.exp(-jnp.abs(x)))


def _mlstm_kernel(bir_ref, bic_ref, bfr_ref, bfc_ref, ng_ref, qt_ref, k_ref, vt_ref, gic_ref, gfc_ref, gir_ref,
                  gfr_ref, ogt_ref,
                  o_ref, hf_scr, hb_scr, st_scr, ir_scr, br_scr, gc_scr, *, n_ctx_chunks, n_chunks):
    L = MLSTM_CHUNK
    hv = HEAD_DIM
    r_i = lax.broadcasted_iota(jnp.int32, (L, L), 0)
    c_i = lax.broadcasted_iota(jnp.int32, (L, L), 1)
    upper = r_i <= c_i
    lower = r_i >= c_i
    t_up = jnp.where(upper, 1.0, 0.0).astype(BF16)
    t_low = jnp.where(lower, 1.0, 0.0).astype(BF16)
    row_q = lax.broadcasted_iota(jnp.int32, (LANES, L), 0)
    sel = [jnp.where(row_q < hv, 1.0, 0.0).astype(BF16), jnp.where(row_q < hv, 0.0, 1.0).astype(BF16)]
    fwd_r = lax.broadcasted_iota(jnp.int32, (GATE_PAD, L), 0) < 2
    fwd_c = lax.broadcasted_iota(jnp.int32, (L, GATE_PAD), 1) < 2
    st_scr[...] = jnp.zeros_like(st_scr)

    def gates(c, carry):
        r0 = pl.multiple_of(c * L, L)
        frow = _log_sigmoid(gfr_ref[0, 0, c] + bfr_ref[0])
        hi, lo = _split_bf16(frow)
        ir_scr[c] = gir_ref[0, 0, c] + bir_ref[0]
        br_scr[c] = jnp.where(fwd_r, _dot(hi, t_up) + _dot(lo, t_up), _dot(hi, t_low) + _dot(lo, t_low))
        fcol = _log_sigmoid(gfc_ref[0, 0, pl.ds(r0, L), :] + bfc_ref[0])
        hi, lo = _split_bf16(fcol)
        bcol = jnp.where(fwd_c, _dot(t_low, hi) + _dot(t_low, lo), _dot(t_up, hi) + _dot(t_up, lo))
        gc_scr[c] = gic_ref[0, 0, pl.ds(r0, L), :] + bic_ref[0] - bcol
        return carry
    lax.fori_loop(0, n_chunks, gates, 0, unroll=4)

    def chunk_step(direction, c, m_prev):
        r0 = pl.multiple_of(c * L, L)
        kc = k_ref[0, pl.ds(r0, L), :]
        qtc = qt_ref[0, 0, c]
        irow8, brow8, gcol8 = ir_scr[c], br_scr[c], gc_scr[c]
        tri = upper if direction == 0 else lower
        h_scr = hf_scr if direction == 0 else hb_scr
        m_news = []
        for a in range(2):
            ci = 2 * direction + a
            b_row, i_row, g_col = brow8[ci:ci + 1], irow8[ci:ci + 1], gcol8[:, ci:ci + 1]
            qtm = qtc * sel[a]
            s_t = _dot(kc, qtm)
            log_d = jnp.where(tri, g_col + b_row, -jnp.inf)
            m_in = b_row + m_prev[a]
            m_t = jnp.maximum(m_in, jnp.max(log_d, axis=0, keepdims=True))
            w_in = jnp.exp(m_in - m_t)
            sd = (s_t * jnp.exp(log_d - m_t)).astype(BF16)
            qw = (qtm.astype(F32) * w_in).astype(BF16)
            vt = vt_ref[0, a, c]
            state = st_scr[ci]
            numden = _dot(state.astype(BF16), qw) + _dot(vt, sd)
            h_scr[c, a * hv:(a + 1) * hv, :] = (
                numden[0:hv] / jnp.maximum(jnp.abs(numden[hv:hv + 1]), jnp.exp(-m_t)))
            b_last = jnp.min(b_row, axis=1, keepdims=True)
            log_w = b_last - b_row + i_row
            m_new = jnp.maximum(b_last + m_prev[a], jnp.max(log_w, axis=1, keepdims=True))
            vw = (vt.astype(F32) * jnp.exp(log_w - m_new)).astype(BF16)
            st_scr[ci] = jnp.exp(b_last + m_prev[a] - m_new) * state + _dot(vw, kc)
            m_news.append(m_new)
        return tuple(m_news)

    def body(j, carry):
        mf, mb = carry
        cb = jnp.where(j < n_ctx_chunks, n_ctx_chunks - 1 - j, n_chunks - 1 + n_ctx_chunks - j)
        return chunk_step(0, j, mf), chunk_step(1, cb, mb)

    z11 = jnp.zeros((1, 1), F32)
    lax.fori_loop(0, n_chunks, body, ((z11, z11), (z11, z11)), unroll=4)

    def finish(c, carry):
        r0 = pl.multiple_of(c * L, L)
        h = hf_scr[c] + hb_scr[c]
        outs = []
        for a in range(2):
            ha = h[a * hv:(a + 1) * hv]
            outs.append(ha * lax.rsqrt(jnp.mean(ha * ha, axis=0, keepdims=True) + LN_EPS))
        y_t = jnp.concatenate(outs, axis=0) * ng_ref[0] * ogt_ref[0, 0, c].astype(F32)
        o_ref[0, pl.ds(r0, L), :] = y_t.T.astype(o_ref.dtype)
        return carry
    lax.fori_loop(0, n_chunks, finish, 0, unroll=4)


def _mlstm(bias_i, bias_f, norm_g, qmt, km, vmt, gic, gfc, gir, gfr, ogt, n_ctx):
    b, pairs, nc, _, L = qmt.shape
    s = nc * L
    vrows = vmt.shape[3]
    seq = pl.BlockSpec((1, s, LANES), lambda bi, p: (bi, 0, p))
    pair_tiles = pl.BlockSpec((1, 1, nc, LANES, L), lambda bi, p: (bi, p, 0, 0, 0))
    gate_c = pl.BlockSpec((1, 1, s, GATE_PAD), lambda bi, p: (bi, p, 0, 0))
    gate_r = pl.BlockSpec((1, 1, nc, GATE_PAD, L), lambda bi, p: (bi, p, 0, 0, 0))
    bias_r = pl.BlockSpec((1, GATE_PAD, 1), lambda bi, p: (p, 0, 0))
    bias_c = pl.BlockSpec((1, 1, GATE_PAD), lambda bi, p: (p, 0, 0))
    return pl.pallas_call(
        functools.partial(_mlstm_kernel, n_ctx_chunks=n_ctx // L, n_chunks=nc),
        grid=(b, pairs),
        in_specs=[
            bias_r, bias_c, bias_r, bias_c,
            pl.BlockSpec((1, LANES, 1), lambda bi, p: (p, 0, 0)),
            pair_tiles, seq,
            pl.BlockSpec((1, 2, nc, vrows, L), lambda bi, p: (bi, p, 0, 0, 0)),
            gate_c, gate_c, gate_r, gate_r,
            pair_tiles,
        ],
        out_specs=seq,
        out_shape=jax.ShapeDtypeStruct((b, s, pairs * LANES), BF16),
        scratch_shapes=[pltpu.VMEM((nc, LANES, L), F32), pltpu.VMEM((nc, LANES, L), F32),
                        pltpu.VMEM((4, vrows, LANES), F32),
                        pltpu.VMEM((nc, GATE_PAD, L), F32), pltpu.VMEM((nc, GATE_PAD, L), F32),
                        pltpu.VMEM((nc, L, GATE_PAD), F32)],
        compiler_params=_params(("arbitrary", "arbitrary")),
        name="mlstm",
    )(bias_i[:, :, None], bias_i[:, None, :], bias_f[:, :, None], bias_f[:, None, :], norm_g[:, :, None],
      qmt, km, vmt, gic, gfc, gir, gfr, ogt)


def _out_kernel(x_ref, mod_ref, ya_ref, yb_ref, *refs, alpha, n_ctx_tiles):
    ym_ref, g_ref, w_ref, lng_ref, lnb_ref, o_ref = refs[-6:]
    x = x_ref[0]
    gate = mod_ref[0, 0, 2:3, :]
    g = g_ref[0].astype(F32)
    na, nb_ = ya_ref.shape[2], yb_ref.shape[2]
    yb = yb_ref[0]
    if len(refs) == 7:
        yb = jnp.where(pl.program_id(1) < n_ctx_tiles, refs[0][0], yb)
    mix_a = (ya_ref[0].astype(F32) * g[:, :na]).astype(BF16)
    mix_b = (yb.astype(F32) * g[:, na:na + nb_]).astype(BF16)
    mix_m = (ym_ref[0].astype(F32) * g[:, na + nb_:]).astype(BF16)
    y = _dot(jnp.concatenate([mix_a, mix_b, mix_m], axis=1), w_ref[...])
    r = alpha * x + gate * y
    mu = jnp.mean(r, axis=1, keepdims=True)
    d = r - mu
    var = jnp.mean(d * d, axis=1, keepdims=True)
    o_ref[0] = d * lax.rsqrt(var + LN_EPS) * lng_ref[...] + lnb_ref[...]


def _out_projection(xa, modsel, ya, yb_lat, yb_ctx, ym, g, w, ln_g, ln_b, alpha, n_ctx_tiles):
    b, s, d = xa.shape
    tm = ROW_TILE
    row0 = 0 if yb_ctx is not None else n_ctx_tiles
    row = lambda n: pl.BlockSpec((1, tm, n), lambda bi, i: (bi, i + row0, 0))
    vec = pl.BlockSpec((1, d), lambda bi, i: (0, 0))
    nb_ = yb_lat.shape[2]
    in_specs = [
        row(d),
        pl.BlockSpec((1, 1, 3, d), lambda bi, i: (bi, jnp.minimum(i + row0, 1), 0, 0)),
        row(ya.shape[2]),
        pl.BlockSpec((1, tm, nb_), lambda bi, i: (bi, jnp.maximum(i + row0 - n_ctx_tiles, 0), 0)),
    ]
    args = [xa, modsel, ya, yb_lat]
    if yb_ctx is not None:
        in_specs.append(pl.BlockSpec((1, tm, nb_), lambda bi, i: (bi, jnp.minimum(i, n_ctx_tiles - 1), 0)))
        args.append(yb_ctx)
    in_specs += [row(ym.shape[2]), row(d), pl.BlockSpec(w.shape, lambda bi, i: (0, 0)), vec, vec]
    args += [ym, g, w, ln_g, ln_b]
    return pl.pallas_call(
        functools.partial(_out_kernel, alpha=alpha, n_ctx_tiles=n_ctx_tiles),
        grid=(b, s // tm - row0),
        in_specs=in_specs,
        out_specs=pl.BlockSpec((1, tm, d), lambda bi, i: (bi, i, 0)),
        out_shape=jax.ShapeDtypeStruct((b, s - row0 * tm, d), F32),
        compiler_params=_params(("arbitrary", "arbitrary")),
        name="out_projection",
    )(*args)


def _in_weight_columns(d_in):
    z = d_in
    src = {}
    off = 0
    for name, n in (("qa", 256), ("ka", 128), ("va", 128), ("za", 256), ("qb", 384), ("kb", 384), ("vb", 384),
                    ("zb", 384), ("qm", 384), ("km", 384), ("vm", 384), ("om", 384), ("zm", 384), ("gm", 24)):
        src[name] = np.arange(off, off + n)
        off += n
    assert off == d_in
    cols = []
    for hd in range(4):
        blk = np.full(LANES, z)
        gq = hd // 2
        blk[64 * gq:64 * gq + 64] = src["qa"][64 * hd:64 * hd + 64]
        cols.append(blk)
    for name in ("ka", "va", "qb", "kb", "vb", "qm", "km", "vm", "om", "za", "zb", "zm"):
        cols.append(src[name])
    heads = len(src["gm"]) // 4
    for p in range(heads // 2):
        for gate in range(2):
            cols.append(np.array([src["gm"][di * 2 * heads + gate * heads + 2 * p + a]
                                  for di in range(2) for a in range(2)] + [z] * (GATE_PAD - 4)))
    cols.append(np.full(LANES - 2 * GATE_PAD * (heads // 2), z))
    cols = np.concatenate(cols)
    assert cols.shape[0] == _W_COLS
    return cols


def _permute_columns(w, cols):
    n = w.shape[-1]
    pieces, start = [], 0
    breaks = np.flatnonzero(np.diff(cols) != 1) + 1
    for stop in list(breaks) + [len(cols)]:
        run = cols[start:stop]
        if len(run) >= 32:
            pieces.append(lax.slice_in_dim(w, int(run[0]), int(run[-1]) + 1, axis=w.ndim - 1))
        elif pieces and isinstance(pieces[-1], list):
            pieces[-1].extend(run)
        else:
            pieces.append(list(run))
        start = stop
    lo = min(min(p) for p in pieces if isinstance(p, list))
    tail = jnp.concatenate([lax.slice_in_dim(w, lo, n, axis=w.ndim - 1), jnp.zeros(w.shape[:-1] + (1,), w.dtype)], -1)
    pieces = [jnp.take(tail, np.array(p) - lo, axis=-1) if isinstance(p, list) else p for p in pieces]
    return jnp.concatenate(pieces, axis=-1)


def _rope_table(n_tokens, n_ctx, dim):
    f32 = np.float32
    rows = n_tokens // GRID_W
    rowp = np.broadcast_to(np.arange(rows, dtype=f32)[:, None], (rows, GRID_W)).reshape(-1)
    colp = np.broadcast_to(np.arange(GRID_W, dtype=f32)[None, :], (rows, GRID_W)).reshape(-1)
    n_freq = dim // 4
    inv = np.power(f32(ROPE_BASE), -np.arange(n_freq, dtype=f32) / f32(n_freq)).astype(f32)
    ar = rowp[:, None] * inv
    ac = colp[:, None] * inv
    ang = np.concatenate([ar, ar, ac, ac], -1).astype(f32)
    cos, sin = np.cos(ang), np.sin(ang)
    odd = (np.arange(dim) // n_freq) % 2 == 1
    sin_p = np.where(odd, sin, f32(0))
    sin_m = np.where(odd, f32(0), -sin)
    tab = np.stack([cos, sin_p, sin_m])
    ident = np.stack([np.ones((n_ctx, dim), f32), np.zeros((n_ctx, dim), f32), np.zeros((n_ctx, dim), f32)])
    tab = np.concatenate([ident, tab], axis=1)
    return jnp.asarray(np.tile(tab, (1, 1, LANES // dim)).astype(f32))


def kernel(x, c, ctx, c_ctx, w_mod, b_mod, w_in, attn_sink, diff_lambda, diff_norm_g, mlstm_i_bias,
           mlstm_f_bias, mlstm_norm_g, w_out, ln_g, ln_b):
    b, t, d = x.shape
    n_ctx = ctx.shape[1]
    depth = w_mod.shape[0]
    d_in = w_in.shape[2]
    alpha = (2 * depth) ** 0.25

    xa = jnp.concatenate([ctx, x], axis=1)
    rope_a = _rope_table(t, n_ctx, HEAD_DIM)
    rope_b = _rope_table(t, n_ctx, B_QK_DIM)

    rows = -(-(b + 1) // 8) * 8
    cc = jnp.concatenate([c, c_ctx[None, :], jnp.zeros((rows - b - 1, d), F32)], axis=0)
    mod = _modulation(cc, w_mod, b_mod).reshape(depth, rows, 3, d)

    w_in_p = _permute_columns(w_in, _in_weight_columns(d_in)).astype(BF16)
    w_out_b = w_out.astype(BF16)
    norm_b = diff_norm_g[:, :, None]

    n_pairs = mlstm_i_bias.shape[2] // 2
    heads = mlstm_i_bias.shape[2]
    chains = [(di, a) for di in range(2) for a in range(2)]
    bias_idx = np.array([[di * heads + 2 * p + a for di, a in chains] for p in range(n_pairs)])
    pad8 = lambda v: jnp.pad(v.reshape(-1)[bias_idx], ((0, 0), (0, GATE_PAD - 4)))

    for l in range(depth):
        with_ctx = l < depth - 1
        lam_init = 0.8 - 0.6 * math.exp(-0.3 * l)
        modsel = jnp.stack([jnp.broadcast_to(mod[l, b], (b, 3, d)), mod[l, :b]], axis=1)
        (qa, ka, va, qb, kb, vb, qm, km, vm, og, g, gic, gfc, gir, gfr) = _in_projection(
            xa, modsel, w_in_p[l], rope_a, rope_b)

        ya = _attn_a(attn_sink[l], qa, ka, va, n_ctx)
        yb, yb_ctx = _attn_b(jnp.full((1,), lam_init, F32), diff_lambda[l], norm_b[l], qb, kb, vb, n_ctx, with_ctx)

        ym = _mlstm(pad8(mlstm_i_bias[l]), pad8(mlstm_f_bias[l]), mlstm_norm_g[l].reshape(n_pairs, LANES),
                    qm, km, vm, gic, gfc, gir, gfr, og, n_ctx)

        xa = _out_projection(xa, modsel, ya, yb, yb_ctx, ym, g, w_out_b[l], ln_g[l][None, :], ln_b[l][None, :],
                             alpha, n_ctx // ROW_TILE)
    return xa
```

```python
import functools
import math

import numpy as np
import jax
import jax.numpy as jnp
from jax import lax
from jax.experimental import pallas as pl
from jax.experimental.pallas import tpu as pltpu

F32 = jnp.float32
BF16 = jnp.bfloat16

LANES = 128
GRID_W = 64
HEAD_DIM = 64
ROPE_BASE = 10000.0
LN_EPS = 1e-5
WINDOW = 128
B_QK_DIM = 32
LOG2E = math.log2(math.e)
VB_ROWS = HEAD_DIM + 16
MLSTM_CHUNK = 128
GATE_PAD = 8
MOD_COL_TILE = 1024
B_KEY_GROUP = 2
ROW_TILE = 256
VMEM_LIMIT = 56 * 1024 * 1024


def _dot(a, b):
    return jnp.dot(a, b, preferred_element_type=F32)


def _dot_nt(a, b):
    return lax.dot_general(a, b, (((1,), (1,)), ((), ())), preferred_element_type=F32)


def _split_bf16(a):
    hi = a.astype(BF16)
    lo = (a - hi.astype(F32)).astype(BF16)
    return hi, lo


def _sigmoid(x):
    return 1.0 / (1.0 + jnp.exp(-x))


def _params(sem):
    return pltpu.CompilerParams(dimension_semantics=sem, vmem_limit_bytes=VMEM_LIMIT)


def _mod_kernel(c_ref, w_ref, b_ref, o_ref):
    c = c_ref[...]
    a = c * _sigmoid(c)
    a_hi, a_lo = _split_bf16(a)
    w_hi, w_lo = _split_bf16(w_ref[0])
    o_ref[0] = _dot(a_hi, w_hi) + _dot(a_lo, w_hi) + _dot(a_hi, w_lo) + b_ref[0]


def _modulation(cc, w_mod, b_mod):
    depth, d, n = w_mod.shape
    r = cc.shape[0]
    tn = MOD_COL_TILE
    return pl.pallas_call(
        _mod_kernel,
        grid=(depth, n // tn),
        in_specs=[
            pl.BlockSpec((r, d), lambda l, j: (0, 0)),
            pl.BlockSpec((1, d, tn), lambda l, j: (l, 0, j)),
            pl.BlockSpec((1, 1, tn), lambda l, j: (l, 0, j)),
        ],
        out_specs=pl.BlockSpec((1, r, tn), lambda l, j: (l, 0, j)),
        out_shape=jax.ShapeDtypeStruct((depth, r, n), F32),
        compiler_params=_params(("arbitrary", "arbitrary")),
        name="modulation",
    )(cc, w_mod, b_mod.reshape(depth, 1, n))


_QA, _KA, _VA = (0, 512), (512, 640), (640, 768)
_QB, _KB, _VB = (768, 1152), (1152, 1536), (1536, 1920)
_QM, _KM, _VM = (1920, 2304), (2304, 2688), (2688, 3072)
_OM, _Z, _GM = (3072, 3456), (3456, 4480), (4480, 4608)
_W_COLS = 4608
_W_GROUPS = ((0, 768), (768, 1536), (1536, 2304), (2304, 3072), (3072, 4608))


def _rope(t, tab_ref, quarter):
    cos, sin_p, sin_m = tab_ref[0], tab_ref[1], tab_ref[2]
    outs = []
    for j in range(t.shape[1] // LANES):
        tj = t[:, j * LANES:(j + 1) * LANES]
        outs.append(tj * cos + pltpu.roll(tj, quarter, 1) * sin_p + pltpu.roll(tj, LANES - quarter, 1) * sin_m)
    return outs[0] if len(outs) == 1 else jnp.concatenate(outs, axis=1)


def _inproj_kernel(x_ref, mod_ref, w_ref, ra_ref, rb_ref,
                   qa_ref, ka_ref, va_ref, qb_ref, kb_ref, vb_ref,
                   qm_ref, km_ref, vm_ref, om_ref, g_ref, gic_ref, gfc_ref, gir_ref, gfr_ref):
    x = x_ref[0]
    shift = mod_ref[0, 0, 0:1, :]
    scale = mod_ref[0, 0, 1:2, :]
    h = (x * (1.0 + scale) + shift).astype(BF16)

    group_dots = {}

    def proj(cols):
        lo, hi = next(g for g in _W_GROUPS if g[0] <= cols[0] and cols[1] <= g[1])
        if lo not in group_dots:
            group_dots[lo] = _dot(h, w_ref[:, lo:hi])
        return group_dots[lo][:, cols[0] - lo:cols[1] - lo]

    qa_ref[0] = (_rope(proj(_QA), ra_ref, HEAD_DIM // 4) * (HEAD_DIM ** -0.5)).astype(BF16)
    ka_ref[0] = _rope(proj(_KA), ra_ref, HEAD_DIM // 4).astype(BF16)
    va_ref[0] = proj(_VA).astype(BF16)
    qb_t = (_rope(proj(_QB), rb_ref, B_QK_DIM // 4) * (B_QK_DIM ** -0.5 * LOG2E)).T.astype(BF16)
    for p in range(qb_t.shape[0] // LANES):
        qb_ref[0, p, 0] = qb_t[p * LANES:(p + 1) * LANES]
    vb_t = proj(_VB).T
    ones_rows = jnp.where(lax.broadcasted_iota(jnp.int32, (VB_ROWS - HEAD_DIM, vb_t.shape[1]), 0) == 0, 1.0, 0.0)
    for hd in range(vb_t.shape[0] // HEAD_DIM):
        vb_ref[0, hd, 0] = jnp.concatenate([vb_t[hd * HEAD_DIM:(hd + 1) * HEAD_DIM], ones_rows], axis=0).astype(BF16)
    kb_ref[0] = _rope(proj(_KB), rb_ref, B_QK_DIM // 4).astype(BF16)
    km_ref[0] = (proj(_KM) * (HEAD_DIM ** -0.5)).astype(BF16)
    qm_t = proj(_QM).T.astype(BF16)
    om_t = _sigmoid(proj(_OM)).T.astype(BF16)
    vm_t = proj(_VM).T
    L = MLSTM_CHUNK
    ones_rows_c = jnp.where(lax.broadcasted_iota(jnp.int32, (VB_ROWS - HEAD_DIM, L), 0) == 0, 1.0, 0.0)
    for c in range(qm_t.shape[1] // L):
        cols = slice(c * L, (c + 1) * L)
        for p in range(qm_t.shape[0] // LANES):
            qm_ref[0, p, c] = qm_t[p * LANES:(p + 1) * LANES, cols]
            om_ref[0, p, c] = om_t[p * LANES:(p + 1) * LANES, cols]
        for hd in range(vm_t.shape[0] // HEAD_DIM):
            vm_ref[0, hd, c] = jnp.concatenate(
                [vm_t[hd * HEAD_DIM:(hd + 1) * HEAD_DIM, cols], ones_rows_c], axis=0).astype(BF16)
    z = proj(_Z)
    g_ref[0] = (z * _sigmoid(z)).astype(BF16)
    gm = proj(_GM)
    gm_t = gm.T
    for p in range(gic_ref.shape[1]):
        lo, hi = 2 * GATE_PAD * p, 2 * GATE_PAD * p + GATE_PAD
        gic_ref[0, p] = gm[:, lo:hi]
        gfc_ref[0, p] = gm[:, hi:hi + GATE_PAD]
        for c in range(gm_t.shape[1] // L):
            gir_ref[0, p, c] = gm_t[lo:hi, c * L:(c + 1) * L]
            gfr_ref[0, p, c] = gm_t[hi:hi + GATE_PAD, c * L:(c + 1) * L]


def _in_projection(xa, modsel, w, layer, rope_a, rope_b):
    b, s, d = xa.shape
    tm = ROW_TILE
    n_pairs = (_QB[1] - _QB[0]) // LANES
    n_heads = (_VB[1] - _VB[0]) // HEAD_DIM
    L = MLSTM_CHUNK
    widths = [512, 128, 128, (n_pairs, LANES, tm), 384, (n_heads, VB_ROWS, tm),
              (n_pairs, LANES, L), 384, (n_heads, VB_ROWS, L), (n_pairs, LANES, L), 1024]
    tposed = lambda g, r, tt: pl.BlockSpec((1, g, tm // tt, r, tt), lambda bi, i: (bi, 0, i, 0, 0))
    row = lambda n: tposed(*n) if isinstance(n, tuple) else pl.BlockSpec((1, tm, n), lambda bi, i: (bi, i, 0))
    out_shape = [jax.ShapeDtypeStruct((b, n[0], s // n[2], n[1], n[2]) if isinstance(n, tuple) else (b, s, n), BF16)
                 for n in widths]
    gate_c = pl.BlockSpec((1, n_pairs, tm, GATE_PAD), lambda bi, i: (bi, 0, i, 0))
    gate_r = pl.BlockSpec((1, n_pairs, tm // L, GATE_PAD, L), lambda bi, i: (bi, 0, i, 0, 0))
    out_shape += [jax.ShapeDtypeStruct((b, n_pairs, s, GATE_PAD), F32)] * 2
    out_shape += [jax.ShapeDtypeStruct((b, n_pairs, s // L, GATE_PAD, L), F32)] * 2
    return pl.pallas_call(
        _inproj_kernel,
        grid=(b, s // tm),
        in_specs=[
            row(d),
            pl.BlockSpec((1, 1, 3, d), lambda bi, i: (bi, jnp.minimum(i, 1), 0, 0)),
            pl.BlockSpec((None, d, _W_COLS), lambda bi, i: (layer, 0, 0)),
            pl.BlockSpec((3, tm, LANES), lambda bi, i: (0, i, 0)),
            pl.BlockSpec((3, tm, LANES), lambda bi, i: (0, i, 0)),
        ],
        out_specs=[row(n) for n in widths] + [gate_c, gate_c, gate_r, gate_r],
        out_shape=out_shape,
        compiler_params=_params(("arbitrary", "arbitrary")),
        name="in_projection",
    )(xa, modsel, w, rope_a, rope_b)


def _attn_a_kernel(sink_ref, q_ref, k_ref, v_ref, o_ref, *, n_ctx, n_blocks):
    w = WINDOW
    s_len = n_blocks * w
    half = LANES // 2
    rows1 = lax.broadcasted_iota(jnp.int32, (4 * w, 1), 0)
    sink = jnp.where(rows1 < w, sink_ref[0],
                     jnp.where(rows1 < 2 * w, sink_ref[1], jnp.where(rows1 < 3 * w, sink_ref[2], sink_ref[3])))
    row = lax.broadcasted_iota(jnp.int32, (4 * w, 3 * w), 0) & (w - 1)
    col = lax.broadcasted_iota(jnp.int32, (4 * w, 3 * w), 1)
    lane = lax.broadcasted_iota(jnp.int32, (w, LANES), 1)
    k_ctx = k_ref[0, 0:n_ctx, :]
    v_ctx = v_ref[0, 0:n_ctx, :]

    def body(i, carry):
        r0 = pl.multiple_of(i * w, w)
        q = q_ref[0, pl.ds(r0, w), :]
        qs = jnp.concatenate([q[:, j * LANES:(j + 1) * LANES] for j in range(4)], axis=0)
        start = pl.multiple_of(jnp.clip(r0 - w, 0, s_len - 3 * w), w)
        s_loc = _dot_nt(qs, k_ref[0, pl.ds(start, 3 * w), :])
        s_ctx = _dot_nt(qs, k_ctx)
        kpos = col + start
        rel = kpos - r0 - row
        ok = (jnp.abs(rel) <= w) & (kpos >= jnp.where(r0 >= n_ctx, n_ctx, s_len))
        s_loc = jnp.where(ok, s_loc, -jnp.inf)
        m = jnp.maximum(jnp.maximum(jnp.max(s_loc, axis=1, keepdims=True),
                                    jnp.max(s_ctx, axis=1, keepdims=True)), sink)
        p_loc = jnp.exp(s_loc - m)
        p_ctx = jnp.exp(s_ctx - m)
        den = (jnp.sum(p_loc, axis=1, keepdims=True) + jnp.sum(p_ctx, axis=1, keepdims=True)
               + jnp.exp(sink - m))
        o = (_dot(p_loc.astype(BF16), v_ref[0, pl.ds(start, 3 * w), :])
             + _dot(p_ctx.astype(BF16), v_ctx)) / den
        left = jnp.where(lane < half, o[0:w], pltpu.roll(o[w:2 * w], half, 1))
        right = jnp.where(lane < half, pltpu.roll(o[2 * w:3 * w], half, 1), o[3 * w:])
        o_ref[0, pl.ds(r0, w), :] = jnp.concatenate([left, right], axis=1).astype(o_ref.dtype)
        return carry

    lax.fori_loop(0, n_blocks, body, 0, unroll=2)


def _attn_a(sink, qa, ka, va, n_ctx):
    b, s, _ = qa.shape
    seq = lambda n: pl.BlockSpec((1, s, n), lambda bi: (bi, 0, 0))
    return pl.pallas_call(
        functools.partial(_attn_a_kernel, n_ctx=n_ctx, n_blocks=s // WINDOW),
        grid=(b,),
        in_specs=[pl.BlockSpec(memory_space=pltpu.SMEM), seq(4 * LANES), seq(LANES), seq(LANES)],
        out_specs=seq(2 * LANES),
        out_shape=jax.ShapeDtypeStruct((b, s, 2 * LANES), BF16),
        compiler_params=_params(("arbitrary",)),
        name="window_attention",
    )(sink, qa, ka, va)


def _attn_b_kernel(li_ref, lam_ref, ng_ref, qt_ref, k_ref, vt_ref, o_ref,
                   qs_scr, sa_scr, sb_scr, sc_scr, mt_scr, acc_scr, *, nt, q0, nq, kg):
    tq, tk = qt_ref.shape[-1], vt_ref.shape[-1]
    hv = HEAD_DIM
    lam_init = li_ref[0]
    lv = lam_ref[...]
    lam = (jnp.exp(jnp.sum(lv[0:1] * lv[1:2], axis=1, keepdims=True))
           - jnp.exp(jnp.sum(lv[2:3] * lv[3:4], axis=1, keepdims=True)) + lam_init)

    def first_tile(i, slot):
        qt = qt_ref[0, 0, q0 + i]
        unit = lax.broadcasted_iota(jnp.int32, qt.shape, 0) // B_QK_DIM
        zero = jnp.zeros_like(qt)
        qs_scr[slot] = jnp.concatenate([jnp.where(unit == u, qt, zero) for u in range(4)], axis=1)
        mt_scr[...] = fetch(0, sc_scr, slot)

    def fetch(t, s_scr, slot, n=1):
        off = pl.multiple_of(t * tk, tk)
        s_new = _dot(k_ref[0, pl.ds(off, n * tk), :], qs_scr[slot])
        s_scr[...] = s_new
        return jnp.max(s_new, axis=0, keepdims=True)

    def consume(s_scr, mt, m, t, n=1):
        m_new = jnp.maximum(m, mt)
        alpha = jnp.exp2(m - m_new)
        p = jnp.exp2(s_scr[...] - m_new).astype(BF16)
        for hd in range(2):
            cols = slice(hd * 2 * tq, (hd + 1) * 2 * tq)
            pv = _dot(vt_ref[0, hd, t], p[0:tk, cols])
            for c in range(1, n):
                pv = pv + _dot(vt_ref[0, hd, t + c], p[c * tk:(c + 1) * tk, cols])
            acc_scr[hd] = alpha[:, cols] * acc_scr[hd] + pv
        return m_new

    def head_out(hd):
        acc = acc_scr[hd]
        o = acc[0:hv] * (1.0 / acc[hv:hv + 1])
        od = o[:, 0:tq] - lam * o[:, tq:]
        ms = jnp.mean(od * od, axis=0, keepdims=True)
        return od * lax.rsqrt(ms + LN_EPS) * ng_ref[...]

    def query_tile(i, carry):
        cur = i % 2

        def successor():
            if nq > 1:
                first_tile(jnp.minimum(i + 1, nq - 1), 1 - cur)

        acc_scr[...] = jnp.zeros_like(acc_scr)
        m = jnp.full((1, 4 * tq), -jnp.inf, F32)
        mt_c = mt_scr[...]

        nb = (nt - 1) // kg
        big = lambda j: 1 + kg * j

        def body(tt, carry):
            m, mt_a = carry
            j0 = 2 * tt
            mt_b = fetch(big(j0 + 1), sb_scr, cur, kg)
            m = consume(sa_scr, mt_a, m, big(j0), kg)
            mt_a = fetch(big(j0 + 2), sa_scr, cur, kg)
            m = consume(sb_scr, mt_b, m, big(j0 + 1), kg)
            return m, mt_a

        if nb == 0:
            successor()
            consume(sc_scr, mt_c, m, 0)
        else:
            mt_a = fetch(big(0), sa_scr, cur, kg)
            m = consume(sc_scr, mt_c, m, 0)
            m, mt_a = lax.fori_loop(0, (nb - 1) // 2, body, (m, mt_a))
            if nb % 2 == 0:
                mt_b = fetch(big(nb - 1), sb_scr, cur, kg)
                m = consume(sa_scr, mt_a, m, big(nb - 2), kg)
                successor()
                consume(sb_scr, mt_b, m, big(nb - 1), kg)
            else:
                successor()
                consume(sa_scr, mt_a, m, big(nb - 1), kg)

        y_t = jnp.concatenate([head_out(0), head_out(1)], axis=0) * (1.0 - lam_init)
        o_ref[0, pl.ds(pl.multiple_of(i * tq, tq), tq), :] = y_t.T.astype(o_ref.dtype)
        return carry

    first_tile(0, 0)
    lax.fori_loop(0, nq, query_tile, 0)


def _attn_b_call(lam_init, lam_vec, norm_g, qbt, kb, vbt, q0, nq, nt):
    b, pairs, n_tiles, _, tq = qbt.shape
    vrows, tk = vbt.shape[-2:]
    kg = B_KEY_GROUP if (nt - 1) % B_KEY_GROUP == 0 else 1
    return pl.pallas_call(
        functools.partial(_attn_b_kernel, nt=nt, q0=q0, nq=nq, kg=kg),
        grid=(b, pairs),
        in_specs=[
            pl.BlockSpec(memory_space=pltpu.SMEM),
            pl.BlockSpec((4, B_QK_DIM), lambda bi, p: (0, 0)),
            pl.BlockSpec((HEAD_DIM, 1), lambda bi, p: (0, 0)),
            pl.BlockSpec((1, 1, n_tiles, LANES, tq), lambda bi, p: (bi, p, 0, 0, 0)),
            pl.BlockSpec((1, nt * tk, LANES), lambda bi, p: (bi, 0, p)),
            pl.BlockSpec((1, 2, nt, vrows, tk), lambda bi, p: (bi, p, 0, 0, 0)),
        ],
        out_specs=pl.BlockSpec((1, nq * tq, LANES), lambda bi, p: (bi, 0, p)),
        out_shape=jax.ShapeDtypeStruct((b, nq * tq, pairs * LANES), BF16),
        scratch_shapes=[pltpu.VMEM((2, LANES, 4 * tq), BF16), pltpu.VMEM((kg * tk, 4 * tq), F32),
                        pltpu.VMEM((kg * tk, 4 * tq), F32), pltpu.VMEM((tk, 4 * tq), F32),
                        pltpu.VMEM((1, 4 * tq), F32), pltpu.VMEM((2, vrows, 2 * tq), F32)],
        compiler_params=_params(("arbitrary", "arbitrary")),
        name="diff_attention",
    )(lam_init, lam_vec, norm_g, qbt, kb, vbt)


def _attn_b(lam_init, lam_vec, norm_g, qbt, kb, vbt, n_ctx, with_ctx):
    n_tiles, tq = qbt.shape[2], qbt.shape[4]
    tk = vbt.shape[-1]
    n_ctx_tiles = n_ctx // tq
    y_lat = _attn_b_call(lam_init, lam_vec, norm_g, qbt, kb, vbt, n_ctx_tiles, n_tiles - n_ctx_tiles, n_tiles)
    y_ctx = _attn_b_call(lam_init, lam_vec, norm_g, qbt, kb, vbt, 0, n_ctx_tiles, n_ctx // tk) if with_ctx else None
    return y_lat, y_ctx


def _log_sigmoid(x):
    return jnp.minimum(x, 0.0) - jnp.log(1.0 + jnp.exp(-jnp.abs(x)))


def _mlstm_kernel(bir_ref, bic_ref, bfr_ref, bfc_ref, ng_ref, qt_ref, k_ref, vt_ref, gic_ref, gfc_ref, gir_ref,
                  gfr_ref, ogt_ref,
                  o_ref, hf_scr, hb_scr, st_scr, ir_scr, br_scr, gc_scr, *, n_ctx_chunks, n_chunks):
    L = MLSTM_CHUNK
    hv = HEAD_DIM
    r_i = lax.broadcasted_iota(jnp.int32, (L, L), 0)
    c_i = lax.broadcasted_iota(jnp.int32, (L, L), 1)
    upper = r_i <= c_i
    lower = r_i >= c_i
    t_up = jnp.where(upper, 1.0, 0.0).astype(BF16)
    t_low = jnp.where(lower, 1.0, 0.0).astype(BF16)
    row_q = lax.broadcasted_iota(jnp.int32, (LANES, L), 0)
    sel = [jnp.where(row_q < hv, 1.0, 0.0).astype(BF16), jnp.where(row_q < hv, 0.0, 1.0).astype(BF16)]
    fwd_r = lax.broadcasted_iota(jnp.int32, (GATE_PAD, L), 0) < 2
    fwd_c = lax.broadcasted_iota(jnp.int32, (L, GATE_PAD), 1) < 2
    st_scr[...] = jnp.zeros_like(st_scr)

    def gates(c, carry):
        r0 = pl.multiple_of(c * L, L)
        frow = _log_sigmoid(gfr_ref[0, 0, c] + bfr_ref[0])
        hi, lo = _split_bf16(frow)
        ir_scr[c] = gir_ref[0, 0, c] + bir_ref[0]
        br_scr[c] = jnp.where(fwd_r, _dot(hi, t_up) + _dot(lo, t_up), _dot(hi, t_low) + _dot(lo, t_low))
        fcol = _log_sigmoid(gfc_ref[0, 0, pl.ds(r0, L), :] + bfc_ref[0])
        hi, lo = _split_bf16(fcol)
        bcol = jnp.where(fwd_c, _dot(t_low, hi) + _dot(t_low, lo), _dot(t_up, hi) + _dot(t_up, lo))
        gc_scr[c] = gic_ref[0, 0, pl.ds(r0, L), :] + bic_ref[0] - bcol
        return carry
    lax.fori_loop(0, n_chunks, gates, 0, unroll=4)

    def chunk_step(direction, c, m_prev):
        r0 = pl.multiple_of(c * L, L)
        kc = k_ref[0, pl.ds(r0, L), :]
        qtc = qt_ref[0, 0, c]
        irow8, brow8, gcol8 = ir_scr[c], br_scr[c], gc_scr[c]
        tri = upper if direction == 0 else lower
        h_scr = hf_scr if direction == 0 else hb_scr
        qtm = [qtc * sel[a] for a in range(2)]
        s_both = _dot(kc, jnp.concatenate(qtm, axis=1))
        m_news, vws, decays = [], [], []
        for a in range(2):
            ci = 2 * direction + a
            b_row, i_row, g_col = brow8[ci:ci + 1], irow8[ci:ci + 1], gcol8[:, ci:ci + 1]
            log_d = jnp.where(tri, g_col + b_row, -jnp.inf)
            m_in = b_row + m_prev[a]
            m_t = jnp.maximum(m_in, jnp.max(log_d, axis=0, keepdims=True))
            w_in = jnp.exp(m_in - m_t)
            sd = (s_both[:, a * L:(a + 1) * L] * jnp.exp(log_d - m_t)).astype(BF16)
            qw = (qtm[a].astype(F32) * w_in).astype(BF16)
            vt = vt_ref[0, a, c]
            numden = _dot(jnp.concatenate([st_scr[ci].astype(BF16), vt], axis=1),
                          jnp.concatenate([qw, sd], axis=0))
            h_scr[c, a * hv:(a + 1) * hv, :] = (
                numden[0:hv] / jnp.maximum(jnp.abs(numden[hv:hv + 1]), jnp.exp(-m_t)))
            b_last = jnp.min(b_row, axis=1, keepdims=True)
            log_w = b_last - b_row + i_row
            m_new = jnp.maximum(b_last + m_prev[a], jnp.max(log_w, axis=1, keepdims=True))
            vws.append((vt.astype(F32) * jnp.exp(log_w - m_new)).astype(BF16))
            decays.append(jnp.exp(b_last + m_prev[a] - m_new))
            m_news.append(m_new)
        rows = vt_ref.shape[3]
        upd = _dot(jnp.concatenate(vws, axis=0), kc)
        for a in range(2):
            ci = 2 * direction + a
            st_scr[ci] = decays[a] * st_scr[ci] + upd[a * rows:(a + 1) * rows]
        return tuple(m_news)

    def body(j, carry):
        mf, mb = carry
        cb = jnp.where(j < n_ctx_chunks, n_ctx_chunks - 1 - j, n_chunks - 1 + n_ctx_chunks - j)
        return chunk_step(0, j, mf), chunk_step(1, cb, mb)

    z11 = jnp.zeros((1, 1), F32)
    lax.fori_loop(0, n_chunks, body, ((z11, z11), (z11, z11)), unroll=4)

    def finish(c, carry):
        r0 = pl.multiple_of(c * L, L)
        h = hf_scr[c] + hb_scr[c]
        outs = []
        for a in range(2):
            ha = h[a * hv:(a + 1) * hv]
            outs.append(ha * lax.rsqrt(jnp.mean(ha * ha, axis=0, keepdims=True) + LN_EPS))
        y_t = jnp.concatenate(outs, axis=0) * ng_ref[0] * ogt_ref[0, 0, c].astype(F32)
        o_ref[0, pl.ds(r0, L), :] = y_t.T.astype(o_ref.dtype)
        return carry
    lax.fori_loop(0, n_chunks, finish, 0, unroll=4)


def _mlstm(bias_i, bias_f, norm_g, qmt, km, vmt, gic, gfc, gir, gfr, ogt, n_ctx):
    b, pairs, nc, _, L = qmt.shape
    s = nc * L
    vrows = vmt.shape[3]
    seq = pl.BlockSpec((1, s, LANES), lambda bi, p: (bi, 0, p))
    pair_tiles = pl.BlockSpec((1, 1, nc, LANES, L), lambda bi, p: (bi, p, 0, 0, 0))
    gate_c = pl.BlockSpec((1, 1, s, GATE_PAD), lambda bi, p: (bi, p, 0, 0))
    gate_r = pl.BlockSpec((1, 1, nc, GATE_PAD, L), lambda bi, p: (bi, p, 0, 0, 0))
    bias_r = pl.BlockSpec((1, GATE_PAD, 1), lambda bi, p: (p, 0, 0))
    bias_c = pl.BlockSpec((1, 1, GATE_PAD), lambda bi, p: (p, 0, 0))
    return pl.pallas_call(
        functools.partial(_mlstm_kernel, n_ctx_chunks=n_ctx // L, n_chunks=nc),
        grid=(b, pairs),
        in_specs=[
            bias_r, bias_c, bias_r, bias_c,
            pl.BlockSpec((1, LANES, 1), lambda bi, p: (p, 0, 0)),
            pair_tiles, seq,
            pl.BlockSpec((1, 2, nc, vrows, L), lambda bi, p: (bi, p, 0, 0, 0)),
            gate_c, gate_c, gate_r, gate_r,
            pair_tiles,
        ],
        out_specs=seq,
        out_shape=jax.ShapeDtypeStruct((b, s, pairs * LANES), BF16),
        scratch_shapes=[pltpu.VMEM((nc, LANES, L), F32), pltpu.VMEM((nc, LANES, L), F32),
                        pltpu.VMEM((4, vrows, LANES), F32),
                        pltpu.VMEM((nc, GATE_PAD, L), F32), pltpu.VMEM((nc, GATE_PAD, L), F32),
                        pltpu.VMEM((nc, L, GATE_PAD), F32)],
        compiler_params=_params(("arbitrary", "arbitrary")),
        name="mlstm",
    )(bias_i[:, :, None], bias_i[:, None, :], bias_f[:, :, None], bias_f[:, None, :], norm_g[:, :, None],
      qmt, km, vmt, gic, gfc, gir, gfr, ogt)


def _out_kernel(x_ref, mod_ref, ya_ref, yb_ref, *refs, alpha, n_ctx_tiles):
    ym_ref, g_ref, w_ref, lng_ref, lnb_ref, o_ref = refs[-6:]
    x = x_ref[0]
    gate = mod_ref[0, 0, 2:3, :]
    g = g_ref[0].astype(F32)
    na, nb_ = ya_ref.shape[2], yb_ref.shape[2]
    yb = yb_ref[0]
    if len(refs) == 7:
        yb = jnp.where(pl.program_id(1) < n_ctx_tiles, refs[0][0], yb)
    mix_a = (ya_ref[0].astype(F32) * g[:, :na]).astype(BF16)
    mix_b = (yb.astype(F32) * g[:, na:na + nb_]).astype(BF16)
    mix_m = (ym_ref[0].astype(F32) * g[:, na + nb_:]).astype(BF16)
    y = _dot(jnp.concatenate([mix_a, mix_b, mix_m], axis=1), w_ref[...])
    r = alpha * x + gate * y
    mu = jnp.mean(r, axis=1, keepdims=True)
    d = r - mu
    var = jnp.mean(d * d, axis=1, keepdims=True)
    o_ref[0] = d * lax.rsqrt(var + LN_EPS) * lng_ref[...] + lnb_ref[...]


def _out_projection(xa, modsel, ya, yb_lat, yb_ctx, ym, g, w, layer, ln_g, ln_b, alpha, n_ctx_tiles):
    b, s, d = xa.shape
    tm = ROW_TILE
    row0 = 0 if yb_ctx is not None else n_ctx_tiles
    row = lambda n: pl.BlockSpec((1, tm, n), lambda bi, i: (bi, i + row0, 0))
    vec = pl.BlockSpec((1, d), lambda bi, i: (0, 0))
    nb_ = yb_lat.shape[2]
    in_specs = [
        row(d),
        pl.BlockSpec((1, 1, 3, d), lambda bi, i: (bi, jnp.minimum(i + row0, 1), 0, 0)),
        row(ya.shape[2]),
        pl.BlockSpec((1, tm, nb_), lambda bi, i: (bi, jnp.maximum(i + row0 - n_ctx_tiles, 0), 0)),
    ]
    args = [xa, modsel, ya, yb_lat]
    if yb_ctx is not None:
        in_specs.append(pl.BlockSpec((1, tm, nb_), lambda bi, i: (bi, jnp.minimum(i, n_ctx_tiles - 1), 0)))
        args.append(yb_ctx)
    in_specs += [row(ym.shape[2]), row(d), pl.BlockSpec((None,) + w.shape[1:], lambda bi, i: (layer, 0, 0)),
                 vec, vec]
    args += [ym, g, w, ln_g, ln_b]
    return pl.pallas_call(
        functools.partial(_out_kernel, alpha=alpha, n_ctx_tiles=n_ctx_tiles),
        grid=(b, s // tm - row0),
        in_specs=in_specs,
        out_specs=pl.BlockSpec((1, tm, d), lambda bi, i: (bi, i, 0)),
        out_shape=jax.ShapeDtypeStruct((b, s - row0 * tm, d), F32),
        compiler_params=_params(("arbitrary", "arbitrary")),
        name="out_projection",
    )(*args)


def _in_weight_columns(d_in):
    z = d_in
    src = {}
    off = 0
    for name, n in (("qa", 256), ("ka", 128), ("va", 128), ("za", 256), ("qb", 384), ("kb", 384), ("vb", 384),
                    ("zb", 384), ("qm", 384), ("km", 384), ("vm", 384), ("om", 384), ("zm", 384), ("gm", 24)):
        src[name] = np.arange(off, off + n)
        off += n
    assert off == d_in
    cols = []
    for hd in range(4):
        blk = np.full(LANES, z)
        gq = hd // 2
        blk[64 * gq:64 * gq + 64] = src["qa"][64 * hd:64 * hd + 64]
        cols.append(blk)
    for name in ("ka", "va", "qb", "kb", "vb", "qm", "km", "vm", "om", "za", "zb", "zm"):
        cols.append(src[name])
    heads = len(src["gm"]) // 4
    for p in range(heads // 2):
        for gate in range(2):
            cols.append(np.array([src["gm"][di * 2 * heads + gate * heads + 2 * p + a]
                                  for di in range(2) for a in range(2)] + [z] * (GATE_PAD - 4)))
    cols.append(np.full(LANES - 2 * GATE_PAD * (heads // 2), z))
    cols = np.concatenate(cols)
    assert cols.shape[0] == _W_COLS
    return cols


def _permute_columns(w, cols):
    n = w.shape[-1]
    pieces, start = [], 0
    breaks = np.flatnonzero(np.diff(cols) != 1) + 1
    for stop in list(breaks) + [len(cols)]:
        run = cols[start:stop]
        if len(run) >= 32:
            pieces.append(lax.slice_in_dim(w, int(run[0]), int(run[-1]) + 1, axis=w.ndim - 1))
        elif pieces and isinstance(pieces[-1], list):
            pieces[-1].extend(run)
        else:
            pieces.append(list(run))
        start = stop
    lo = min(min(p) for p in pieces if isinstance(p, list))
    tail = jnp.concatenate([lax.slice_in_dim(w, lo, n, axis=w.ndim - 1), jnp.zeros(w.shape[:-1] + (1,), w.dtype)], -1)
    pieces = [jnp.take(tail, np.array(p) - lo, axis=-1) if isinstance(p, list) else p for p in pieces]
    return jnp.concatenate(pieces, axis=-1)


def _rope_table(n_tokens, n_ctx, dim):
    f32 = np.float32
    rows = n_tokens // GRID_W
    rowp = np.broadcast_to(np.arange(rows, dtype=f32)[:, None], (rows, GRID_W)).reshape(-1)
    colp = np.broadcast_to(np.arange(GRID_W, dtype=f32)[None, :], (rows, GRID_W)).reshape(-1)
    n_freq = dim // 4
    inv = np.power(f32(ROPE_BASE), -np.arange(n_freq, dtype=f32) / f32(n_freq)).astype(f32)
    ar = rowp[:, None] * inv
    ac = colp[:, None] * inv
    ang = np.concatenate([ar, ar, ac, ac], -1).astype(f32)
    cos, sin = np.cos(ang), np.sin(ang)
    odd = (np.arange(dim) // n_freq) % 2 == 1
    sin_p = np.where(odd, sin, f32(0))
    sin_m = np.where(odd, f32(0), -sin)
    tab = np.stack([cos, sin_p, sin_m])
    ident = np.stack([np.ones((n_ctx, dim), f32), np.zeros((n_ctx, dim), f32), np.zeros((n_ctx, dim), f32)])
    tab = np.concatenate([ident, tab], axis=1)
    return jnp.asarray(np.tile(tab, (1, 1, LANES // dim)).astype(f32))


def kernel(x, c, ctx, c_ctx, w_mod, b_mod, w_in, attn_sink, diff_lambda, diff_norm_g, mlstm_i_bias,
           mlstm_f_bias, mlstm_norm_g, w_out, ln_g, ln_b):
    b, t, d = x.shape
    n_ctx = ctx.shape[1]
    depth = w_mod.shape[0]
    d_in = w_in.shape[2]
    alpha = (2 * depth) ** 0.25

    xa = jnp.concatenate([ctx, x], axis=1)
    rope_a = _rope_table(t, n_ctx, HEAD_DIM)
    rope_b = _rope_table(t, n_ctx, B_QK_DIM)

    rows = -(-(b + 1) // 8) * 8
    cc = jnp.concatenate([c, c_ctx[None, :], jnp.zeros((rows - b - 1, d), F32)], axis=0)
    mod = _modulation(cc, w_mod, b_mod).reshape(depth, rows, 3, d)

    w_in_p = _permute_columns(w_in, _in_weight_columns(d_in)).astype(BF16)
    w_out_b = w_out.astype(BF16)
    norm_b = diff_norm_g[:, :, None]

    n_pairs = mlstm_i_bias.shape[2] // 2
    heads = mlstm_i_bias.shape[2]
    chains = [(di, a) for di in range(2) for a in range(2)]
    bias_idx = np.array([[di * heads + 2 * p + a for di, a in chains] for p in range(n_pairs)])
    pad8 = lambda v: jnp.pad(v.reshape(-1)[bias_idx], ((0, 0), (0, GATE_PAD - 4)))

    for l in range(depth):
        with_ctx = l < depth - 1
        lam_init = 0.8 - 0.6 * math.exp(-0.3 * l)
        modsel = jnp.stack([jnp.broadcast_to(mod[l, b], (b, 3, d)), mod[l, :b]], axis=1)
        (qa, ka, va, qb, kb, vb, qm, km, vm, og, g, gic, gfc, gir, gfr) = _in_projection(
            xa, modsel, w_in_p, l, rope_a, rope_b)

        ya = _attn_a(attn_sink[l], qa, ka, va, n_ctx)
        yb, yb_ctx = _attn_b(jnp.full((1,), lam_init, F32), diff_lambda[l], norm_b[l], qb, kb, vb, n_ctx, with_ctx)

        ym = _mlstm(pad8(mlstm_i_bias[l]), pad8(mlstm_f_bias[l]), mlstm_norm_g[l].reshape(n_pairs, LANES),
                    qm, km, vm, gic, gfc, gir, gfr, og, n_ctx)

        xa = _out_projection(xa, modsel, ya, yb, yb_ctx, ym, g, w_out_b, l, ln_g[l][None, :], ln_b[l][None, :],
                             alpha, n_ctx // ROW_TILE)
    return xa
```

```python
import functools
import math

import numpy as np
import jax
import jax.numpy as jnp
from jax import lax
from jax.experimental import pallas as pl
from jax.experimental.pallas import tpu as pltpu

F32 = jnp.float32
BF16 = jnp.bfloat16

LANES = 128
GRID_W = 64
HEAD_DIM = 64
ROPE_BASE = 10000.0
LN_EPS = 1e-5
WINDOW = 128
B_QK_DIM = 32
LOG2E = math.log2(math.e)
VB_ROWS = HEAD_DIM + 16
MLSTM_CHUNK = 128
GATE_PAD = 8
MOD_COL_TILE = 1024
B_KEY_GROUP = 2
ROW_TILE = 256
VMEM_LIMIT = 56 * 1024 * 1024


def _dot(a, b):
    return jnp.dot(a, b, preferred_element_type=F32)


def _dot_nt(a, b):
    return lax.dot_general(a, b, (((1,), (1,)), ((), ())), preferred_element_type=F32)


def _split_bf16(a):
    hi = a.astype(BF16)
    lo = (a - hi.astype(F32)).astype(BF16)
    return hi, lo


def _sigmoid(x):
    return 1.0 / (1.0 + jnp.exp(-x))


def _params(sem):
    return pltpu.CompilerParams(dimension_semantics=sem, vmem_limit_bytes=VMEM_LIMIT)


def _mod_kernel(c_ref, w_ref, b_ref, o_ref):
    c = c_ref[...]
    a = c * _sigmoid(c)
    a_hi, a_lo = _split_bf16(a)
    w_hi, w_lo = _split_bf16(w_ref[0])
    o_ref[0] = _dot(a_hi, w_hi) + _dot(a_lo, w_hi) + _dot(a_hi, w_lo) + b_ref[0]


def _modulation(cc, w_mod, b_mod):
    depth, d, n = w_mod.shape
    r = cc.shape[0]
    tn = MOD_COL_TILE
    return pl.pallas_call(
        _mod_kernel,
        grid=(depth, n // tn),
        in_specs=[
            pl.BlockSpec((r, d), lambda l, j: (0, 0)),
            pl.BlockSpec((1, d, tn), lambda l, j: (l, 0, j)),
            pl.BlockSpec((1, 1, tn), lambda l, j: (l, 0, j)),
        ],
        out_specs=pl.BlockSpec((1, r, tn), lambda l, j: (l, 0, j)),
        out_shape=jax.ShapeDtypeStruct((depth, r, n), F32),
        compiler_params=_params(("arbitrary", "arbitrary")),
        name="modulation",
    )(cc, w_mod, b_mod.reshape(depth, 1, n))


def _stream_shape(xa):
    if isinstance(xa, tuple):
        ctx, x = xa
        return x.shape[0], ctx.shape[1] + x.shape[1], x.shape[2]
    return xa.shape


def _stream_specs(xa, tm, row0):
    if not isinstance(xa, tuple):
        return [pl.BlockSpec((1, tm, xa.shape[2]), lambda bi, i: (bi, i + row0, 0))], [xa]
    ctx, x = xa
    nct = ctx.shape[1] // tm
    return ([pl.BlockSpec((1, tm, ctx.shape[2]), lambda bi, i: (bi, jnp.minimum(i + row0, nct - 1), 0)),
             pl.BlockSpec((1, tm, x.shape[2]), lambda bi, i: (bi, jnp.maximum(i + row0 - nct, 0), 0))], [ctx, x])


def _stream_tile(refs, n_ctx_tiles, row0):
    if len(refs) == 1:
        return refs[0][0]
    return jnp.where(pl.program_id(1) + row0 < n_ctx_tiles, refs[0][0], refs[1][0])


_QA, _KA, _VA = (0, 512), (512, 640), (640, 768)
_QB, _KB, _VB = (768, 1152), (1152, 1536), (1536, 1920)
_QM, _KM, _VM = (1920, 2304), (2304, 2688), (2688, 3072)
_OM, _Z, _GM = (3072, 3456), (3456, 4480), (4480, 4608)
_W_COLS = 4608
_W_GROUPS = ((0, 768), (768, 1536), (1536, 2304), (2304, 3072), (3072, 4608))


def _rope(t, tab_ref, quarter):
    cos, sin_p, sin_m = tab_ref[0], tab_ref[1], tab_ref[2]
    outs = []
    for j in range(t.shape[1] // LANES):
        tj = t[:, j * LANES:(j + 1) * LANES]
        outs.append(tj * cos + pltpu.roll(tj, quarter, 1) * sin_p + pltpu.roll(tj, LANES - quarter, 1) * sin_m)
    return outs[0] if len(outs) == 1 else jnp.concatenate(outs, axis=1)


def _inproj_kernel(*refs, n_stream, n_ctx_tiles):
    (mod_ref, w_ref, ra_ref, rb_ref, qa_ref, ka_ref, va_ref, qb_ref, kb_ref, vb_ref,
     qm_ref, km_ref, vm_ref, om_ref, g_ref, gic_ref, gfc_ref, gir_ref, gfr_ref) = refs[n_stream:]
    x = _stream_tile(refs[:n_stream], n_ctx_tiles, 0)
    shift = mod_ref[0, 0, 0:1, :]
    scale = mod_ref[0, 0, 1:2, :]
    h = (x * (1.0 + scale) + shift).astype(BF16)

    group_dots = {}

    def proj(cols):
        lo, hi = next(g for g in _W_GROUPS if g[0] <= cols[0] and cols[1] <= g[1])
        if lo not in group_dots:
            group_dots[lo] = _dot(h, w_ref[:, lo:hi])
        return group_dots[lo][:, cols[0] - lo:cols[1] - lo]

    qa_ref[0] = (_rope(proj(_QA), ra_ref, HEAD_DIM // 4) * (HEAD_DIM ** -0.5)).astype(BF16)
    ka_ref[0] = _rope(proj(_KA), ra_ref, HEAD_DIM // 4).astype(BF16)
    va_ref[0] = proj(_VA).astype(BF16)
    qb_t = (_rope(proj(_QB), rb_ref, B_QK_DIM // 4) * (B_QK_DIM ** -0.5 * LOG2E)).T.astype(BF16)
    for p in range(qb_t.shape[0] // LANES):
        qb_ref[0, p, 0] = qb_t[p * LANES:(p + 1) * LANES]
    vb_t = proj(_VB).T
    ones_rows = jnp.where(lax.broadcasted_iota(jnp.int32, (VB_ROWS - HEAD_DIM, vb_t.shape[1]), 0) == 0, 1.0, 0.0)
    for hd in range(vb_t.shape[0] // HEAD_DIM):
        vb_ref[0, hd, 0] = jnp.concatenate([vb_t[hd * HEAD_DIM:(hd + 1) * HEAD_DIM], ones_rows], axis=0).astype(BF16)
    kb_ref[0] = _rope(proj(_KB), rb_ref, B_QK_DIM // 4).astype(BF16)
    km_ref[0] = (proj(_KM) * (HEAD_DIM ** -0.5)).astype(BF16)
    qm_t = proj(_QM).T.astype(BF16)
    om_t = _sigmoid(proj(_OM)).T.astype(BF16)
    vm_t = proj(_VM).T
    L = MLSTM_CHUNK
    ones_rows_c = jnp.where(lax.broadcasted_iota(jnp.int32, (VB_ROWS - HEAD_DIM, L), 0) == 0, 1.0, 0.0)
    for c in range(qm_t.shape[1] // L):
        cols = slice(c * L, (c + 1) * L)
        for p in range(qm_t.shape[0] // LANES):
            qm_ref[0, p, c] = qm_t[p * LANES:(p + 1) * LANES, cols]
            om_ref[0, p, c] = om_t[p * LANES:(p + 1) * LANES, cols]
        for hd in range(vm_t.shape[0] // HEAD_DIM):
            vm_ref[0, hd, c] = jnp.concatenate(
                [vm_t[hd * HEAD_DIM:(hd + 1) * HEAD_DIM, cols], ones_rows_c], axis=0).astype(BF16)
    z = proj(_Z)
    g_ref[0] = (z * _sigmoid(z)).astype(BF16)
    gm = proj(_GM)
    gm_t = gm.T
    for p in range(gic_ref.shape[1]):
        lo, hi = 2 * GATE_PAD * p, 2 * GATE_PAD * p + GATE_PAD
        gic_ref[0, p] = gm[:, lo:hi]
        gfc_ref[0, p] = gm[:, hi:hi + GATE_PAD]
        for c in range(gm_t.shape[1] // L):
            gir_ref[0, p, c] = gm_t[lo:hi, c * L:(c + 1) * L]
            gfr_ref[0, p, c] = gm_t[hi:hi + GATE_PAD, c * L:(c + 1) * L]


def _in_projection(xa, modsel, w, layer, rope_a, rope_b, n_ctx):
    b, s, d = _stream_shape(xa)
    tm = ROW_TILE
    x_specs, x_args = _stream_specs(xa, tm, 0)
    n_pairs = (_QB[1] - _QB[0]) // LANES
    n_heads = (_VB[1] - _VB[0]) // HEAD_DIM
    L = MLSTM_CHUNK
    widths = [512, 128, 128, (n_pairs, LANES, tm), 384, (n_heads, VB_ROWS, tm),
              (n_pairs, LANES, L), 384, (n_heads, VB_ROWS, L), (n_pairs, LANES, L), 1024]
    tposed = lambda g, r, tt: pl.BlockSpec((1, g, tm // tt, r, tt), lambda bi, i: (bi, 0, i, 0, 0))
    row = lambda n: tposed(*n) if isinstance(n, tuple) else pl.BlockSpec((1, tm, n), lambda bi, i: (bi, i, 0))
    out_shape = [jax.ShapeDtypeStruct((b, n[0], s // n[2], n[1], n[2]) if isinstance(n, tuple) else (b, s, n), BF16)
                 for n in widths]
    gate_c = pl.BlockSpec((1, n_pairs, tm, GATE_PAD), lambda bi, i: (bi, 0, i, 0))
    gate_r = pl.BlockSpec((1, n_pairs, tm // L, GATE_PAD, L), lambda bi, i: (bi, 0, i, 0, 0))
    out_shape += [jax.ShapeDtypeStruct((b, n_pairs, s, GATE_PAD), F32)] * 2
    out_shape += [jax.ShapeDtypeStruct((b, n_pairs, s // L, GATE_PAD, L), F32)] * 2
    return pl.pallas_call(
        functools.partial(_inproj_kernel, n_stream=len(x_args), n_ctx_tiles=n_ctx // tm),
        grid=(b, s // tm),
        in_specs=x_specs + [
            pl.BlockSpec((1, 1, 3, d), lambda bi, i: (bi, jnp.minimum(i, 1), 0, 0)),
            pl.BlockSpec((None, d, _W_COLS), lambda bi, i: (layer, 0, 0)),
            pl.BlockSpec((3, tm, LANES), lambda bi, i: (0, i, 0)),
            pl.BlockSpec((3, tm, LANES), lambda bi, i: (0, i, 0)),
        ],
        out_specs=[row(n) for n in widths] + [gate_c, gate_c, gate_r, gate_r],
        out_shape=out_shape,
        compiler_params=_params(("arbitrary", "arbitrary")),
        name="in_projection",
    )(*x_args, modsel, w, rope_a, rope_b)


def _attn_a_kernel(sink_ref, q_ref, k_ref, v_ref, o_ref, *, n_ctx, n_blocks):
    w = WINDOW
    s_len = n_blocks * w
    half = LANES // 2
    rows1 = lax.broadcasted_iota(jnp.int32, (4 * w, 1), 0)
    sink = jnp.where(rows1 < w, sink_ref[0],
                     jnp.where(rows1 < 2 * w, sink_ref[1], jnp.where(rows1 < 3 * w, sink_ref[2], sink_ref[3])))
    row = lax.broadcasted_iota(jnp.int32, (4 * w, 3 * w), 0) & (w - 1)
    col = lax.broadcasted_iota(jnp.int32, (4 * w, 3 * w), 1)
    lane = lax.broadcasted_iota(jnp.int32, (w, LANES), 1)
    k_ctx = k_ref[0, 0:n_ctx, :]
    v_ctx = v_ref[0, 0:n_ctx, :]

    def body(i, carry):
        r0 = pl.multiple_of(i * w, w)
        q = q_ref[0, pl.ds(r0, w), :]
        qs = jnp.concatenate([q[:, j * LANES:(j + 1) * LANES] for j in range(4)], axis=0)
        start = pl.multiple_of(jnp.clip(r0 - w, 0, s_len - 3 * w), w)
        s_loc = _dot_nt(qs, k_ref[0, pl.ds(start, 3 * w), :])
        s_ctx = _dot_nt(qs, k_ctx)
        kpos = col + start
        rel = kpos - r0 - row
        ok = (jnp.abs(rel) <= w) & (kpos >= jnp.where(r0 >= n_ctx, n_ctx, s_len))
        s_loc = jnp.where(ok, s_loc, -jnp.inf)
        m = jnp.maximum(jnp.maximum(jnp.max(s_loc, axis=1, keepdims=True),
                                    jnp.max(s_ctx, axis=1, keepdims=True)), sink)
        p_loc = jnp.exp(s_loc - m)
        p_ctx = jnp.exp(s_ctx - m)
        den = (jnp.sum(p_loc, axis=1, keepdims=True) + jnp.sum(p_ctx, axis=1, keepdims=True)
               + jnp.exp(sink - m))
        o = (_dot(p_loc.astype(BF16), v_ref[0, pl.ds(start, 3 * w), :])
             + _dot(p_ctx.astype(BF16), v_ctx)) / den
        left = jnp.where(lane < half, o[0:w], pltpu.roll(o[w:2 * w], half, 1))
        right = jnp.where(lane < half, pltpu.roll(o[2 * w:3 * w], half, 1), o[3 * w:])
        o_ref[0, pl.ds(r0, w), :] = jnp.concatenate([left, right], axis=1).astype(o_ref.dtype)
        return carry

    lax.fori_loop(0, n_blocks, body, 0, unroll=2)


def _attn_a(sink, qa, ka, va, n_ctx):
    b, s, _ = qa.shape
    seq = lambda n: pl.BlockSpec((1, s, n), lambda bi: (bi, 0, 0))
    return pl.pallas_call(
        functools.partial(_attn_a_kernel, n_ctx=n_ctx, n_blocks=s // WINDOW),
        grid=(b,),
        in_specs=[pl.BlockSpec(memory_space=pltpu.SMEM), seq(4 * LANES), seq(LANES), seq(LANES)],
        out_specs=seq(2 * LANES),
        out_shape=jax.ShapeDtypeStruct((b, s, 2 * LANES), BF16),
        compiler_params=_params(("arbitrary",)),
        name="window_attention",
    )(sink, qa, ka, va)


def _attn_b_kernel(li_ref, lam_ref, ng_ref, qt_ref, k_ref, vt_ref, o_ref,
                   qs_scr, sa_scr, sb_scr, sc_scr, mt_scr, acc_scr, *, nt, q0, nq, kg):
    tq, tk = qt_ref.shape[-1], vt_ref.shape[-1]
    hv = HEAD_DIM
    lam_init = li_ref[0]
    lv = lam_ref[...]
    lam = (jnp.exp(jnp.sum(lv[0:1] * lv[1:2], axis=1, keepdims=True))
           - jnp.exp(jnp.sum(lv[2:3] * lv[3:4], axis=1, keepdims=True)) + lam_init)

    def first_tile(i, slot):
        qt = qt_ref[0, 0, q0 + i]
        unit = lax.broadcasted_iota(jnp.int32, qt.shape, 0) // B_QK_DIM
        zero = jnp.zeros_like(qt)
        qs_scr[slot] = jnp.concatenate([jnp.where(unit == u, qt, zero) for u in range(4)], axis=1)
        mt_scr[...] = fetch(0, sc_scr, slot)

    def fetch(t, s_scr, slot, n=1):
        off = pl.multiple_of(t * tk, tk)
        s_new = _dot(k_ref[0, pl.ds(off, n * tk), :], qs_scr[slot])
        s_scr[...] = s_new
        return jnp.max(s_new, axis=0, keepdims=True)

    def consume(s_scr, mt, m, t, n=1):
        m_new = jnp.maximum(m, mt)
        alpha = jnp.exp2(m - m_new)
        p = jnp.exp2(s_scr[...] - m_new).astype(BF16)
        for hd in range(2):
            cols = slice(hd * 2 * tq, (hd + 1) * 2 * tq)
            pv = _dot(vt_ref[0, hd, t], p[0:tk, cols])
            for c in range(1, n):
                pv = pv + _dot(vt_ref[0, hd, t + c], p[c * tk:(c + 1) * tk, cols])
            acc_scr[hd] = alpha[:, cols] * acc_scr[hd] + pv
        return m_new

    def head_out(hd):
        acc = acc_scr[hd]
        o = acc[0:hv] * (1.0 / acc[hv:hv + 1])
        od = o[:, 0:tq] - lam * o[:, tq:]
        ms = jnp.mean(od * od, axis=0, keepdims=True)
        return od * lax.rsqrt(ms + LN_EPS) * ng_ref[...]

    def query_tile(i, carry):
        cur = i % 2

        def successor():
            if nq > 1:
                first_tile(jnp.minimum(i + 1, nq - 1), 1 - cur)

        acc_scr[...] = jnp.zeros_like(acc_scr)
        m = jnp.full((1, 4 * tq), -jnp.inf, F32)
        mt_c = mt_scr[...]

        nb = (nt - 1) // kg
        big = lambda j: 1 + kg * j

        def body(tt, carry):
            m, mt_a = carry
            j0 = 2 * tt
            mt_b = fetch(big(j0 + 1), sb_scr, cur, kg)
            m = consume(sa_scr, mt_a, m, big(j0), kg)
            mt_a = fetch(big(j0 + 2), sa_scr, cur, kg)
            m = consume(sb_scr, mt_b, m, big(j0 + 1), kg)
            return m, mt_a

        if nb == 0:
            successor()
            consume(sc_scr, mt_c, m, 0)
        else:
            mt_a = fetch(big(0), sa_scr, cur, kg)
            m = consume(sc_scr, mt_c, m, 0)
            m, mt_a = lax.fori_loop(0, (nb - 1) // 2, body, (m, mt_a))
            if nb % 2 == 0:
                mt_b = fetch(big(nb - 1), sb_scr, cur, kg)
                m = consume(sa_scr, mt_a, m, big(nb - 2), kg)
                successor()
                consume(sb_scr, mt_b, m, big(nb - 1), kg)
            else:
                successor()
                consume(sa_scr, mt_a, m, big(nb - 1), kg)

        y_t = jnp.concatenate([head_out(0), head_out(1)], axis=0) * (1.0 - lam_init)
        o_ref[0, pl.ds(pl.multiple_of(i * tq, tq), tq), :] = y_t.T.astype(o_ref.dtype)
        return carry

    first_tile(0, 0)
    lax.fori_loop(0, nq, query_tile, 0)


def _attn_b_call(lam_init, lam_vec, norm_g, qbt, kb, vbt, q0, nq, nt):
    b, pairs, n_tiles, _, tq = qbt.shape
    vrows, tk = vbt.shape[-2:]
    kg = B_KEY_GROUP if (nt - 1) % B_KEY_GROUP == 0 else 1
    return pl.pallas_call(
        functools.partial(_attn_b_kernel, nt=nt, q0=q0, nq=nq, kg=kg),
        grid=(b, pairs),
        in_specs=[
            pl.BlockSpec(memory_space=pltpu.SMEM),
            pl.BlockSpec((4, B_QK_DIM), lambda bi, p: (0, 0)),
            pl.BlockSpec((HEAD_DIM, 1), lambda bi, p: (0, 0)),
            pl.BlockSpec((1, 1, n_tiles, LANES, tq), lambda bi, p: (bi, p, 0, 0, 0)),
            pl.BlockSpec((1, nt * tk, LANES), lambda bi, p: (bi, 0, p)),
            pl.BlockSpec((1, 2, nt, vrows, tk), lambda bi, p: (bi, p, 0, 0, 0)),
        ],
        out_specs=pl.BlockSpec((1, nq * tq, LANES), lambda bi, p: (bi, 0, p)),
        out_shape=jax.ShapeDtypeStruct((b, nq * tq, pairs * LANES), BF16),
        scratch_shapes=[pltpu.VMEM((2, LANES, 4 * tq), BF16), pltpu.VMEM((kg * tk, 4 * tq), F32),
                        pltpu.VMEM((kg * tk, 4 * tq), F32), pltpu.VMEM((tk, 4 * tq), F32),
                        pltpu.VMEM((1, 4 * tq), F32), pltpu.VMEM((2, vrows, 2 * tq), F32)],
        compiler_params=_params(("arbitrary", "arbitrary")),
        name="diff_attention",
    )(lam_init, lam_vec, norm_g, qbt, kb, vbt)


def _attn_b(lam_init, lam_vec, norm_g, qbt, kb, vbt, n_ctx, with_ctx):
    n_tiles, tq = qbt.shape[2], qbt.shape[4]
    tk = vbt.shape[-1]
    n_ctx_tiles = n_ctx // tq
    y_lat = _attn_b_call(lam_init, lam_vec, norm_g, qbt, kb, vbt, n_ctx_tiles, n_tiles - n_ctx_tiles, n_tiles)
    y_ctx = _attn_b_call(lam_init, lam_vec, norm_g, qbt, kb, vbt, 0, n_ctx_tiles, n_ctx // tk) if with_ctx else None
    return y_lat, y_ctx


def _log_sigmoid(x):
    return jnp.minimum(x, 0.0) - jnp.log(1.0 + jnp.exp(-jnp.abs(x)))


def _mlstm_kernel(bir_ref, bic_ref, bfr_ref, bfc_ref, ng_ref, qt_ref, k_ref, vt_ref, gic_ref, gfc_ref, gir_ref,
                  gfr_ref, ogt_ref,
                  o_ref, hf_scr, hb_scr, st_scr, ir_scr, br_scr, gc_scr, *, n_ctx_chunks, n_chunks):
    L = MLSTM_CHUNK
    hv = HEAD_DIM
    r_i = lax.broadcasted_iota(jnp.int32, (L, L), 0)
    c_i = lax.broadcasted_iota(jnp.int32, (L, L), 1)
    upper = r_i <= c_i
    lower = r_i >= c_i
    t_up = jnp.where(upper, 1.0, 0.0).astype(BF16)
    t_low = jnp.where(lower, 1.0, 0.0).astype(BF16)
    row_q = lax.broadcasted_iota(jnp.int32, (LANES, L), 0)
    sel = [jnp.where(row_q < hv, 1.0, 0.0).astype(BF16), jnp.where(row_q < hv, 0.0, 1.0).astype(BF16)]
    fwd_r = lax.broadcasted_iota(jnp.int32, (GATE_PAD, L), 0) < 2
    fwd_c = lax.broadcasted_iota(jnp.int32, (L, GATE_PAD), 1) < 2
    st_scr[...] = jnp.zeros_like(st_scr)

    def gates(c, carry):
        r0 = pl.multiple_of(c * L, L)
        frow = _log_sigmoid(gfr_ref[0, 0, c] + bfr_ref[0])
        hi, lo = _split_bf16(frow)
        ir_scr[c] = gir_ref[0, 0, c] + bir_ref[0]
        br_scr[c] = jnp.where(fwd_r, _dot(hi, t_up) + _dot(lo, t_up), _dot(hi, t_low) + _dot(lo, t_low))
        fcol = _log_sigmoid(gfc_ref[0, 0, pl.ds(r0, L), :] + bfc_ref[0])
        hi, lo = _split_bf16(fcol)
        bcol = jnp.where(fwd_c, _dot(t_low, hi) + _dot(t_low, lo), _dot(t_up, hi) + _dot(t_up, lo))
        gc_scr[c] = gic_ref[0, 0, pl.ds(r0, L), :] + bic_ref[0] - bcol
        return carry
    lax.fori_loop(0, n_chunks, gates, 0, unroll=4)

    def chunk_step(direction, c, m_prev):
        r0 = pl.multiple_of(c * L, L)
        kc = k_ref[0, pl.ds(r0, L), :]
        qtc = qt_ref[0, 0, c]
        irow8, brow8, gcol8 = ir_scr[c], br_scr[c], gc_scr[c]
        tri = upper if direction == 0 else lower
        h_scr = hf_scr if direction == 0 else hb_scr
        qtm = [qtc * sel[a] for a in range(2)]
        s_both = _dot(kc, jnp.concatenate(qtm, axis=1))
        m_news, vws, decays = [], [], []
        for a in range(2):
            ci = 2 * direction + a
            b_row, i_row, g_col = brow8[ci:ci + 1], irow8[ci:ci + 1], gcol8[:, ci:ci + 1]
            log_d = jnp.where(tri, g_col + b_row, -jnp.inf)
            m_in = b_row + m_prev[a]
            m_t = jnp.maximum(m_in, jnp.max(log_d, axis=0, keepdims=True))
            w_in = jnp.exp(m_in - m_t)
            sd = (s_both[:, a * L:(a + 1) * L] * jnp.exp(log_d - m_t)).astype(BF16)
            qw = (qtm[a].astype(F32) * w_in).astype(BF16)
            vt = vt_ref[0, a, c]
            numden = _dot(jnp.concatenate([st_scr[ci].astype(BF16), vt], axis=1),
                          jnp.concatenate([qw, sd], axis=0))
            h_scr[c, a * hv:(a + 1) * hv, :] = (
                numden[0:hv] / jnp.maximum(jnp.abs(numden[hv:hv + 1]), jnp.exp(-m_t)))
            b_last = jnp.min(b_row, axis=1, keepdims=True)
            log_w = b_last - b_row + i_row
            m_new = jnp.maximum(b_last + m_prev[a], jnp.max(log_w, axis=1, keepdims=True))
            vws.append((vt.astype(F32) * jnp.exp(log_w - m_new)).astype(BF16))
            decays.append(jnp.exp(b_last + m_prev[a] - m_new))
            m_news.append(m_new)
        rows = vt_ref.shape[3]
        upd = _dot(jnp.concatenate(vws, axis=0), kc)
        for a in range(2):
            ci = 2 * direction + a
            st_scr[ci] = decays[a] * st_scr[ci] + upd[a * rows:(a + 1) * rows]
        return tuple(m_news)

    def body(j, carry):
        mf, mb = carry
        cb = jnp.where(j < n_ctx_chunks, n_ctx_chunks - 1 - j, n_chunks - 1 + n_ctx_chunks - j)
        return chunk_step(0, j, mf), chunk_step(1, cb, mb)

    z11 = jnp.zeros((1, 1), F32)
    lax.fori_loop(0, n_chunks, body, ((z11, z11), (z11, z11)), unroll=4)

    def finish(c, carry):
        r0 = pl.multiple_of(c * L, L)
        h = hf_scr[c] + hb_scr[c]
        outs = []
        for a in range(2):
            ha = h[a * hv:(a + 1) * hv]
            outs.append(ha * lax.rsqrt(jnp.mean(ha * ha, axis=0, keepdims=True) + LN_EPS))
        y_t = jnp.concatenate(outs, axis=0) * ng_ref[0] * ogt_ref[0, 0, c].astype(F32)
        o_ref[0, pl.ds(r0, L), :] = y_t.T.astype(o_ref.dtype)
        return carry
    lax.fori_loop(0, n_chunks, finish, 0, unroll=4)


def _mlstm(bias_i, bias_f, norm_g, qmt, km, vmt, gic, gfc, gir, gfr, ogt, n_ctx):
    b, pairs, nc, _, L = qmt.shape
    s = nc * L
    vrows = vmt.shape[3]
    seq = pl.BlockSpec((1, s, LANES), lambda bi, p: (bi, 0, p))
    pair_tiles = pl.BlockSpec((1, 1, nc, LANES, L), lambda bi, p: (bi, p, 0, 0, 0))
    gate_c = pl.BlockSpec((1, 1, s, GATE_PAD), lambda bi, p: (bi, p, 0, 0))
    gate_r = pl.BlockSpec((1, 1, nc, GATE_PAD, L), lambda bi, p: (bi, p, 0, 0, 0))
    bias_r = pl.BlockSpec((1, GATE_PAD, 1), lambda bi, p: (p, 0, 0))
    bias_c = pl.BlockSpec((1, 1, GATE_PAD), lambda bi, p: (p, 0, 0))
    return pl.pallas_call(
        functools.partial(_mlstm_kernel, n_ctx_chunks=n_ctx // L, n_chunks=nc),
        grid=(b, pairs),
        in_specs=[
            bias_r, bias_c, bias_r, bias_c,
            pl.BlockSpec((1, LANES, 1), lambda bi, p: (p, 0, 0)),
            pair_tiles, seq,
            pl.BlockSpec((1, 2, nc, vrows, L), lambda bi, p: (bi, p, 0, 0, 0)),
            gate_c, gate_c, gate_r, gate_r,
            pair_tiles,
        ],
        out_specs=seq,
        out_shape=jax.ShapeDtypeStruct((b, s, pairs * LANES), BF16),
        scratch_shapes=[pltpu.VMEM((nc, LANES, L), F32), pltpu.VMEM((nc, LANES, L), F32),
                        pltpu.VMEM((4, vrows, LANES), F32),
                        pltpu.VMEM((nc, GATE_PAD, L), F32), pltpu.VMEM((nc, GATE_PAD, L), F32),
                        pltpu.VMEM((nc, L, GATE_PAD), F32)],
        compiler_params=_params(("arbitrary", "arbitrary")),
        name="mlstm",
    )(bias_i[:, :, None], bias_i[:, None, :], bias_f[:, :, None], bias_f[:, None, :], norm_g[:, :, None],
      qmt, km, vmt, gic, gfc, gir, gfr, ogt)


def _out_kernel(*refs, alpha, n_ctx_tiles, n_stream, row0):
    x = _stream_tile(refs[:n_stream], n_ctx_tiles, row0)
    mod_ref, ya_ref, yb_ref = refs[n_stream:n_stream + 3]
    refs = refs[n_stream + 3:]
    ym_ref, g_ref, w_ref, lng_ref, lnb_ref, o_ref = refs[-6:]
    gate = mod_ref[0, 0, 2:3, :]
    g = g_ref[0].astype(F32)
    na, nb_ = ya_ref.shape[2], yb_ref.shape[2]
    yb = yb_ref[0]
    if len(refs) == 7:
        yb = jnp.where(pl.program_id(1) < n_ctx_tiles, refs[0][0], yb)
    mix_a = (ya_ref[0].astype(F32) * g[:, :na]).astype(BF16)
    mix_b = (yb.astype(F32) * g[:, na:na + nb_]).astype(BF16)
    mix_m = (ym_ref[0].astype(F32) * g[:, na + nb_:]).astype(BF16)
    y = _dot(jnp.concatenate([mix_a, mix_b, mix_m], axis=1), w_ref[...])
    r = alpha * x + gate * y
    mu = jnp.mean(r, axis=1, keepdims=True)
    d = r - mu
    var = jnp.mean(d * d, axis=1, keepdims=True)
    o_ref[0] = d * lax.rsqrt(var + LN_EPS) * lng_ref[...] + lnb_ref[...]


def _out_projection(xa, modsel, ya, yb_lat, yb_ctx, ym, g, w, layer, ln_g, ln_b, alpha, n_ctx_tiles):
    b, s, d = _stream_shape(xa)
    tm = ROW_TILE
    row0 = 0 if yb_ctx is not None else n_ctx_tiles
    row = lambda n: pl.BlockSpec((1, tm, n), lambda bi, i: (bi, i + row0, 0))
    vec = pl.BlockSpec((1, d), lambda bi, i: (0, 0))
    nb_ = yb_lat.shape[2]
    x_specs, x_args = _stream_specs(xa, tm, row0)
    in_specs = x_specs + [
        pl.BlockSpec((1, 1, 3, d), lambda bi, i: (bi, jnp.minimum(i + row0, 1), 0, 0)),
        row(ya.shape[2]),
        pl.BlockSpec((1, tm, nb_), lambda bi, i: (bi, jnp.maximum(i + row0 - n_ctx_tiles, 0), 0)),
    ]
    args = x_args + [modsel, ya, yb_lat]
    if yb_ctx is not None:
        in_specs.append(pl.BlockSpec((1, tm, nb_), lambda bi, i: (bi, jnp.minimum(i, n_ctx_tiles - 1), 0)))
        args.append(yb_ctx)
    in_specs += [row(ym.shape[2]), row(d), pl.BlockSpec((None,) + w.shape[1:], lambda bi, i: (layer, 0, 0)),
                 vec, vec]
    args += [ym, g, w, ln_g, ln_b]
    return pl.pallas_call(
        functools.partial(_out_kernel, alpha=alpha, n_ctx_tiles=n_ctx_tiles, n_stream=len(x_args), row0=row0),
        grid=(b, s // tm - row0),
        in_specs=in_specs,
        out_specs=pl.BlockSpec((1, tm, d), lambda bi, i: (bi, i, 0)),
        out_shape=jax.ShapeDtypeStruct((b, s - row0 * tm, d), F32),
        compiler_params=_params(("arbitrary", "arbitrary")),
        name="out_projection",
    )(*args)


def _in_weight_columns(d_in):
    z = d_in
    src = {}
    off = 0
    for name, n in (("qa", 256), ("ka", 128), ("va", 128), ("za", 256), ("qb", 384), ("kb", 384), ("vb", 384),
                    ("zb", 384), ("qm", 384), ("km", 384), ("vm", 384), ("om", 384), ("zm", 384), ("gm", 24)):
        src[name] = np.arange(off, off + n)
        off += n
    assert off == d_in
    cols = []
    for hd in range(4):
        blk = np.full(LANES, z)
        gq = hd // 2
        blk[64 * gq:64 * gq + 64] = src["qa"][64 * hd:64 * hd + 64]
        cols.append(blk)
    for name in ("ka", "va", "qb", "kb", "vb", "qm", "km", "vm", "om", "za", "zb", "zm"):
        cols.append(src[name])
    heads = len(src["gm"]) // 4
    for p in range(heads // 2):
        for gate in range(2):
            cols.append(np.array([src["gm"][di * 2 * heads + gate * heads + 2 * p + a]
                                  for di in range(2) for a in range(2)] + [z] * (GATE_PAD - 4)))
    cols.append(np.full(LANES - 2 * GATE_PAD * (heads // 2), z))
    cols = np.concatenate(cols)
    assert cols.shape[0] == _W_COLS
    return cols


def _permute_columns(w, cols):
    n = w.shape[-1]
    pieces, start = [], 0
    breaks = np.flatnonzero(np.diff(cols) != 1) + 1
    for stop in list(breaks) + [len(cols)]:
        run = cols[start:stop]
        if len(run) >= 32:
            pieces.append(lax.slice_in_dim(w, int(run[0]), int(run[-1]) + 1, axis=w.ndim - 1))
        elif pieces and isinstance(pieces[-1], list):
            pieces[-1].extend(run)
        else:
            pieces.append(list(run))
        start = stop
    lo = min(min(p) for p in pieces if isinstance(p, list))
    tail = jnp.concatenate([lax.slice_in_dim(w, lo, n, axis=w.ndim - 1), jnp.zeros(w.shape[:-1] + (1,), w.dtype)], -1)
    pieces = [jnp.take(tail, np.array(p) - lo, axis=-1) if isinstance(p, list) else p for p in pieces]
    return jnp.concatenate(pieces, axis=-1)


def _rope_table(n_tokens, n_ctx, dim):
    f32 = np.float32
    rows = n_tokens // GRID_W
    rowp = np.broadcast_to(np.arange(rows, dtype=f32)[:, None], (rows, GRID_W)).reshape(-1)
    colp = np.broadcast_to(np.arange(GRID_W, dtype=f32)[None, :], (rows, GRID_W)).reshape(-1)
    n_freq = dim // 4
    inv = np.power(f32(ROPE_BASE), -np.arange(n_freq, dtype=f32) / f32(n_freq)).astype(f32)
    ar = rowp[:, None] * inv
    ac = colp[:, None] * inv
    ang = np.concatenate([ar, ar, ac, ac], -1).astype(f32)
    cos, sin = np.cos(ang), np.sin(ang)
    odd = (np.arange(dim) // n_freq) % 2 == 1
    sin_p = np.where(odd, sin, f32(0))
    sin_m = np.where(odd, f32(0), -sin)
    tab = np.stack([cos, sin_p, sin_m])
    ident = np.stack([np.ones((n_ctx, dim), f32), np.zeros((n_ctx, dim), f32), np.zeros((n_ctx, dim), f32)])
    tab = np.concatenate([ident, tab], axis=1)
    return jnp.asarray(np.tile(tab, (1, 1, LANES // dim)).astype(f32))


def kernel(x, c, ctx, c_ctx, w_mod, b_mod, w_in, attn_sink, diff_lambda, diff_norm_g, mlstm_i_bias,
           mlstm_f_bias, mlstm_norm_g, w_out, ln_g, ln_b):
    b, t, d = x.shape
    n_ctx = ctx.shape[1]
    depth = w_mod.shape[0]
    d_in = w_in.shape[2]
    alpha = (2 * depth) ** 0.25

    xa = (ctx, x)
    rope_a = _rope_table(t, n_ctx, HEAD_DIM)
    rope_b = _rope_table(t, n_ctx, B_QK_DIM)

    rows = -(-(b + 1) // 8) * 8
    cc = jnp.concatenate([c, c_ctx[None, :], jnp.zeros((rows - b - 1, d), F32)], axis=0)
    mod = _modulation(cc, w_mod, b_mod).reshape(depth, rows, 3, d)

    w_in_p = _permute_columns(w_in.astype(BF16), _in_weight_columns(d_in))
    w_out_b = w_out.astype(BF16)
    norm_b = diff_norm_g[:, :, None]

    n_pairs = mlstm_i_bias.shape[2] // 2
    heads = mlstm_i_bias.shape[2]
    chains = [(di, a) for di in range(2) for a in range(2)]
    bias_idx = np.array([[di * heads + 2 * p + a for di, a in chains] for p in range(n_pairs)])
    pad8 = lambda v: jnp.pad(v.reshape(-1)[bias_idx], ((0, 0), (0, GATE_PAD - 4)))

    for l in range(depth):
        with_ctx = l < depth - 1
        lam_init = 0.8 - 0.6 * math.exp(-0.3 * l)
        modsel = jnp.stack([jnp.broadcast_to(mod[l, b], (b, 3, d)), mod[l, :b]], axis=1)
        (qa, ka, va, qb, kb, vb, qm, km, vm, og, g, gic, gfc, gir, gfr) = _in_projection(
            xa, modsel, w_in_p, l, rope_a, rope_b, n_ctx)

        ya = _attn_a(attn_sink[l], qa, ka, va, n_ctx)
        yb, yb_ctx = _attn_b(jnp.full((1,), lam_init, F32), diff_lambda[l], norm_b[l], qb, kb, vb, n_ctx, with_ctx)

        ym = _mlstm(pad8(mlstm_i_bias[l]), pad8(mlstm_f_bias[l]), mlstm_norm_g[l].reshape(n_pairs, LANES),
                    qm, km, vm, gic, gfc, gir, gfr, og, n_ctx)

        xa = _out_projection(xa, modsel, ya, yb, yb_ctx, ym, g, w_out_b, l, ln_g[l][None, :], ln_b[l][None, :],
                             alpha, n_ctx // ROW_TILE)
    return xa
```

```python
import functools
import math

import numpy as np
import jax
import jax.numpy as jnp
from jax import lax
from jax.experimental import pallas as pl
from jax.experimental.pallas import tpu as pltpu

F32 = jnp.float32
BF16 = jnp.bfloat16

LANES = 128
GRID_W = 64
HEAD_DIM = 64
ROPE_BASE = 10000.0
LN_EPS = 1e-5
WINDOW = 128
B_QK_DIM = 32
LOG2E = math.log2(math.e)
VB_ROWS = HEAD_DIM + 16
MLSTM_CHUNK = 128
GATE_PAD = 8
MOD_COL_TILE = 1024
B_BOUND_CAP = 32.0
B_BOUND_MARGIN = 1.02
B_KEY_GROUP_BOUNDED = 4
B_KEY_GROUP = 2
ROW_TILE = 256
VMEM_LIMIT = 56 * 1024 * 1024


def _dot(a, b):
    return jnp.dot(a, b, preferred_element_type=F32)


def _dot_nt(a, b):
    return lax.dot_general(a, b, (((1,), (1,)), ((), ())), preferred_element_type=F32)


def _split_bf16(a):
    hi = a.astype(BF16)
    lo = (a - hi.astype(F32)).astype(BF16)
    return hi, lo


def _sigmoid(x):
    return 1.0 / (1.0 + jnp.exp(-x))


def _params(sem):
    return pltpu.CompilerParams(dimension_semantics=sem, vmem_limit_bytes=VMEM_LIMIT)


def _mod_kernel(c_ref, w_ref, b_ref, o_ref):
    c = c_ref[...]
    a = c * _sigmoid(c)
    a_hi, a_lo = _split_bf16(a)
    w_hi, w_lo = _split_bf16(w_ref[0])
    o_ref[0] = _dot(a_hi, w_hi) + _dot(a_lo, w_hi) + _dot(a_hi, w_lo) + b_ref[0]


def _modulation(cc, w_mod, b_mod):
    depth, d, n = w_mod.shape
    r = cc.shape[0]
    tn = MOD_COL_TILE
    return pl.pallas_call(
        _mod_kernel,
        grid=(depth, n // tn),
        in_specs=[
            pl.BlockSpec((r, d), lambda l, j: (0, 0)),
            pl.BlockSpec((1, d, tn), lambda l, j: (l, 0, j)),
            pl.BlockSpec((1, 1, tn), lambda l, j: (l, 0, j)),
        ],
        out_specs=pl.BlockSpec((1, r, tn), lambda l, j: (l, 0, j)),
        out_shape=jax.ShapeDtypeStruct((depth, r, n), F32),
        compiler_params=_params(("arbitrary", "arbitrary")),
        name="modulation",
    )(cc, w_mod, b_mod.reshape(depth, 1, n))


def _stream_shape(xa):
    if isinstance(xa, tuple):
        ctx, x = xa
        return x.shape[0], ctx.shape[1] + x.shape[1], x.shape[2]
    return xa.shape


def _stream_specs(xa, tm, row0):
    if not isinstance(xa, tuple):
        return [pl.BlockSpec((1, tm, xa.shape[2]), lambda bi, i: (bi, i + row0, 0))], [xa]
    ctx, x = xa
    nct = ctx.shape[1] // tm
    return ([pl.BlockSpec((1, tm, ctx.shape[2]), lambda bi, i: (bi, jnp.minimum(i + row0, nct - 1), 0)),
             pl.BlockSpec((1, tm, x.shape[2]), lambda bi, i: (bi, jnp.maximum(i + row0 - nct, 0), 0))], [ctx, x])


def _stream_tile(refs, n_ctx_tiles, row0):
    if len(refs) == 1:
        return refs[0][0]
    return jnp.where(pl.program_id(1) + row0 < n_ctx_tiles, refs[0][0], refs[1][0])


_QA, _KA, _VA = (0, 512), (512, 640), (640, 768)
_QB, _KB, _VB = (768, 1152), (1152, 1536), (1536, 1920)
_QM, _KM, _VM = (1920, 2304), (2304, 2688), (2688, 3072)
_OM, _Z, _GM = (3072, 3456), (3456, 4480), (4480, 4608)
_W_COLS = 4608
_W_GROUPS = ((0, 768), (768, 1536), (1536, 2304), (2304, 3072), (3072, 4608))


def _rope(t, tab_ref, quarter):
    cos, sin_p, sin_m = tab_ref[0], tab_ref[1], tab_ref[2]
    outs = []
    for j in range(t.shape[1] // LANES):
        tj = t[:, j * LANES:(j + 1) * LANES]
        outs.append(tj * cos + pltpu.roll(tj, quarter, 1) * sin_p + pltpu.roll(tj, LANES - quarter, 1) * sin_m)
    return outs[0] if len(outs) == 1 else jnp.concatenate(outs, axis=1)


def _inproj_kernel(*refs, n_stream, n_ctx_tiles):
    (mod_ref, w_ref, ra_ref, rb_ref, qa_ref, ka_ref, va_ref, qb_ref, kb_ref, vb_ref,
     qm_ref, km_ref, vm_ref, om_ref, g_ref, gic_ref, gfc_ref, gir_ref, gfr_ref) = refs[n_stream:]
    x = _stream_tile(refs[:n_stream], n_ctx_tiles, 0)
    shift = mod_ref[0, 0, 0:1, :]
    scale = mod_ref[0, 0, 1:2, :]
    h = (x * (1.0 + scale) + shift).astype(BF16)

    group_dots = {}

    def proj(cols):
        lo, hi = next(g for g in _W_GROUPS if g[0] <= cols[0] and cols[1] <= g[1])
        if lo not in group_dots:
            group_dots[lo] = _dot(h, w_ref[:, lo:hi])
        return group_dots[lo][:, cols[0] - lo:cols[1] - lo]

    qa_ref[0] = (_rope(proj(_QA), ra_ref, HEAD_DIM // 4) * (HEAD_DIM ** -0.5)).astype(BF16)
    ka_ref[0] = _rope(proj(_KA), ra_ref, HEAD_DIM // 4).astype(BF16)
    va_ref[0] = proj(_VA).astype(BF16)
    qb_t = (_rope(proj(_QB), rb_ref, B_QK_DIM // 4) * (B_QK_DIM ** -0.5 * LOG2E)).T.astype(BF16)
    for p in range(qb_t.shape[0] // LANES):
        qb_ref[0, p, 0] = qb_t[p * LANES:(p + 1) * LANES]
    vb_t = proj(_VB).T
    ones_rows = jnp.where(lax.broadcasted_iota(jnp.int32, (VB_ROWS - HEAD_DIM, vb_t.shape[1]), 0) == 0, 1.0, 0.0)
    for hd in range(vb_t.shape[0] // HEAD_DIM):
        vb_ref[0, hd, 0] = jnp.concatenate([vb_t[hd * HEAD_DIM:(hd + 1) * HEAD_DIM], ones_rows], axis=0).astype(BF16)
    kb_ref[0] = _rope(proj(_KB), rb_ref, B_QK_DIM // 4).astype(BF16)
    km_ref[0] = (proj(_KM) * (HEAD_DIM ** -0.5)).astype(BF16)
    qm_t = proj(_QM).T.astype(BF16)
    om_t = _sigmoid(proj(_OM)).T.astype(BF16)
    vm_t = proj(_VM).T
    L = MLSTM_CHUNK
    ones_rows_c = jnp.where(lax.broadcasted_iota(jnp.int32, (VB_ROWS - HEAD_DIM, L), 0) == 0, 1.0, 0.0)
    for c in range(qm_t.shape[1] // L):
        cols = slice(c * L, (c + 1) * L)
        for p in range(qm_t.shape[0] // LANES):
            qm_ref[0, p, c] = qm_t[p * LANES:(p + 1) * LANES, cols]
            om_ref[0, p, c] = om_t[p * LANES:(p + 1) * LANES, cols]
        for hd in range(vm_t.shape[0] // HEAD_DIM):
            vm_ref[0, hd, c] = jnp.concatenate(
                [vm_t[hd * HEAD_DIM:(hd + 1) * HEAD_DIM, cols], ones_rows_c], axis=0).astype(BF16)
    z = proj(_Z)
    g_ref[0] = (z * _sigmoid(z)).astype(BF16)
    gm = proj(_GM)
    gm_t = gm.T
    for p in range(gic_ref.shape[1]):
        lo, hi = 2 * GATE_PAD * p, 2 * GATE_PAD * p + GATE_PAD
        gic_ref[0, p] = gm[:, lo:hi]
        gfc_ref[0, p] = gm[:, hi:hi + GATE_PAD]
        for c in range(gm_t.shape[1] // L):
            gir_ref[0, p, c] = gm_t[lo:hi, c * L:(c + 1) * L]
            gfr_ref[0, p, c] = gm_t[hi:hi + GATE_PAD, c * L:(c + 1) * L]


def _in_projection(xa, modsel, w, layer, rope_a, rope_b, n_ctx):
    b, s, d = _stream_shape(xa)
    tm = ROW_TILE
    x_specs, x_args = _stream_specs(xa, tm, 0)
    n_pairs = (_QB[1] - _QB[0]) // LANES
    n_heads = (_VB[1] - _VB[0]) // HEAD_DIM
    L = MLSTM_CHUNK
    widths = [512, 128, 128, (n_pairs, LANES, tm), 384, (n_heads, VB_ROWS, tm),
              (n_pairs, LANES, L), 384, (n_heads, VB_ROWS, L), (n_pairs, LANES, L), 1024]
    tposed = lambda g, r, tt: pl.BlockSpec((1, g, tm // tt, r, tt), lambda bi, i: (bi, 0, i, 0, 0))
    row = lambda n: tposed(*n) if isinstance(n, tuple) else pl.BlockSpec((1, tm, n), lambda bi, i: (bi, i, 0))
    out_shape = [jax.ShapeDtypeStruct((b, n[0], s // n[2], n[1], n[2]) if isinstance(n, tuple) else (b, s, n), BF16)
                 for n in widths]
    gate_c = pl.BlockSpec((1, n_pairs, tm, GATE_PAD), lambda bi, i: (bi, 0, i, 0))
    gate_r = pl.BlockSpec((1, n_pairs, tm // L, GATE_PAD, L), lambda bi, i: (bi, 0, i, 0, 0))
    out_shape += [jax.ShapeDtypeStruct((b, n_pairs, s, GATE_PAD), F32)] * 2
    out_shape += [jax.ShapeDtypeStruct((b, n_pairs, s // L, GATE_PAD, L), F32)] * 2
    return pl.pallas_call(
        functools.partial(_inproj_kernel, n_stream=len(x_args), n_ctx_tiles=n_ctx // tm),
        grid=(b, s // tm),
        in_specs=x_specs + [
            pl.BlockSpec((1, 1, 3, d), lambda bi, i: (bi, jnp.minimum(i, 1), 0, 0)),
            pl.BlockSpec((None, d, _W_COLS), lambda bi, i: (layer, 0, 0)),
            pl.BlockSpec((3, tm, LANES), lambda bi, i: (0, i, 0)),
            pl.BlockSpec((3, tm, LANES), lambda bi, i: (0, i, 0)),
        ],
        out_specs=[row(n) for n in widths] + [gate_c, gate_c, gate_r, gate_r],
        out_shape=out_shape,
        compiler_params=_params(("arbitrary", "arbitrary")),
        name="in_projection",
    )(*x_args, modsel, w, rope_a, rope_b)


def _attn_a_kernel(sink_ref, q_ref, k_ref, v_ref, o_ref, *, n_ctx, n_blocks):
    w = WINDOW
    s_len = n_blocks * w
    half = LANES // 2
    rows1 = lax.broadcasted_iota(jnp.int32, (4 * w, 1), 0)
    sink = jnp.where(rows1 < w, sink_ref[0],
                     jnp.where(rows1 < 2 * w, sink_ref[1], jnp.where(rows1 < 3 * w, sink_ref[2], sink_ref[3])))
    row = lax.broadcasted_iota(jnp.int32, (4 * w, 3 * w), 0) & (w - 1)
    col = lax.broadcasted_iota(jnp.int32, (4 * w, 3 * w), 1)
    lane = lax.broadcasted_iota(jnp.int32, (w, LANES), 1)
    k_ctx = k_ref[0, 0:n_ctx, :]
    v_ctx = v_ref[0, 0:n_ctx, :]

    def body(i, carry):
        r0 = pl.multiple_of(i * w, w)
        q = q_ref[0, pl.ds(r0, w), :]
        qs = jnp.concatenate([q[:, j * LANES:(j + 1) * LANES] for j in range(4)], axis=0)
        start = pl.multiple_of(jnp.clip(r0 - w, 0, s_len - 3 * w), w)
        s_loc = _dot_nt(qs, k_ref[0, pl.ds(start, 3 * w), :])
        s_ctx = _dot_nt(qs, k_ctx)
        kpos = col + start
        rel = kpos - r0 - row
        ok = (jnp.abs(rel) <= w) & (kpos >= jnp.where(r0 >= n_ctx, n_ctx, s_len))
        s_loc = jnp.where(ok, s_loc, -jnp.inf)
        m = jnp.maximum(jnp.maximum(jnp.max(s_loc, axis=1, keepdims=True),
                                    jnp.max(s_ctx, axis=1, keepdims=True)), sink)
        p_loc = jnp.exp(s_loc - m)
        p_ctx = jnp.exp(s_ctx - m)
        den = (jnp.sum(p_loc, axis=1, keepdims=True) + jnp.sum(p_ctx, axis=1, keepdims=True)
               + jnp.exp(sink - m))
        o = (_dot(p_loc.astype(BF16), v_ref[0, pl.ds(start, 3 * w), :])
             + _dot(p_ctx.astype(BF16), v_ctx)) / den
        left = jnp.where(lane < half, o[0:w], pltpu.roll(o[w:2 * w], half, 1))
        right = jnp.where(lane < half, pltpu.roll(o[2 * w:3 * w], half, 1), o[3 * w:])
        o_ref[0, pl.ds(r0, w), :] = jnp.concatenate([left, right], axis=1).astype(o_ref.dtype)
        return carry

    lax.fori_loop(0, n_blocks, body, 0, unroll=2)


def _attn_a(sink, qa, ka, va, n_ctx):
    b, s, _ = qa.shape
    seq = lambda n: pl.BlockSpec((1, s, n), lambda bi: (bi, 0, 0))
    return pl.pallas_call(
        functools.partial(_attn_a_kernel, n_ctx=n_ctx, n_blocks=s // WINDOW),
        grid=(b,),
        in_specs=[pl.BlockSpec(memory_space=pltpu.SMEM), seq(4 * LANES), seq(LANES), seq(LANES)],
        out_specs=seq(2 * LANES),
        out_shape=jax.ShapeDtypeStruct((b, s, 2 * LANES), BF16),
        compiler_params=_params(("arbitrary",)),
        name="window_attention",
    )(sink, qa, ka, va)


def _attn_b_kernel(li_ref, lam_ref, ng_ref, qt_ref, k_ref, vt_ref, o_ref,
                   qs_scr, sa_scr, sb_scr, sc_scr, ea_scr, eb_scr, ec_scr, acc_scr, *, nt, q0, nq, kg, kgb):
    tq, tk = qt_ref.shape[-1], vt_ref.shape[-1]
    hv = HEAD_DIM
    q_cols = 4 * tq
    lam_init = li_ref[0]
    lv = lam_ref[...]
    lam = (jnp.exp(jnp.sum(lv[0:1] * lv[1:2], axis=1, keepdims=True))
           - jnp.exp(jnp.sum(lv[2:3] * lv[3:4], axis=1, keepdims=True)) + lam_init)

    u_r = lax.broadcasted_iota(jnp.int32, (LANES, LANES), 0) // B_QK_DIM
    u_c = lax.broadcasted_iota(jnp.int32, (LANES, LANES), 1) // B_QK_DIM
    unit_ones = jnp.where(u_r == u_c, 1.0, 0.0).astype(BF16)

    def key_norms(t, best):
        kt = k_ref[0, pl.ds(pl.multiple_of(t * tk, tk), tk), :].astype(F32)
        return jnp.maximum(best, jnp.max(_dot((kt * kt).astype(BF16), unit_ones), axis=0, keepdims=True))
    kmax2 = lax.fori_loop(0, nt, key_norms, jnp.zeros((1, LANES), F32))

    def scores(t, n):
        off = pl.multiple_of(t * tk, tk)
        return _dot(k_ref[0, pl.ds(off, n * tk), :], qs_scr[...])

    def values(p, t, n, hd):
        cols = slice(hd * 2 * tq, (hd + 1) * 2 * tq)
        pv = _dot(vt_ref[0, hd, t], p[0:tk, cols])
        for c in range(1, n):
            pv = pv + _dot(vt_ref[0, hd, t + c], p[c * tk:(c + 1) * tk, cols])
        return pv

    def finish(i):
        def head_out(hd):
            acc = acc_scr[hd]
            o = acc[0:hv] * (1.0 / acc[hv:hv + 1])
            od = o[:, 0:tq] - lam * o[:, tq:]
            ms = jnp.mean(od * od, axis=0, keepdims=True)
            return od * lax.rsqrt(ms + LN_EPS) * ng_ref[...]
        y_t = jnp.concatenate([head_out(0), head_out(1)], axis=0) * (1.0 - lam_init)
        o_ref[0, pl.ds(pl.multiple_of(i * tq, tq), tq), :] = y_t.T.astype(o_ref.dtype)

    def run_pipeline(fetch, consume, bufs, g, state):
        buf_a, buf_b, buf_c = bufs
        n_groups = (nt - 1) // g
        first = lambda j: 1 + g * j

        def body(tt, carry):
            state, tag_a = carry
            j0 = 2 * tt
            tag_b = fetch(first(j0 + 1), buf_b, g)
            state = consume(buf_a, tag_a, state, first(j0), g)
            tag_a = fetch(first(j0 + 2), buf_a, g)
            state = consume(buf_b, tag_b, state, first(j0 + 1), g)
            return state, tag_a

        tag_c = fetch(0, buf_c, 1)
        if n_groups == 0:
            return consume(buf_c, tag_c, state, 0, 1)
        tag_a = fetch(first(0), buf_a, g)
        state = consume(buf_c, tag_c, state, 0, 1)
        state, tag_a = lax.fori_loop(0, (n_groups - 1) // 2, body, (state, tag_a))
        if n_groups % 2 == 0:
            tag_b = fetch(first(n_groups - 1), buf_b, g)
            state = consume(buf_a, tag_a, state, first(n_groups - 2), g)
            return consume(buf_b, tag_b, state, first(n_groups - 1), g)
        return consume(buf_a, tag_a, state, first(n_groups - 1), g)

    def bounded(i, bound):
        def fetch(t, e_scr, n):
            e_scr[...] = jnp.exp2(scores(t, n) - bound).astype(BF16)
            return bound

        def consume(e_scr, tag, state, t, n):
            for hd in range(2):
                acc_scr[hd] = acc_scr[hd] + values(e_scr[...], t, n, hd)
            return state

        run_pipeline(fetch, consume, (ea_scr, eb_scr, ec_scr), kgb, bound)
        finish(i)

    def online(i):
        def fetch(t, s_scr, n):
            s_new = scores(t, n)
            s_scr[...] = s_new
            return jnp.max(s_new, axis=0, keepdims=True)

        def consume(s_scr, mt, m, t, n):
            m_new = jnp.maximum(m, mt)
            alpha = jnp.exp2(m - m_new)
            p = jnp.exp2(s_scr[...] - m_new).astype(BF16)
            for hd in range(2):
                cols = slice(hd * 2 * tq, (hd + 1) * 2 * tq)
                acc_scr[hd] = alpha[:, cols] * acc_scr[hd] + values(p, t, n, hd)
            return m_new

        run_pipeline(fetch, consume, (sa_scr, sb_scr, sc_scr), kg, jnp.full((1, q_cols), -jnp.inf, F32))
        finish(i)

    def query_tile(i, carry):
        qt = qt_ref[0, 0, q0 + i]
        unit = lax.broadcasted_iota(jnp.int32, qt.shape, 0) // B_QK_DIM
        zero = jnp.zeros_like(qt)
        qs_scr[...] = jnp.concatenate([jnp.where(unit == u, qt, zero) for u in range(4)], axis=1)
        acc_scr[...] = jnp.zeros_like(acc_scr)
        qf = qt.astype(F32)
        qn2 = jnp.sum((qf * qf).reshape(4, B_QK_DIM, tq), axis=1)
        bound = jnp.concatenate(
            [jnp.sqrt(qn2[u:u + 1] * kmax2[:, u * B_QK_DIM:u * B_QK_DIM + 1]) for u in range(4)], axis=1)
        bound = bound * B_BOUND_MARGIN
        small = jnp.max(bound) <= B_BOUND_CAP
        pl.when(small)(lambda: bounded(i, bound))
        pl.when(jnp.logical_not(small))(lambda: online(i))
        return carry

    lax.fori_loop(0, nq, query_tile, 0)


def _attn_b_call(lam_init, lam_vec, norm_g, qbt, kb, vbt, q0, nq, nt):
    b, pairs, n_tiles, _, tq = qbt.shape
    vrows, tk = vbt.shape[-2:]
    kg = B_KEY_GROUP if (nt - 1) % B_KEY_GROUP == 0 else 1
    kgb = B_KEY_GROUP_BOUNDED if (nt - 1) % B_KEY_GROUP_BOUNDED == 0 else 1
    score_bufs = lambda dt, g: [pltpu.VMEM((g * tk, 4 * tq), dt), pltpu.VMEM((g * tk, 4 * tq), dt),
                                pltpu.VMEM((tk, 4 * tq), dt)]
    return pl.pallas_call(
        functools.partial(_attn_b_kernel, nt=nt, q0=q0, nq=nq, kg=kg, kgb=kgb),
        grid=(b, pairs),
        in_specs=[
            pl.BlockSpec(memory_space=pltpu.SMEM),
            pl.BlockSpec((4, B_QK_DIM), lambda bi, p: (0, 0)),
            pl.BlockSpec((HEAD_DIM, 1), lambda bi, p: (0, 0)),
            pl.BlockSpec((1, 1, n_tiles, LANES, tq), lambda bi, p: (bi, p, 0, 0, 0)),
            pl.BlockSpec((1, nt * tk, LANES), lambda bi, p: (bi, 0, p)),
            pl.BlockSpec((1, 2, nt, vrows, tk), lambda bi, p: (bi, p, 0, 0, 0)),
        ],
        out_specs=pl.BlockSpec((1, nq * tq, LANES), lambda bi, p: (bi, 0, p)),
        out_shape=jax.ShapeDtypeStruct((b, nq * tq, pairs * LANES), BF16),
        scratch_shapes=[pltpu.VMEM((LANES, 4 * tq), BF16)] + score_bufs(F32, kg) + score_bufs(BF16, kgb)
                       + [pltpu.VMEM((2, vrows, 2 * tq), F32)],
        compiler_params=_params(("arbitrary", "arbitrary")),
        name="diff_attention",
    )(lam_init, lam_vec, norm_g, qbt, kb, vbt)


def _attn_b(lam_init, lam_vec, norm_g, qbt, kb, vbt, n_ctx, with_ctx):
    n_tiles, tq = qbt.shape[2], qbt.shape[4]
    tk = vbt.shape[-1]
    n_ctx_tiles = n_ctx // tq
    y_lat = _attn_b_call(lam_init, lam_vec, norm_g, qbt, kb, vbt, n_ctx_tiles, n_tiles - n_ctx_tiles, n_tiles)
    y_ctx = _attn_b_call(lam_init, lam_vec, norm_g, qbt, kb, vbt, 0, n_ctx_tiles, n_ctx // tk) if with_ctx else None
    return y_lat, y_ctx


def _log_sigmoid(x):
    return jnp.minimum(x, 0.0) - jnp.log(1.0 + jnp.exp(-jnp.abs(x)))


def _mlstm_kernel(bir_ref, bic_ref, bfr_ref, bfc_ref, ng_ref, qt_ref, k_ref, vt_ref, gic_ref, gfc_ref, gir_ref,
                  gfr_ref, ogt_ref,
                  o_ref, hf_scr, hb_scr, st_scr, ir_scr, br_scr, gc_scr, *, n_ctx_chunks, n_chunks):
    L = MLSTM_CHUNK
    hv = HEAD_DIM
    r_i = lax.broadcasted_iota(jnp.int32, (L, L), 0)
    c_i = lax.broadcasted_iota(jnp.int32, (L, L), 1)
    upper = r_i <= c_i
    lower = r_i >= c_i
    t_up = jnp.where(upper, 1.0, 0.0).astype(BF16)
    t_low = jnp.where(lower, 1.0, 0.0).astype(BF16)
    row_q = lax.broadcasted_iota(jnp.int32, (LANES, L), 0)
    sel = [jnp.where(row_q < hv, 1.0, 0.0).astype(BF16), jnp.where(row_q < hv, 0.0, 1.0).astype(BF16)]
    fwd_r = lax.broadcasted_iota(jnp.int32, (GATE_PAD, L), 0) < 2
    fwd_c = lax.broadcasted_iota(jnp.int32, (L, GATE_PAD), 1) < 2
    st_scr[...] = jnp.zeros_like(st_scr)

    def gates(c, carry):
        r0 = pl.multiple_of(c * L, L)
        frow = _log_sigmoid(gfr_ref[0, 0, c] + bfr_ref[0])
        hi, lo = _split_bf16(frow)
        ir_scr[c] = gir_ref[0, 0, c] + bir_ref[0]
        br_scr[c] = jnp.where(fwd_r, _dot(hi, t_up) + _dot(lo, t_up), _dot(hi, t_low) + _dot(lo, t_low))
        fcol = _log_sigmoid(gfc_ref[0, 0, pl.ds(r0, L), :] + bfc_ref[0])
        hi, lo = _split_bf16(fcol)
        bcol = jnp.where(fwd_c, _dot(t_low, hi) + _dot(t_low, lo), _dot(t_up, hi) + _dot(t_up, lo))
        gc_scr[c] = gic_ref[0, 0, pl.ds(r0, L), :] + bic_ref[0] - bcol
        return carry
    lax.fori_loop(0, n_chunks, gates, 0, unroll=4)

    def chunk_step(direction, c, m_prev):
        r0 = pl.multiple_of(c * L, L)
        kc = k_ref[0, pl.ds(r0, L), :]
        qtc = qt_ref[0, 0, c]
        irow8, brow8, gcol8 = ir_scr[c], br_scr[c], gc_scr[c]
        tri = upper if direction == 0 else lower
        h_scr = hf_scr if direction == 0 else hb_scr
        qtm = [qtc * sel[a] for a in range(2)]
        s_both = _dot(kc, jnp.concatenate(qtm, axis=1))
        m_news, vws, decays = [], [], []
        for a in range(2):
            ci = 2 * direction + a
            b_row, i_row, g_col = brow8[ci:ci + 1], irow8[ci:ci + 1], gcol8[:, ci:ci + 1]
            log_d = jnp.where(tri, g_col + b_row, -jnp.inf)
            m_in = b_row + m_prev[a]
            m_t = jnp.maximum(m_in, jnp.max(log_d, axis=0, keepdims=True))
            w_in = jnp.exp(m_in - m_t)
            sd = (s_both[:, a * L:(a + 1) * L] * jnp.exp(log_d - m_t)).astype(BF16)
            qw = (qtm[a].astype(F32) * w_in).astype(BF16)
            vt = vt_ref[0, a, c]
            numden = _dot(jnp.concatenate([st_scr[ci].astype(BF16), vt], axis=1),
                          jnp.concatenate([qw, sd], axis=0))
            h_scr[c, a * hv:(a + 1) * hv, :] = (
                numden[0:hv] / jnp.maximum(jnp.abs(numden[hv:hv + 1]), jnp.exp(-m_t)))
            b_last = jnp.min(b_row, axis=1, keepdims=True)
            log_w = b_last - b_row + i_row
            m_new = jnp.maximum(b_last + m_prev[a], jnp.max(log_w, axis=1, keepdims=True))
            vws.append((vt.astype(F32) * jnp.exp(log_w - m_new)).astype(BF16))
            decays.append(jnp.exp(b_last + m_prev[a] - m_new))
            m_news.append(m_new)
        rows = vt_ref.shape[3]
        upd = _dot(jnp.concatenate(vws, axis=0), kc)
        for a in range(2):
            ci = 2 * direction + a
            st_scr[ci] = decays[a] * st_scr[ci] + upd[a * rows:(a + 1) * rows]
        return tuple(m_news)

    def body(j, carry):
        mf, mb = carry
        cb = jnp.where(j < n_ctx_chunks, n_ctx_chunks - 1 - j, n_chunks - 1 + n_ctx_chunks - j)
        return chunk_step(0, j, mf), chunk_step(1, cb, mb)

    z11 = jnp.zeros((1, 1), F32)
    lax.fori_loop(0, n_chunks, body, ((z11, z11), (z11, z11)), unroll=4)

    def finish(c, carry):
        r0 = pl.multiple_of(c * L, L)
        h = hf_scr[c] + hb_scr[c]
        outs = []
        for a in range(2):
            ha = h[a * hv:(a + 1) * hv]
            outs.append(ha * lax.rsqrt(jnp.mean(ha * ha, axis=0, keepdims=True) + LN_EPS))
        y_t = jnp.concatenate(outs, axis=0) * ng_ref[0] * ogt_ref[0, 0, c].astype(F32)
        o_ref[0, pl.ds(r0, L), :] = y_t.T.astype(o_ref.dtype)
        return carry
    lax.fori_loop(0, n_chunks, finish, 0, unroll=4)


def _mlstm(bias_i, bias_f, norm_g, qmt, km, vmt, gic, gfc, gir, gfr, ogt, n_ctx):
    b, pairs, nc, _, L = qmt.shape
    s = nc * L
    vrows = vmt.shape[3]
    seq = pl.BlockSpec((1, s, LANES), lambda bi, p: (bi, 0, p))
    pair_tiles = pl.BlockSpec((1, 1, nc, LANES, L), lambda bi, p: (bi, p, 0, 0, 0))
    gate_c = pl.BlockSpec((1, 1, s, GATE_PAD), lambda bi, p: (bi, p, 0, 0))
    gate_r = pl.BlockSpec((1, 1, nc, GATE_PAD, L), lambda bi, p: (bi, p, 0, 0, 0))
    bias_r = pl.BlockSpec((1, GATE_PAD, 1), lambda bi, p: (p, 0, 0))
    bias_c = pl.BlockSpec((1, 1, GATE_PAD), lambda bi, p: (p, 0, 0))
    return pl.pallas_call(
        functools.partial(_mlstm_kernel, n_ctx_chunks=n_ctx // L, n_chunks=nc),
        grid=(b, pairs),
        in_specs=[
            bias_r, bias_c, bias_r, bias_c,
            pl.BlockSpec((1, LANES, 1), lambda bi, p: (p, 0, 0)),
            pair_tiles, seq,
            pl.BlockSpec((1, 2, nc, vrows, L), lambda bi, p: (bi, p, 0, 0, 0)),
            gate_c, gate_c, gate_r, gate_r,
            pair_tiles,
        ],
        out_specs=seq,
        out_shape=jax.ShapeDtypeStruct((b, s, pairs * LANES), BF16),
        scratch_shapes=[pltpu.VMEM((nc, LANES, L), F32), pltpu.VMEM((nc, LANES, L), F32),
                        pltpu.VMEM((4, vrows, LANES), F32),
                        pltpu.VMEM((nc, GATE_PAD, L), F32), pltpu.VMEM((nc, GATE_PAD, L), F32),
                        pltpu.VMEM((nc, L, GATE_PAD), F32)],
        compiler_params=_params(("arbitrary", "arbitrary")),
        name="mlstm",
    )(bias_i[:, :, None], bias_i[:, None, :], bias_f[:, :, None], bias_f[:, None, :], norm_g[:, :, None],
      qmt, km, vmt, gic, gfc, gir, gfr, ogt)


def _out_kernel(*refs, alpha, n_ctx_tiles, n_stream, row0):
    x = _stream_tile(refs[:n_stream], n_ctx_tiles, row0)
    mod_ref, ya_ref, yb_ref = refs[n_stream:n_stream + 3]
    refs = refs[n_stream + 3:]
    ym_ref, g_ref, w_ref, lng_ref, lnb_ref, o_ref = refs[-6:]
    gate = mod_ref[0, 0, 2:3, :]
    g = g_ref[0].astype(F32)
    na, nb_ = ya_ref.shape[2], yb_ref.shape[2]
    yb = yb_ref[0]
    if len(refs) == 7:
        yb = jnp.where(pl.program_id(1) < n_ctx_tiles, refs[0][0], yb)
    mix_a = (ya_ref[0].astype(F32) * g[:, :na]).astype(BF16)
    mix_b = (yb.astype(F32) * g[:, na:na + nb_]).astype(BF16)
    mix_m = (ym_ref[0].astype(F32) * g[:, na + nb_:]).astype(BF16)
    y = _dot(jnp.concatenate([mix_a, mix_b, mix_m], axis=1), w_ref[...])
    r = alpha * x + gate * y
    mu = jnp.mean(r, axis=1, keepdims=True)
    d = r - mu
    var = jnp.mean(d * d, axis=1, keepdims=True)
    o_ref[0] = d * lax.rsqrt(var + LN_EPS) * lng_ref[...] + lnb_ref[...]


def _out_projection(xa, modsel, ya, yb_lat, yb_ctx, ym, g, w, layer, ln_g, ln_b, alpha, n_ctx_tiles):
    b, s, d = _stream_shape(xa)
    tm = ROW_TILE
    row0 = 0 if yb_ctx is not None else n_ctx_tiles
    row = lambda n: pl.BlockSpec((1, tm, n), lambda bi, i: (bi, i + row0, 0))
    vec = pl.BlockSpec((1, d), lambda bi, i: (0, 0))
    nb_ = yb_lat.shape[2]
    x_specs, x_args = _stream_specs(xa, tm, row0)
    in_specs = x_specs + [
        pl.BlockSpec((1, 1, 3, d), lambda bi, i: (bi, jnp.minimum(i + row0, 1), 0, 0)),
        row(ya.shape[2]),
        pl.BlockSpec((1, tm, nb_), lambda bi, i: (bi, jnp.maximum(i + row0 - n_ctx_tiles, 0), 0)),
    ]
    args = x_args + [modsel, ya, yb_lat]
    if yb_ctx is not None:
        in_specs.append(pl.BlockSpec((1, tm, nb_), lambda bi, i: (bi, jnp.minimum(i, n_ctx_tiles - 1), 0)))
        args.append(yb_ctx)
    in_specs += [row(ym.shape[2]), row(d), pl.BlockSpec((None,) + w.shape[1:], lambda bi, i: (layer, 0, 0)),
                 vec, vec]
    args += [ym, g, w, ln_g, ln_b]
    return pl.pallas_call(
        functools.partial(_out_kernel, alpha=alpha, n_ctx_tiles=n_ctx_tiles, n_stream=len(x_args), row0=row0),
        grid=(b, s // tm - row0),
        in_specs=in_specs,
        out_specs=pl.BlockSpec((1, tm, d), lambda bi, i: (bi, i, 0)),
        out_shape=jax.ShapeDtypeStruct((b, s - row0 * tm, d), F32),
        compiler_params=_params(("arbitrary", "arbitrary")),
        name="out_projection",
    )(*args)


def _in_weight_columns(d_in):
    z = d_in
    src = {}
    off = 0
    for name, n in (("qa", 256), ("ka", 128), ("va", 128), ("za", 256), ("qb", 384), ("kb", 384), ("vb", 384),
                    ("zb", 384), ("qm", 384), ("km", 384), ("vm", 384), ("om", 384), ("zm", 384), ("gm", 24)):
        src[name] = np.arange(off, off + n)
        off += n
    assert off == d_in
    cols = []
    for hd in range(4):
        blk = np.full(LANES, z)
        gq = hd // 2
        blk[64 * gq:64 * gq + 64] = src["qa"][64 * hd:64 * hd + 64]
        cols.append(blk)
    for name in ("ka", "va", "qb", "kb", "vb", "qm", "km", "vm", "om", "za", "zb", "zm"):
        cols.append(src[name])
    heads = len(src["gm"]) // 4
    for p in range(heads // 2):
        for gate in range(2):
            cols.append(np.array([src["gm"][di * 2 * heads + gate * heads + 2 * p + a]
                                  for di in range(2) for a in range(2)] + [z] * (GATE_PAD - 4)))
    cols.append(np.full(LANES - 2 * GATE_PAD * (heads // 2), z))
    cols = np.concatenate(cols)
    assert cols.shape[0] == _W_COLS
    return cols


def _permute_columns(w, cols):
    n = w.shape[-1]
    pieces, start = [], 0
    breaks = np.flatnonzero(np.diff(cols) != 1) + 1
    for stop in list(breaks) + [len(cols)]:
        run = cols[start:stop]
        if len(run) >= 32:
            pieces.append(lax.slice_in_dim(w, int(run[0]), int(run[-1]) + 1, axis=w.ndim - 1))
        elif pieces and isinstance(pieces[-1], list):
            pieces[-1].extend(run)
        else:
            pieces.append(list(run))
        start = stop
    lo = min(min(p) for p in pieces if isinstance(p, list))
    tail = jnp.concatenate([lax.slice_in_dim(w, lo, n, axis=w.ndim - 1), jnp.zeros(w.shape[:-1] + (1,), w.dtype)], -1)
    pieces = [jnp.take(tail, np.array(p) - lo, axis=-1) if isinstance(p, list) else p for p in pieces]
    return jnp.concatenate(pieces, axis=-1)


def _rope_table(n_tokens, n_ctx, dim):
    f32 = np.float32
    rows = n_tokens // GRID_W
    rowp = np.broadcast_to(np.arange(rows, dtype=f32)[:, None], (rows, GRID_W)).reshape(-1)
    colp = np.broadcast_to(np.arange(GRID_W, dtype=f32)[None, :], (rows, GRID_W)).reshape(-1)
    n_freq = dim // 4
    inv = np.power(f32(ROPE_BASE), -np.arange(n_freq, dtype=f32) / f32(n_freq)).astype(f32)
    ar = rowp[:, None] * inv
    ac = colp[:, None] * inv
    ang = np.concatenate([ar, ar, ac, ac], -1).astype(f32)
    cos, sin = np.cos(ang), np.sin(ang)
    odd = (np.arange(dim) // n_freq) % 2 == 1
    sin_p = np.where(odd, sin, f32(0))
    sin_m = np.where(odd, f32(0), -sin)
    tab = np.stack([cos, sin_p, sin_m])
    ident = np.stack([np.ones((n_ctx, dim), f32), np.zeros((n_ctx, dim), f32), np.zeros((n_ctx, dim), f32)])
    tab = np.concatenate([ident, tab], axis=1)
    return jnp.asarray(np.tile(tab, (1, 1, LANES // dim)).astype(f32))


def kernel(x, c, ctx, c_ctx, w_mod, b_mod, w_in, attn_sink, diff_lambda, diff_norm_g, mlstm_i_bias,
           mlstm_f_bias, mlstm_norm_g, w_out, ln_g, ln_b):
    b, t, d = x.shape
    n_ctx = ctx.shape[1]
    depth = w_mod.shape[0]
    d_in = w_in.shape[2]
    alpha = (2 * depth) ** 0.25

    xa = (ctx, x)
    rope_a = _rope_table(t, n_ctx, HEAD_DIM)
    rope_b = _rope_table(t, n_ctx, B_QK_DIM)

    rows = -(-(b + 1) // 8) * 8
    cc = jnp.concatenate([c, c_ctx[None, :], jnp.zeros((rows - b - 1, d), F32)], axis=0)
    mod = _modulation(cc, w_mod, b_mod).reshape(depth, rows, 3, d)

    w_in_p = _permute_columns(w_in.astype(BF16), _in_weight_columns(d_in))
    w_out_b = w_out.astype(BF16)
    norm_b = diff_norm_g[:, :, None]

    n_pairs = mlstm_i_bias.shape[2] // 2
    heads = mlstm_i_bias.shape[2]
    chains = [(di, a) for di in range(2) for a in range(2)]
    bias_idx = np.array([[di * heads + 2 * p + a for di, a in chains] for p in range(n_pairs)])
    pad8 = lambda v: jnp.pad(v.reshape(-1)[bias_idx], ((0, 0), (0, GATE_PAD - 4)))

    for l in range(depth):
        with_ctx = l < depth - 1
        lam_init = 0.8 - 0.6 * math.exp(-0.3 * l)
        modsel = jnp.stack([jnp.broadcast_to(mod[l, b], (b, 3, d)), mod[l, :b]], axis=1)
        (qa, ka, va, qb, kb, vb, qm, km, vm, og, g, gic, gfc, gir, gfr) = _in_projection(
            xa, modsel, w_in_p, l, rope_a, rope_b, n_ctx)

        ya = _attn_a(attn_sink[l], qa, ka, va, n_ctx)
        yb, yb_ctx = _attn_b(jnp.full((1,), lam_init, F32), diff_lambda[l], norm_b[l], qb, kb, vb, n_ctx, with_ctx)

        ym = _mlstm(pad8(mlstm_i_bias[l]), pad8(mlstm_f_bias[l]), mlstm_norm_g[l].reshape(n_pairs, LANES),
                    qm, km, vm, gic, gfc, gir, gfr, og, n_ctx)

        xa = _out_projection(xa, modsel, ya, yb, yb_ctx, ym, g, w_out_b, l, ln_g[l][None, :], ln_b[l][None, :],
                             alpha, n_ctx // ROW_TILE)
    return xa
```

```python
import functools
import math

import numpy as np
import jax
import jax.numpy as jnp
from jax import lax
from jax.experimental import pallas as pl
from jax.experimental.pallas import tpu as pltpu

F32 = jnp.float32
BF16 = jnp.bfloat16

LANES = 128
GRID_W = 64
HEAD_DIM = 64
ROPE_BASE = 10000.0
LN_EPS = 1e-5
WINDOW = 128
B_QK_DIM = 32
LOG2E = math.log2(math.e)
VB_ROWS = HEAD_DIM + 16
MLSTM_CHUNK = 128
GATE_PAD = 8
MOD_COL_TILE = 1024
B_BOUND_CAP = 32.0
B_BOUND_MARGIN = 1.02
B_KEY_GROUP_BOUNDED = 4
B_KEY_GROUP = 2
ROW_TILE = 256
VMEM_LIMIT = 56 * 1024 * 1024


def _dot(a, b):
    return jnp.dot(a, b, preferred_element_type=F32)


def _dot_nt(a, b):
    return lax.dot_general(a, b, (((1,), (1,)), ((), ())), preferred_element_type=F32)


def _split_bf16(a):
    hi = a.astype(BF16)
    lo = (a - hi.astype(F32)).astype(BF16)
    return hi, lo


def _sigmoid(x):
    return 1.0 / (1.0 + jnp.exp(-x))


def _params(sem):
    return pltpu.CompilerParams(dimension_semantics=sem, vmem_limit_bytes=VMEM_LIMIT)


def _mod_kernel(c_ref, w_ref, b_ref, o_ref):
    c = c_ref[...]
    a = c * _sigmoid(c)
    a_hi, a_lo = _split_bf16(a)
    w_hi, w_lo = _split_bf16(w_ref[0])
    o_ref[0] = _dot(a_hi, w_hi) + _dot(a_lo, w_hi) + _dot(a_hi, w_lo) + b_ref[0]


def _modulation(cc, w_mod, b_mod):
    depth, d, n = w_mod.shape
    r = cc.shape[0]
    tn = MOD_COL_TILE
    return pl.pallas_call(
        _mod_kernel,
        grid=(depth, n // tn),
        in_specs=[
            pl.BlockSpec((r, d), lambda l, j: (0, 0)),
            pl.BlockSpec((1, d, tn), lambda l, j: (l, 0, j)),
            pl.BlockSpec((1, 1, tn), lambda l, j: (l, 0, j)),
        ],
        out_specs=pl.BlockSpec((1, r, tn), lambda l, j: (l, 0, j)),
        out_shape=jax.ShapeDtypeStruct((depth, r, n), F32),
        compiler_params=_params(("arbitrary", "arbitrary")),
        name="modulation",
    )(cc, w_mod, b_mod.reshape(depth, 1, n))


def _stream_shape(xa):
    if isinstance(xa, tuple):
        ctx, x = xa
        return x.shape[0], ctx.shape[1] + x.shape[1], x.shape[2]
    return xa.shape


def _stream_specs(xa, tm, row0):
    if not isinstance(xa, tuple):
        return [pl.BlockSpec((1, tm, xa.shape[2]), lambda bi, i: (bi, i + row0, 0))], [xa]
    ctx, x = xa
    nct = ctx.shape[1] // tm
    return ([pl.BlockSpec((1, tm, ctx.shape[2]), lambda bi, i: (bi, jnp.minimum(i + row0, nct - 1), 0)),
             pl.BlockSpec((1, tm, x.shape[2]), lambda bi, i: (bi, jnp.maximum(i + row0 - nct, 0), 0))], [ctx, x])


def _stream_tile(refs, n_ctx_tiles, row0):
    if len(refs) == 1:
        return refs[0][0]
    return jnp.where(pl.program_id(1) + row0 < n_ctx_tiles, refs[0][0], refs[1][0])


_QA, _KA, _VA = (0, 512), (512, 640), (640, 768)
_QB, _KB, _VB = (768, 1152), (1152, 1536), (1536, 1920)
_QM, _KM, _VM = (1920, 2304), (2304, 2688), (2688, 3072)
_OM, _Z, _GM = (3072, 3456), (3456, 4480), (4480, 4608)
_W_COLS = 4608
_W_GROUPS = ((0, 768), (768, 1536), (1536, 2304), (2304, 3072), (3072, 4608))


def _rope(t, tab_ref, quarter):
    cos, sin_p, sin_m = tab_ref[0], tab_ref[1], tab_ref[2]
    outs = []
    for j in range(t.shape[1] // LANES):
        tj = t[:, j * LANES:(j + 1) * LANES]
        outs.append(tj * cos + pltpu.roll(tj, quarter, 1) * sin_p + pltpu.roll(tj, LANES - quarter, 1) * sin_m)
    return outs[0] if len(outs) == 1 else jnp.concatenate(outs, axis=1)


def _inproj_kernel(*refs, n_stream, n_ctx_tiles):
    (mod_ref, w_ref, ra_ref, rb_ref, qa_ref, ka_ref, va_ref, qb_ref, kb_ref, vb_ref,
     qm_ref, km_ref, vm_ref, om_ref, g_ref, gic_ref, gfc_ref, gir_ref, gfr_ref) = refs[n_stream:]
    x = _stream_tile(refs[:n_stream], n_ctx_tiles, 0)
    shift = mod_ref[0, 0, 0:1, :]
    scale = mod_ref[0, 0, 1:2, :]
    h = (x * (1.0 + scale) + shift).astype(BF16)

    group_dots = {}

    def proj(cols):
        lo, hi = next(g for g in _W_GROUPS if g[0] <= cols[0] and cols[1] <= g[1])
        if lo not in group_dots:
            group_dots[lo] = _dot(h, w_ref[:, lo:hi])
        return group_dots[lo][:, cols[0] - lo:cols[1] - lo]

    qa_ref[0] = (_rope(proj(_QA), ra_ref, HEAD_DIM // 4) * (HEAD_DIM ** -0.5)).astype(BF16)
    ka_ref[0] = _rope(proj(_KA), ra_ref, HEAD_DIM // 4).astype(BF16)
    va_ref[0] = proj(_VA).astype(BF16)
    qb_t = (_rope(proj(_QB), rb_ref, B_QK_DIM // 4) * (B_QK_DIM ** -0.5 * LOG2E)).T.astype(BF16)
    for p in range(qb_t.shape[0] // LANES):
        qb_ref[0, p, 0] = qb_t[p * LANES:(p + 1) * LANES]
    vb_t = proj(_VB).T
    ones_rows = jnp.where(lax.broadcasted_iota(jnp.int32, (VB_ROWS - HEAD_DIM, vb_t.shape[1]), 0) == 0, 1.0, 0.0)
    for hd in range(vb_t.shape[0] // HEAD_DIM):
        vb_ref[0, hd, 0] = jnp.concatenate([vb_t[hd * HEAD_DIM:(hd + 1) * HEAD_DIM], ones_rows], axis=0).astype(BF16)
    kb_ref[0] = _rope(proj(_KB), rb_ref, B_QK_DIM // 4).astype(BF16)
    km_ref[0] = (proj(_KM) * (HEAD_DIM ** -0.5)).astype(BF16)
    qm_t = proj(_QM).T.astype(BF16)
    om_t = _sigmoid(proj(_OM)).T.astype(BF16)
    vm_t = proj(_VM).T
    L = MLSTM_CHUNK
    ones_rows_c = jnp.where(lax.broadcasted_iota(jnp.int32, (VB_ROWS - HEAD_DIM, L), 0) == 0, 1.0, 0.0)
    for c in range(qm_t.shape[1] // L):
        cols = slice(c * L, (c + 1) * L)
        for p in range(qm_t.shape[0] // LANES):
            qm_ref[0, p, c] = qm_t[p * LANES:(p + 1) * LANES, cols]
            om_ref[0, p, c] = om_t[p * LANES:(p + 1) * LANES, cols]
        for hd in range(vm_t.shape[0] // HEAD_DIM):
            vm_ref[0, hd, c] = jnp.concatenate(
                [vm_t[hd * HEAD_DIM:(hd + 1) * HEAD_DIM, cols], ones_rows_c], axis=0).astype(BF16)
    z = proj(_Z)
    g_ref[0] = (z * _sigmoid(z)).astype(BF16)
    gm = proj(_GM)
    gm_t = gm.T
    for p in range(gic_ref.shape[1]):
        lo, hi = 2 * GATE_PAD * p, 2 * GATE_PAD * p + GATE_PAD
        gic_ref[0, p] = gm[:, lo:hi]
        gfc_ref[0, p] = gm[:, hi:hi + GATE_PAD]
        for c in range(gm_t.shape[1] // L):
            gir_ref[0, p, c] = gm_t[lo:hi, c * L:(c + 1) * L]
            gfr_ref[0, p, c] = gm_t[hi:hi + GATE_PAD, c * L:(c + 1) * L]


def _in_projection(xa, modsel, w, layer, rope_a, rope_b, n_ctx):
    b, s, d = _stream_shape(xa)
    tm = ROW_TILE
    x_specs, x_args = _stream_specs(xa, tm, 0)
    n_pairs = (_QB[1] - _QB[0]) // LANES
    n_heads = (_VB[1] - _VB[0]) // HEAD_DIM
    L = MLSTM_CHUNK
    widths = [512, 128, 128, (n_pairs, LANES, tm), 384, (n_heads, VB_ROWS, tm),
              (n_pairs, LANES, L), 384, (n_heads, VB_ROWS, L), (n_pairs, LANES, L), 1024]
    tposed = lambda g, r, tt: pl.BlockSpec((1, g, tm // tt, r, tt), lambda bi, i: (bi, 0, i, 0, 0))
    row = lambda n: tposed(*n) if isinstance(n, tuple) else pl.BlockSpec((1, tm, n), lambda bi, i: (bi, i, 0))
    out_shape = [jax.ShapeDtypeStruct((b, n[0], s // n[2], n[1], n[2]) if isinstance(n, tuple) else (b, s, n), BF16)
                 for n in widths]
    gate_c = pl.BlockSpec((1, n_pairs, tm, GATE_PAD), lambda bi, i: (bi, 0, i, 0))
    gate_r = pl.BlockSpec((1, n_pairs, tm // L, GATE_PAD, L), lambda bi, i: (bi, 0, i, 0, 0))
    out_shape += [jax.ShapeDtypeStruct((b, n_pairs, s, GATE_PAD), F32)] * 2
    out_shape += [jax.ShapeDtypeStruct((b, n_pairs, s // L, GATE_PAD, L), F32)] * 2
    return pl.pallas_call(
        functools.partial(_inproj_kernel, n_stream=len(x_args), n_ctx_tiles=n_ctx // tm),
        grid=(b, s // tm),
        in_specs=x_specs + [
            pl.BlockSpec((1, 1, 3, d), lambda bi, i: (bi, jnp.minimum(i, 1), 0, 0)),
            pl.BlockSpec((None, d, _W_COLS), lambda bi, i: (layer, 0, 0)),
            pl.BlockSpec((3, tm, LANES), lambda bi, i: (0, i, 0)),
            pl.BlockSpec((3, tm, LANES), lambda bi, i: (0, i, 0)),
        ],
        out_specs=[row(n) for n in widths] + [gate_c, gate_c, gate_r, gate_r],
        out_shape=out_shape,
        compiler_params=_params(("arbitrary", "arbitrary")),
        name="in_projection",
    )(*x_args, modsel, w, rope_a, rope_b)


def _attn_a_kernel(sink_ref, q_ref, k_ref, v_ref, o_ref, *, n_ctx, n_blocks):
    w = WINDOW
    s_len = n_blocks * w
    half = LANES // 2
    rows1 = lax.broadcasted_iota(jnp.int32, (4 * w, 1), 0)
    sink = jnp.where(rows1 < w, sink_ref[0],
                     jnp.where(rows1 < 2 * w, sink_ref[1], jnp.where(rows1 < 3 * w, sink_ref[2], sink_ref[3])))
    row = lax.broadcasted_iota(jnp.int32, (4 * w, 3 * w), 0) & (w - 1)
    col = lax.broadcasted_iota(jnp.int32, (4 * w, 3 * w), 1)
    lane = lax.broadcasted_iota(jnp.int32, (w, LANES), 1)
    k_ctx = k_ref[0, 0:n_ctx, :]
    v_ctx = v_ref[0, 0:n_ctx, :]

    def body(i, carry):
        r0 = pl.multiple_of(i * w, w)
        q = q_ref[0, pl.ds(r0, w), :]
        qs = jnp.concatenate([q[:, j * LANES:(j + 1) * LANES] for j in range(4)], axis=0)
        start = pl.multiple_of(jnp.clip(r0 - w, 0, s_len - 3 * w), w)
        s_loc = _dot_nt(qs, k_ref[0, pl.ds(start, 3 * w), :])
        s_ctx = _dot_nt(qs, k_ctx)
        kpos = col + start
        rel = kpos - r0 - row
        ok = (jnp.abs(rel) <= w) & (kpos >= jnp.where(r0 >= n_ctx, n_ctx, s_len))
        s_loc = jnp.where(ok, s_loc, -jnp.inf)
        m = jnp.maximum(jnp.maximum(jnp.max(s_loc, axis=1, keepdims=True),
                                    jnp.max(s_ctx, axis=1, keepdims=True)), sink)
        p_loc = jnp.exp(s_loc - m)
        p_ctx = jnp.exp(s_ctx - m)
        den = (jnp.sum(p_loc, axis=1, keepdims=True) + jnp.sum(p_ctx, axis=1, keepdims=True)
               + jnp.exp(sink - m))
        o = (_dot(p_loc.astype(BF16), v_ref[0, pl.ds(start, 3 * w), :])
             + _dot(p_ctx.astype(BF16), v_ctx)) / den
        left = jnp.where(lane < half, o[0:w], pltpu.roll(o[w:2 * w], half, 1))
        right = jnp.where(lane < half, pltpu.roll(o[2 * w:3 * w], half, 1), o[3 * w:])
        o_ref[0, pl.ds(r0, w), :] = jnp.concatenate([left, right], axis=1).astype(o_ref.dtype)
        return carry

    lax.fori_loop(0, n_blocks, body, 0, unroll=2)


def _attn_a(sink, qa, ka, va, n_ctx):
    b, s, _ = qa.shape
    seq = lambda n: pl.BlockSpec((1, s, n), lambda bi: (bi, 0, 0))
    return pl.pallas_call(
        functools.partial(_attn_a_kernel, n_ctx=n_ctx, n_blocks=s // WINDOW),
        grid=(b,),
        in_specs=[pl.BlockSpec(memory_space=pltpu.SMEM), seq(4 * LANES), seq(LANES), seq(LANES)],
        out_specs=seq(2 * LANES),
        out_shape=jax.ShapeDtypeStruct((b, s, 2 * LANES), BF16),
        compiler_params=_params(("arbitrary",)),
        name="window_attention",
    )(sink, qa, ka, va)


def _attn_b_kernel(li_ref, lam_ref, ng_ref, qt_ref, k_ref, vt_ref, o_ref,
                   qs_scr, sa_scr, sb_scr, sc_scr, ea_scr, eb_scr, ec_scr, acc_scr, *, nt, q0, nq, kg, kgb):
    tq, tk = qt_ref.shape[-1], vt_ref.shape[-1]
    hv = HEAD_DIM
    q_cols = 4 * tq
    lam_init = li_ref[0]
    lv = lam_ref[...]
    lam = (jnp.exp(jnp.sum(lv[0:1] * lv[1:2], axis=1, keepdims=True))
           - jnp.exp(jnp.sum(lv[2:3] * lv[3:4], axis=1, keepdims=True)) + lam_init)

    u_r = lax.broadcasted_iota(jnp.int32, (LANES, LANES), 0) // B_QK_DIM
    u_c = lax.broadcasted_iota(jnp.int32, (LANES, LANES), 1) // B_QK_DIM
    unit_ones = jnp.where(u_r == u_c, 1.0, 0.0).astype(BF16)

    def key_norms(t, best):
        kt = k_ref[0, pl.ds(pl.multiple_of(t * tk, tk), tk), :].astype(F32)
        return jnp.maximum(best, jnp.max(_dot((kt * kt).astype(BF16), unit_ones), axis=0, keepdims=True))
    kmax2 = lax.fori_loop(0, nt, key_norms, jnp.zeros((1, LANES), F32))

    def scores(t, n, z):
        off = pl.multiple_of(t * tk, tk)
        return _dot(k_ref[0, pl.ds(off, n * tk), :], qs_scr[z])

    def values(p, t, n, hd):
        cols = slice(hd * 2 * tq, (hd + 1) * 2 * tq)
        pv = _dot(vt_ref[0, hd, t], p[0:tk, cols])
        for c in range(1, n):
            pv = pv + _dot(vt_ref[0, hd, t + c], p[c * tk:(c + 1) * tk, cols])
        return pv

    def finish(i, z):
        def head_out(hd):
            acc = acc_scr[z, hd]
            o = acc[0:hv] * (1.0 / acc[hv:hv + 1])
            od = o[:, 0:tq] - lam * o[:, tq:]
            ms = jnp.mean(od * od, axis=0, keepdims=True)
            return od * lax.rsqrt(ms + LN_EPS) * ng_ref[...]
        y_t = jnp.concatenate([head_out(0), head_out(1)], axis=0) * (1.0 - lam_init)
        o_ref[0, pl.ds(pl.multiple_of(i * tq, tq), tq), :] = y_t.T.astype(o_ref.dtype)

    def run_pipeline(fetch, consume, bufs, g, state):
        buf_a, buf_b, buf_c = bufs
        n_groups = (nt - 1) // g
        first = lambda j: 1 + g * j

        def body(tt, carry):
            state, tag_a = carry
            j0 = 2 * tt
            tag_b = fetch(first(j0 + 1), buf_b, g)
            state = consume(buf_a, tag_a, state, first(j0), g)
            tag_a = fetch(first(j0 + 2), buf_a, g)
            state = consume(buf_b, tag_b, state, first(j0 + 1), g)
            return state, tag_a

        tag_c = fetch(0, buf_c, 1)
        if n_groups == 0:
            return consume(buf_c, tag_c, state, 0, 1)
        tag_a = fetch(first(0), buf_a, g)
        state = consume(buf_c, tag_c, state, 0, 1)
        state, tag_a = lax.fori_loop(0, (n_groups - 1) // 2, body, (state, tag_a))
        if n_groups % 2 == 0:
            tag_b = fetch(first(n_groups - 1), buf_b, g)
            state = consume(buf_a, tag_a, state, first(n_groups - 2), g)
            return consume(buf_b, tag_b, state, first(n_groups - 1), g)
        return consume(buf_a, tag_a, state, first(n_groups - 1), g)

    def bounded(i, bound, z):
        def fetch(t, e_buf, n):
            e_buf[z] = jnp.exp2(scores(t, n, z) - bound).astype(BF16)
            return bound

        def consume(e_buf, tag, state, t, n):
            for hd in range(2):
                acc_scr[z, hd] = acc_scr[z, hd] + values(e_buf[z], t, n, hd)
            return state

        run_pipeline(fetch, consume, (ea_scr, eb_scr, ec_scr), kgb, bound)
        finish(i, z)

    def online(i):
        def fetch(t, s_scr, n):
            s_new = scores(t, n, 0)
            s_scr[...] = s_new
            return jnp.max(s_new, axis=0, keepdims=True)

        def consume(s_scr, mt, m, t, n):
            m_new = jnp.maximum(m, mt)
            alpha = jnp.exp2(m - m_new)
            p = jnp.exp2(s_scr[...] - m_new).astype(BF16)
            for hd in range(2):
                cols = slice(hd * 2 * tq, (hd + 1) * 2 * tq)
                acc_scr[0, hd] = alpha[:, cols] * acc_scr[0, hd] + values(p, t, n, hd)
            return m_new

        run_pipeline(fetch, consume, (sa_scr, sb_scr, sc_scr), kg, jnp.full((1, q_cols), -jnp.inf, F32))
        finish(i, 0)

    def prepare(i, z):
        qt = qt_ref[0, 0, q0 + i]
        unit = lax.broadcasted_iota(jnp.int32, qt.shape, 0) // B_QK_DIM
        zero = jnp.zeros_like(qt)
        qs_scr[z] = jnp.concatenate([jnp.where(unit == u, qt, zero) for u in range(4)], axis=1)
        acc_scr[z] = jnp.zeros(acc_scr.shape[1:], F32)
        qf = qt.astype(F32)
        qn2 = jnp.sum((qf * qf).reshape(4, B_QK_DIM, tq), axis=1)
        bound = jnp.concatenate(
            [jnp.sqrt(qn2[u:u + 1] * kmax2[:, u * B_QK_DIM:u * B_QK_DIM + 1]) for u in range(4)], axis=1)
        return bound * B_BOUND_MARGIN

    def one_tile(i):
        bound = prepare(i, 0)
        small = jnp.max(bound) <= B_BOUND_CAP
        pl.when(small)(lambda: bounded(i, bound, 0))
        pl.when(jnp.logical_not(small))(lambda: online(i))

    def tile_pair(j, carry):
        i0, i1 = 2 * j, 2 * j + 1
        b0, b1 = prepare(i0, 0), prepare(i1, 1)
        both = jnp.maximum(jnp.max(b0), jnp.max(b1)) <= B_BOUND_CAP

        def fast():
            bounded(i0, b0, 0)
            bounded(i1, b1, 1)

        def slow():
            one_tile(i0)
            one_tile(i1)

        pl.when(both)(fast)
        pl.when(jnp.logical_not(both))(slow)
        return carry

    lax.fori_loop(0, nq // 2, tile_pair, 0)
    if nq % 2:
        one_tile(nq - 1)


def _attn_b_call(lam_init, lam_vec, norm_g, qbt, kb, vbt, q0, nq, nt):
    b, pairs, n_tiles, _, tq = qbt.shape
    vrows, tk = vbt.shape[-2:]
    kg = B_KEY_GROUP if (nt - 1) % B_KEY_GROUP == 0 else 1
    kgb = B_KEY_GROUP_BOUNDED if (nt - 1) % B_KEY_GROUP_BOUNDED == 0 else 1
    score_bufs = lambda dt, g, *z: [pltpu.VMEM(z + (g * tk, 4 * tq), dt), pltpu.VMEM(z + (g * tk, 4 * tq), dt),
                                    pltpu.VMEM(z + (tk, 4 * tq), dt)]
    return pl.pallas_call(
        functools.partial(_attn_b_kernel, nt=nt, q0=q0, nq=nq, kg=kg, kgb=kgb),
        grid=(b, pairs),
        in_specs=[
            pl.BlockSpec(memory_space=pltpu.SMEM),
            pl.BlockSpec((4, B_QK_DIM), lambda bi, p: (0, 0)),
            pl.BlockSpec((HEAD_DIM, 1), lambda bi, p: (0, 0)),
            pl.BlockSpec((1, 1, n_tiles, LANES, tq), lambda bi, p: (bi, p, 0, 0, 0)),
            pl.BlockSpec((1, nt * tk, LANES), lambda bi, p: (bi, 0, p)),
            pl.BlockSpec((1, 2, nt, vrows, tk), lambda bi, p: (bi, p, 0, 0, 0)),
        ],
        out_specs=pl.BlockSpec((1, nq * tq, LANES), lambda bi, p: (bi, 0, p)),
        out_shape=jax.ShapeDtypeStruct((b, nq * tq, pairs * LANES), BF16),
        scratch_shapes=[pltpu.VMEM((2, LANES, 4 * tq), BF16)] + score_bufs(F32, kg) + score_bufs(BF16, kgb, 2)
                       + [pltpu.VMEM((2, 2, vrows, 2 * tq), F32)],
        compiler_params=_params(("arbitrary", "arbitrary")),
        name="diff_attention",
    )(lam_init, lam_vec, norm_g, qbt, kb, vbt)


def _attn_b(lam_init, lam_vec, norm_g, qbt, kb, vbt, n_ctx, with_ctx):
    n_tiles, tq = qbt.shape[2], qbt.shape[4]
    tk = vbt.shape[-1]
    n_ctx_tiles = n_ctx // tq
    y_lat = _attn_b_call(lam_init, lam_vec, norm_g, qbt, kb, vbt, n_ctx_tiles, n_tiles - n_ctx_tiles, n_tiles)
    y_ctx = _attn_b_call(lam_init, lam_vec, norm_g, qbt, kb, vbt, 0, n_ctx_tiles, n_ctx // tk) if with_ctx else None
    return y_lat, y_ctx


def _log_sigmoid(x):
    return jnp.minimum(x, 0.0) - jnp.log(1.0 + jnp.exp(-jnp.abs(x)))


def _mlstm_kernel(bir_ref, bic_ref, bfr_ref, bfc_ref, ng_ref, qt_ref, k_ref, vt_ref, gic_ref, gfc_ref, gir_ref,
                  gfr_ref, ogt_ref,
                  o_ref, hf_scr, hb_scr, st_scr, ir_scr, br_scr, gc_scr, *, n_ctx_chunks, n_chunks):
    L = MLSTM_CHUNK
    hv = HEAD_DIM
    r_i = lax.broadcasted_iota(jnp.int32, (L, L), 0)
    c_i = lax.broadcasted_iota(jnp.int32, (L, L), 1)
    upper = r_i <= c_i
    lower = r_i >= c_i
    t_up = jnp.where(upper, 1.0, 0.0).astype(BF16)
    t_low = jnp.where(lower, 1.0, 0.0).astype(BF16)
    row_q = lax.broadcasted_iota(jnp.int32, (LANES, L), 0)
    sel = [jnp.where(row_q < hv, 1.0, 0.0).astype(BF16), jnp.where(row_q < hv, 0.0, 1.0).astype(BF16)]
    fwd_r = lax.broadcasted_iota(jnp.int32, (GATE_PAD, L), 0) < 2
    fwd_c = lax.broadcasted_iota(jnp.int32, (L, GATE_PAD), 1) < 2
    st_scr[...] = jnp.zeros_like(st_scr)

    def gates(c, carry):
        r0 = pl.multiple_of(c * L, L)
        frow = _log_sigmoid(gfr_ref[0, 0, c] + bfr_ref[0])
        hi, lo = _split_bf16(frow)
        ir_scr[c] = gir_ref[0, 0, c] + bir_ref[0]
        br_scr[c] = jnp.where(fwd_r, _dot(hi, t_up) + _dot(lo, t_up), _dot(hi, t_low) + _dot(lo, t_low))
        fcol = _log_sigmoid(gfc_ref[0, 0, pl.ds(r0, L), :] + bfc_ref[0])
        hi, lo = _split_bf16(fcol)
        bcol = jnp.where(fwd_c, _dot(t_low, hi) + _dot(t_low, lo), _dot(t_up, hi) + _dot(t_up, lo))
        gc_scr[c] = gic_ref[0, 0, pl.ds(r0, L), :] + bic_ref[0] - bcol
        return carry
    lax.fori_loop(0, n_chunks, gates, 0, unroll=4)

    def chunk_step(direction, c, m_prev):
        r0 = pl.multiple_of(c * L, L)
        kc = k_ref[0, pl.ds(r0, L), :]
        qtc = qt_ref[0, 0, c]
        irow8, brow8, gcol8 = ir_scr[c], br_scr[c], gc_scr[c]
        tri = upper if direction == 0 else lower
        h_scr = hf_scr if direction == 0 else hb_scr
        qtm = [qtc * sel[a] for a in range(2)]
        s_both = _dot(kc, jnp.concatenate(qtm, axis=1))
        m_news, vws, decays = [], [], []
        for a in range(2):
            ci = 2 * direction + a
            b_row, i_row, g_col = brow8[ci:ci + 1], irow8[ci:ci + 1], gcol8[:, ci:ci + 1]
            log_d = jnp.where(tri, g_col + b_row, -jnp.inf)
            m_in = b_row + m_prev[a]
            m_t = jnp.maximum(m_in, jnp.max(log_d, axis=0, keepdims=True))
            w_in = jnp.exp(m_in - m_t)
            sd = (s_both[:, a * L:(a + 1) * L] * jnp.exp(log_d - m_t)).astype(BF16)
            qw = (qtm[a].astype(F32) * w_in).astype(BF16)
            vt = vt_ref[0, a, c]
            numden = _dot(jnp.concatenate([st_scr[ci].astype(BF16), vt], axis=1),
                          jnp.concatenate([qw, sd], axis=0))
            h_scr[c, a * hv:(a + 1) * hv, :] = (
                numden[0:hv] / jnp.maximum(jnp.abs(numden[hv:hv + 1]), jnp.exp(-m_t)))
            b_last = jnp.min(b_row, axis=1, keepdims=True)
            log_w = b_last - b_row + i_row
            m_new = jnp.maximum(b_last + m_prev[a], jnp.max(log_w, axis=1, keepdims=True))
            vws.append((vt.astype(F32) * jnp.exp(log_w - m_new)).astype(BF16))
            decays.append(jnp.exp(b_last + m_prev[a] - m_new))
            m_news.append(m_new)
        rows = vt_ref.shape[3]
        upd = _dot(jnp.concatenate(vws, axis=0), kc)
        for a in range(2):
            ci = 2 * direction + a
            st_scr[ci] = decays[a] * st_scr[ci] + upd[a * rows:(a + 1) * rows]
        return tuple(m_news)

    def body(j, carry):
        mf, mb = carry
        cb = jnp.where(j < n_ctx_chunks, n_ctx_chunks - 1 - j, n_chunks - 1 + n_ctx_chunks - j)
        return chunk_step(0, j, mf), chunk_step(1, cb, mb)

    z11 = jnp.zeros((1, 1), F32)
    lax.fori_loop(0, n_chunks, body, ((z11, z11), (z11, z11)), unroll=4)

    def finish(c, carry):
        r0 = pl.multiple_of(c * L, L)
        h = hf_scr[c] + hb_scr[c]
        outs = []
        for a in range(2):
            ha = h[a * hv:(a + 1) * hv]
            outs.append(ha * lax.rsqrt(jnp.mean(ha * ha, axis=0, keepdims=True) + LN_EPS))
        y_t = jnp.concatenate(outs, axis=0) * ng_ref[0] * ogt_ref[0, 0, c].astype(F32)
        o_ref[0, pl.ds(r0, L), :] = y_t.T.astype(o_ref.dtype)
        return carry
    lax.fori_loop(0, n_chunks, finish, 0, unroll=4)


def _mlstm(bias_i, bias_f, norm_g, qmt, km, vmt, gic, gfc, gir, gfr, ogt, n_ctx):
    b, pairs, nc, _, L = qmt.shape
    s = nc * L
    vrows = vmt.shape[3]
    seq = pl.BlockSpec((1, s, LANES), lambda bi, p: (bi, 0, p))
    pair_tiles = pl.BlockSpec((1, 1, nc, LANES, L), lambda bi, p: (bi, p, 0, 0, 0))
    gate_c = pl.BlockSpec((1, 1, s, GATE_PAD), lambda bi, p: (bi, p, 0, 0))
    gate_r = pl.BlockSpec((1, 1, nc, GATE_PAD, L), lambda bi, p: (bi, p, 0, 0, 0))
    bias_r = pl.BlockSpec((1, GATE_PAD, 1), lambda bi, p: (p, 0, 0))
    bias_c = pl.BlockSpec((1, 1, GATE_PAD), lambda bi, p: (p, 0, 0))
    return pl.pallas_call(
        functools.partial(_mlstm_kernel, n_ctx_chunks=n_ctx // L, n_chunks=nc),
        grid=(b, pairs),
        in_specs=[
            bias_r, bias_c, bias_r, bias_c,
            pl.BlockSpec((1, LANES, 1), lambda bi, p: (p, 0, 0)),
            pair_tiles, seq,
            pl.BlockSpec((1, 2, nc, vrows, L), lambda bi, p: (bi, p, 0, 0, 0)),
            gate_c, gate_c, gate_r, gate_r,
            pair_tiles,
        ],
        out_specs=seq,
        out_shape=jax.ShapeDtypeStruct((b, s, pairs * LANES), BF16),
        scratch_shapes=[pltpu.VMEM((nc, LANES, L), F32), pltpu.VMEM((nc, LANES, L), F32),
                        pltpu.VMEM((4, vrows, LANES), F32),
                        pltpu.VMEM((nc, GATE_PAD, L), F32), pltpu.VMEM((nc, GATE_PAD, L), F32),
                        pltpu.VMEM((nc, L, GATE_PAD), F32)],
        compiler_params=_params(("arbitrary", "arbitrary")),
        name="mlstm",
    )(bias_i[:, :, None], bias_i[:, None, :], bias_f[:, :, None], bias_f[:, None, :], norm_g[:, :, None],
      qmt, km, vmt, gic, gfc, gir, gfr, ogt)


def _out_kernel(*refs, alpha, n_ctx_tiles, n_stream, row0):
    x = _stream_tile(refs[:n_stream], n_ctx_tiles, row0)
    mod_ref, ya_ref, yb_ref = refs[n_stream:n_stream + 3]
    refs = refs[n_stream + 3:]
    ym_ref, g_ref, w_ref, lng_ref, lnb_ref, o_ref = refs[-6:]
    gate = mod_ref[0, 0, 2:3, :]
    g = g_ref[0].astype(F32)
    na, nb_ = ya_ref.shape[2], yb_ref.shape[2]
    yb = yb_ref[0]
    if len(refs) == 7:
        yb = jnp.where(pl.program_id(1) < n_ctx_tiles, refs[0][0], yb)
    mix_a = (ya_ref[0].astype(F32) * g[:, :na]).astype(BF16)
    mix_b = (yb.astype(F32) * g[:, na:na + nb_]).astype(BF16)
    mix_m = (ym_ref[0].astype(F32) * g[:, na + nb_:]).astype(BF16)
    y = _dot(jnp.concatenate([mix_a, mix_b, mix_m], axis=1), w_ref[...])
    r = alpha * x + gate * y
    mu = jnp.mean(r, axis=1, keepdims=True)
    d = r - mu
    var = jnp.mean(d * d, axis=1, keepdims=True)
    o_ref[0] = d * lax.rsqrt(var + LN_EPS) * lng_ref[...] + lnb_ref[...]


def _out_projection(xa, modsel, ya, yb_lat, yb_ctx, ym, g, w, layer, ln_g, ln_b, alpha, n_ctx_tiles):
    b, s, d = _stream_shape(xa)
    tm = ROW_TILE
    row0 = 0 if yb_ctx is not None else n_ctx_tiles
    row = lambda n: pl.BlockSpec((1, tm, n), lambda bi, i: (bi, i + row0, 0))
    vec = pl.BlockSpec((1, d), lambda bi, i: (0, 0))
    nb_ = yb_lat.shape[2]
    x_specs, x_args = _stream_specs(xa, tm, row0)
    in_specs = x_specs + [
        pl.BlockSpec((1, 1, 3, d), lambda bi, i: (bi, jnp.minimum(i + row0, 1), 0, 0)),
        row(ya.shape[2]),
        pl.BlockSpec((1, tm, nb_), lambda bi, i: (bi, jnp.maximum(i + row0 - n_ctx_tiles, 0), 0)),
    ]
    args = x_args + [modsel, ya, yb_lat]
    if yb_ctx is not None:
        in_specs.append(pl.BlockSpec((1, tm, nb_), lambda bi, i: (bi, jnp.minimum(i, n_ctx_tiles - 1), 0)))
        args.append(yb_ctx)
    in_specs += [row(ym.shape[2]), row(d), pl.BlockSpec((None,) + w.shape[1:], lambda bi, i: (layer, 0, 0)),
                 vec, vec]
    args += [ym, g, w, ln_g, ln_b]
    return pl.pallas_call(
        functools.partial(_out_kernel, alpha=alpha, n_ctx_tiles=n_ctx_tiles, n_stream=len(x_args), row0=row0),
        grid=(b, s // tm - row0),
        in_specs=in_specs,
        out_specs=pl.BlockSpec((1, tm, d), lambda bi, i: (bi, i, 0)),
        out_shape=jax.ShapeDtypeStruct((b, s - row0 * tm, d), F32),
        compiler_params=_params(("arbitrary", "arbitrary")),
        name="out_projection",
    )(*args)


def _in_weight_columns(d_in):
    z = d_in
    src = {}
    off = 0
    for name, n in (("qa", 256), ("ka", 128), ("va", 128), ("za", 256), ("qb", 384), ("kb", 384), ("vb", 384),
                    ("zb", 384), ("qm", 384), ("km", 384), ("vm", 384), ("om", 384), ("zm", 384), ("gm", 24)):
        src[name] = np.arange(off, off + n)
        off += n
    assert off == d_in
    cols = []
    for hd in range(4):
        blk = np.full(LANES, z)
        gq = hd // 2
        blk[64 * gq:64 * gq + 64] = src["qa"][64 * hd:64 * hd + 64]
        cols.append(blk)
    for name in ("ka", "va", "qb", "kb", "vb", "qm", "km", "vm", "om", "za", "zb", "zm"):
        cols.append(src[name])
    heads = len(src["gm"]) // 4
    for p in range(heads // 2):
        for gate in range(2):
            cols.append(np.array([src["gm"][di * 2 * heads + gate * heads + 2 * p + a]
                                  for di in range(2) for a in range(2)] + [z] * (GATE_PAD - 4)))
    cols.append(np.full(LANES - 2 * GATE_PAD * (heads // 2), z))
    cols = np.concatenate(cols)
    assert cols.shape[0] == _W_COLS
    return cols


def _permute_columns(w, cols):
    n = w.shape[-1]
    pieces, start = [], 0
    breaks = np.flatnonzero(np.diff(cols) != 1) + 1
    for stop in list(breaks) + [len(cols)]:
        run = cols[start:stop]
        if len(run) >= 32:
            pieces.append(lax.slice_in_dim(w, int(run[0]), int(run[-1]) + 1, axis=w.ndim - 1))
        elif pieces and isinstance(pieces[-1], list):
            pieces[-1].extend(run)
        else:
            pieces.append(list(run))
        start = stop
    lo = min(min(p) for p in pieces if isinstance(p, list))
    tail = jnp.concatenate([lax.slice_in_dim(w, lo, n, axis=w.ndim - 1), jnp.zeros(w.shape[:-1] + (1,), w.dtype)], -1)
    pieces = [jnp.take(tail, np.array(p) - lo, axis=-1) if isinstance(p, list) else p for p in pieces]
    return jnp.concatenate(pieces, axis=-1)


def _rope_table(n_tokens, n_ctx, dim):
    f32 = np.float32
    rows = n_tokens // GRID_W
    rowp = np.broadcast_to(np.arange(rows, dtype=f32)[:, None], (rows, GRID_W)).reshape(-1)
    colp = np.broadcast_to(np.arange(GRID_W, dtype=f32)[None, :], (rows, GRID_W)).reshape(-1)
    n_freq = dim // 4
    inv = np.power(f32(ROPE_BASE), -np.arange(n_freq, dtype=f32) / f32(n_freq)).astype(f32)
    ar = rowp[:, None] * inv
    ac = colp[:, None] * inv
    ang = np.concatenate([ar, ar, ac, ac], -1).astype(f32)
    cos, sin = np.cos(ang), np.sin(ang)
    odd = (np.arange(dim) // n_freq) % 2 == 1
    sin_p = np.where(odd, sin, f32(0))
    sin_m = np.where(odd, f32(0), -sin)
    tab = np.stack([cos, sin_p, sin_m])
    ident = np.stack([np.ones((n_ctx, dim), f32), np.zeros((n_ctx, dim), f32), np.zeros((n_ctx, dim), f32)])
    tab = np.concatenate([ident, tab], axis=1)
    return jnp.asarray(np.tile(tab, (1, 1, LANES // dim)).astype(f32))


def kernel(x, c, ctx, c_ctx, w_mod, b_mod, w_in, attn_sink, diff_lambda, diff_norm_g, mlstm_i_bias,
           mlstm_f_bias, mlstm_norm_g, w_out, ln_g, ln_b):
    b, t, d = x.shape
    n_ctx = ctx.shape[1]
    depth = w_mod.shape[0]
    d_in = w_in.shape[2]
    alpha = (2 * depth) ** 0.25

    xa = (ctx, x)
    rope_a = _rope_table(t, n_ctx, HEAD_DIM)
    rope_b = _rope_table(t, n_ctx, B_QK_DIM)

    rows = -(-(b + 1) // 8) * 8
    cc = jnp.concatenate([c, c_ctx[None, :], jnp.zeros((rows - b - 1, d), F32)], axis=0)
    mod = _modulation(cc, w_mod, b_mod).reshape(depth, rows, 3, d)

    w_in_p = _permute_columns(w_in.astype(BF16), _in_weight_columns(d_in))
    w_out_b = w_out.astype(BF16)
    norm_b = diff_norm_g[:, :, None]

    n_pairs = mlstm_i_bias.shape[2] // 2
    heads = mlstm_i_bias.shape[2]
    chains = [(di, a) for di in range(2) for a in range(2)]
    bias_idx = np.array([[di * heads + 2 * p + a for di, a in chains] for p in range(n_pairs)])
    pad8 = lambda v: jnp.pad(v.reshape(-1)[bias_idx], ((0, 0), (0, GATE_PAD - 4)))

    for l in range(depth):
        with_ctx = l < depth - 1
        lam_init = 0.8 - 0.6 * math.exp(-0.3 * l)
        modsel = jnp.stack([jnp.broadcast_to(mod[l, b], (b, 3, d)), mod[l, :b]], axis=1)
        (qa, ka, va, qb, kb, vb, qm, km, vm, og, g, gic, gfc, gir, gfr) = _in_projection(
            xa, modsel, w_in_p, l, rope_a, rope_b, n_ctx)

        ya = _attn_a(attn_sink[l], qa, ka, va, n_ctx)
        yb, yb_ctx = _attn_b(jnp.full((1,), lam_init, F32), diff_lambda[l], norm_b[l], qb, kb, vb, n_ctx, with_ctx)

        ym = _mlstm(pad8(mlstm_i_bias[l]), pad8(mlstm_f_bias[l]), mlstm_norm_g[l].reshape(n_pairs, LANES),
                    qm, km, vm, gic, gfc, gir, gfr, og, n_ctx)

        xa = _out_projection(xa, modsel, ya, yb, yb_ctx, ym, g, w_out_b, l, ln_g[l][None, :], ln_b[l][None, :],
                             alpha, n_ctx // ROW_TILE)
    return xa
```

```python
import functools
import math

import numpy as np
import jax
import jax.numpy as jnp
from jax import lax
from jax.experimental import pallas as pl
from jax.experimental.pallas import tpu as pltpu

F32 = jnp.float32
BF16 = jnp.bfloat16

LANES = 128
GRID_W = 64
HEAD_DIM = 64
ROPE_BASE = 10000.0
LN_EPS = 1e-5
WINDOW = 128
B_QK_DIM = 32
LOG2E = math.log2(math.e)
VB_ROWS = HEAD_DIM + 16
MLSTM_CHUNK = 128
GATE_PAD = 8
MOD_COL_TILE = 1024
B_BOUND_CAP = 32.0
B_BOUND_MARGIN = 1.02
B_KEY_GROUP_BOUNDED = 4
B_KEY_GROUP = 2
ROW_TILE = 256
VMEM_LIMIT = 56 * 1024 * 1024


def _dot(a, b):
    return jnp.dot(a, b, preferred_element_type=F32)


def _dot_nt(a, b):
    return lax.dot_general(a, b, (((1,), (1,)), ((), ())), preferred_element_type=F32)


def _split_bf16(a):
    hi = a.astype(BF16)
    lo = (a - hi.astype(F32)).astype(BF16)
    return hi, lo


def _sigmoid(x):
    return 1.0 / (1.0 + jnp.exp(-x))


def _params(sem):
    return pltpu.CompilerParams(dimension_semantics=sem, vmem_limit_bytes=VMEM_LIMIT)


def _mod_kernel(c_ref, w_ref, b_ref, o_ref):
    c = c_ref[...]
    a = c * _sigmoid(c)
    a_hi, a_lo = _split_bf16(a)
    w_hi, w_lo = _split_bf16(w_ref[0])
    o_ref[0] = _dot(a_hi, w_hi) + _dot(a_lo, w_hi) + _dot(a_hi, w_lo) + b_ref[0]


def _modulation(cc, w_mod, b_mod):
    depth, d, n = w_mod.shape
    r = cc.shape[0]
    tn = MOD_COL_TILE
    return pl.pallas_call(
        _mod_kernel,
        grid=(depth, n // tn),
        in_specs=[
            pl.BlockSpec((r, d), lambda l, j: (0, 0)),
            pl.BlockSpec((1, d, tn), lambda l, j: (l, 0, j)),
            pl.BlockSpec((1, 1, tn), lambda l, j: (l, 0, j)),
        ],
        out_specs=pl.BlockSpec((1, r, tn), lambda l, j: (l, 0, j)),
        out_shape=jax.ShapeDtypeStruct((depth, r, n), F32),
        compiler_params=_params(("arbitrary", "arbitrary")),
        name="modulation",
    )(cc, w_mod, b_mod.reshape(depth, 1, n))


def _stream_shape(xa):
    if isinstance(xa, tuple):
        ctx, x = xa
        return x.shape[0], ctx.shape[1] + x.shape[1], x.shape[2]
    return xa.shape


def _stream_specs(xa, tm, row0):
    if not isinstance(xa, tuple):
        return [pl.BlockSpec((1, tm, xa.shape[2]), lambda bi, i: (bi, i + row0, 0))], [xa]
    ctx, x = xa
    nct = ctx.shape[1] // tm
    return ([pl.BlockSpec((1, tm, ctx.shape[2]), lambda bi, i: (bi, jnp.minimum(i + row0, nct - 1), 0)),
             pl.BlockSpec((1, tm, x.shape[2]), lambda bi, i: (bi, jnp.maximum(i + row0 - nct, 0), 0))], [ctx, x])


def _stream_tile(refs, n_ctx_tiles, row0):
    if len(refs) == 1:
        return refs[0][0]
    return jnp.where(pl.program_id(1) + row0 < n_ctx_tiles, refs[0][0], refs[1][0])


_QA, _KA, _VA = (0, 512), (512, 640), (640, 768)
_QB, _KB, _VB = (768, 1152), (1152, 1536), (1536, 1920)
_QM, _KM, _VM = (1920, 2304), (2304, 2688), (2688, 3072)
_OM, _Z, _GM = (3072, 3456), (3456, 4480), (4480, 4608)
_W_COLS = 4608
_W_GROUPS = ((0, 768), (768, 1536), (1536, 2304), (2304, 3072), (3072, 4608))


def _rope(t, tab_ref, quarter):
    cos, sin_p, sin_m = tab_ref[0], tab_ref[1], tab_ref[2]
    outs = []
    for j in range(t.shape[1] // LANES):
        tj = t[:, j * LANES:(j + 1) * LANES]
        outs.append(tj * cos + pltpu.roll(tj, quarter, 1) * sin_p + pltpu.roll(tj, LANES - quarter, 1) * sin_m)
    return outs[0] if len(outs) == 1 else jnp.concatenate(outs, axis=1)


def _inproj_kernel(*refs, n_stream, n_ctx_tiles):
    (mod_ref, w_ref, ra_ref, rb_ref, qa_ref, ka_ref, va_ref, qb_ref, kb_ref, vb_ref,
     qm_ref, km_ref, vm_ref, om_ref, g_ref, gic_ref, gfc_ref, gir_ref, gfr_ref) = refs[n_stream:]
    x = _stream_tile(refs[:n_stream], n_ctx_tiles, 0)
    shift = mod_ref[0, 0, 0:1, :]
    scale = mod_ref[0, 0, 1:2, :]
    h = (x * (1.0 + scale) + shift).astype(BF16)

    group_dots = {}

    def proj(cols):
        lo, hi = next(g for g in _W_GROUPS if g[0] <= cols[0] and cols[1] <= g[1])
        if lo not in group_dots:
            group_dots[lo] = _dot(h, w_ref[:, lo:hi])
        return group_dots[lo][:, cols[0] - lo:cols[1] - lo]

    qa_ref[0] = (_rope(proj(_QA), ra_ref, HEAD_DIM // 4) * (HEAD_DIM ** -0.5)).astype(BF16)
    ka_ref[0] = _rope(proj(_KA), ra_ref, HEAD_DIM // 4).astype(BF16)
    va_ref[0] = proj(_VA).astype(BF16)
    qb_t = (_rope(proj(_QB), rb_ref, B_QK_DIM // 4) * (B_QK_DIM ** -0.5 * LOG2E)).T.astype(BF16)
    for p in range(qb_t.shape[0] // LANES):
        qb_ref[0, p, 0] = qb_t[p * LANES:(p + 1) * LANES]
    vb_t = proj(_VB).T
    ones_rows = jnp.where(lax.broadcasted_iota(jnp.int32, (VB_ROWS - HEAD_DIM, vb_t.shape[1]), 0) == 0, 1.0, 0.0)
    for hd in range(vb_t.shape[0] // HEAD_DIM):
        vb_ref[0, hd, 0] = jnp.concatenate([vb_t[hd * HEAD_DIM:(hd + 1) * HEAD_DIM], ones_rows], axis=0).astype(BF16)
    kb_ref[0] = _rope(proj(_KB), rb_ref, B_QK_DIM // 4).astype(BF16)
    km_ref[0] = (proj(_KM) * (HEAD_DIM ** -0.5)).astype(BF16)
    qm_t = proj(_QM).T.astype(BF16)
    om_t = _sigmoid(proj(_OM)).T.astype(BF16)
    vm_t = proj(_VM).T
    L = MLSTM_CHUNK
    ones_rows_c = jnp.where(lax.broadcasted_iota(jnp.int32, (VB_ROWS - HEAD_DIM, L), 0) == 0, 1.0, 0.0)
    for c in range(qm_t.shape[1] // L):
        cols = slice(c * L, (c + 1) * L)
        for p in range(qm_t.shape[0] // LANES):
            qm_ref[0, p, c] = qm_t[p * LANES:(p + 1) * LANES, cols]
            om_ref[0, p, c] = om_t[p * LANES:(p + 1) * LANES, cols]
        for hd in range(vm_t.shape[0] // HEAD_DIM):
            vm_ref[0, hd, c] = jnp.concatenate(
                [vm_t[hd * HEAD_DIM:(hd + 1) * HEAD_DIM, cols], ones_rows_c], axis=0).astype(BF16)
    z = proj(_Z)
    g_ref[0] = (z * _sigmoid(z)).astype(BF16)
    gm = proj(_GM)
    gm_t = gm.T
    for p in range(gic_ref.shape[1]):
        lo, hi = 2 * GATE_PAD * p, 2 * GATE_PAD * p + GATE_PAD
        gic_ref[0, p] = gm[:, lo:hi]
        gfc_ref[0, p] = gm[:, hi:hi + GATE_PAD]
        for c in range(gm_t.shape[1] // L):
            gir_ref[0, p, c] = gm_t[lo:hi, c * L:(c + 1) * L]
            gfr_ref[0, p, c] = gm_t[hi:hi + GATE_PAD, c * L:(c + 1) * L]


def _in_projection(xa, modsel, w, layer, rope_a, rope_b, n_ctx):
    b, s, d = _stream_shape(xa)
    tm = ROW_TILE
    x_specs, x_args = _stream_specs(xa, tm, 0)
    n_pairs = (_QB[1] - _QB[0]) // LANES
    n_heads = (_VB[1] - _VB[0]) // HEAD_DIM
    L = MLSTM_CHUNK
    widths = [512, 128, 128, (n_pairs, LANES, tm), 384, (n_heads, VB_ROWS, tm),
              (n_pairs, LANES, L), 384, (n_heads, VB_ROWS, L), (n_pairs, LANES, L), 1024]
    tposed = lambda g, r, tt: pl.BlockSpec((1, g, tm // tt, r, tt), lambda bi, i: (bi, 0, i, 0, 0))
    row = lambda n: tposed(*n) if isinstance(n, tuple) else pl.BlockSpec((1, tm, n), lambda bi, i: (bi, i, 0))
    out_shape = [jax.ShapeDtypeStruct((b, n[0], s // n[2], n[1], n[2]) if isinstance(n, tuple) else (b, s, n), BF16)
                 for n in widths]
    gate_c = pl.BlockSpec((1, n_pairs, tm, GATE_PAD), lambda bi, i: (bi, 0, i, 0))
    gate_r = pl.BlockSpec((1, n_pairs, tm // L, GATE_PAD, L), lambda bi, i: (bi, 0, i, 0, 0))
    out_shape += [jax.ShapeDtypeStruct((b, n_pairs, s, GATE_PAD), F32)] * 2
    out_shape += [jax.ShapeDtypeStruct((b, n_pairs, s // L, GATE_PAD, L), F32)] * 2
    return pl.pallas_call(
        functools.partial(_inproj_kernel, n_stream=len(x_args), n_ctx_tiles=n_ctx // tm),
        grid=(b, s // tm),
        in_specs=x_specs + [
            pl.BlockSpec((1, 1, 3, d), lambda bi, i: (bi, jnp.minimum(i // (n_ctx // tm), 1), 0, 0)),
            pl.BlockSpec((None, d, _W_COLS), lambda bi, i: (layer, 0, 0)),
            pl.BlockSpec((3, tm, LANES), lambda bi, i: (0, i, 0)),
            pl.BlockSpec((3, tm, LANES), lambda bi, i: (0, i, 0)),
        ],
        out_specs=[row(n) for n in widths] + [gate_c, gate_c, gate_r, gate_r],
        out_shape=out_shape,
        compiler_params=_params(("arbitrary", "arbitrary")),
        name="in_projection",
    )(*x_args, modsel, w, rope_a, rope_b)


def _attn_a_kernel(sink_ref, q_ref, k_ref, v_ref, o_ref, *, n_ctx, n_blocks):
    w = WINDOW
    s_len = n_blocks * w
    half = LANES // 2
    rows1 = lax.broadcasted_iota(jnp.int32, (4 * w, 1), 0)
    sink = jnp.where(rows1 < w, sink_ref[0],
                     jnp.where(rows1 < 2 * w, sink_ref[1], jnp.where(rows1 < 3 * w, sink_ref[2], sink_ref[3])))
    row = lax.broadcasted_iota(jnp.int32, (4 * w, 3 * w), 0) & (w - 1)
    col = lax.broadcasted_iota(jnp.int32, (4 * w, 3 * w), 1)
    lane = lax.broadcasted_iota(jnp.int32, (w, LANES), 1)
    k_ctx = k_ref[0, 0:n_ctx, :]
    v_ctx = v_ref[0, 0:n_ctx, :]

    def body(i, carry):
        r0 = pl.multiple_of(i * w, w)
        q = q_ref[0, pl.ds(r0, w), :]
        qs = jnp.concatenate([q[:, j * LANES:(j + 1) * LANES] for j in range(4)], axis=0)
        start = pl.multiple_of(jnp.clip(r0 - w, 0, s_len - 3 * w), w)
        s_loc = _dot_nt(qs, k_ref[0, pl.ds(start, 3 * w), :])
        s_ctx = _dot_nt(qs, k_ctx)
        kpos = col + start
        rel = kpos - r0 - row
        ok = (jnp.abs(rel) <= w) & (kpos >= jnp.where(r0 >= n_ctx, n_ctx, s_len))
        s_loc = jnp.where(ok, s_loc, -jnp.inf)
        m = jnp.maximum(jnp.maximum(jnp.max(s_loc, axis=1, keepdims=True),
                                    jnp.max(s_ctx, axis=1, keepdims=True)), sink)
        p_loc = jnp.exp(s_loc - m)
        p_ctx = jnp.exp(s_ctx - m)
        den = (jnp.sum(p_loc, axis=1, keepdims=True) + jnp.sum(p_ctx, axis=1, keepdims=True)
               + jnp.exp(sink - m))
        o = (_dot(p_loc.astype(BF16), v_ref[0, pl.ds(start, 3 * w), :])
             + _dot(p_ctx.astype(BF16), v_ctx)) / den
        left = jnp.where(lane < half, o[0:w], pltpu.roll(o[w:2 * w], half, 1))
        right = jnp.where(lane < half, pltpu.roll(o[2 * w:3 * w], half, 1), o[3 * w:])
        o_ref[0, pl.ds(r0, w), :] = jnp.concatenate([left, right], axis=1).astype(o_ref.dtype)
        return carry

    lax.fori_loop(0, n_blocks, body, 0, unroll=2)


def _attn_a(sink, qa, ka, va, n_ctx):
    b, s, _ = qa.shape
    seq = lambda n: pl.BlockSpec((1, s, n), lambda bi: (bi, 0, 0))
    return pl.pallas_call(
        functools.partial(_attn_a_kernel, n_ctx=n_ctx, n_blocks=s // WINDOW),
        grid=(b,),
        in_specs=[pl.BlockSpec(memory_space=pltpu.SMEM), seq(4 * LANES), seq(LANES), seq(LANES)],
        out_specs=seq(2 * LANES),
        out_shape=jax.ShapeDtypeStruct((b, s, 2 * LANES), BF16),
        compiler_params=_params(("arbitrary",)),
        name="window_attention",
    )(sink, qa, ka, va)


def _attn_b_kernel(li_ref, lam_ref, ng_ref, qt_ref, k_ref, vt_ref, o_ref,
                   qs_scr, sa_scr, sb_scr, sc_scr, ea_scr, eb_scr, ec_scr, acc_scr, *, nt, q0, nq, kg, kgb):
    tq, tk = qt_ref.shape[-1], vt_ref.shape[-1]
    hv = HEAD_DIM
    q_cols = 4 * tq
    lam_init = li_ref[0]
    lv = lam_ref[...]
    lam = (jnp.exp(jnp.sum(lv[0:1] * lv[1:2], axis=1, keepdims=True))
           - jnp.exp(jnp.sum(lv[2:3] * lv[3:4], axis=1, keepdims=True)) + lam_init)

    u_r = lax.broadcasted_iota(jnp.int32, (LANES, LANES), 0) // B_QK_DIM
    u_c = lax.broadcasted_iota(jnp.int32, (LANES, LANES), 1) // B_QK_DIM
    unit_ones = jnp.where(u_r == u_c, 1.0, 0.0).astype(BF16)

    def key_norms(t, best):
        kt = k_ref[0, pl.ds(pl.multiple_of(t * tk, tk), tk), :].astype(F32)
        return jnp.maximum(best, jnp.max(_dot((kt * kt).astype(BF16), unit_ones), axis=0, keepdims=True))
    kmax2 = lax.fori_loop(0, nt, key_norms, jnp.zeros((1, LANES), F32))

    def scores(t, n, z):
        off = pl.multiple_of(t * tk, tk)
        return _dot(k_ref[0, pl.ds(off, n * tk), :], qs_scr[z])

    def values(p, t, n, hd):
        cols = slice(hd * 2 * tq, (hd + 1) * 2 * tq)
        pv = _dot(vt_ref[0, hd, t], p[0:tk, cols])
        for c in range(1, n):
            pv = pv + _dot(vt_ref[0, hd, t + c], p[c * tk:(c + 1) * tk, cols])
        return pv

    def finish(i, z):
        def head_out(hd):
            acc = acc_scr[z, hd]
            o = acc[0:hv] * (1.0 / acc[hv:hv + 1])
            od = o[:, 0:tq] - lam * o[:, tq:]
            ms = jnp.mean(od * od, axis=0, keepdims=True)
            return od * lax.rsqrt(ms + LN_EPS) * ng_ref[...]
        y_t = jnp.concatenate([head_out(0), head_out(1)], axis=0) * (1.0 - lam_init)
        o_ref[0, pl.ds(pl.multiple_of(i * tq, tq), tq), :] = y_t.T.astype(o_ref.dtype)

    def run_pipeline(fetch, consume, bufs, g, state):
        buf_a, buf_b, buf_c = bufs
        n_groups = (nt - 1) // g
        first = lambda j: 1 + g * j

        def body(tt, carry):
            state, tag_a = carry
            j0 = 2 * tt
            tag_b = fetch(first(j0 + 1), buf_b, g)
            state = consume(buf_a, tag_a, state, first(j0), g)
            tag_a = fetch(first(j0 + 2), buf_a, g)
            state = consume(buf_b, tag_b, state, first(j0 + 1), g)
            return state, tag_a

        tag_c = fetch(0, buf_c, 1)
        if n_groups == 0:
            return consume(buf_c, tag_c, state, 0, 1)
        tag_a = fetch(first(0), buf_a, g)
        state = consume(buf_c, tag_c, state, 0, 1)
        state, tag_a = lax.fori_loop(0, (n_groups - 1) // 2, body, (state, tag_a))
        if n_groups % 2 == 0:
            tag_b = fetch(first(n_groups - 1), buf_b, g)
            state = consume(buf_a, tag_a, state, first(n_groups - 2), g)
            return consume(buf_b, tag_b, state, first(n_groups - 1), g)
        return consume(buf_a, tag_a, state, first(n_groups - 1), g)

    def bounded(i, bound, z):
        def fetch(t, e_buf, n):
            e_buf[z] = jnp.exp2(scores(t, n, z) - bound).astype(BF16)
            return bound

        def consume(e_buf, tag, state, t, n):
            for hd in range(2):
                acc_scr[z, hd] = acc_scr[z, hd] + values(e_buf[z], t, n, hd)
            return state

        run_pipeline(fetch, consume, (ea_scr, eb_scr, ec_scr), kgb, bound)
        finish(i, z)

    def online(i):
        def fetch(t, s_scr, n):
            s_new = scores(t, n, 0)
            s_scr[...] = s_new
            return jnp.max(s_new, axis=0, keepdims=True)

        def consume(s_scr, mt, m, t, n):
            m_new = jnp.maximum(m, mt)
            alpha = jnp.exp2(m - m_new)
            p = jnp.exp2(s_scr[...] - m_new).astype(BF16)
            for hd in range(2):
                cols = slice(hd * 2 * tq, (hd + 1) * 2 * tq)
                acc_scr[0, hd] = alpha[:, cols] * acc_scr[0, hd] + values(p, t, n, hd)
            return m_new

        run_pipeline(fetch, consume, (sa_scr, sb_scr, sc_scr), kg, jnp.full((1, q_cols), -jnp.inf, F32))
        finish(i, 0)

    def prepare(i, z):
        qt = qt_ref[0, 0, q0 + i]
        unit = lax.broadcasted_iota(jnp.int32, qt.shape, 0) // B_QK_DIM
        zero = jnp.zeros_like(qt)
        qs_scr[z] = jnp.concatenate([jnp.where(unit == u, qt, zero) for u in range(4)], axis=1)
        acc_scr[z] = jnp.zeros(acc_scr.shape[1:], F32)
        qf = qt.astype(F32)
        qn2 = jnp.sum((qf * qf).reshape(4, B_QK_DIM, tq), axis=1)
        bound = jnp.concatenate(
            [jnp.sqrt(qn2[u:u + 1] * kmax2[:, u * B_QK_DIM:u * B_QK_DIM + 1]) for u in range(4)], axis=1)
        return bound * B_BOUND_MARGIN

    def one_tile(i):
        bound = prepare(i, 0)
        small = jnp.max(bound) <= B_BOUND_CAP
        pl.when(small)(lambda: bounded(i, bound, 0))
        pl.when(jnp.logical_not(small))(lambda: online(i))

    def tile_pair(j, carry):
        i0, i1 = 2 * j, 2 * j + 1
        b0, b1 = prepare(i0, 0), prepare(i1, 1)
        both = jnp.maximum(jnp.max(b0), jnp.max(b1)) <= B_BOUND_CAP

        def fast():
            bounded(i0, b0, 0)
            bounded(i1, b1, 1)

        def slow():
            one_tile(i0)
            one_tile(i1)

        pl.when(both)(fast)
        pl.when(jnp.logical_not(both))(slow)
        return carry

    lax.fori_loop(0, nq // 2, tile_pair, 0)
    if nq % 2:
        one_tile(nq - 1)


def _attn_b_call(lam_init, lam_vec, norm_g, qbt, kb, vbt, q0, nq, nt):
    b, pairs, n_tiles, _, tq = qbt.shape
    vrows, tk = vbt.shape[-2:]
    kg = B_KEY_GROUP if (nt - 1) % B_KEY_GROUP == 0 else 1
    kgb = B_KEY_GROUP_BOUNDED if (nt - 1) % B_KEY_GROUP_BOUNDED == 0 else 1
    score_bufs = lambda dt, g, *z: [pltpu.VMEM(z + (g * tk, 4 * tq), dt), pltpu.VMEM(z + (g * tk, 4 * tq), dt),
                                    pltpu.VMEM(z + (tk, 4 * tq), dt)]
    return pl.pallas_call(
        functools.partial(_attn_b_kernel, nt=nt, q0=q0, nq=nq, kg=kg, kgb=kgb),
        grid=(b, pairs),
        in_specs=[
            pl.BlockSpec(memory_space=pltpu.SMEM),
            pl.BlockSpec((4, B_QK_DIM), lambda bi, p: (0, 0)),
            pl.BlockSpec((HEAD_DIM, 1), lambda bi, p: (0, 0)),
            pl.BlockSpec((1, 1, n_tiles, LANES, tq), lambda bi, p: (bi, p, 0, 0, 0)),
            pl.BlockSpec((1, nt * tk, LANES), lambda bi, p: (bi, 0, p)),
            pl.BlockSpec((1, 2, nt, vrows, tk), lambda bi, p: (bi, p, 0, 0, 0)),
        ],
        out_specs=pl.BlockSpec((1, nq * tq, LANES), lambda bi, p: (bi, 0, p)),
        out_shape=jax.ShapeDtypeStruct((b, nq * tq, pairs * LANES), BF16),
        scratch_shapes=[pltpu.VMEM((2, LANES, 4 * tq), BF16)] + score_bufs(F32, kg) + score_bufs(BF16, kgb, 2)
                       + [pltpu.VMEM((2, 2, vrows, 2 * tq), F32)],
        compiler_params=_params(("arbitrary", "arbitrary")),
        name="diff_attention",
    )(lam_init, lam_vec, norm_g, qbt, kb, vbt)


def _attn_b(lam_init, lam_vec, norm_g, qbt, kb, vbt, n_ctx, with_ctx):
    n_tiles, tq = qbt.shape[2], qbt.shape[4]
    tk = vbt.shape[-1]
    n_ctx_tiles = n_ctx // tq
    y_lat = _attn_b_call(lam_init, lam_vec, norm_g, qbt, kb, vbt, n_ctx_tiles, n_tiles - n_ctx_tiles, n_tiles)
    y_ctx = _attn_b_call(lam_init, lam_vec, norm_g, qbt, kb, vbt, 0, n_ctx_tiles, n_ctx // tk) if with_ctx else None
    return y_lat, y_ctx


def _log_sigmoid(x):
    return jnp.minimum(x, 0.0) - jnp.log(1.0 + jnp.exp(-jnp.abs(x)))


def _mlstm_kernel(bir_ref, bic_ref, bfr_ref, bfc_ref, ng_ref, qt_ref, k_ref, vt_ref, gic_ref, gfc_ref, gir_ref,
                  gfr_ref, ogt_ref,
                  o_ref, hf_scr, hb_scr, st_scr, ir_scr, br_scr, gc_scr, *, n_ctx_chunks, n_chunks):
    L = MLSTM_CHUNK
    hv = HEAD_DIM
    r_i = lax.broadcasted_iota(jnp.int32, (L, L), 0)
    c_i = lax.broadcasted_iota(jnp.int32, (L, L), 1)
    upper = r_i <= c_i
    lower = r_i >= c_i
    t_up = jnp.where(upper, 1.0, 0.0).astype(BF16)
    t_low = jnp.where(lower, 1.0, 0.0).astype(BF16)
    row_q = lax.broadcasted_iota(jnp.int32, (LANES, L), 0)
    sel = [jnp.where(row_q < hv, 1.0, 0.0).astype(BF16), jnp.where(row_q < hv, 0.0, 1.0).astype(BF16)]
    fwd_r = lax.broadcasted_iota(jnp.int32, (GATE_PAD, L), 0) < 2
    fwd_c = lax.broadcasted_iota(jnp.int32, (L, GATE_PAD), 1) < 2
    st_scr[...] = jnp.zeros_like(st_scr)

    def gates(c, carry):
        r0 = pl.multiple_of(c * L, L)
        frow = _log_sigmoid(gfr_ref[0, 0, c] + bfr_ref[0])
        hi, lo = _split_bf16(frow)
        ir_scr[c] = gir_ref[0, 0, c] + bir_ref[0]
        br_scr[c] = jnp.where(fwd_r, _dot(hi, t_up) + _dot(lo, t_up), _dot(hi, t_low) + _dot(lo, t_low))
        fcol = _log_sigmoid(gfc_ref[0, 0, pl.ds(r0, L), :] + bfc_ref[0])
        hi, lo = _split_bf16(fcol)
        bcol = jnp.where(fwd_c, _dot(t_low, hi) + _dot(t_low, lo), _dot(t_up, hi) + _dot(t_up, lo))
        gc_scr[c] = gic_ref[0, 0, pl.ds(r0, L), :] + bic_ref[0] - bcol
        return carry
    lax.fori_loop(0, n_chunks, gates, 0, unroll=4)

    def chunk_step(direction, c, m_prev):
        r0 = pl.multiple_of(c * L, L)
        kc = k_ref[0, pl.ds(r0, L), :]
        qtc = qt_ref[0, 0, c]
        irow8, brow8, gcol8 = ir_scr[c], br_scr[c], gc_scr[c]
        tri = upper if direction == 0 else lower
        h_scr = hf_scr if direction == 0 else hb_scr
        qtm = [qtc * sel[a] for a in range(2)]
        s_both = _dot(kc, jnp.concatenate(qtm, axis=1))
        m_news, vws, decays = [], [], []
        for a in range(2):
            ci = 2 * direction + a
            b_row, i_row, g_col = brow8[ci:ci + 1], irow8[ci:ci + 1], gcol8[:, ci:ci + 1]
            log_d = jnp.where(tri, g_col + b_row, -jnp.inf)
            m_in = b_row + m_prev[a]
            m_t = jnp.maximum(m_in, jnp.max(log_d, axis=0, keepdims=True))
            w_in = jnp.exp(m_in - m_t)
            sd = (s_both[:, a * L:(a + 1) * L] * jnp.exp(log_d - m_t)).astype(BF16)
            qw = (qtm[a].astype(F32) * w_in).astype(BF16)
            vt = vt_ref[0, a, c]
            numden = _dot(jnp.concatenate([st_scr[ci].astype(BF16), vt], axis=1),
                          jnp.concatenate([qw, sd], axis=0))
            h_scr[c, a * hv:(a + 1) * hv, :] = (
                numden[0:hv] / jnp.maximum(jnp.abs(numden[hv:hv + 1]), jnp.exp(-m_t)))
            b_last = jnp.min(b_row, axis=1, keepdims=True)
            log_w = b_last - b_row + i_row
            m_new = jnp.maximum(b_last + m_prev[a], jnp.max(log_w, axis=1, keepdims=True))
            vws.append((vt.astype(F32) * jnp.exp(log_w - m_new)).astype(BF16))
            decays.append(jnp.exp(b_last + m_prev[a] - m_new))
            m_news.append(m_new)
        rows = vt_ref.shape[3]
        upd = _dot(jnp.concatenate(vws, axis=0), kc)
        for a in range(2):
            ci = 2 * direction + a
            st_scr[ci] = decays[a] * st_scr[ci] + upd[a * rows:(a + 1) * rows]
        return tuple(m_news)

    def body(j, carry):
        mf, mb = carry
        cb = jnp.where(j < n_ctx_chunks, n_ctx_chunks - 1 - j, n_chunks - 1 + n_ctx_chunks - j)
        return chunk_step(0, j, mf), chunk_step(1, cb, mb)

    z11 = jnp.zeros((1, 1), F32)
    lax.fori_loop(0, n_chunks, body, ((z11, z11), (z11, z11)), unroll=4)

    def finish(c, carry):
        r0 = pl.multiple_of(c * L, L)
        h = hf_scr[c] + hb_scr[c]
        outs = []
        for a in range(2):
            ha = h[a * hv:(a + 1) * hv]
            outs.append(ha * lax.rsqrt(jnp.mean(ha * ha, axis=0, keepdims=True) + LN_EPS))
        y_t = jnp.concatenate(outs, axis=0) * ng_ref[0] * ogt_ref[0, 0, c].astype(F32)
        o_ref[0, pl.ds(r0, L), :] = y_t.T.astype(o_ref.dtype)
        return carry
    lax.fori_loop(0, n_chunks, finish, 0, unroll=4)


def _mlstm(bias_i, bias_f, norm_g, qmt, km, vmt, gic, gfc, gir, gfr, ogt, n_ctx):
    b, pairs, nc, _, L = qmt.shape
    s = nc * L
    vrows = vmt.shape[3]
    seq = pl.BlockSpec((1, s, LANES), lambda bi, p: (bi, 0, p))
    pair_tiles = pl.BlockSpec((1, 1, nc, LANES, L), lambda bi, p: (bi, p, 0, 0, 0))
    gate_c = pl.BlockSpec((1, 1, s, GATE_PAD), lambda bi, p: (bi, p, 0, 0))
    gate_r = pl.BlockSpec((1, 1, nc, GATE_PAD, L), lambda bi, p: (bi, p, 0, 0, 0))
    bias_r = pl.BlockSpec((1, GATE_PAD, 1), lambda bi, p: (p, 0, 0))
    bias_c = pl.BlockSpec((1, 1, GATE_PAD), lambda bi, p: (p, 0, 0))
    return pl.pallas_call(
        functools.partial(_mlstm_kernel, n_ctx_chunks=n_ctx // L, n_chunks=nc),
        grid=(b, pairs),
        in_specs=[
            bias_r, bias_c, bias_r, bias_c,
            pl.BlockSpec((1, LANES, 1), lambda bi, p: (p, 0, 0)),
            pair_tiles, seq,
            pl.BlockSpec((1, 2, nc, vrows, L), lambda bi, p: (bi, p, 0, 0, 0)),
            gate_c, gate_c, gate_r, gate_r,
            pair_tiles,
        ],
        out_specs=seq,
        out_shape=jax.ShapeDtypeStruct((b, s, pairs * LANES), BF16),
        scratch_shapes=[pltpu.VMEM((nc, LANES, L), F32), pltpu.VMEM((nc, LANES, L), F32),
                        pltpu.VMEM((4, vrows, LANES), F32),
                        pltpu.VMEM((nc, GATE_PAD, L), F32), pltpu.VMEM((nc, GATE_PAD, L), F32),
                        pltpu.VMEM((nc, L, GATE_PAD), F32)],
        compiler_params=_params(("arbitrary", "arbitrary")),
        name="mlstm",
    )(bias_i[:, :, None], bias_i[:, None, :], bias_f[:, :, None], bias_f[:, None, :], norm_g[:, :, None],
      qmt, km, vmt, gic, gfc, gir, gfr, ogt)


def _out_kernel(*refs, alpha, n_ctx_tiles, n_stream, row0):
    x = _stream_tile(refs[:n_stream], n_ctx_tiles, row0)
    mod_ref, ya_ref, yb_ref = refs[n_stream:n_stream + 3]
    refs = refs[n_stream + 3:]
    ym_ref, g_ref, w_ref, lng_ref, lnb_ref, o_ref = refs[-6:]
    gate = mod_ref[0, 0, 2:3, :]
    g = g_ref[0]
    na, nb_ = ya_ref.shape[2], yb_ref.shape[2]
    yb = yb_ref[0]
    if len(refs) == 7:
        yb = jnp.where(pl.program_id(1) < n_ctx_tiles, refs[0][0], yb)
    mix_a = ya_ref[0] * g[:, :na]
    mix_b = yb * g[:, na:na + nb_]
    mix_m = ym_ref[0] * g[:, na + nb_:]
    y = _dot(jnp.concatenate([mix_a, mix_b, mix_m], axis=1), w_ref[...])
    r = alpha * x + gate * y
    mu = jnp.mean(r, axis=1, keepdims=True)
    d = r - mu
    var = jnp.mean(d * d, axis=1, keepdims=True)
    o_ref[0] = d * lax.rsqrt(var + LN_EPS) * lng_ref[...] + lnb_ref[...]


def _out_projection(xa, modsel, ya, yb_lat, yb_ctx, ym, g, w, layer, ln_g, ln_b, alpha, n_ctx_tiles):
    b, s, d = _stream_shape(xa)
    tm = ROW_TILE
    row0 = 0 if yb_ctx is not None else n_ctx_tiles
    row = lambda n: pl.BlockSpec((1, tm, n), lambda bi, i: (bi, i + row0, 0))
    vec = pl.BlockSpec((1, d), lambda bi, i: (0, 0))
    nb_ = yb_lat.shape[2]
    x_specs, x_args = _stream_specs(xa, tm, row0)
    in_specs = x_specs + [
        pl.BlockSpec((1, 1, 3, d), lambda bi, i: (bi, jnp.minimum((i + row0) // n_ctx_tiles, 1), 0, 0)),
        row(ya.shape[2]),
        pl.BlockSpec((1, tm, nb_), lambda bi, i: (bi, jnp.maximum(i + row0 - n_ctx_tiles, 0), 0)),
    ]
    args = x_args + [modsel, ya, yb_lat]
    if yb_ctx is not None:
        in_specs.append(pl.BlockSpec((1, tm, nb_), lambda bi, i: (bi, jnp.minimum(i, n_ctx_tiles - 1), 0)))
        args.append(yb_ctx)
    in_specs += [row(ym.shape[2]), row(d), pl.BlockSpec((None,) + w.shape[1:], lambda bi, i: (layer, 0, 0)),
                 vec, vec]
    args += [ym, g, w, ln_g, ln_b]
    return pl.pallas_call(
        functools.partial(_out_kernel, alpha=alpha, n_ctx_tiles=n_ctx_tiles, n_stream=len(x_args), row0=row0),
        grid=(b, s // tm - row0),
        in_specs=in_specs,
        out_specs=pl.BlockSpec((1, tm, d), lambda bi, i: (bi, i, 0)),
        out_shape=jax.ShapeDtypeStruct((b, s - row0 * tm, d), F32),
        compiler_params=_params(("arbitrary", "arbitrary")),
        name="out_projection",
    )(*args)


def _in_weight_columns(d_in):
    z = d_in
    src = {}
    off = 0
    for name, n in (("qa", 256), ("ka", 128), ("va", 128), ("za", 256), ("qb", 384), ("kb", 384), ("vb", 384),
                    ("zb", 384), ("qm", 384), ("km", 384), ("vm", 384), ("om", 384), ("zm", 384), ("gm", 24)):
        src[name] = np.arange(off, off + n)
        off += n
    assert off == d_in
    cols = []
    for hd in range(4):
        blk = np.full(LANES, z)
        gq = hd // 2
        blk[64 * gq:64 * gq + 64] = src["qa"][64 * hd:64 * hd + 64]
        cols.append(blk)
    for name in ("ka", "va", "qb", "kb", "vb", "qm", "km", "vm", "om", "za", "zb", "zm"):
        cols.append(src[name])
    heads = len(src["gm"]) // 4
    for p in range(heads // 2):
        for gate in range(2):
            cols.append(np.array([src["gm"][di * 2 * heads + gate * heads + 2 * p + a]
                                  for di in range(2) for a in range(2)] + [z] * (GATE_PAD - 4)))
    cols.append(np.full(LANES - 2 * GATE_PAD * (heads // 2), z))
    cols = np.concatenate(cols)
    assert cols.shape[0] == _W_COLS
    return cols


def _permute_columns(w, cols):
    n = w.shape[-1]
    pieces, start = [], 0
    breaks = np.flatnonzero(np.diff(cols) != 1) + 1
    for stop in list(breaks) + [len(cols)]:
        run = cols[start:stop]
        if len(run) >= 32:
            pieces.append(lax.slice_in_dim(w, int(run[0]), int(run[-1]) + 1, axis=w.ndim - 1))
        elif pieces and isinstance(pieces[-1], list):
            pieces[-1].extend(run)
        else:
            pieces.append(list(run))
        start = stop
    lo = min(min(p) for p in pieces if isinstance(p, list))
    tail = jnp.concatenate([lax.slice_in_dim(w, lo, n, axis=w.ndim - 1), jnp.zeros(w.shape[:-1] + (1,), w.dtype)], -1)
    pieces = [jnp.take(tail, np.array(p) - lo, axis=-1) if isinstance(p, list) else p for p in pieces]
    return jnp.concatenate(pieces, axis=-1)


def _rope_table(n_tokens, n_ctx, dim):
    f32 = np.float32
    rows = n_tokens // GRID_W
    rowp = np.broadcast_to(np.arange(rows, dtype=f32)[:, None], (rows, GRID_W)).reshape(-1)
    colp = np.broadcast_to(np.arange(GRID_W, dtype=f32)[None, :], (rows, GRID_W)).reshape(-1)
    n_freq = dim // 4
    inv = np.power(f32(ROPE_BASE), -np.arange(n_freq, dtype=f32) / f32(n_freq)).astype(f32)
    ar = rowp[:, None] * inv
    ac = colp[:, None] * inv
    ang = np.concatenate([ar, ar, ac, ac], -1).astype(f32)
    cos, sin = np.cos(ang), np.sin(ang)
    odd = (np.arange(dim) // n_freq) % 2 == 1
    sin_p = np.where(odd, sin, f32(0))
    sin_m = np.where(odd, f32(0), -sin)
    tab = np.stack([cos, sin_p, sin_m])
    ident = np.stack([np.ones((n_ctx, dim), f32), np.zeros((n_ctx, dim), f32), np.zeros((n_ctx, dim), f32)])
    tab = np.concatenate([ident, tab], axis=1)
    return jnp.asarray(np.tile(tab, (1, 1, LANES // dim)).astype(f32))


def kernel(x, c, ctx, c_ctx, w_mod, b_mod, w_in, attn_sink, diff_lambda, diff_norm_g, mlstm_i_bias,
           mlstm_f_bias, mlstm_norm_g, w_out, ln_g, ln_b):
    b, t, d = x.shape
    n_ctx = ctx.shape[1]
    depth = w_mod.shape[0]
    d_in = w_in.shape[2]
    assert n_ctx % ROW_TILE == 0 and t % ROW_TILE == 0 and t % GRID_W == 0, (n_ctx, t)
    assert (d, d_in, w_out.shape[1]) == (1024, 4248, 1024), "column layout below is written for d_model = 1024"
    alpha = (2 * depth) ** 0.25

    xa = (ctx, x)
    rope_a = _rope_table(t, n_ctx, HEAD_DIM)
    rope_b = _rope_table(t, n_ctx, B_QK_DIM)

    rows = -(-(b + 1) // 8) * 8
    cc = jnp.concatenate([c, c_ctx[None, :], jnp.zeros((rows - b - 1, d), F32)], axis=0)
    mod = _modulation(cc, w_mod, b_mod).reshape(depth, rows, 3, d)

    w_in_p = _permute_columns(w_in.astype(BF16), _in_weight_columns(d_in))
    w_out_b = w_out.astype(BF16)
    norm_b = diff_norm_g[:, :, None]

    n_pairs = mlstm_i_bias.shape[2] // 2
    heads = mlstm_i_bias.shape[2]
    chains = [(di, a) for di in range(2) for a in range(2)]
    bias_idx = np.array([[di * heads + 2 * p + a for di, a in chains] for p in range(n_pairs)])
    pad8 = lambda v: jnp.pad(v.reshape(-1)[bias_idx], ((0, 0), (0, GATE_PAD - 4)))

    for l in range(depth):
        with_ctx = l < depth - 1
        lam_init = 0.8 - 0.6 * math.exp(-0.3 * l)
        modsel = jnp.stack([jnp.broadcast_to(mod[l, b], (b, 3, d)), mod[l, :b]], axis=1)
        (qa, ka, va, qb, kb, vb, qm, km, vm, og, g, gic, gfc, gir, gfr) = _in_projection(
            xa, modsel, w_in_p, l, rope_a, rope_b, n_ctx)

        ya = _attn_a(attn_sink[l], qa, ka, va, n_ctx)
        yb, yb_ctx = _attn_b(jnp.full((1,), lam_init, F32), diff_lambda[l], norm_b[l], qb, kb, vb, n_ctx, with_ctx)

        ym = _mlstm(pad8(mlstm_i_bias[l]), pad8(mlstm_f_bias[l]), mlstm_norm_g[l].reshape(n_pairs, LANES),
                    qm, km, vm, gic, gfc, gir, gfr, og, n_ctx)

        xa = _out_projection(xa, modsel, ya, yb, yb_ctx, ym, g, w_out_b, l, ln_g[l][None, :], ln_b[l][None, :],
                             alpha, n_ctx // ROW_TILE)
    return xa
```

```python
import functools
import math

import numpy as np
import jax
import jax.numpy as jnp
from jax import lax
from jax.experimental import pallas as pl
from jax.experimental.pallas import tpu as pltpu

F32 = jnp.float32
BF16 = jnp.bfloat16

LANES = 128
GRID_W = 64
HEAD_DIM = 64
ROPE_BASE = 10000.0
LN_EPS = 1e-5
WINDOW = 128
B_QK_DIM = 32
LOG2E = math.log2(math.e)
VB_ROWS = HEAD_DIM + 16
MLSTM_CHUNK = 128
GATE_PAD = 8
MOD_COL_TILE = 1024
B_BOUND_CAP = 32.0
B_BOUND_MARGIN = 1.02
B_KEY_GROUP_BOUNDED = 4
B_TILE_GROUP = 4
B_KEY_GROUP = 2
ROW_TILE = 256
VMEM_LIMIT = 56 * 1024 * 1024


def _dot(a, b):
    return jnp.dot(a, b, preferred_element_type=F32)


def _dot_nt(a, b):
    return lax.dot_general(a, b, (((1,), (1,)), ((), ())), preferred_element_type=F32)


def _split_bf16(a):
    hi = a.astype(BF16)
    lo = (a - hi.astype(F32)).astype(BF16)
    return hi, lo


def _sigmoid(x):
    return 1.0 / (1.0 + jnp.exp(-x))


def _params(sem):
    return pltpu.CompilerParams(dimension_semantics=sem, vmem_limit_bytes=VMEM_LIMIT)


def _mod_kernel(c_ref, w_ref, b_ref, o_ref):
    c = c_ref[...]
    a = c * _sigmoid(c)
    a_hi, a_lo = _split_bf16(a)
    w_hi, w_lo = _split_bf16(w_ref[0])
    o_ref[0] = _dot(a_hi, w_hi) + _dot(a_lo, w_hi) + _dot(a_hi, w_lo) + b_ref[0]


def _modulation(cc, w_mod, b_mod):
    depth, d, n = w_mod.shape
    r = cc.shape[0]
    tn = MOD_COL_TILE
    return pl.pallas_call(
        _mod_kernel,
        grid=(depth, n // tn),
        in_specs=[
            pl.BlockSpec((r, d), lambda l, j: (0, 0)),
            pl.BlockSpec((1, d, tn), lambda l, j: (l, 0, j)),
            pl.BlockSpec((1, 1, tn), lambda l, j: (l, 0, j)),
        ],
        out_specs=pl.BlockSpec((1, r, tn), lambda l, j: (l, 0, j)),
        out_shape=jax.ShapeDtypeStruct((depth, r, n), F32),
        compiler_params=_params(("arbitrary", "arbitrary")),
        name="modulation",
    )(cc, w_mod, b_mod.reshape(depth, 1, n))


def _stream_shape(xa):
    if isinstance(xa, tuple):
        ctx, x = xa
        return x.shape[0], ctx.shape[1] + x.shape[1], x.shape[2]
    return xa.shape


def _stream_specs(xa, tm, row0):
    if not isinstance(xa, tuple):
        return [pl.BlockSpec((1, tm, xa.shape[2]), lambda bi, i: (bi, i + row0, 0))], [xa]
    ctx, x = xa
    nct = ctx.shape[1] // tm
    return ([pl.BlockSpec((1, tm, ctx.shape[2]), lambda bi, i: (bi, jnp.minimum(i + row0, nct - 1), 0)),
             pl.BlockSpec((1, tm, x.shape[2]), lambda bi, i: (bi, jnp.maximum(i + row0 - nct, 0), 0))], [ctx, x])


def _stream_tile(refs, n_ctx_tiles, row0):
    if len(refs) == 1:
        return refs[0][0]
    return jnp.where(pl.program_id(1) + row0 < n_ctx_tiles, refs[0][0], refs[1][0])


_QA, _KA, _VA = (0, 512), (512, 640), (640, 768)
_QB, _KB, _VB = (768, 1152), (1152, 1536), (1536, 1920)
_QM, _KM, _VM = (1920, 2304), (2304, 2688), (2688, 3072)
_OM, _Z, _GM = (3072, 3456), (3456, 4480), (4480, 4608)
_W_COLS = 4608
_W_GROUPS = ((0, 768), (768, 1536), (1536, 2304), (2304, 3072), (3072, 4608))


def _rope(t, tab_ref, quarter):
    cos, sin_p, sin_m = tab_ref[0], tab_ref[1], tab_ref[2]
    outs = []
    for j in range(t.shape[1] // LANES):
        tj = t[:, j * LANES:(j + 1) * LANES]
        outs.append(tj * cos + pltpu.roll(tj, quarter, 1) * sin_p + pltpu.roll(tj, LANES - quarter, 1) * sin_m)
    return outs[0] if len(outs) == 1 else jnp.concatenate(outs, axis=1)


def _inproj_kernel(*refs, n_stream, n_ctx_tiles):
    (mod_ref, w_ref, ra_ref, rb_ref, qa_ref, ka_ref, va_ref, qb_ref, kb_ref, vb_ref,
     qm_ref, km_ref, vm_ref, om_ref, g_ref, gic_ref, gfc_ref, gir_ref, gfr_ref) = refs[n_stream:]
    x = _stream_tile(refs[:n_stream], n_ctx_tiles, 0)
    shift = mod_ref[0, 0, 0:1, :]
    scale = mod_ref[0, 0, 1:2, :]
    h = (x * (1.0 + scale) + shift).astype(BF16)

    group_dots = {}

    def proj(cols):
        lo, hi = next(g for g in _W_GROUPS if g[0] <= cols[0] and cols[1] <= g[1])
        if lo not in group_dots:
            group_dots[lo] = _dot(h, w_ref[:, lo:hi])
        return group_dots[lo][:, cols[0] - lo:cols[1] - lo]

    qa_ref[0] = (_rope(proj(_QA), ra_ref, HEAD_DIM // 4) * (HEAD_DIM ** -0.5)).astype(BF16)
    ka_ref[0] = _rope(proj(_KA), ra_ref, HEAD_DIM // 4).astype(BF16)
    va_ref[0] = proj(_VA).astype(BF16)
    qb_t = (_rope(proj(_QB), rb_ref, B_QK_DIM // 4) * (B_QK_DIM ** -0.5 * LOG2E)).T.astype(BF16)
    for p in range(qb_t.shape[0] // LANES):
        qb_ref[0, p, 0] = qb_t[p * LANES:(p + 1) * LANES]
    vb_t = proj(_VB).T
    ones_rows = jnp.where(lax.broadcasted_iota(jnp.int32, (VB_ROWS - HEAD_DIM, vb_t.shape[1]), 0) == 0, 1.0, 0.0)
    for hd in range(vb_t.shape[0] // HEAD_DIM):
        vb_ref[0, hd, 0] = jnp.concatenate([vb_t[hd * HEAD_DIM:(hd + 1) * HEAD_DIM], ones_rows], axis=0).astype(BF16)
    kb_ref[0] = _rope(proj(_KB), rb_ref, B_QK_DIM // 4).astype(BF16)
    km_ref[0] = (proj(_KM) * (HEAD_DIM ** -0.5)).astype(BF16)
    qm_t = proj(_QM).T.astype(BF16)
    om_t = _sigmoid(proj(_OM)).T.astype(BF16)
    vm_t = proj(_VM).T
    L = MLSTM_CHUNK
    ones_rows_c = jnp.where(lax.broadcasted_iota(jnp.int32, (VB_ROWS - HEAD_DIM, L), 0) == 0, 1.0, 0.0)
    for c in range(qm_t.shape[1] // L):
        cols = slice(c * L, (c + 1) * L)
        for p in range(qm_t.shape[0] // LANES):
            qm_ref[0, p, c] = qm_t[p * LANES:(p + 1) * LANES, cols]
            om_ref[0, p, c] = om_t[p * LANES:(p + 1) * LANES, cols]
        for hd in range(vm_t.shape[0] // HEAD_DIM):
            vm_ref[0, hd, c] = jnp.concatenate(
                [vm_t[hd * HEAD_DIM:(hd + 1) * HEAD_DIM, cols], ones_rows_c], axis=0).astype(BF16)
    z = proj(_Z)
    g_ref[0] = (z * _sigmoid(z)).astype(BF16)
    gm = proj(_GM)
    gm_t = gm.T
    for p in range(gic_ref.shape[1]):
        lo, hi = 2 * GATE_PAD * p, 2 * GATE_PAD * p + GATE_PAD
        gic_ref[0, p] = gm[:, lo:hi]
        gfc_ref[0, p] = gm[:, hi:hi + GATE_PAD]
        for c in range(gm_t.shape[1] // L):
            gir_ref[0, p, c] = gm_t[lo:hi, c * L:(c + 1) * L]
            gfr_ref[0, p, c] = gm_t[hi:hi + GATE_PAD, c * L:(c + 1) * L]


def _in_projection(xa, modsel, w, layer, rope_a, rope_b, n_ctx):
    b, s, d = _stream_shape(xa)
    tm = ROW_TILE
    x_specs, x_args = _stream_specs(xa, tm, 0)
    n_pairs = (_QB[1] - _QB[0]) // LANES
    n_heads = (_VB[1] - _VB[0]) // HEAD_DIM
    L = MLSTM_CHUNK
    widths = [512, 128, 128, (n_pairs, LANES, tm), 384, (n_heads, VB_ROWS, tm),
              (n_pairs, LANES, L), 384, (n_heads, VB_ROWS, L), (n_pairs, LANES, L), 1024]
    tposed = lambda g, r, tt: pl.BlockSpec((1, g, tm // tt, r, tt), lambda bi, i: (bi, 0, i, 0, 0))
    row = lambda n: tposed(*n) if isinstance(n, tuple) else pl.BlockSpec((1, tm, n), lambda bi, i: (bi, i, 0))
    out_shape = [jax.ShapeDtypeStruct((b, n[0], s // n[2], n[1], n[2]) if isinstance(n, tuple) else (b, s, n), BF16)
                 for n in widths]
    gate_c = pl.BlockSpec((1, n_pairs, tm, GATE_PAD), lambda bi, i: (bi, 0, i, 0))
    gate_r = pl.BlockSpec((1, n_pairs, tm // L, GATE_PAD, L), lambda bi, i: (bi, 0, i, 0, 0))
    out_shape += [jax.ShapeDtypeStruct((b, n_pairs, s, GATE_PAD), F32)] * 2
    out_shape += [jax.ShapeDtypeStruct((b, n_pairs, s // L, GATE_PAD, L), F32)] * 2
    return pl.pallas_call(
        functools.partial(_inproj_kernel, n_stream=len(x_args), n_ctx_tiles=n_ctx // tm),
        grid=(b, s // tm),
        in_specs=x_specs + [
            pl.BlockSpec((1, 1, 3, d), lambda bi, i: (bi, jnp.minimum(i // (n_ctx // tm), 1), 0, 0)),
            pl.BlockSpec((None, d, _W_COLS), lambda bi, i: (layer, 0, 0)),
            pl.BlockSpec((3, tm, LANES), lambda bi, i: (0, i, 0)),
            pl.BlockSpec((3, tm, LANES), lambda bi, i: (0, i, 0)),
        ],
        out_specs=[row(n) for n in widths] + [gate_c, gate_c, gate_r, gate_r],
        out_shape=out_shape,
        compiler_params=_params(("arbitrary", "arbitrary")),
        name="in_projection",
    )(*x_args, modsel, w, rope_a, rope_b)


def _attn_a_kernel(sink_ref, q_ref, k_ref, v_ref, o_ref, *, n_ctx, n_blocks):
    w = WINDOW
    s_len = n_blocks * w
    half = LANES // 2
    rows1 = lax.broadcasted_iota(jnp.int32, (4 * w, 1), 0)
    sink = jnp.where(rows1 < w, sink_ref[0],
                     jnp.where(rows1 < 2 * w, sink_ref[1], jnp.where(rows1 < 3 * w, sink_ref[2], sink_ref[3])))
    row = lax.broadcasted_iota(jnp.int32, (4 * w, 3 * w), 0) & (w - 1)
    col = lax.broadcasted_iota(jnp.int32, (4 * w, 3 * w), 1)
    lane = lax.broadcasted_iota(jnp.int32, (w, LANES), 1)
    k_ctx = k_ref[0, 0:n_ctx, :]
    v_ctx = v_ref[0, 0:n_ctx, :]

    def body(i, carry):
        r0 = pl.multiple_of(i * w, w)
        q = q_ref[0, pl.ds(r0, w), :]
        qs = jnp.concatenate([q[:, j * LANES:(j + 1) * LANES] for j in range(4)], axis=0)
        start = pl.multiple_of(jnp.clip(r0 - w, 0, s_len - 3 * w), w)
        s_loc = _dot_nt(qs, k_ref[0, pl.ds(start, 3 * w), :])
        s_ctx = _dot_nt(qs, k_ctx)
        kpos = col + start
        rel = kpos - r0 - row
        ok = (jnp.abs(rel) <= w) & (kpos >= jnp.where(r0 >= n_ctx, n_ctx, s_len))
        s_loc = jnp.where(ok, s_loc, -jnp.inf)
        m = jnp.maximum(jnp.maximum(jnp.max(s_loc, axis=1, keepdims=True),
                                    jnp.max(s_ctx, axis=1, keepdims=True)), sink)
        p_loc = jnp.exp(s_loc - m)
        p_ctx = jnp.exp(s_ctx - m)
        den = (jnp.sum(p_loc, axis=1, keepdims=True) + jnp.sum(p_ctx, axis=1, keepdims=True)
               + jnp.exp(sink - m))
        o = (_dot(p_loc.astype(BF16), v_ref[0, pl.ds(start, 3 * w), :])
             + _dot(p_ctx.astype(BF16), v_ctx)) / den
        left = jnp.where(lane < half, o[0:w], pltpu.roll(o[w:2 * w], half, 1))
        right = jnp.where(lane < half, pltpu.roll(o[2 * w:3 * w], half, 1), o[3 * w:])
        o_ref[0, pl.ds(r0, w), :] = jnp.concatenate([left, right], axis=1).astype(o_ref.dtype)
        return carry

    lax.fori_loop(0, n_blocks, body, 0, unroll=2)


def _attn_a(sink, qa, ka, va, n_ctx):
    b, s, _ = qa.shape
    seq = lambda n: pl.BlockSpec((1, s, n), lambda bi: (bi, 0, 0))
    return pl.pallas_call(
        functools.partial(_attn_a_kernel, n_ctx=n_ctx, n_blocks=s // WINDOW),
        grid=(b,),
        in_specs=[pl.BlockSpec(memory_space=pltpu.SMEM), seq(4 * LANES), seq(LANES), seq(LANES)],
        out_specs=seq(2 * LANES),
        out_shape=jax.ShapeDtypeStruct((b, s, 2 * LANES), BF16),
        compiler_params=_params(("arbitrary",)),
        name="window_attention",
    )(sink, qa, ka, va)


def _attn_b_kernel(li_ref, lam_ref, ng_ref, qt_ref, k_ref, vt_ref, o_ref,
                   qs_scr, sa_scr, sb_scr, sc_scr, ea_scr, eb_scr, ec_scr, acc_scr, *, nt, q0, nq, kg, kgb):
    tq, tk = qt_ref.shape[-1], vt_ref.shape[-1]
    hv = HEAD_DIM
    q_cols = 4 * tq
    lam_init = li_ref[0]
    lv = lam_ref[...]
    lam = (jnp.exp(jnp.sum(lv[0:1] * lv[1:2], axis=1, keepdims=True))
           - jnp.exp(jnp.sum(lv[2:3] * lv[3:4], axis=1, keepdims=True)) + lam_init)

    u_r = lax.broadcasted_iota(jnp.int32, (LANES, LANES), 0) // B_QK_DIM
    u_c = lax.broadcasted_iota(jnp.int32, (LANES, LANES), 1) // B_QK_DIM
    unit_ones = jnp.where(u_r == u_c, 1.0, 0.0).astype(BF16)

    def key_norms(t, best):
        kt = k_ref[0, pl.ds(pl.multiple_of(t * tk, tk), tk), :].astype(F32)
        return jnp.maximum(best, jnp.max(_dot((kt * kt).astype(BF16), unit_ones), axis=0, keepdims=True))
    kmax2 = lax.fori_loop(0, nt, key_norms, jnp.zeros((1, LANES), F32))

    def scores(t, n, z):
        off = pl.multiple_of(t * tk, tk)
        return _dot(k_ref[0, pl.ds(off, n * tk), :], qs_scr[z])

    def values(p, t, n, hd):
        cols = slice(hd * 2 * tq, (hd + 1) * 2 * tq)
        pv = _dot(vt_ref[0, hd, t], p[0:tk, cols])
        for c in range(1, n):
            pv = pv + _dot(vt_ref[0, hd, t + c], p[c * tk:(c + 1) * tk, cols])
        return pv

    def finish(i, z):
        def head_out(hd):
            acc = acc_scr[z, hd]
            o = acc[0:hv] * (1.0 / acc[hv:hv + 1])
            od = o[:, 0:tq] - lam * o[:, tq:]
            ms = jnp.mean(od * od, axis=0, keepdims=True)
            return od * lax.rsqrt(ms + LN_EPS) * ng_ref[...]
        y_t = jnp.concatenate([head_out(0), head_out(1)], axis=0) * (1.0 - lam_init)
        o_ref[0, pl.ds(pl.multiple_of(i * tq, tq), tq), :] = y_t.T.astype(o_ref.dtype)

    def run_pipeline(fetch, consume, bufs, g, state):
        buf_a, buf_b, buf_c = bufs
        n_groups = (nt - 1) // g
        first = lambda j: 1 + g * j

        def body(tt, carry):
            state, tag_a = carry
            j0 = 2 * tt
            tag_b = fetch(first(j0 + 1), buf_b, g)
            state = consume(buf_a, tag_a, state, first(j0), g)
            tag_a = fetch(first(j0 + 2), buf_a, g)
            state = consume(buf_b, tag_b, state, first(j0 + 1), g)
            return state, tag_a

        tag_c = fetch(0, buf_c, 1)
        if n_groups == 0:
            return consume(buf_c, tag_c, state, 0, 1)
        tag_a = fetch(first(0), buf_a, g)
        state = consume(buf_c, tag_c, state, 0, 1)
        state, tag_a = lax.fori_loop(0, (n_groups - 1) // 2, body, (state, tag_a))
        if n_groups % 2 == 0:
            tag_b = fetch(first(n_groups - 1), buf_b, g)
            state = consume(buf_a, tag_a, state, first(n_groups - 2), g)
            return consume(buf_b, tag_b, state, first(n_groups - 1), g)
        return consume(buf_a, tag_a, state, first(n_groups - 1), g)

    def bounded(i, bound, z):
        def fetch(t, e_buf, n):
            e_buf[z] = jnp.exp2(scores(t, n, z) - bound).astype(BF16)
            return bound

        def consume(e_buf, tag, state, t, n):
            for hd in range(2):
                acc_scr[z, hd] = acc_scr[z, hd] + values(e_buf[z], t, n, hd)
            return state

        run_pipeline(fetch, consume, (ea_scr, eb_scr, ec_scr), kgb, bound)
        finish(i, z)

    def online(i):
        def fetch(t, s_scr, n):
            s_new = scores(t, n, 0)
            s_scr[...] = s_new
            return jnp.max(s_new, axis=0, keepdims=True)

        def consume(s_scr, mt, m, t, n):
            m_new = jnp.maximum(m, mt)
            alpha = jnp.exp2(m - m_new)
            p = jnp.exp2(s_scr[...] - m_new).astype(BF16)
            for hd in range(2):
                cols = slice(hd * 2 * tq, (hd + 1) * 2 * tq)
                acc_scr[0, hd] = alpha[:, cols] * acc_scr[0, hd] + values(p, t, n, hd)
            return m_new

        run_pipeline(fetch, consume, (sa_scr, sb_scr, sc_scr), kg, jnp.full((1, q_cols), -jnp.inf, F32))
        finish(i, 0)

    def prepare(i, z):
        qt = qt_ref[0, 0, q0 + i]
        unit = lax.broadcasted_iota(jnp.int32, qt.shape, 0) // B_QK_DIM
        zero = jnp.zeros_like(qt)
        qs_scr[z] = jnp.concatenate([jnp.where(unit == u, qt, zero) for u in range(4)], axis=1)
        acc_scr[z] = jnp.zeros(acc_scr.shape[1:], F32)
        qf = qt.astype(F32)
        qn2 = jnp.sum((qf * qf).reshape(4, B_QK_DIM, tq), axis=1)
        bound = jnp.concatenate(
            [jnp.sqrt(qn2[u:u + 1] * kmax2[:, u * B_QK_DIM:u * B_QK_DIM + 1]) for u in range(4)], axis=1)
        return bound * B_BOUND_MARGIN

    def one_tile(i):
        bound = prepare(i, 0)
        small = jnp.max(bound) <= B_BOUND_CAP
        pl.when(small)(lambda: bounded(i, bound, 0))
        pl.when(jnp.logical_not(small))(lambda: online(i))

    def tile_group(j, carry):
        tiles = [B_TILE_GROUP * j + z for z in range(B_TILE_GROUP)]
        bounds = [prepare(i, z) for z, i in enumerate(tiles)]
        worst = functools.reduce(jnp.maximum, [jnp.max(bd) for bd in bounds])
        all_small = worst <= B_BOUND_CAP

        def fast():
            for z, i in enumerate(tiles):
                bounded(i, bounds[z], z)

        def slow():
            for i in tiles:
                one_tile(i)

        pl.when(all_small)(fast)
        pl.when(jnp.logical_not(all_small))(slow)
        return carry

    lax.fori_loop(0, nq // B_TILE_GROUP, tile_group, 0)
    for i in range(nq - nq % B_TILE_GROUP, nq):
        one_tile(i)


def _attn_b_call(lam_init, lam_vec, norm_g, qbt, kb, vbt, q0, nq, nt):
    b, pairs, n_tiles, _, tq = qbt.shape
    vrows, tk = vbt.shape[-2:]
    kg = B_KEY_GROUP if (nt - 1) % B_KEY_GROUP == 0 else 1
    kgb = B_KEY_GROUP_BOUNDED if (nt - 1) % B_KEY_GROUP_BOUNDED == 0 else 1
    score_bufs = lambda dt, g, *z: [pltpu.VMEM(z + (g * tk, 4 * tq), dt), pltpu.VMEM(z + (g * tk, 4 * tq), dt),
                                    pltpu.VMEM(z + (tk, 4 * tq), dt)]
    return pl.pallas_call(
        functools.partial(_attn_b_kernel, nt=nt, q0=q0, nq=nq, kg=kg, kgb=kgb),
        grid=(b, pairs),
        in_specs=[
            pl.BlockSpec(memory_space=pltpu.SMEM),
            pl.BlockSpec((4, B_QK_DIM), lambda bi, p: (0, 0)),
            pl.BlockSpec((HEAD_DIM, 1), lambda bi, p: (0, 0)),
            pl.BlockSpec((1, 1, n_tiles, LANES, tq), lambda bi, p: (bi, p, 0, 0, 0)),
            pl.BlockSpec((1, nt * tk, LANES), lambda bi, p: (bi, 0, p)),
            pl.BlockSpec((1, 2, nt, vrows, tk), lambda bi, p: (bi, p, 0, 0, 0)),
        ],
        out_specs=pl.BlockSpec((1, nq * tq, LANES), lambda bi, p: (bi, 0, p)),
        out_shape=jax.ShapeDtypeStruct((b, nq * tq, pairs * LANES), BF16),
        scratch_shapes=[pltpu.VMEM((B_TILE_GROUP, LANES, 4 * tq), BF16)] + score_bufs(F32, kg)
                       + score_bufs(BF16, kgb, B_TILE_GROUP) + [pltpu.VMEM((B_TILE_GROUP, 2, vrows, 2 * tq), F32)],
        compiler_params=_params(("arbitrary", "arbitrary")),
        name="diff_attention",
    )(lam_init, lam_vec, norm_g, qbt, kb, vbt)


def _attn_b(lam_init, lam_vec, norm_g, qbt, kb, vbt, n_ctx, with_ctx):
    n_tiles, tq = qbt.shape[2], qbt.shape[4]
    tk = vbt.shape[-1]
    n_ctx_tiles = n_ctx // tq
    y_lat = _attn_b_call(lam_init, lam_vec, norm_g, qbt, kb, vbt, n_ctx_tiles, n_tiles - n_ctx_tiles, n_tiles)
    y_ctx = _attn_b_call(lam_init, lam_vec, norm_g, qbt, kb, vbt, 0, n_ctx_tiles, n_ctx // tk) if with_ctx else None
    return y_lat, y_ctx


def _log_sigmoid(x):
    return jnp.minimum(x, 0.0) - jnp.log(1.0 + jnp.exp(-jnp.abs(x)))


def _mlstm_kernel(bir_ref, bic_ref, bfr_ref, bfc_ref, ng_ref, qt_ref, k_ref, vt_ref, gic_ref, gfc_ref, gir_ref,
                  gfr_ref, ogt_ref,
                  o_ref, hf_scr, hb_scr, st_scr, ir_scr, br_scr, gc_scr, *, n_ctx_chunks, n_chunks):
    L = MLSTM_CHUNK
    hv = HEAD_DIM
    r_i = lax.broadcasted_iota(jnp.int32, (L, L), 0)
    c_i = lax.broadcasted_iota(jnp.int32, (L, L), 1)
    upper = r_i <= c_i
    lower = r_i >= c_i
    t_up = jnp.where(upper, 1.0, 0.0).astype(BF16)
    t_low = jnp.where(lower, 1.0, 0.0).astype(BF16)
    row_q = lax.broadcasted_iota(jnp.int32, (LANES, L), 0)
    sel = [jnp.where(row_q < hv, 1.0, 0.0).astype(BF16), jnp.where(row_q < hv, 0.0, 1.0).astype(BF16)]
    fwd_r = lax.broadcasted_iota(jnp.int32, (GATE_PAD, L), 0) < 2
    fwd_c = lax.broadcasted_iota(jnp.int32, (L, GATE_PAD), 1) < 2
    st_scr[...] = jnp.zeros_like(st_scr)

    def gates(c, carry):
        r0 = pl.multiple_of(c * L, L)
        frow = _log_sigmoid(gfr_ref[0, 0, c] + bfr_ref[0])
        hi, lo = _split_bf16(frow)
        ir_scr[c] = gir_ref[0, 0, c] + bir_ref[0]
        br_scr[c] = jnp.where(fwd_r, _dot(hi, t_up) + _dot(lo, t_up), _dot(hi, t_low) + _dot(lo, t_low))
        fcol = _log_sigmoid(gfc_ref[0, 0, pl.ds(r0, L), :] + bfc_ref[0])
        hi, lo = _split_bf16(fcol)
        bcol = jnp.where(fwd_c, _dot(t_low, hi) + _dot(t_low, lo), _dot(t_up, hi) + _dot(t_up, lo))
        gc_scr[c] = gic_ref[0, 0, pl.ds(r0, L), :] + bic_ref[0] - bcol
        return carry
    lax.fori_loop(0, n_chunks, gates, 0, unroll=4)

    def chunk_step(direction, c, m_prev):
        r0 = pl.multiple_of(c * L, L)
        kc = k_ref[0, pl.ds(r0, L), :]
        qtc = qt_ref[0, 0, c]
        irow8, brow8, gcol8 = ir_scr[c], br_scr[c], gc_scr[c]
        tri = upper if direction == 0 else lower
        h_scr = hf_scr if direction == 0 else hb_scr
        qtm = [qtc * sel[a] for a in range(2)]
        s_both = _dot(kc, jnp.concatenate(qtm, axis=1))
        m_news, vws, decays = [], [], []
        for a in range(2):
            ci = 2 * direction + a
            b_row, i_row, g_col = brow8[ci:ci + 1], irow8[ci:ci + 1], gcol8[:, ci:ci + 1]
            log_d = jnp.where(tri, g_col + b_row, -jnp.inf)
            m_in = b_row + m_prev[a]
            m_t = jnp.maximum(m_in, jnp.max(log_d, axis=0, keepdims=True))
            w_in = jnp.exp(m_in - m_t)
            sd = (s_both[:, a * L:(a + 1) * L] * jnp.exp(log_d - m_t)).astype(BF16)
            qw = (qtm[a].astype(F32) * w_in).astype(BF16)
            vt = vt_ref[0, a, c]
            numden = _dot(jnp.concatenate([st_scr[ci].astype(BF16), vt], axis=1),
                          jnp.concatenate([qw, sd], axis=0))
            h_scr[c, a * hv:(a + 1) * hv, :] = (
                numden[0:hv] / jnp.maximum(jnp.abs(numden[hv:hv + 1]), jnp.exp(-m_t)))
            b_last = jnp.min(b_row, axis=1, keepdims=True)
            log_w = b_last - b_row + i_row
            m_new = jnp.maximum(b_last + m_prev[a], jnp.max(log_w, axis=1, keepdims=True))
            vws.append((vt.astype(F32) * jnp.exp(log_w - m_new)).astype(BF16))
            decays.append(jnp.exp(b_last + m_prev[a] - m_new))
            m_news.append(m_new)
        rows = vt_ref.shape[3]
        upd = _dot(jnp.concatenate(vws, axis=0), kc)
        for a in range(2):
            ci = 2 * direction + a
            st_scr[ci] = decays[a] * st_scr[ci] + upd[a * rows:(a + 1) * rows]
        return tuple(m_news)

    def body(j, carry):
        mf, mb = carry
        cb = jnp.where(j < n_ctx_chunks, n_ctx_chunks - 1 - j, n_chunks - 1 + n_ctx_chunks - j)
        return chunk_step(0, j, mf), chunk_step(1, cb, mb)

    z11 = jnp.zeros((1, 1), F32)
    lax.fori_loop(0, n_chunks, body, ((z11, z11), (z11, z11)), unroll=4)

    def finish(c, carry):
        r0 = pl.multiple_of(c * L, L)
        h = hf_scr[c] + hb_scr[c]
        outs = []
        for a in range(2):
            ha = h[a * hv:(a + 1) * hv]
            outs.append(ha * lax.rsqrt(jnp.mean(ha * ha, axis=0, keepdims=True) + LN_EPS))
        y_t = jnp.concatenate(outs, axis=0) * ng_ref[0] * ogt_ref[0, 0, c].astype(F32)
        o_ref[0, pl.ds(r0, L), :] = y_t.T.astype(o_ref.dtype)
        return carry
    lax.fori_loop(0, n_chunks, finish, 0, unroll=4)


def _mlstm(bias_i, bias_f, norm_g, qmt, km, vmt, gic, gfc, gir, gfr, ogt, n_ctx):
    b, pairs, nc, _, L = qmt.shape
    s = nc * L
    vrows = vmt.shape[3]
    seq = pl.BlockSpec((1, s, LANES), lambda bi, p: (bi, 0, p))
    pair_tiles = pl.BlockSpec((1, 1, nc, LANES, L), lambda bi, p: (bi, p, 0, 0, 0))
    gate_c = pl.BlockSpec((1, 1, s, GATE_PAD), lambda bi, p: (bi, p, 0, 0))
    gate_r = pl.BlockSpec((1, 1, nc, GATE_PAD, L), lambda bi, p: (bi, p, 0, 0, 0))
    bias_r = pl.BlockSpec((1, GATE_PAD, 1), lambda bi, p: (p, 0, 0))
    bias_c = pl.BlockSpec((1, 1, GATE_PAD), lambda bi, p: (p, 0, 0))
    return pl.pallas_call(
        functools.partial(_mlstm_kernel, n_ctx_chunks=n_ctx // L, n_chunks=nc),
        grid=(b, pairs),
        in_specs=[
            bias_r, bias_c, bias_r, bias_c,
            pl.BlockSpec((1, LANES, 1), lambda bi, p: (p, 0, 0)),
            pair_tiles, seq,
            pl.BlockSpec((1, 2, nc, vrows, L), lambda bi, p: (bi, p, 0, 0, 0)),
            gate_c, gate_c, gate_r, gate_r,
            pair_tiles,
        ],
        out_specs=seq,
        out_shape=jax.ShapeDtypeStruct((b, s, pairs * LANES), BF16),
        scratch_shapes=[pltpu.VMEM((nc, LANES, L), F32), pltpu.VMEM((nc, LANES, L), F32),
                        pltpu.VMEM((4, vrows, LANES), F32),
                        pltpu.VMEM((nc, GATE_PAD, L), F32), pltpu.VMEM((nc, GATE_PAD, L), F32),
                        pltpu.VMEM((nc, L, GATE_PAD), F32)],
        compiler_params=_params(("arbitrary", "arbitrary")),
        name="mlstm",
    )(bias_i[:, :, None], bias_i[:, None, :], bias_f[:, :, None], bias_f[:, None, :], norm_g[:, :, None],
      qmt, km, vmt, gic, gfc, gir, gfr, ogt)


def _out_kernel(*refs, alpha, n_ctx_tiles, n_stream, row0):
    x = _stream_tile(refs[:n_stream], n_ctx_tiles, row0)
    mod_ref, ya_ref, yb_ref = refs[n_stream:n_stream + 3]
    refs = refs[n_stream + 3:]
    ym_ref, g_ref, w_ref, lng_ref, lnb_ref, o_ref = refs[-6:]
    gate = mod_ref[0, 0, 2:3, :]
    g = g_ref[0]
    na, nb_ = ya_ref.shape[2], yb_ref.shape[2]
    yb = yb_ref[0]
    if len(refs) == 7:
        yb = jnp.where(pl.program_id(1) < n_ctx_tiles, refs[0][0], yb)
    mix_a = ya_ref[0] * g[:, :na]
    mix_b = yb * g[:, na:na + nb_]
    mix_m = ym_ref[0] * g[:, na + nb_:]
    y = _dot(jnp.concatenate([mix_a, mix_b, mix_m], axis=1), w_ref[...])
    r = alpha * x + gate * y
    mu = jnp.mean(r, axis=1, keepdims=True)
    d = r - mu
    var = jnp.mean(d * d, axis=1, keepdims=True)
    o_ref[0] = d * lax.rsqrt(var + LN_EPS) * lng_ref[...] + lnb_ref[...]


def _out_projection(xa, modsel, ya, yb_lat, yb_ctx, ym, g, w, layer, ln_g, ln_b, alpha, n_ctx_tiles):
    b, s, d = _stream_shape(xa)
    tm = ROW_TILE
    row0 = 0 if yb_ctx is not None else n_ctx_tiles
    row = lambda n: pl.BlockSpec((1, tm, n), lambda bi, i: (bi, i + row0, 0))
    vec = pl.BlockSpec((1, d), lambda bi, i: (0, 0))
    nb_ = yb_lat.shape[2]
    x_specs, x_args = _stream_specs(xa, tm, row0)
    in_specs = x_specs + [
        pl.BlockSpec((1, 1, 3, d), lambda bi, i: (bi, jnp.minimum((i + row0) // n_ctx_tiles, 1), 0, 0)),
        row(ya.shape[2]),
        pl.BlockSpec((1, tm, nb_), lambda bi, i: (bi, jnp.maximum(i + row0 - n_ctx_tiles, 0), 0)),
    ]
    args = x_args + [modsel, ya, yb_lat]
    if yb_ctx is not None:
        in_specs.append(pl.BlockSpec((1, tm, nb_), lambda bi, i: (bi, jnp.minimum(i, n_ctx_tiles - 1), 0)))
        args.append(yb_ctx)
    in_specs += [row(ym.shape[2]), row(d), pl.BlockSpec((None,) + w.shape[1:], lambda bi, i: (layer, 0, 0)),
                 vec, vec]
    args += [ym, g, w, ln_g, ln_b]
    return pl.pallas_call(
        functools.partial(_out_kernel, alpha=alpha, n_ctx_tiles=n_ctx_tiles, n_stream=len(x_args), row0=row0),
        grid=(b, s // tm - row0),
        in_specs=in_specs,
        out_specs=pl.BlockSpec((1, tm, d), lambda bi, i: (bi, i, 0)),
        out_shape=jax.ShapeDtypeStruct((b, s - row0 * tm, d), F32),
        compiler_params=_params(("arbitrary", "arbitrary")),
        name="out_projection",
    )(*args)


def _in_weight_columns(d_in):
    z = d_in
    src = {}
    off = 0
    for name, n in (("qa", 256), ("ka", 128), ("va", 128), ("za", 256), ("qb", 384), ("kb", 384), ("vb", 384),
                    ("zb", 384), ("qm", 384), ("km", 384), ("vm", 384), ("om", 384), ("zm", 384), ("gm", 24)):
        src[name] = np.arange(off, off + n)
        off += n
    assert off == d_in
    cols = []
    for hd in range(4):
        blk = np.full(LANES, z)
        gq = hd // 2
        blk[64 * gq:64 * gq + 64] = src["qa"][64 * hd:64 * hd + 64]
        cols.append(blk)
    for name in ("ka", "va", "qb", "kb", "vb", "qm", "km", "vm", "om", "za", "zb", "zm"):
        cols.append(src[name])
    heads = len(src["gm"]) // 4
    for p in range(heads // 2):
        for gate in range(2):
            cols.append(np.array([src["gm"][di * 2 * heads + gate * heads + 2 * p + a]
                                  for di in range(2) for a in range(2)] + [z] * (GATE_PAD - 4)))
    cols.append(np.full(LANES - 2 * GATE_PAD * (heads // 2), z))
    cols = np.concatenate(cols)
    assert cols.shape[0] == _W_COLS
    return cols


def _permute_columns(w, cols):
    n = w.shape[-1]
    pieces, start = [], 0
    breaks = np.flatnonzero(np.diff(cols) != 1) + 1
    for stop in list(breaks) + [len(cols)]:
        run = cols[start:stop]
        if len(run) >= 32:
            pieces.append(lax.slice_in_dim(w, int(run[0]), int(run[-1]) + 1, axis=w.ndim - 1))
        elif pieces and isinstance(pieces[-1], list):
            pieces[-1].extend(run)
        else:
            pieces.append(list(run))
        start = stop
    lo = min(min(p) for p in pieces if isinstance(p, list))
    tail = jnp.concatenate([lax.slice_in_dim(w, lo, n, axis=w.ndim - 1), jnp.zeros(w.shape[:-1] + (1,), w.dtype)], -1)
    pieces = [jnp.take(tail, np.array(p) - lo, axis=-1) if isinstance(p, list) else p for p in pieces]
    return jnp.concatenate(pieces, axis=-1)


def _rope_table(n_tokens, n_ctx, dim):
    f32 = np.float32
    rows = n_tokens // GRID_W
    rowp = np.broadcast_to(np.arange(rows, dtype=f32)[:, None], (rows, GRID_W)).reshape(-1)
    colp = np.broadcast_to(np.arange(GRID_W, dtype=f32)[None, :], (rows, GRID_W)).reshape(-1)
    n_freq = dim // 4
    inv = np.power(f32(ROPE_BASE), -np.arange(n_freq, dtype=f32) / f32(n_freq)).astype(f32)
    ar = rowp[:, None] * inv
    ac = colp[:, None] * inv
    ang = np.concatenate([ar, ar, ac, ac], -1).astype(f32)
    cos, sin = np.cos(ang), np.sin(ang)
    odd = (np.arange(dim) // n_freq) % 2 == 1
    sin_p = np.where(odd, sin, f32(0))
    sin_m = np.where(odd, f32(0), -sin)
    tab = np.stack([cos, sin_p, sin_m])
    ident = np.stack([np.ones((n_ctx, dim), f32), np.zeros((n_ctx, dim), f32), np.zeros((n_ctx, dim), f32)])
    tab = np.concatenate([ident, tab], axis=1)
    return jnp.asarray(np.tile(tab, (1, 1, LANES // dim)).astype(f32))


def kernel(x, c, ctx, c_ctx, w_mod, b_mod, w_in, attn_sink, diff_lambda, diff_norm_g, mlstm_i_bias,
           mlstm_f_bias, mlstm_norm_g, w_out, ln_g, ln_b):
    b, t, d = x.shape
    n_ctx = ctx.shape[1]
    depth = w_mod.shape[0]
    d_in = w_in.shape[2]
    assert n_ctx % ROW_TILE == 0 and t % ROW_TILE == 0 and t % GRID_W == 0, (n_ctx, t)
    assert (d, d_in, w_out.shape[1]) == (1024, 4248, 1024), "column layout below is written for d_model = 1024"
    alpha = (2 * depth) ** 0.25

    xa = (ctx, x)
    rope_a = _rope_table(t, n_ctx, HEAD_DIM)
    rope_b = _rope_table(t, n_ctx, B_QK_DIM)

    rows = -(-(b + 1) // 8) * 8
    cc = jnp.concatenate([c, c_ctx[None, :], jnp.zeros((rows - b - 1, d), F32)], axis=0)
    mod = _modulation(cc, w_mod, b_mod).reshape(depth, rows, 3, d)

    w_in_p = _permute_columns(w_in.astype(BF16), _in_weight_columns(d_in))
    w_out_b = w_out.astype(BF16)
    norm_b = diff_norm_g[:, :, None]

    n_pairs = mlstm_i_bias.shape[2] // 2
    heads = mlstm_i_bias.shape[2]
    chains = [(di, a) for di in range(2) for a in range(2)]
    bias_idx = np.array([[di * heads + 2 * p + a for di, a in chains] for p in range(n_pairs)])
    pad8 = lambda v: jnp.pad(v.reshape(-1)[bias_idx], ((0, 0), (0, GATE_PAD - 4)))

    for l in range(depth):
        with_ctx = l < depth - 1
        lam_init = 0.8 - 0.6 * math.exp(-0.3 * l)
        modsel = jnp.stack([jnp.broadcast_to(mod[l, b], (b, 3, d)), mod[l, :b]], axis=1)
        (qa, ka, va, qb, kb, vb, qm, km, vm, og, g, gic, gfc, gir, gfr) = _in_projection(
            xa, modsel, w_in_p, l, rope_a, rope_b, n_ctx)

        ya = _attn_a(attn_sink[l], qa, ka, va, n_ctx)
        yb, yb_ctx = _attn_b(jnp.full((1,), lam_init, F32), diff_lambda[l], norm_b[l], qb, kb, vb, n_ctx, with_ctx)

        ym = _mlstm(pad8(mlstm_i_bias[l]), pad8(mlstm_f_bias[l]), mlstm_norm_g[l].reshape(n_pairs, LANES),
                    qm, km, vm, gic, gfc, gir, gfr, og, n_ctx)

        xa = _out_projection(xa, modsel, ya, yb, yb_ctx, ym, g, w_out_b, l, ln_g[l][None, :], ln_b[l][None, :],
                             alpha, n_ctx // ROW_TILE)
    return xa
```

```python
import functools
import math

import numpy as np
import jax
import jax.numpy as jnp
from jax import lax
from jax.experimental import pallas as pl
from jax.experimental.pallas import tpu as pltpu

F32 = jnp.float32
BF16 = jnp.bfloat16

LANES = 128
GRID_W = 64
HEAD_DIM = 64
ROPE_BASE = 10000.0
LN_EPS = 1e-5
WINDOW = 128
B_QK_DIM = 32
LOG2E = math.log2(math.e)
VB_ROWS = HEAD_DIM + 16
MLSTM_CHUNK = 128
GATE_PAD = 8
MOD_COL_TILE = 1024
B_BOUND_CAP = 32.0
B_BOUND_MARGIN = 1.02
B_KEY_GROUP_BOUNDED = 4
B_KEY_GROUP = 2
ROW_TILE = 256
VMEM_LIMIT = 56 * 1024 * 1024


def _dot(a, b):
    return jnp.dot(a, b, preferred_element_type=F32)


def _dot_nt(a, b):
    return lax.dot_general(a, b, (((1,), (1,)), ((), ())), preferred_element_type=F32)


def _split_bf16(a):
    hi = a.astype(BF16)
    lo = (a - hi.astype(F32)).astype(BF16)
    return hi, lo


def _sigmoid(x):
    return 1.0 / (1.0 + jnp.exp(-x))


def _params(sem):
    return pltpu.CompilerParams(dimension_semantics=sem, vmem_limit_bytes=VMEM_LIMIT)


def _mod_kernel(c_ref, w_ref, b_ref, o_ref):
    c = c_ref[...]
    a = c * _sigmoid(c)
    a_hi, a_lo = _split_bf16(a)
    w_hi, w_lo = _split_bf16(w_ref[0])
    o_ref[0] = _dot(a_hi, w_hi) + _dot(a_lo, w_hi) + _dot(a_hi, w_lo) + b_ref[0]


def _modulation(cc, w_mod, b_mod):
    depth, d, n = w_mod.shape
    r = cc.shape[0]
    tn = MOD_COL_TILE
    return pl.pallas_call(
        _mod_kernel,
        grid=(depth, n // tn),
        in_specs=[
            pl.BlockSpec((r, d), lambda l, j: (0, 0)),
            pl.BlockSpec((1, d, tn), lambda l, j: (l, 0, j)),
            pl.BlockSpec((1, 1, tn), lambda l, j: (l, 0, j)),
        ],
        out_specs=pl.BlockSpec((1, r, tn), lambda l, j: (l, 0, j)),
        out_shape=jax.ShapeDtypeStruct((depth, r, n), F32),
        compiler_params=_params(("arbitrary", "arbitrary")),
        name="modulation",
    )(cc, w_mod, b_mod.reshape(depth, 1, n))


def _stream_shape(xa):
    if isinstance(xa, tuple):
        ctx, x = xa
        return x.shape[0], ctx.shape[1] + x.shape[1], x.shape[2]
    return xa.shape


def _stream_specs(xa, tm, row0):
    if not isinstance(xa, tuple):
        return [pl.BlockSpec((1, tm, xa.shape[2]), lambda bi, i: (bi, i + row0, 0))], [xa]
    ctx, x = xa
    nct = ctx.shape[1] // tm
    return ([pl.BlockSpec((1, tm, ctx.shape[2]), lambda bi, i: (bi, jnp.minimum(i + row0, nct - 1), 0)),
             pl.BlockSpec((1, tm, x.shape[2]), lambda bi, i: (bi, jnp.maximum(i + row0 - nct, 0), 0))], [ctx, x])


def _stream_tile(refs, n_ctx_tiles, row0):
    if len(refs) == 1:
        return refs[0][0]
    return jnp.where(pl.program_id(1) + row0 < n_ctx_tiles, refs[0][0], refs[1][0])


_QA, _KA, _VA = (0, 512), (512, 640), (640, 768)
_QB, _KB, _VB = (768, 1152), (1152, 1536), (1536, 1920)
_QM, _KM, _VM = (1920, 2304), (2304, 2688), (2688, 3072)
_OM, _Z, _GM = (3072, 3456), (3456, 4480), (4480, 4608)
_W_COLS = 4608
_W_GROUPS = ((0, 768), (768, 1536), (1536, 2304), (2304, 3072), (3072, 4608))


def _rope(t, tab_ref, quarter):
    cos, sin_p, sin_m = tab_ref[0], tab_ref[1], tab_ref[2]
    outs = []
    for j in range(t.shape[1] // LANES):
        tj = t[:, j * LANES:(j + 1) * LANES]
        outs.append(tj * cos + pltpu.roll(tj, quarter, 1) * sin_p + pltpu.roll(tj, LANES - quarter, 1) * sin_m)
    return outs[0] if len(outs) == 1 else jnp.concatenate(outs, axis=1)


def _inproj_kernel(*refs, n_stream, n_ctx_tiles):
    (mod_ref, w_ref, ra_ref, rb_ref, qa_ref, ka_ref, va_ref, qb_ref, kb_ref, vb_ref,
     qm_ref, km_ref, vm_ref, om_ref, g_ref, gic_ref, gfc_ref, gir_ref, gfr_ref) = refs[n_stream:]
    x = _stream_tile(refs[:n_stream], n_ctx_tiles, 0)
    shift = mod_ref[0, 0, 0:1, :]
    scale = mod_ref[0, 0, 1:2, :]
    h = (x * (1.0 + scale) + shift).astype(BF16)

    group_dots = {}

    def proj(cols):
        lo, hi = next(g for g in _W_GROUPS if g[0] <= cols[0] and cols[1] <= g[1])
        if lo not in group_dots:
            group_dots[lo] = _dot(h, w_ref[:, lo:hi])
        return group_dots[lo][:, cols[0] - lo:cols[1] - lo]

    qa_ref[0] = (_rope(proj(_QA), ra_ref, HEAD_DIM // 4) * (HEAD_DIM ** -0.5)).astype(BF16)
    ka_ref[0] = _rope(proj(_KA), ra_ref, HEAD_DIM // 4).astype(BF16)
    va_ref[0] = proj(_VA).astype(BF16)
    qb_t = (_rope(proj(_QB), rb_ref, B_QK_DIM // 4) * (B_QK_DIM ** -0.5 * LOG2E)).T.astype(BF16)
    for p in range(qb_t.shape[0] // LANES):
        qb_ref[0, p, 0] = qb_t[p * LANES:(p + 1) * LANES]
    vb_t = proj(_VB).T
    ones_rows = jnp.where(lax.broadcasted_iota(jnp.int32, (VB_ROWS - HEAD_DIM, vb_t.shape[1]), 0) == 0, 1.0, 0.0)
    for hd in range(vb_t.shape[0] // HEAD_DIM):
        vb_ref[0, hd, 0] = jnp.concatenate([vb_t[hd * HEAD_DIM:(hd + 1) * HEAD_DIM], ones_rows], axis=0).astype(BF16)
    kb_ref[0] = _rope(proj(_KB), rb_ref, B_QK_DIM // 4).astype(BF16)
    km_ref[0] = (proj(_KM) * (HEAD_DIM ** -0.5)).astype(BF16)
    qm_t = proj(_QM).T.astype(BF16)
    om_t = _sigmoid(proj(_OM)).T.astype(BF16)
    vm_t = proj(_VM).T
    L = MLSTM_CHUNK
    ones_rows_c = jnp.where(lax.broadcasted_iota(jnp.int32, (VB_ROWS - HEAD_DIM, L), 0) == 0, 1.0, 0.0)
    for c in range(qm_t.shape[1] // L):
        cols = slice(c * L, (c + 1) * L)
        for p in range(qm_t.shape[0] // LANES):
            qm_ref[0, p, c] = qm_t[p * LANES:(p + 1) * LANES, cols]
            om_ref[0, p, c] = om_t[p * LANES:(p + 1) * LANES, cols]
        for hd in range(vm_t.shape[0] // HEAD_DIM):
            vm_ref[0, hd, c] = jnp.concatenate(
                [vm_t[hd * HEAD_DIM:(hd + 1) * HEAD_DIM, cols], ones_rows_c], axis=0).astype(BF16)
    z = proj(_Z)
    g_ref[0] = (z * _sigmoid(z)).astype(BF16)
    gm = proj(_GM)
    gm_t = gm.T
    for p in range(gic_ref.shape[1]):
        lo, hi = 2 * GATE_PAD * p, 2 * GATE_PAD * p + GATE_PAD
        gic_ref[0, p] = gm[:, lo:hi]
        gfc_ref[0, p] = gm[:, hi:hi + GATE_PAD]
        for c in range(gm_t.shape[1] // L):
            gir_ref[0, p, c] = gm_t[lo:hi, c * L:(c + 1) * L]
            gfr_ref[0, p, c] = gm_t[hi:hi + GATE_PAD, c * L:(c + 1) * L]


def _in_projection(xa, modsel, w, layer, rope_a, rope_b, n_ctx):
    b, s, d = _stream_shape(xa)
    tm = ROW_TILE
    x_specs, x_args = _stream_specs(xa, tm, 0)
    n_pairs = (_QB[1] - _QB[0]) // LANES
    n_heads = (_VB[1] - _VB[0]) // HEAD_DIM
    L = MLSTM_CHUNK
    widths = [512, 128, 128, (n_pairs, LANES, tm), 384, (n_heads, VB_ROWS, tm),
              (n_pairs, LANES, L), 384, (n_heads, VB_ROWS, L), (n_pairs, LANES, L), 1024]
    tposed = lambda g, r, tt: pl.BlockSpec((1, g, tm // tt, r, tt), lambda bi, i: (bi, 0, i, 0, 0))
    row = lambda n: tposed(*n) if isinstance(n, tuple) else pl.BlockSpec((1, tm, n), lambda bi, i: (bi, i, 0))
    out_shape = [jax.ShapeDtypeStruct((b, n[0], s // n[2], n[1], n[2]) if isinstance(n, tuple) else (b, s, n), BF16)
                 for n in widths]
    gate_c = pl.BlockSpec((1, n_pairs, tm, GATE_PAD), lambda bi, i: (bi, 0, i, 0))
    gate_r = pl.BlockSpec((1, n_pairs, tm // L, GATE_PAD, L), lambda bi, i: (bi, 0, i, 0, 0))
    out_shape += [jax.ShapeDtypeStruct((b, n_pairs, s, GATE_PAD), F32)] * 2
    out_shape += [jax.ShapeDtypeStruct((b, n_pairs, s // L, GATE_PAD, L), F32)] * 2
    return pl.pallas_call(
        functools.partial(_inproj_kernel, n_stream=len(x_args), n_ctx_tiles=n_ctx // tm),
        grid=(b, s // tm),
        in_specs=x_specs + [
            pl.BlockSpec((1, 1, 3, d), lambda bi, i: (bi, jnp.minimum(i // (n_ctx // tm), 1), 0, 0)),
            pl.BlockSpec((None, d, _W_COLS), lambda bi, i: (layer, 0, 0)),
            pl.BlockSpec((3, tm, LANES), lambda bi, i: (0, i, 0)),
            pl.BlockSpec((3, tm, LANES), lambda bi, i: (0, i, 0)),
        ],
        out_specs=[row(n) for n in widths] + [gate_c, gate_c, gate_r, gate_r],
        out_shape=out_shape,
        compiler_params=_params(("arbitrary", "arbitrary")),
        name="in_projection",
    )(*x_args, modsel, w, rope_a, rope_b)


def _attn_a_kernel(sink_ref, q_ref, k_ref, v_ref, o_ref, *, n_ctx, n_blocks):
    w = WINDOW
    s_len = n_blocks * w
    half = LANES // 2
    rows1 = lax.broadcasted_iota(jnp.int32, (4 * w, 1), 0)
    sink = jnp.where(rows1 < w, sink_ref[0],
                     jnp.where(rows1 < 2 * w, sink_ref[1], jnp.where(rows1 < 3 * w, sink_ref[2], sink_ref[3])))
    row = lax.broadcasted_iota(jnp.int32, (4 * w, 3 * w), 0) & (w - 1)
    col = lax.broadcasted_iota(jnp.int32, (4 * w, 3 * w), 1)
    lane = lax.broadcasted_iota(jnp.int32, (w, LANES), 1)
    k_ctx = k_ref[0, 0:n_ctx, :]
    v_ctx = v_ref[0, 0:n_ctx, :]

    def body(i, carry):
        r0 = pl.multiple_of(i * w, w)
        q = q_ref[0, pl.ds(r0, w), :]
        qs = jnp.concatenate([q[:, j * LANES:(j + 1) * LANES] for j in range(4)], axis=0)
        start = pl.multiple_of(jnp.clip(r0 - w, 0, s_len - 3 * w), w)
        s_loc = _dot_nt(qs, k_ref[0, pl.ds(start, 3 * w), :])
        s_ctx = _dot_nt(qs, k_ctx)
        kpos = col + start
        rel = kpos - r0 - row
        ok = (jnp.abs(rel) <= w) & (kpos >= jnp.where(r0 >= n_ctx, n_ctx, s_len))
        s_loc = jnp.where(ok, s_loc, -jnp.inf)
        m = jnp.maximum(jnp.maximum(jnp.max(s_loc, axis=1, keepdims=True),
                                    jnp.max(s_ctx, axis=1, keepdims=True)), sink)
        p_loc = jnp.exp(s_loc - m)
        p_ctx = jnp.exp(s_ctx - m)
        den = (jnp.sum(p_loc, axis=1, keepdims=True) + jnp.sum(p_ctx, axis=1, keepdims=True)
               + jnp.exp(sink - m))
        o = (_dot(p_loc.astype(BF16), v_ref[0, pl.ds(start, 3 * w), :])
             + _dot(p_ctx.astype(BF16), v_ctx)) / den
        left = jnp.where(lane < half, o[0:w], pltpu.roll(o[w:2 * w], half, 1))
        right = jnp.where(lane < half, pltpu.roll(o[2 * w:3 * w], half, 1), o[3 * w:])
        o_ref[0, pl.ds(r0, w), :] = jnp.concatenate([left, right], axis=1).astype(o_ref.dtype)
        return carry

    lax.fori_loop(0, n_blocks, body, 0, unroll=2)


def _attn_a(sink, qa, ka, va, n_ctx):
    b, s, _ = qa.shape
    seq = lambda n: pl.BlockSpec((1, s, n), lambda bi: (bi, 0, 0))
    return pl.pallas_call(
        functools.partial(_attn_a_kernel, n_ctx=n_ctx, n_blocks=s // WINDOW),
        grid=(b,),
        in_specs=[pl.BlockSpec(memory_space=pltpu.SMEM), seq(4 * LANES), seq(LANES), seq(LANES)],
        out_specs=seq(2 * LANES),
        out_shape=jax.ShapeDtypeStruct((b, s, 2 * LANES), BF16),
        compiler_params=_params(("arbitrary",)),
        name="window_attention",
    )(sink, qa, ka, va)


def _attn_b_kernel(li_ref, lam_ref, ng_ref, qt_ref, k_ref, vt_ref, o_ref,
                   qs_scr, sa_scr, sb_scr, sc_scr, ea_scr, eb_scr, ec_scr, acc_scr, *, nt, q0, nq, kg, kgb):
    tq, tk = qt_ref.shape[-1], vt_ref.shape[-1]
    hv = HEAD_DIM
    q_cols = 4 * tq
    lam_init = li_ref[0]
    lv = lam_ref[...]
    lam = (jnp.exp(jnp.sum(lv[0:1] * lv[1:2], axis=1, keepdims=True))
           - jnp.exp(jnp.sum(lv[2:3] * lv[3:4], axis=1, keepdims=True)) + lam_init)

    u_r = lax.broadcasted_iota(jnp.int32, (LANES, LANES), 0) // B_QK_DIM
    u_c = lax.broadcasted_iota(jnp.int32, (LANES, LANES), 1) // B_QK_DIM
    unit_ones = jnp.where(u_r == u_c, 1.0, 0.0).astype(BF16)

    def key_norms(t, best):
        kt = k_ref[0, pl.ds(pl.multiple_of(t * tk, tk), tk), :].astype(F32)
        return jnp.maximum(best, jnp.max(_dot((kt * kt).astype(BF16), unit_ones), axis=0, keepdims=True))
    kmax2 = lax.fori_loop(0, nt, key_norms, jnp.zeros((1, LANES), F32))

    def scores(t, n, z):
        off = pl.multiple_of(t * tk, tk)
        return _dot(k_ref[0, pl.ds(off, n * tk), :], qs_scr[z])

    def values(p, t, n, hd):
        cols = slice(hd * 2 * tq, (hd + 1) * 2 * tq)
        pv = _dot(vt_ref[0, hd, t], p[0:tk, cols])
        for c in range(1, n):
            pv = pv + _dot(vt_ref[0, hd, t + c], p[c * tk:(c + 1) * tk, cols])
        return pv

    def finish(i, z):
        def head_out(hd):
            acc = acc_scr[z, hd]
            o = acc[0:hv] * (1.0 / acc[hv:hv + 1])
            od = o[:, 0:tq] - lam * o[:, tq:]
            ms = jnp.mean(od * od, axis=0, keepdims=True)
            return od * lax.rsqrt(ms + LN_EPS) * ng_ref[...]
        y_t = jnp.concatenate([head_out(0), head_out(1)], axis=0) * (1.0 - lam_init)
        o_ref[0, pl.ds(pl.multiple_of(i * tq, tq), tq), :] = y_t.T.astype(o_ref.dtype)

    def run_pipeline(fetch, consume, bufs, g, state):
        buf_a, buf_b, buf_c = bufs
        n_groups = (nt - 1) // g
        first = lambda j: 1 + g * j

        def body(tt, carry):
            state, tag_a = carry
            j0 = 2 * tt
            tag_b = fetch(first(j0 + 1), buf_b, g)
            state = consume(buf_a, tag_a, state, first(j0), g)
            tag_a = fetch(first(j0 + 2), buf_a, g)
            state = consume(buf_b, tag_b, state, first(j0 + 1), g)
            return state, tag_a

        tag_c = fetch(0, buf_c, 1)
        if n_groups == 0:
            return consume(buf_c, tag_c, state, 0, 1)
        tag_a = fetch(first(0), buf_a, g)
        state = consume(buf_c, tag_c, state, 0, 1)
        state, tag_a = lax.fori_loop(0, (n_groups - 1) // 2, body, (state, tag_a))
        if n_groups % 2 == 0:
            tag_b = fetch(first(n_groups - 1), buf_b, g)
            state = consume(buf_a, tag_a, state, first(n_groups - 2), g)
            return consume(buf_b, tag_b, state, first(n_groups - 1), g)
        return consume(buf_a, tag_a, state, first(n_groups - 1), g)

    def bounded(i, bound, z):
        def fetch(t, e_buf, n):
            e_buf[z] = jnp.exp2(scores(t, n, z) - bound).astype(BF16)
            return bound

        def consume(e_buf, tag, state, t, n):
            for hd in range(2):
                acc_scr[z, hd] = acc_scr[z, hd] + values(e_buf[z], t, n, hd)
            return state

        run_pipeline(fetch, consume, (ea_scr, eb_scr, ec_scr), kgb, bound)
        finish(i, z)

    def online(i):
        def fetch(t, s_scr, n):
            s_new = scores(t, n, 0)
            s_scr[...] = s_new
            return jnp.max(s_new, axis=0, keepdims=True)

        def consume(s_scr, mt, m, t, n):
            m_new = jnp.maximum(m, mt)
            alpha = jnp.exp2(m - m_new)
            p = jnp.exp2(s_scr[...] - m_new).astype(BF16)
            for hd in range(2):
                cols = slice(hd * 2 * tq, (hd + 1) * 2 * tq)
                acc_scr[0, hd] = alpha[:, cols] * acc_scr[0, hd] + values(p, t, n, hd)
            return m_new

        run_pipeline(fetch, consume, (sa_scr, sb_scr, sc_scr), kg, jnp.full((1, q_cols), -jnp.inf, F32))
        finish(i, 0)

    def prepare(i, z):
        qt = qt_ref[0, 0, q0 + i]
        unit = lax.broadcasted_iota(jnp.int32, qt.shape, 0) // B_QK_DIM
        zero = jnp.zeros_like(qt)
        qs_scr[z] = jnp.concatenate([jnp.where(unit == u, qt, zero) for u in range(4)], axis=1)
        acc_scr[z] = jnp.zeros(acc_scr.shape[1:], F32)
        qf = qt.astype(F32)
        qn2 = jnp.sum((qf * qf).reshape(4, B_QK_DIM, tq), axis=1)
        bound = jnp.concatenate(
            [jnp.sqrt(qn2[u:u + 1] * kmax2[:, u * B_QK_DIM:u * B_QK_DIM + 1]) for u in range(4)], axis=1)
        return bound * B_BOUND_MARGIN

    def one_tile(i):
        bound = prepare(i, 0)
        small = jnp.max(bound) <= B_BOUND_CAP
        pl.when(small)(lambda: bounded(i, bound, 0))
        pl.when(jnp.logical_not(small))(lambda: online(i))

    def tile_pair(j, carry):
        i0, i1 = 2 * j, 2 * j + 1
        b0, b1 = prepare(i0, 0), prepare(i1, 1)
        both = jnp.maximum(jnp.max(b0), jnp.max(b1)) <= B_BOUND_CAP

        def fast():
            bounded(i0, b0, 0)
            bounded(i1, b1, 1)

        def slow():
            online(i0)
            qs_scr[0] = qs_scr[1]
            acc_scr[0] = acc_scr[1]
            online(i1)

        pl.when(both)(fast)
        pl.when(jnp.logical_not(both))(slow)
        return carry

    lax.fori_loop(0, nq // 2, tile_pair, 0)
    if nq % 2:
        one_tile(nq - 1)


def _attn_b_call(lam_init, lam_vec, norm_g, qbt, kb, vbt, q0, nq, nt):
    b, pairs, n_tiles, _, tq = qbt.shape
    vrows, tk = vbt.shape[-2:]
    kg = B_KEY_GROUP if (nt - 1) % B_KEY_GROUP == 0 else 1
    kgb = B_KEY_GROUP_BOUNDED if (nt - 1) % B_KEY_GROUP_BOUNDED == 0 else 1
    score_bufs = lambda dt, g, *z: [pltpu.VMEM(z + (g * tk, 4 * tq), dt), pltpu.VMEM(z + (g * tk, 4 * tq), dt),
                                    pltpu.VMEM(z + (tk, 4 * tq), dt)]
    return pl.pallas_call(
        functools.partial(_attn_b_kernel, nt=nt, q0=q0, nq=nq, kg=kg, kgb=kgb),
        grid=(b, pairs),
        in_specs=[
            pl.BlockSpec(memory_space=pltpu.SMEM),
            pl.BlockSpec((4, B_QK_DIM), lambda bi, p: (0, 0)),
            pl.BlockSpec((HEAD_DIM, 1), lambda bi, p: (0, 0)),
            pl.BlockSpec((1, 1, n_tiles, LANES, tq), lambda bi, p: (bi, p, 0, 0, 0)),
            pl.BlockSpec((1, nt * tk, LANES), lambda bi, p: (bi, 0, p)),
            pl.BlockSpec((1, 2, nt, vrows, tk), lambda bi, p: (bi, p, 0, 0, 0)),
        ],
        out_specs=pl.BlockSpec((1, nq * tq, LANES), lambda bi, p: (bi, 0, p)),
        out_shape=jax.ShapeDtypeStruct((b, nq * tq, pairs * LANES), BF16),
        scratch_shapes=[pltpu.VMEM((2, LANES, 4 * tq), BF16)] + score_bufs(F32, kg) + score_bufs(BF16, kgb, 2)
                       + [pltpu.VMEM((2, 2, vrows, 2 * tq), F32)],
        compiler_params=_params(("arbitrary", "arbitrary")),
        name="diff_attention",
    )(lam_init, lam_vec, norm_g, qbt, kb, vbt)


def _attn_b(lam_init, lam_vec, norm_g, qbt, kb, vbt, n_ctx, with_ctx):
    n_tiles, tq = qbt.shape[2], qbt.shape[4]
    tk = vbt.shape[-1]
    n_ctx_tiles = n_ctx // tq
    y_lat = _attn_b_call(lam_init, lam_vec, norm_g, qbt, kb, vbt, n_ctx_tiles, n_tiles - n_ctx_tiles, n_tiles)
    y_ctx = _attn_b_call(lam_init, lam_vec, norm_g, qbt, kb, vbt, 0, n_ctx_tiles, n_ctx // tk) if with_ctx else None
    return y_lat, y_ctx


def _log_sigmoid(x):
    return jnp.minimum(x, 0.0) - jnp.log(1.0 + jnp.exp(-jnp.abs(x)))


def _mlstm_kernel(bir_ref, bic_ref, bfr_ref, bfc_ref, ng_ref, qt_ref, k_ref, vt_ref, gic_ref, gfc_ref, gir_ref,
                  gfr_ref, ogt_ref,
                  o_ref, hf_scr, hb_scr, st_scr, ir_scr, br_scr, gc_scr, *, n_ctx_chunks, n_chunks):
    L = MLSTM_CHUNK
    hv = HEAD_DIM
    r_i = lax.broadcasted_iota(jnp.int32, (L, L), 0)
    c_i = lax.broadcasted_iota(jnp.int32, (L, L), 1)
    upper = r_i <= c_i
    lower = r_i >= c_i
    t_up = jnp.where(upper, 1.0, 0.0).astype(BF16)
    t_low = jnp.where(lower, 1.0, 0.0).astype(BF16)
    row_q = lax.broadcasted_iota(jnp.int32, (LANES, L), 0)
    sel = [jnp.where(row_q < hv, 1.0, 0.0).astype(BF16), jnp.where(row_q < hv, 0.0, 1.0).astype(BF16)]
    fwd_r = lax.broadcasted_iota(jnp.int32, (GATE_PAD, L), 0) < 2
    fwd_c = lax.broadcasted_iota(jnp.int32, (L, GATE_PAD), 1) < 2
    st_scr[...] = jnp.zeros_like(st_scr)

    def gates(c, carry):
        r0 = pl.multiple_of(c * L, L)
        frow = _log_sigmoid(gfr_ref[0, 0, c] + bfr_ref[0])
        hi, lo = _split_bf16(frow)
        ir_scr[c] = gir_ref[0, 0, c] + bir_ref[0]
        br_scr[c] = jnp.where(fwd_r, _dot(hi, t_up) + _dot(lo, t_up), _dot(hi, t_low) + _dot(lo, t_low))
        fcol = _log_sigmoid(gfc_ref[0, 0, pl.ds(r0, L), :] + bfc_ref[0])
        hi, lo = _split_bf16(fcol)
        bcol = jnp.where(fwd_c, _dot(t_low, hi) + _dot(t_low, lo), _dot(t_up, hi) + _dot(t_up, lo))
        gc_scr[c] = gic_ref[0, 0, pl.ds(r0, L), :] + bic_ref[0] - bcol
        return carry
    lax.fori_loop(0, n_chunks, gates, 0, unroll=4)

    def chunk_step(direction, c, m_prev):
        r0 = pl.multiple_of(c * L, L)
        kc = k_ref[0, pl.ds(r0, L), :]
        qtc = qt_ref[0, 0, c]
        irow8, brow8, gcol8 = ir_scr[c], br_scr[c], gc_scr[c]
        tri = upper if direction == 0 else lower
        h_scr = hf_scr if direction == 0 else hb_scr
        qtm = [qtc * sel[a] for a in range(2)]
        s_both = _dot(kc, jnp.concatenate(qtm, axis=1))
        m_news, vws, decays = [], [], []
        for a in range(2):
            ci = 2 * direction + a
            b_row, i_row, g_col = brow8[ci:ci + 1], irow8[ci:ci + 1], gcol8[:, ci:ci + 1]
            log_d = jnp.where(tri, g_col + b_row, -jnp.inf)
            m_in = b_row + m_prev[a]
            m_t = jnp.maximum(m_in, jnp.max(log_d, axis=0, keepdims=True))
            w_in = jnp.exp(m_in - m_t)
            sd = (s_both[:, a * L:(a + 1) * L] * jnp.exp(log_d - m_t)).astype(BF16)
            qw = (qtm[a].astype(F32) * w_in).astype(BF16)
            vt = vt_ref[0, a, c]
            numden = _dot(jnp.concatenate([st_scr[ci].astype(BF16), vt], axis=1),
                          jnp.concatenate([qw, sd], axis=0))
            h_scr[c, a * hv:(a + 1) * hv, :] = (
                numden[0:hv] / jnp.maximum(jnp.abs(numden[hv:hv + 1]), jnp.exp(-m_t)))
            b_last = jnp.min(b_row, axis=1, keepdims=True)
            log_w = b_last - b_row + i_row
            m_new = jnp.maximum(b_last + m_prev[a], jnp.max(log_w, axis=1, keepdims=True))
            vws.append((vt.astype(F32) * jnp.exp(log_w - m_new)).astype(BF16))
            decays.append(jnp.exp(b_last + m_prev[a] - m_new))
            m_news.append(m_new)
        rows = vt_ref.shape[3]
        upd = _dot(jnp.concatenate(vws, axis=0), kc)
        for a in range(2):
            ci = 2 * direction + a
            st_scr[ci] = decays[a] * st_scr[ci] + upd[a * rows:(a + 1) * rows]
        return tuple(m_news)

    def body(j, carry):
        mf, mb = carry
        cb = jnp.where(j < n_ctx_chunks, n_ctx_chunks - 1 - j, n_chunks - 1 + n_ctx_chunks - j)
        return chunk_step(0, j, mf), chunk_step(1, cb, mb)

    z11 = jnp.zeros((1, 1), F32)
    lax.fori_loop(0, n_chunks, body, ((z11, z11), (z11, z11)), unroll=4)

    def finish(c, carry):
        r0 = pl.multiple_of(c * L, L)
        h = hf_scr[c] + hb_scr[c]
        outs = []
        for a in range(2):
            ha = h[a * hv:(a + 1) * hv]
            outs.append(ha * lax.rsqrt(jnp.mean(ha * ha, axis=0, keepdims=True) + LN_EPS))
        y_t = jnp.concatenate(outs, axis=0) * ng_ref[0] * ogt_ref[0, 0, c].astype(F32)
        o_ref[0, pl.ds(r0, L), :] = y_t.T.astype(o_ref.dtype)
        return carry
    lax.fori_loop(0, n_chunks, finish, 0, unroll=4)


def _mlstm(bias_i, bias_f, norm_g, qmt, km, vmt, gic, gfc, gir, gfr, ogt, n_ctx):
    b, pairs, nc, _, L = qmt.shape
    s = nc * L
    vrows = vmt.shape[3]
    seq = pl.BlockSpec((1, s, LANES), lambda bi, p: (bi, 0, p))
    pair_tiles = pl.BlockSpec((1, 1, nc, LANES, L), lambda bi, p: (bi, p, 0, 0, 0))
    gate_c = pl.BlockSpec((1, 1, s, GATE_PAD), lambda bi, p: (bi, p, 0, 0))
    gate_r = pl.BlockSpec((1, 1, nc, GATE_PAD, L), lambda bi, p: (bi, p, 0, 0, 0))
    bias_r = pl.BlockSpec((1, GATE_PAD, 1), lambda bi, p: (p, 0, 0))
    bias_c = pl.BlockSpec((1, 1, GATE_PAD), lambda bi, p: (p, 0, 0))
    return pl.pallas_call(
        functools.partial(_mlstm_kernel, n_ctx_chunks=n_ctx // L, n_chunks=nc),
        grid=(b, pairs),
        in_specs=[
            bias_r, bias_c, bias_r, bias_c,
            pl.BlockSpec((1, LANES, 1), lambda bi, p: (p, 0, 0)),
            pair_tiles, seq,
            pl.BlockSpec((1, 2, nc, vrows, L), lambda bi, p: (bi, p, 0, 0, 0)),
            gate_c, gate_c, gate_r, gate_r,
            pair_tiles,
        ],
        out_specs=seq,
        out_shape=jax.ShapeDtypeStruct((b, s, pairs * LANES), BF16),
        scratch_shapes=[pltpu.VMEM((nc, LANES, L), F32), pltpu.VMEM((nc, LANES, L), F32),
                        pltpu.VMEM((4, vrows, LANES), F32),
                        pltpu.VMEM((nc, GATE_PAD, L), F32), pltpu.VMEM((nc, GATE_PAD, L), F32),
                        pltpu.VMEM((nc, L, GATE_PAD), F32)],
        compiler_params=_params(("arbitrary", "arbitrary")),
        name="mlstm",
    )(bias_i[:, :, None], bias_i[:, None, :], bias_f[:, :, None], bias_f[:, None, :], norm_g[:, :, None],
      qmt, km, vmt, gic, gfc, gir, gfr, ogt)


def _out_kernel(*refs, alpha, n_ctx_tiles, n_stream, row0):
    x = _stream_tile(refs[:n_stream], n_ctx_tiles, row0)
    mod_ref, ya_ref, yb_ref = refs[n_stream:n_stream + 3]
    refs = refs[n_stream + 3:]
    ym_ref, g_ref, w_ref, lng_ref, lnb_ref, o_ref = refs[-6:]
    gate = mod_ref[0, 0, 2:3, :]
    g = g_ref[0]
    na, nb_ = ya_ref.shape[2], yb_ref.shape[2]
    yb = yb_ref[0]
    if len(refs) == 7:
        yb = jnp.where(pl.program_id(1) < n_ctx_tiles, refs[0][0], yb)
    mix_a = ya_ref[0] * g[:, :na]
    mix_b = yb * g[:, na:na + nb_]
    mix_m = ym_ref[0] * g[:, na + nb_:]
    y = _dot(jnp.concatenate([mix_a, mix_b, mix_m], axis=1), w_ref[...])
    r = alpha * x + gate * y
    mu = jnp.mean(r, axis=1, keepdims=True)
    d = r - mu
    var = jnp.mean(d * d, axis=1, keepdims=True)
    o_ref[0] = d * lax.rsqrt(var + LN_EPS) * lng_ref[...] + lnb_ref[...]


def _out_projection(xa, modsel, ya, yb_lat, yb_ctx, ym, g, w, layer, ln_g, ln_b, alpha, n_ctx_tiles):
    b, s, d = _stream_shape(xa)
    tm = ROW_TILE
    row0 = 0 if yb_ctx is not None else n_ctx_tiles
    row = lambda n: pl.BlockSpec((1, tm, n), lambda bi, i: (bi, i + row0, 0))
    vec = pl.BlockSpec((1, d), lambda bi, i: (0, 0))
    nb_ = yb_lat.shape[2]
    x_specs, x_args = _stream_specs(xa, tm, row0)
    in_specs = x_specs + [
        pl.BlockSpec((1, 1, 3, d), lambda bi, i: (bi, jnp.minimum((i + row0) // n_ctx_tiles, 1), 0, 0)),
        row(ya.shape[2]),
        pl.BlockSpec((1, tm, nb_), lambda bi, i: (bi, jnp.maximum(i + row0 - n_ctx_tiles, 0), 0)),
    ]
    args = x_args + [modsel, ya, yb_lat]
    if yb_ctx is not None:
        in_specs.append(pl.BlockSpec((1, tm, nb_), lambda bi, i: (bi, jnp.minimum(i, n_ctx_tiles - 1), 0)))
        args.append(yb_ctx)
    in_specs += [row(ym.shape[2]), row(d), pl.BlockSpec((None,) + w.shape[1:], lambda bi, i: (layer, 0, 0)),
                 vec, vec]
    args += [ym, g, w, ln_g, ln_b]
    return pl.pallas_call(
        functools.partial(_out_kernel, alpha=alpha, n_ctx_tiles=n_ctx_tiles, n_stream=len(x_args), row0=row0),
        grid=(b, s // tm - row0),
        in_specs=in_specs,
        out_specs=pl.BlockSpec((1, tm, d), lambda bi, i: (bi, i, 0)),
        out_shape=jax.ShapeDtypeStruct((b, s - row0 * tm, d), F32),
        compiler_params=_params(("arbitrary", "arbitrary")),
        name="out_projection",
    )(*args)


def _in_weight_columns(d_in):
    z = d_in
    src = {}
    off = 0
    for name, n in (("qa", 256), ("ka", 128), ("va", 128), ("za", 256), ("qb", 384), ("kb", 384), ("vb", 384),
                    ("zb", 384), ("qm", 384), ("km", 384), ("vm", 384), ("om", 384), ("zm", 384), ("gm", 24)):
        src[name] = np.arange(off, off + n)
        off += n
    assert off == d_in
    cols = []
    for hd in range(4):
        blk = np.full(LANES, z)
        gq = hd // 2
        blk[64 * gq:64 * gq + 64] = src["qa"][64 * hd:64 * hd + 64]
        cols.append(blk)
    for name in ("ka", "va", "qb", "kb", "vb", "qm", "km", "vm", "om", "za", "zb", "zm"):
        cols.append(src[name])
    heads = len(src["gm"]) // 4
    for p in range(heads // 2):
        for gate in range(2):
            cols.append(np.array([src["gm"][di * 2 * heads + gate * heads + 2 * p + a]
                                  for di in range(2) for a in range(2)] + [z] * (GATE_PAD - 4)))
    cols.append(np.full(LANES - 2 * GATE_PAD * (heads // 2), z))
    cols = np.concatenate(cols)
    assert cols.shape[0] == _W_COLS
    return cols


def _permute_columns(w, cols):
    n = w.shape[-1]
    pieces, start = [], 0
    breaks = np.flatnonzero(np.diff(cols) != 1) + 1
    for stop in list(breaks) + [len(cols)]:
        run = cols[start:stop]
        if len(run) >= 32:
            pieces.append(lax.slice_in_dim(w, int(run[0]), int(run[-1]) + 1, axis=w.ndim - 1))
        elif pieces and isinstance(pieces[-1], list):
            pieces[-1].extend(run)
        else:
            pieces.append(list(run))
        start = stop
    lo = min(min(p) for p in pieces if isinstance(p, list))
    tail = jnp.concatenate([lax.slice_in_dim(w, lo, n, axis=w.ndim - 1), jnp.zeros(w.shape[:-1] + (1,), w.dtype)], -1)
    pieces = [jnp.take(tail, np.array(p) - lo, axis=-1) if isinstance(p, list) else p for p in pieces]
    return jnp.concatenate(pieces, axis=-1)


def _rope_table(n_tokens, n_ctx, dim):
    f32 = np.float32
    rows = n_tokens // GRID_W
    rowp = np.broadcast_to(np.arange(rows, dtype=f32)[:, None], (rows, GRID_W)).reshape(-1)
    colp = np.broadcast_to(np.arange(GRID_W, dtype=f32)[None, :], (rows, GRID_W)).reshape(-1)
    n_freq = dim // 4
    inv = np.power(f32(ROPE_BASE), -np.arange(n_freq, dtype=f32) / f32(n_freq)).astype(f32)
    ar = rowp[:, None] * inv
    ac = colp[:, None] * inv
    ang = np.concatenate([ar, ar, ac, ac], -1).astype(f32)
    cos, sin = np.cos(ang), np.sin(ang)
    odd = (np.arange(dim) // n_freq) % 2 == 1
    sin_p = np.where(odd, sin, f32(0))
    sin_m = np.where(odd, f32(0), -sin)
    tab = np.stack([cos, sin_p, sin_m])
    ident = np.stack([np.ones((n_ctx, dim), f32), np.zeros((n_ctx, dim), f32), np.zeros((n_ctx, dim), f32)])
    tab = np.concatenate([ident, tab], axis=1)
    return jnp.asarray(np.tile(tab, (1, 1, LANES // dim)).astype(f32))


def kernel(x, c, ctx, c_ctx, w_mod, b_mod, w_in, attn_sink, diff_lambda, diff_norm_g, mlstm_i_bias,
           mlstm_f_bias, mlstm_norm_g, w_out, ln_g, ln_b):
    b, t, d = x.shape
    n_ctx = ctx.shape[1]
    depth = w_mod.shape[0]
    d_in = w_in.shape[2]
    assert n_ctx % ROW_TILE == 0 and t % ROW_TILE == 0 and t % GRID_W == 0, (n_ctx, t)
    assert (d, d_in, w_out.shape[1]) == (1024, 4248, 1024), "column layout below is written for d_model = 1024"
    alpha = (2 * depth) ** 0.25

    xa = (ctx, x)
    rope_a = _rope_table(t, n_ctx, HEAD_DIM)
    rope_b = _rope_table(t, n_ctx, B_QK_DIM)

    rows = -(-(b + 1) // 8) * 8
    cc = jnp.concatenate([c, c_ctx[None, :], jnp.zeros((rows - b - 1, d), F32)], axis=0)
    mod = _modulation(cc, w_mod, b_mod).reshape(depth, rows, 3, d)

    w_in_p = _permute_columns(w_in.astype(BF16), _in_weight_columns(d_in))
    w_out_b = w_out.astype(BF16)
    norm_b = diff_norm_g[:, :, None]

    n_pairs = mlstm_i_bias.shape[2] // 2
    heads = mlstm_i_bias.shape[2]
    chains = [(di, a) for di in range(2) for a in range(2)]
    bias_idx = np.array([[di * heads + 2 * p + a for di, a in chains] for p in range(n_pairs)])
    pad8 = lambda v: jnp.pad(v.reshape(-1)[bias_idx], ((0, 0), (0, GATE_PAD - 4)))

    for l in range(depth):
        with_ctx = l < depth - 1
        lam_init = 0.8 - 0.6 * math.exp(-0.3 * l)
        modsel = jnp.stack([jnp.broadcast_to(mod[l, b], (b, 3, d)), mod[l, :b]], axis=1)
        (qa, ka, va, qb, kb, vb, qm, km, vm, og, g, gic, gfc, gir, gfr) = _in_projection(
            xa, modsel, w_in_p, l, rope_a, rope_b, n_ctx)

        ya = _attn_a(attn_sink[l], qa, ka, va, n_ctx)
        yb, yb_ctx = _attn_b(jnp.full((1,), lam_init, F32), diff_lambda[l], norm_b[l], qb, kb, vb, n_ctx, with_ctx)

        ym = _mlstm(pad8(mlstm_i_bias[l]), pad8(mlstm_f_bias[l]), mlstm_norm_g[l].reshape(n_pairs, LANES),
                    qm, km, vm, gic, gfc, gir, gfr, og, n_ctx)

        xa = _out_projection(xa, modsel, ya, yb, yb_ctx, ym, g, w_out_b, l, ln_g[l][None, :], ln_b[l][None, :],
                             alpha, n_ctx // ROW_TILE)
    return xa
```

```python
import functools
import math

import numpy as np
import jax
import jax.numpy as jnp
from jax import lax
from jax.experimental import pallas as pl
from jax.experimental.pallas import tpu as pltpu

F32 = jnp.float32
BF16 = jnp.bfloat16

LANES = 128
GRID_W = 64
HEAD_DIM = 64
ROPE_BASE = 10000.0
LN_EPS = 1e-5
WINDOW = 128
B_QK_DIM = 32
LOG2E = math.log2(math.e)
VB_ROWS = HEAD_DIM + 16
MLSTM_CHUNK = 128
GATE_PAD = 8
MOD_COL_TILE = 1024
B_BOUND_CAP = 32.0
B_BOUND_MARGIN = 1.02
B_KEY_GROUP_BOUNDED = 4
B_KEY_GROUP = 2
ROW_TILE = 256
VMEM_LIMIT = 56 * 1024 * 1024


def _dot(a, b):
    return jnp.dot(a, b, preferred_element_type=F32)


def _dot_nt(a, b):
    return lax.dot_general(a, b, (((1,), (1,)), ((), ())), preferred_element_type=F32)


def _split_bf16(a):
    hi = a.astype(BF16)
    lo = (a - hi.astype(F32)).astype(BF16)
    return hi, lo


def _sigmoid(x):
    return 1.0 / (1.0 + jnp.exp(-x))


def _params(sem):
    return pltpu.CompilerParams(dimension_semantics=sem, vmem_limit_bytes=VMEM_LIMIT)


def _mod_kernel(c_ref, w_ref, b_ref, o_ref):
    c = c_ref[...]
    a = c * _sigmoid(c)
    a_hi, a_lo = _split_bf16(a)
    w_hi, w_lo = _split_bf16(w_ref[0])
    o_ref[0] = _dot(a_hi, w_hi) + _dot(a_lo, w_hi) + _dot(a_hi, w_lo) + b_ref[0]


def _modulation(cc, w_mod, b_mod):
    depth, d, n = w_mod.shape
    r = cc.shape[0]
    tn = MOD_COL_TILE
    return pl.pallas_call(
        _mod_kernel,
        grid=(depth, n // tn),
        in_specs=[
            pl.BlockSpec((r, d), lambda l, j: (0, 0)),
            pl.BlockSpec((1, d, tn), lambda l, j: (l, 0, j)),
            pl.BlockSpec((1, 1, tn), lambda l, j: (l, 0, j)),
        ],
        out_specs=pl.BlockSpec((1, r, tn), lambda l, j: (l, 0, j)),
        out_shape=jax.ShapeDtypeStruct((depth, r, n), F32),
        compiler_params=_params(("arbitrary", "arbitrary")),
        name="modulation",
    )(cc, w_mod, b_mod.reshape(depth, 1, n))


def _stream_shape(xa):
    if isinstance(xa, tuple):
        ctx, x = xa
        return x.shape[0], ctx.shape[1] + x.shape[1], x.shape[2]
    return xa.shape


def _stream_specs(xa, tm, row0):
    if not isinstance(xa, tuple):
        return [pl.BlockSpec((1, tm, xa.shape[2]), lambda bi, i: (bi, i + row0, 0))], [xa]
    ctx, x = xa
    nct = ctx.shape[1] // tm
    return ([pl.BlockSpec((1, tm, ctx.shape[2]), lambda bi, i: (bi, jnp.minimum(i + row0, nct - 1), 0)),
             pl.BlockSpec((1, tm, x.shape[2]), lambda bi, i: (bi, jnp.maximum(i + row0 - nct, 0), 0))], [ctx, x])


def _stream_tile(refs, n_ctx_tiles, row0):
    if len(refs) == 1:
        return refs[0][0]
    return jnp.where(pl.program_id(1) + row0 < n_ctx_tiles, refs[0][0], refs[1][0])


_QA, _KA, _VA = (0, 512), (512, 640), (640, 768)
_QB, _KB, _VB = (768, 1152), (1152, 1536), (1536, 1920)
_QM, _KM, _VM = (1920, 2304), (2304, 2688), (2688, 3072)
_OM, _Z, _GM = (3072, 3456), (3456, 4480), (4480, 4608)
_W_COLS = 4608
_W_GROUPS = ((0, 768), (768, 1536), (1536, 2304), (2304, 3072), (3072, 4608))


def _rope(t, tab_ref, quarter):
    cos, sin_p, sin_m = tab_ref[0], tab_ref[1], tab_ref[2]
    outs = []
    for j in range(t.shape[1] // LANES):
        tj = t[:, j * LANES:(j + 1) * LANES]
        outs.append(tj * cos + pltpu.roll(tj, quarter, 1) * sin_p + pltpu.roll(tj, LANES - quarter, 1) * sin_m)
    return outs[0] if len(outs) == 1 else jnp.concatenate(outs, axis=1)


def _inproj_kernel(*refs, n_stream, n_ctx_tiles):
    (mod_ref, w_ref, ra_ref, rb_ref, qa_ref, ka_ref, va_ref, qb_ref, kb_ref, vb_ref,
     qm_ref, km_ref, vm_ref, om_ref, g_ref, gic_ref, gfc_ref, gir_ref, gfr_ref) = refs[n_stream:]
    x = _stream_tile(refs[:n_stream], n_ctx_tiles, 0)
    shift = mod_ref[0, 0, 0:1, :]
    scale = mod_ref[0, 0, 1:2, :]
    h = (x * (1.0 + scale) + shift).astype(BF16)

    group_dots = {}

    def proj(cols):
        lo, hi = next(g for g in _W_GROUPS if g[0] <= cols[0] and cols[1] <= g[1])
        if lo not in group_dots:
            group_dots[lo] = _dot(h, w_ref[:, lo:hi])
        return group_dots[lo][:, cols[0] - lo:cols[1] - lo]

    qa_ref[0] = (_rope(proj(_QA), ra_ref, HEAD_DIM // 4) * (HEAD_DIM ** -0.5)).astype(BF16)
    ka_ref[0] = _rope(proj(_KA), ra_ref, HEAD_DIM // 4).astype(BF16)
    va_ref[0] = proj(_VA).astype(BF16)
    qb_t = (_rope(proj(_QB), rb_ref, B_QK_DIM // 4) * (B_QK_DIM ** -0.5 * LOG2E)).T.astype(BF16)
    for p in range(qb_t.shape[0] // LANES):
        qb_ref[0, p, 0] = qb_t[p * LANES:(p + 1) * LANES]
    vb_t = proj(_VB).T
    ones_rows = jnp.where(lax.broadcasted_iota(jnp.int32, (VB_ROWS - HEAD_DIM, vb_t.shape[1]), 0) == 0, 1.0, 0.0)
    for hd in range(vb_t.shape[0] // HEAD_DIM):
        vb_ref[0, hd, 0] = jnp.concatenate([vb_t[hd * HEAD_DIM:(hd + 1) * HEAD_DIM], ones_rows], axis=0).astype(BF16)
    kb_ref[0] = _rope(proj(_KB), rb_ref, B_QK_DIM // 4).astype(BF16)
    km_ref[0] = (proj(_KM) * (HEAD_DIM ** -0.5)).astype(BF16)
    qm_t = proj(_QM).T.astype(BF16)
    om_t = _sigmoid(proj(_OM)).T.astype(BF16)
    vm_t = proj(_VM).T
    L = MLSTM_CHUNK
    ones_rows_c = jnp.where(lax.broadcasted_iota(jnp.int32, (VB_ROWS - HEAD_DIM, L), 0) == 0, 1.0, 0.0)
    for c in range(qm_t.shape[1] // L):
        cols = slice(c * L, (c + 1) * L)
        for p in range(qm_t.shape[0] // LANES):
            qm_ref[0, p, c] = qm_t[p * LANES:(p + 1) * LANES, cols]
            om_ref[0, p, c] = om_t[p * LANES:(p + 1) * LANES, cols]
        for hd in range(vm_t.shape[0] // HEAD_DIM):
            vm_ref[0, hd, c] = jnp.concatenate(
                [vm_t[hd * HEAD_DIM:(hd + 1) * HEAD_DIM, cols], ones_rows_c], axis=0).astype(BF16)
    z = proj(_Z)
    g_ref[0] = (z * _sigmoid(z)).astype(BF16)
    gm = proj(_GM)
    gm_t = gm.T
    for p in range(gic_ref.shape[1]):
        lo, hi = 2 * GATE_PAD * p, 2 * GATE_PAD * p + GATE_PAD
        gic_ref[0, p] = gm[:, lo:hi]
        gfc_ref[0, p] = gm[:, hi:hi + GATE_PAD]
        for c in range(gm_t.shape[1] // L):
            gir_ref[0, p, c] = gm_t[lo:hi, c * L:(c + 1) * L]
            gfr_ref[0, p, c] = gm_t[hi:hi + GATE_PAD, c * L:(c + 1) * L]


def _in_projection(xa, modsel, w, layer, rope_a, rope_b, n_ctx):
    b, s, d = _stream_shape(xa)
    tm = ROW_TILE
    x_specs, x_args = _stream_specs(xa, tm, 0)
    n_pairs = (_QB[1] - _QB[0]) // LANES
    n_heads = (_VB[1] - _VB[0]) // HEAD_DIM
    L = MLSTM_CHUNK
    widths = [512, 128, 128, (n_pairs, LANES, tm), 384, (n_heads, VB_ROWS, tm),
              (n_pairs, LANES, L), 384, (n_heads, VB_ROWS, L), (n_pairs, LANES, L), 1024]
    tposed = lambda g, r, tt: pl.BlockSpec((1, g, tm // tt, r, tt), lambda bi, i: (bi, 0, i, 0, 0))
    row = lambda n: tposed(*n) if isinstance(n, tuple) else pl.BlockSpec((1, tm, n), lambda bi, i: (bi, i, 0))
    out_shape = [jax.ShapeDtypeStruct((b, n[0], s // n[2], n[1], n[2]) if isinstance(n, tuple) else (b, s, n), BF16)
                 for n in widths]
    gate_c = pl.BlockSpec((1, n_pairs, tm, GATE_PAD), lambda bi, i: (bi, 0, i, 0))
    gate_r = pl.BlockSpec((1, n_pairs, tm // L, GATE_PAD, L), lambda bi, i: (bi, 0, i, 0, 0))
    out_shape += [jax.ShapeDtypeStruct((b, n_pairs, s, GATE_PAD), F32)] * 2
    out_shape += [jax.ShapeDtypeStruct((b, n_pairs, s // L, GATE_PAD, L), F32)] * 2
    return pl.pallas_call(
        functools.partial(_inproj_kernel, n_stream=len(x_args), n_ctx_tiles=n_ctx // tm),
        grid=(b, s // tm),
        in_specs=x_specs + [
            pl.BlockSpec((1, 1, 3, d), lambda bi, i: (bi, jnp.minimum(i // (n_ctx // tm), 1), 0, 0)),
            pl.BlockSpec((None, d, _W_COLS), lambda bi, i: (layer, 0, 0)),
            pl.BlockSpec((3, tm, LANES), lambda bi, i: (0, i, 0)),
            pl.BlockSpec((3, tm, LANES), lambda bi, i: (0, i, 0)),
        ],
        out_specs=[row(n) for n in widths] + [gate_c, gate_c, gate_r, gate_r],
        out_shape=out_shape,
        compiler_params=_params(("arbitrary", "arbitrary")),
        name="in_projection",
    )(*x_args, modsel, w, rope_a, rope_b)


def _attn_a_kernel(sink_ref, q_ref, k_ref, v_ref, o_ref, *, n_ctx, n_blocks):
    w = WINDOW
    s_len = n_blocks * w
    half = LANES // 2
    rows1 = lax.broadcasted_iota(jnp.int32, (4 * w, 1), 0)
    sink = jnp.where(rows1 < w, sink_ref[0],
                     jnp.where(rows1 < 2 * w, sink_ref[1], jnp.where(rows1 < 3 * w, sink_ref[2], sink_ref[3])))
    row = lax.broadcasted_iota(jnp.int32, (4 * w, 3 * w), 0) & (w - 1)
    col = lax.broadcasted_iota(jnp.int32, (4 * w, 3 * w), 1)
    lane = lax.broadcasted_iota(jnp.int32, (w, LANES), 1)
    k_ctx = k_ref[0, 0:n_ctx, :]
    v_ctx = v_ref[0, 0:n_ctx, :]

    def body(i, carry):
        r0 = pl.multiple_of(i * w, w)
        q = q_ref[0, pl.ds(r0, w), :]
        qs = jnp.concatenate([q[:, j * LANES:(j + 1) * LANES] for j in range(4)], axis=0)
        start = pl.multiple_of(jnp.clip(r0 - w, 0, s_len - 3 * w), w)
        s_loc = _dot_nt(qs, k_ref[0, pl.ds(start, 3 * w), :])
        s_ctx = _dot_nt(qs, k_ctx)
        kpos = col + start
        rel = kpos - r0 - row
        ok = (jnp.abs(rel) <= w) & (kpos >= jnp.where(r0 >= n_ctx, n_ctx, s_len))
        s_loc = jnp.where(ok, s_loc, -jnp.inf)
        m = jnp.maximum(jnp.maximum(jnp.max(s_loc, axis=1, keepdims=True),
                                    jnp.max(s_ctx, axis=1, keepdims=True)), sink)
        p_loc = jnp.exp(s_loc - m)
        p_ctx = jnp.exp(s_ctx - m)
        den = (jnp.sum(p_loc, axis=1, keepdims=True) + jnp.sum(p_ctx, axis=1, keepdims=True)
               + jnp.exp(sink - m))
        o = (_dot(p_loc.astype(BF16), v_ref[0, pl.ds(start, 3 * w), :])
             + _dot(p_ctx.astype(BF16), v_ctx)) / den
        left = jnp.where(lane < half, o[0:w], pltpu.roll(o[w:2 * w], half, 1))
        right = jnp.where(lane < half, pltpu.roll(o[2 * w:3 * w], half, 1), o[3 * w:])
        o_ref[0, pl.ds(r0, w), :] = jnp.concatenate([left, right], axis=1).astype(o_ref.dtype)
        return carry

    lax.fori_loop(0, n_blocks, body, 0, unroll=2)


def _attn_a(sink, qa, ka, va, n_ctx):
    b, s, _ = qa.shape
    seq = lambda n: pl.BlockSpec((1, s, n), lambda bi: (bi, 0, 0))
    return pl.pallas_call(
        functools.partial(_attn_a_kernel, n_ctx=n_ctx, n_blocks=s // WINDOW),
        grid=(b,),
        in_specs=[pl.BlockSpec(memory_space=pltpu.SMEM), seq(4 * LANES), seq(LANES), seq(LANES)],
        out_specs=seq(2 * LANES),
        out_shape=jax.ShapeDtypeStruct((b, s, 2 * LANES), BF16),
        compiler_params=_params(("arbitrary",)),
        name="window_attention",
    )(sink, qa, ka, va)


def _attn_b_kernel(li_ref, lam_ref, ng_ref, qt_ref, k_ref, vt_ref, o_ref,
                   qs_scr, sa_scr, sb_scr, sc_scr, ea_scr, eb_scr, ec_scr, acc_scr, *, nt, q0, nq, kg, kgb):
    tq, tk = qt_ref.shape[-1], vt_ref.shape[-1]
    hv = HEAD_DIM
    q_cols = 4 * tq
    lam_init = li_ref[0]
    lv = lam_ref[...]
    lam = (jnp.exp(jnp.sum(lv[0:1] * lv[1:2], axis=1, keepdims=True))
           - jnp.exp(jnp.sum(lv[2:3] * lv[3:4], axis=1, keepdims=True)) + lam_init)

    u_r = lax.broadcasted_iota(jnp.int32, (LANES, LANES), 0) // B_QK_DIM
    u_c = lax.broadcasted_iota(jnp.int32, (LANES, LANES), 1) // B_QK_DIM
    unit_ones = jnp.where(u_r == u_c, 1.0, 0.0).astype(BF16)

    def key_norms(t, best):
        kt = k_ref[0, pl.ds(pl.multiple_of(t * tk, tk), tk), :].astype(F32)
        return jnp.maximum(best, jnp.max(_dot((kt * kt).astype(BF16), unit_ones), axis=0, keepdims=True))
    kmax2 = lax.fori_loop(0, nt, key_norms, jnp.zeros((1, LANES), F32))

    def scores(t, n, z):
        off = pl.multiple_of(t * tk, tk)
        return _dot(k_ref[0, pl.ds(off, n * tk), :], qs_scr[z])

    def values(p, t, n, hd):
        cols = slice(hd * 2 * tq, (hd + 1) * 2 * tq)
        pv = _dot(vt_ref[0, hd, t], p[0:tk, cols])
        for c in range(1, n):
            pv = pv + _dot(vt_ref[0, hd, t + c], p[c * tk:(c + 1) * tk, cols])
        return pv

    def finish(i, z):
        def head_out(hd):
            acc = acc_scr[z, hd]
            o = acc[0:hv] * (1.0 / acc[hv:hv + 1])
            od = o[:, 0:tq] - lam * o[:, tq:]
            ms = jnp.mean(od * od, axis=0, keepdims=True)
            return od * lax.rsqrt(ms + LN_EPS) * ng_ref[...]
        y_t = jnp.concatenate([head_out(0), head_out(1)], axis=0) * (1.0 - lam_init)
        o_ref[0, pl.ds(pl.multiple_of(i * tq, tq), tq), :] = y_t.T.astype(o_ref.dtype)

    def run_pipeline(fetch, consume, bufs, g, state):
        buf_a, buf_b, buf_c = bufs
        n_groups = (nt - 1) // g
        first = lambda j: 1 + g * j

        def body(tt, carry):
            state, tag_a = carry
            j0 = 2 * tt
            tag_b = fetch(first(j0 + 1), buf_b, g)
            state = consume(buf_a, tag_a, state, first(j0), g)
            tag_a = fetch(first(j0 + 2), buf_a, g)
            state = consume(buf_b, tag_b, state, first(j0 + 1), g)
            return state, tag_a

        tag_c = fetch(0, buf_c, 1)
        if n_groups == 0:
            return consume(buf_c, tag_c, state, 0, 1)
        tag_a = fetch(first(0), buf_a, g)
        state = consume(buf_c, tag_c, state, 0, 1)
        state, tag_a = lax.fori_loop(0, (n_groups - 1) // 2, body, (state, tag_a))
        if n_groups % 2 == 0:
            tag_b = fetch(first(n_groups - 1), buf_b, g)
            state = consume(buf_a, tag_a, state, first(n_groups - 2), g)
            return consume(buf_b, tag_b, state, first(n_groups - 1), g)
        return consume(buf_a, tag_a, state, first(n_groups - 1), g)

    def bounded(i, bound, z):
        def fetch(t, e_buf, n):
            e_buf[z] = jnp.exp2(scores(t, n, z) - bound).astype(BF16)
            return bound

        def consume(e_buf, tag, state, t, n):
            for hd in range(2):
                acc_scr[z, hd] = acc_scr[z, hd] + values(e_buf[z], t, n, hd)
            return state

        run_pipeline(fetch, consume, (ea_scr, eb_scr, ec_scr), kgb, bound)
        finish(i, z)

    def online(i):
        def fetch(t, s_scr, n):
            s_new = scores(t, n, 0)
            s_scr[...] = s_new
            return jnp.max(s_new, axis=0, keepdims=True)

        def consume(s_scr, mt, m, t, n):
            m_new = jnp.maximum(m, mt)
            alpha = jnp.exp2(m - m_new)
            p = jnp.exp2(s_scr[...] - m_new).astype(BF16)
            for hd in range(2):
                cols = slice(hd * 2 * tq, (hd + 1) * 2 * tq)
                acc_scr[0, hd] = alpha[:, cols] * acc_scr[0, hd] + values(p, t, n, hd)
            return m_new

        run_pipeline(fetch, consume, (sa_scr, sb_scr, sc_scr), kg, jnp.full((1, q_cols), -jnp.inf, F32))
        finish(i, 0)

    def prepare(i, z):
        qt = qt_ref[0, 0, q0 + i]
        unit = lax.broadcasted_iota(jnp.int32, qt.shape, 0) // B_QK_DIM
        zero = jnp.zeros_like(qt)
        qs_scr[z] = jnp.concatenate([jnp.where(unit == u, qt, zero) for u in range(4)], axis=1)
        acc_scr[z] = jnp.zeros(acc_scr.shape[1:], F32)
        qf = qt.astype(F32)
        qn2 = jnp.sum((qf * qf).reshape(4, B_QK_DIM, tq), axis=1)
        bound = jnp.concatenate(
            [jnp.sqrt(qn2[u:u + 1] * kmax2[:, u * B_QK_DIM:u * B_QK_DIM + 1]) for u in range(4)], axis=1)
        return bound * B_BOUND_MARGIN

    def one_tile(i):
        bound = prepare(i, 0)
        small = jnp.max(bound) <= B_BOUND_CAP
        pl.when(small)(lambda: bounded(i, bound, 0))
        pl.when(jnp.logical_not(small))(lambda: online(i))

    def tile_pair(j, carry):
        i0, i1 = 2 * j, 2 * j + 1
        b0, b1 = prepare(i0, 0), prepare(i1, 1)
        bounded(i0, b0, 0)
        bounded(i1, b1, 1)
        return carry

    def any_tile(i, carry):
        one_tile(i)
        return carry

    def query_norms(i, best):
        qf = qt_ref[0, 0, q0 + i].astype(F32)
        return jnp.maximum(best, jnp.max(jnp.sum((qf * qf).reshape(4, B_QK_DIM, tq), axis=1), axis=1, keepdims=True))
    qmax2 = lax.fori_loop(0, nq, query_norms, jnp.zeros((4, 1), F32))
    worst2 = functools.reduce(jnp.maximum, [qmax2[u:u + 1] * kmax2[:, u * B_QK_DIM:u * B_QK_DIM + 1]
                                            for u in range(4)])
    all_small = jnp.max(jnp.sqrt(worst2)) * B_BOUND_MARGIN <= B_BOUND_CAP

    @pl.when(all_small)
    def _():
        lax.fori_loop(0, nq // 2, tile_pair, 0)
        if nq % 2:
            bounded(nq - 1, prepare(nq - 1, 0), 0)

    @pl.when(jnp.logical_not(all_small))
    def _():
        lax.fori_loop(0, nq, any_tile, 0)


def _attn_b_call(lam_init, lam_vec, norm_g, qbt, kb, vbt, q0, nq, nt):
    b, pairs, n_tiles, _, tq = qbt.shape
    vrows, tk = vbt.shape[-2:]
    kg = B_KEY_GROUP if (nt - 1) % B_KEY_GROUP == 0 else 1
    kgb = B_KEY_GROUP_BOUNDED if (nt - 1) % B_KEY_GROUP_BOUNDED == 0 else 1
    score_bufs = lambda dt, g, *z: [pltpu.VMEM(z + (g * tk, 4 * tq), dt), pltpu.VMEM(z + (g * tk, 4 * tq), dt),
                                    pltpu.VMEM(z + (tk, 4 * tq), dt)]
    return pl.pallas_call(
        functools.partial(_attn_b_kernel, nt=nt, q0=q0, nq=nq, kg=kg, kgb=kgb),
        grid=(b, pairs),
        in_specs=[
            pl.BlockSpec(memory_space=pltpu.SMEM),
            pl.BlockSpec((4, B_QK_DIM), lambda bi, p: (0, 0)),
            pl.BlockSpec((HEAD_DIM, 1), lambda bi, p: (0, 0)),
            pl.BlockSpec((1, 1, n_tiles, LANES, tq), lambda bi, p: (bi, p, 0, 0, 0)),
            pl.BlockSpec((1, nt * tk, LANES), lambda bi, p: (bi, 0, p)),
            pl.BlockSpec((1, 2, nt, vrows, tk), lambda bi, p: (bi, p, 0, 0, 0)),
        ],
        out_specs=pl.BlockSpec((1, nq * tq, LANES), lambda bi, p: (bi, 0, p)),
        out_shape=jax.ShapeDtypeStruct((b, nq * tq, pairs * LANES), BF16),
        scratch_shapes=[pltpu.VMEM((2, LANES, 4 * tq), BF16)] + score_bufs(F32, kg) + score_bufs(BF16, kgb, 2)
                       + [pltpu.VMEM((2, 2, vrows, 2 * tq), F32)],
        compiler_params=_params(("arbitrary", "arbitrary")),
        name="diff_attention",
    )(lam_init, lam_vec, norm_g, qbt, kb, vbt)


def _attn_b(lam_init, lam_vec, norm_g, qbt, kb, vbt, n_ctx, with_ctx):
    n_tiles, tq = qbt.shape[2], qbt.shape[4]
    tk = vbt.shape[-1]
    n_ctx_tiles = n_ctx // tq
    y_lat = _attn_b_call(lam_init, lam_vec, norm_g, qbt, kb, vbt, n_ctx_tiles, n_tiles - n_ctx_tiles, n_tiles)
    y_ctx = _attn_b_call(lam_init, lam_vec, norm_g, qbt, kb, vbt, 0, n_ctx_tiles, n_ctx // tk) if with_ctx else None
    return y_lat, y_ctx


def _log_sigmoid(x):
    return jnp.minimum(x, 0.0) - jnp.log(1.0 + jnp.exp(-jnp.abs(x)))


def _mlstm_kernel(bir_ref, bic_ref, bfr_ref, bfc_ref, ng_ref, qt_ref, k_ref, vt_ref, gic_ref, gfc_ref, gir_ref,
                  gfr_ref, ogt_ref,
                  o_ref, hf_scr, hb_scr, st_scr, ir_scr, br_scr, gc_scr, *, n_ctx_chunks, n_chunks):
    L = MLSTM_CHUNK
    hv = HEAD_DIM
    r_i = lax.broadcasted_iota(jnp.int32, (L, L), 0)
    c_i = lax.broadcasted_iota(jnp.int32, (L, L), 1)
    upper = r_i <= c_i
    lower = r_i >= c_i
    t_up = jnp.where(upper, 1.0, 0.0).astype(BF16)
    t_low = jnp.where(lower, 1.0, 0.0).astype(BF16)
    row_q = lax.broadcasted_iota(jnp.int32, (LANES, L), 0)
    sel = [jnp.where(row_q < hv, 1.0, 0.0).astype(BF16), jnp.where(row_q < hv, 0.0, 1.0).astype(BF16)]
    fwd_r = lax.broadcasted_iota(jnp.int32, (GATE_PAD, L), 0) < 2
    fwd_c = lax.broadcasted_iota(jnp.int32, (L, GATE_PAD), 1) < 2
    st_scr[...] = jnp.zeros_like(st_scr)

    def gates(c, carry):
        r0 = pl.multiple_of(c * L, L)
        frow = _log_sigmoid(gfr_ref[0, 0, c] + bfr_ref[0])
        hi, lo = _split_bf16(frow)
        ir_scr[c] = gir_ref[0, 0, c] + bir_ref[0]
        br_scr[c] = jnp.where(fwd_r, _dot(hi, t_up) + _dot(lo, t_up), _dot(hi, t_low) + _dot(lo, t_low))
        fcol = _log_sigmoid(gfc_ref[0, 0, pl.ds(r0, L), :] + bfc_ref[0])
        hi, lo = _split_bf16(fcol)
        bcol = jnp.where(fwd_c, _dot(t_low, hi) + _dot(t_low, lo), _dot(t_up, hi) + _dot(t_up, lo))
        gc_scr[c] = gic_ref[0, 0, pl.ds(r0, L), :] + bic_ref[0] - bcol
        return carry
    lax.fori_loop(0, n_chunks, gates, 0, unroll=4)

    def chunk_step(direction, c, m_prev):
        r0 = pl.multiple_of(c * L, L)
        kc = k_ref[0, pl.ds(r0, L), :]
        qtc = qt_ref[0, 0, c]
        irow8, brow8, gcol8 = ir_scr[c], br_scr[c], gc_scr[c]
        tri = upper if direction == 0 else lower
        h_scr = hf_scr if direction == 0 else hb_scr
        qtm = [qtc * sel[a] for a in range(2)]
        s_both = _dot(kc, jnp.concatenate(qtm, axis=1))
        m_news, vws, decays = [], [], []
        for a in range(2):
            ci = 2 * direction + a
            b_row, i_row, g_col = brow8[ci:ci + 1], irow8[ci:ci + 1], gcol8[:, ci:ci + 1]
            log_d = jnp.where(tri, g_col + b_row, -jnp.inf)
            m_in = b_row + m_prev[a]
            m_t = jnp.maximum(m_in, jnp.max(log_d, axis=0, keepdims=True))
            w_in = jnp.exp(m_in - m_t)
            sd = (s_both[:, a * L:(a + 1) * L] * jnp.exp(log_d - m_t)).astype(BF16)
            qw = (qtm[a].astype(F32) * w_in).astype(BF16)
            vt = vt_ref[0, a, c]
            numden = _dot(jnp.concatenate([st_scr[ci].astype(BF16), vt], axis=1),
                          jnp.concatenate([qw, sd], axis=0))
            h_scr[c, a * hv:(a + 1) * hv, :] = (
                numden[0:hv] / jnp.maximum(jnp.abs(numden[hv:hv + 1]), jnp.exp(-m_t)))
            b_last = jnp.min(b_row, axis=1, keepdims=True)
            log_w = b_last - b_row + i_row
            m_new = jnp.maximum(b_last + m_prev[a], jnp.max(log_w, axis=1, keepdims=True))
            vws.append((vt.astype(F32) * jnp.exp(log_w - m_new)).astype(BF16))
            decays.append(jnp.exp(b_last + m_prev[a] - m_new))
            m_news.append(m_new)
        rows = vt_ref.shape[3]
        upd = _dot(jnp.concatenate(vws, axis=0), kc)
        for a in range(2):
            ci = 2 * direction + a
            st_scr[ci] = decays[a] * st_scr[ci] + upd[a * rows:(a + 1) * rows]
        return tuple(m_news)

    def body(j, carry):
        mf, mb = carry
        cb = jnp.where(j < n_ctx_chunks, n_ctx_chunks - 1 - j, n_chunks - 1 + n_ctx_chunks - j)
        return chunk_step(0, j, mf), chunk_step(1, cb, mb)

    z11 = jnp.zeros((1, 1), F32)
    lax.fori_loop(0, n_chunks, body, ((z11, z11), (z11, z11)), unroll=4)

    def finish(c, carry):
        r0 = pl.multiple_of(c * L, L)
        h = hf_scr[c] + hb_scr[c]
        outs = []
        for a in range(2):
            ha = h[a * hv:(a + 1) * hv]
            outs.append(ha * lax.rsqrt(jnp.mean(ha * ha, axis=0, keepdims=True) + LN_EPS))
        y_t = jnp.concatenate(outs, axis=0) * ng_ref[0] * ogt_ref[0, 0, c].astype(F32)
        o_ref[0, pl.ds(r0, L), :] = y_t.T.astype(o_ref.dtype)
        return carry
    lax.fori_loop(0, n_chunks, finish, 0, unroll=4)


def _mlstm(bias_i, bias_f, norm_g, qmt, km, vmt, gic, gfc, gir, gfr, ogt, n_ctx):
    b, pairs, nc, _, L = qmt.shape
    s = nc * L
    vrows = vmt.shape[3]
    seq = pl.BlockSpec((1, s, LANES), lambda bi, p: (bi, 0, p))
    pair_tiles = pl.BlockSpec((1, 1, nc, LANES, L), lambda bi, p: (bi, p, 0, 0, 0))
    gate_c = pl.BlockSpec((1, 1, s, GATE_PAD), lambda bi, p: (bi, p, 0, 0))
    gate_r = pl.BlockSpec((1, 1, nc, GATE_PAD, L), lambda bi, p: (bi, p, 0, 0, 0))
    bias_r = pl.BlockSpec((1, GATE_PAD, 1), lambda bi, p: (p, 0, 0))
    bias_c = pl.BlockSpec((1, 1, GATE_PAD), lambda bi, p: (p, 0, 0))
    return pl.pallas_call(
        functools.partial(_mlstm_kernel, n_ctx_chunks=n_ctx // L, n_chunks=nc),
        grid=(b, pairs),
        in_specs=[
            bias_r, bias_c, bias_r, bias_c,
            pl.BlockSpec((1, LANES, 1), lambda bi, p: (p, 0, 0)),
            pair_tiles, seq,
            pl.BlockSpec((1, 2, nc, vrows, L), lambda bi, p: (bi, p, 0, 0, 0)),
            gate_c, gate_c, gate_r, gate_r,
            pair_tiles,
        ],
        out_specs=seq,
        out_shape=jax.ShapeDtypeStruct((b, s, pairs * LANES), BF16),
        scratch_shapes=[pltpu.VMEM((nc, LANES, L), F32), pltpu.VMEM((nc, LANES, L), F32),
                        pltpu.VMEM((4, vrows, LANES), F32),
                        pltpu.VMEM((nc, GATE_PAD, L), F32), pltpu.VMEM((nc, GATE_PAD, L), F32),
                        pltpu.VMEM((nc, L, GATE_PAD), F32)],
        compiler_params=_params(("arbitrary", "arbitrary")),
        name="mlstm",
    )(bias_i[:, :, None], bias_i[:, None, :], bias_f[:, :, None], bias_f[:, None, :], norm_g[:, :, None],
      qmt, km, vmt, gic, gfc, gir, gfr, ogt)


def _out_kernel(*refs, alpha, n_ctx_tiles, n_stream, row0):
    x = _stream_tile(refs[:n_stream], n_ctx_tiles, row0)
    mod_ref, ya_ref, yb_ref = refs[n_stream:n_stream + 3]
    refs = refs[n_stream + 3:]
    ym_ref, g_ref, w_ref, lng_ref, lnb_ref, o_ref = refs[-6:]
    gate = mod_ref[0, 0, 2:3, :]
    g = g_ref[0]
    na, nb_ = ya_ref.shape[2], yb_ref.shape[2]
    yb = yb_ref[0]
    if len(refs) == 7:
        yb = jnp.where(pl.program_id(1) < n_ctx_tiles, refs[0][0], yb)
    mix_a = ya_ref[0] * g[:, :na]
    mix_b = yb * g[:, na:na + nb_]
    mix_m = ym_ref[0] * g[:, na + nb_:]
    y = _dot(jnp.concatenate([mix_a, mix_b, mix_m], axis=1), w_ref[...])
    r = alpha * x + gate * y
    mu = jnp.mean(r, axis=1, keepdims=True)
    d = r - mu
    var = jnp.mean(d * d, axis=1, keepdims=True)
    o_ref[0] = d * lax.rsqrt(var + LN_EPS) * lng_ref[...] + lnb_ref[...]


def _out_projection(xa, modsel, ya, yb_lat, yb_ctx, ym, g, w, layer, ln_g, ln_b, alpha, n_ctx_tiles):
    b, s, d = _stream_shape(xa)
    tm = ROW_TILE
    row0 = 0 if yb_ctx is not None else n_ctx_tiles
    row = lambda n: pl.BlockSpec((1, tm, n), lambda bi, i: (bi, i + row0, 0))
    vec = pl.BlockSpec((1, d), lambda bi, i: (0, 0))
    nb_ = yb_lat.shape[2]
    x_specs, x_args = _stream_specs(xa, tm, row0)
    in_specs = x_specs + [
        pl.BlockSpec((1, 1, 3, d), lambda bi, i: (bi, jnp.minimum((i + row0) // n_ctx_tiles, 1), 0, 0)),
        row(ya.shape[2]),
        pl.BlockSpec((1, tm, nb_), lambda bi, i: (bi, jnp.maximum(i + row0 - n_ctx_tiles, 0), 0)),
    ]
    args = x_args + [modsel, ya, yb_lat]
    if yb_ctx is not None:
        in_specs.append(pl.BlockSpec((1, tm, nb_), lambda bi, i: (bi, jnp.minimum(i, n_ctx_tiles - 1), 0)))
        args.append(yb_ctx)
    in_specs += [row(ym.shape[2]), row(d), pl.BlockSpec((None,) + w.shape[1:], lambda bi, i: (layer, 0, 0)),
                 vec, vec]
    args += [ym, g, w, ln_g, ln_b]
    return pl.pallas_call(
        functools.partial(_out_kernel, alpha=alpha, n_ctx_tiles=n_ctx_tiles, n_stream=len(x_args), row0=row0),
        grid=(b, s // tm - row0),
        in_specs=in_specs,
        out_specs=pl.BlockSpec((1, tm, d), lambda bi, i: (bi, i, 0)),
        out_shape=jax.ShapeDtypeStruct((b, s - row0 * tm, d), F32),
        compiler_params=_params(("arbitrary", "arbitrary")),
        name="out_projection",
    )(*args)


def _in_weight_columns(d_in):
    z = d_in
    src = {}
    off = 0
    for name, n in (("qa", 256), ("ka", 128), ("va", 128), ("za", 256), ("qb", 384), ("kb", 384), ("vb", 384),
                    ("zb", 384), ("qm", 384), ("km", 384), ("vm", 384), ("om", 384), ("zm", 384), ("gm", 24)):
        src[name] = np.arange(off, off + n)
        off += n
    assert off == d_in
    cols = []
    for hd in range(4):
        blk = np.full(LANES, z)
        gq = hd // 2
        blk[64 * gq:64 * gq + 64] = src["qa"][64 * hd:64 * hd + 64]
        cols.append(blk)
    for name in ("ka", "va", "qb", "kb", "vb", "qm", "km", "vm", "om", "za", "zb", "zm"):
        cols.append(src[name])
    heads = len(src["gm"]) // 4
    for p in range(heads // 2):
        for gate in range(2):
            cols.append(np.array([src["gm"][di * 2 * heads + gate * heads + 2 * p + a]
                                  for di in range(2) for a in range(2)] + [z] * (GATE_PAD - 4)))
    cols.append(np.full(LANES - 2 * GATE_PAD * (heads // 2), z))
    cols = np.concatenate(cols)
    assert cols.shape[0] == _W_COLS
    return cols


def _permute_columns(w, cols):
    n = w.shape[-1]
    pieces, start = [], 0
    breaks = np.flatnonzero(np.diff(cols) != 1) + 1
    for stop in list(breaks) + [len(cols)]:
        run = cols[start:stop]
        if len(run) >= 32:
            pieces.append(lax.slice_in_dim(w, int(run[0]), int(run[-1]) + 1, axis=w.ndim - 1))
        elif pieces and isinstance(pieces[-1], list):
            pieces[-1].extend(run)
        else:
            pieces.append(list(run))
        start = stop
    lo = min(min(p) for p in pieces if isinstance(p, list))
    tail = jnp.concatenate([lax.slice_in_dim(w, lo, n, axis=w.ndim - 1), jnp.zeros(w.shape[:-1] + (1,), w.dtype)], -1)
    pieces = [jnp.take(tail, np.array(p) - lo, axis=-1) if isinstance(p, list) else p for p in pieces]
    return jnp.concatenate(pieces, axis=-1)


def _rope_table(n_tokens, n_ctx, dim):
    f32 = np.float32
    rows = n_tokens // GRID_W
    rowp = np.broadcast_to(np.arange(rows, dtype=f32)[:, None], (rows, GRID_W)).reshape(-1)
    colp = np.broadcast_to(np.arange(GRID_W, dtype=f32)[None, :], (rows, GRID_W)).reshape(-1)
    n_freq = dim // 4
    inv = np.power(f32(ROPE_BASE), -np.arange(n_freq, dtype=f32) / f32(n_freq)).astype(f32)
    ar = rowp[:, None] * inv
    ac = colp[:, None] * inv
    ang = np.concatenate([ar, ar, ac, ac], -1).astype(f32)
    cos, sin = np.cos(ang), np.sin(ang)
    odd = (np.arange(dim) // n_freq) % 2 == 1
    sin_p = np.where(odd, sin, f32(0))
    sin_m = np.where(odd, f32(0), -sin)
    tab = np.stack([cos, sin_p, sin_m])
    ident = np.stack([np.ones((n_ctx, dim), f32), np.zeros((n_ctx, dim), f32), np.zeros((n_ctx, dim), f32)])
    tab = np.concatenate([ident, tab], axis=1)
    return jnp.asarray(np.tile(tab, (1, 1, LANES // dim)).astype(f32))


def kernel(x, c, ctx, c_ctx, w_mod, b_mod, w_in, attn_sink, diff_lambda, diff_norm_g, mlstm_i_bias,
           mlstm_f_bias, mlstm_norm_g, w_out, ln_g, ln_b):
    b, t, d = x.shape
    n_ctx = ctx.shape[1]
    depth = w_mod.shape[0]
    d_in = w_in.shape[2]
    assert n_ctx % ROW_TILE == 0 and t % ROW_TILE == 0 and t % GRID_W == 0, (n_ctx, t)
    assert (d, d_in, w_out.shape[1]) == (1024, 4248, 1024), "column layout below is written for d_model = 1024"
    alpha = (2 * depth) ** 0.25

    xa = (ctx, x)
    rope_a = _rope_table(t, n_ctx, HEAD_DIM)
    rope_b = _rope_table(t, n_ctx, B_QK_DIM)

    rows = -(-(b + 1) // 8) * 8
    cc = jnp.concatenate([c, c_ctx[None, :], jnp.zeros((rows - b - 1, d), F32)], axis=0)
    mod = _modulation(cc, w_mod, b_mod).reshape(depth, rows, 3, d)

    w_in_p = _permute_columns(w_in.astype(BF16), _in_weight_columns(d_in))
    w_out_b = w_out.astype(BF16)
    norm_b = diff_norm_g[:, :, None]

    n_pairs = mlstm_i_bias.shape[2] // 2
    heads = mlstm_i_bias.shape[2]
    chains = [(di, a) for di in range(2) for a in range(2)]
    bias_idx = np.array([[di * heads + 2 * p + a for di, a in chains] for p in range(n_pairs)])
    pad8 = lambda v: jnp.pad(v.reshape(-1)[bias_idx], ((0, 0), (0, GATE_PAD - 4)))

    for l in range(depth):
        with_ctx = l < depth - 1
        lam_init = 0.8 - 0.6 * math.exp(-0.3 * l)
        modsel = jnp.stack([jnp.broadcast_to(mod[l, b], (b, 3, d)), mod[l, :b]], axis=1)
        (qa, ka, va, qb, kb, vb, qm, km, vm, og, g, gic, gfc, gir, gfr) = _in_projection(
            xa, modsel, w_in_p, l, rope_a, rope_b, n_ctx)

        ya = _attn_a(attn_sink[l], qa, ka, va, n_ctx)
        yb, yb_ctx = _attn_b(jnp.full((1,), lam_init, F32), diff_lambda[l], norm_b[l], qb, kb, vb, n_ctx, with_ctx)

        ym = _mlstm(pad8(mlstm_i_bias[l]), pad8(mlstm_f_bias[l]), mlstm_norm_g[l].reshape(n_pairs, LANES),
                    qm, km, vm, gic, gfc, gir, gfr, og, n_ctx)

        xa = _out_projection(xa, modsel, ya, yb, yb_ctx, ym, g, w_out_b, l, ln_g[l][None, :], ln_b[l][None, :],
                             alpha, n_ctx // ROW_TILE)
    return xa
```

```python
import functools
import math

import numpy as np
import jax
import jax.numpy as jnp
from jax import lax
from jax.experimental import pallas as pl
from jax.experimental.pallas import tpu as pltpu

F32 = jnp.float32
BF16 = jnp.bfloat16

LANES = 128
GRID_W = 64
HEAD_DIM = 64
ROPE_BASE = 10000.0
LN_EPS = 1e-5
WINDOW = 128
B_QK_DIM = 32
LOG2E = math.log2(math.e)
VB_ROWS = HEAD_DIM + 16
MLSTM_CHUNK = 128
GATE_PAD = 8
MOD_COL_TILE = 1024
B_BOUND_CAP = 32.0
B_BOUND_MARGIN = 1.02
B_KEY_GROUP_BOUNDED = 4
B_KEY_GROUP = 2
ROW_TILE = 256
VMEM_LIMIT = 56 * 1024 * 1024


def _dot(a, b):
    return jnp.dot(a, b, preferred_element_type=F32)


def _dot_nt(a, b):
    return lax.dot_general(a, b, (((1,), (1,)), ((), ())), preferred_element_type=F32)


def _split_bf16(a):
    hi = a.astype(BF16)
    lo = (a - hi.astype(F32)).astype(BF16)
    return hi, lo


def _sigmoid(x):
    return 1.0 / (1.0 + jnp.exp(-x))


def _params(sem):
    return pltpu.CompilerParams(dimension_semantics=sem, vmem_limit_bytes=VMEM_LIMIT)


def _mod_kernel(c_ref, w_ref, b_ref, o_ref):
    c = c_ref[...]
    a = c * _sigmoid(c)
    a_hi, a_lo = _split_bf16(a)
    w_hi, w_lo = _split_bf16(w_ref[0])
    o_ref[0] = _dot(a_hi, w_hi) + _dot(a_lo, w_hi) + _dot(a_hi, w_lo) + b_ref[0]


def _modulation(cc, w_mod, b_mod):
    depth, d, n = w_mod.shape
    r = cc.shape[0]
    tn = MOD_COL_TILE
    return pl.pallas_call(
        _mod_kernel,
        grid=(depth, n // tn),
        in_specs=[
            pl.BlockSpec((r, d), lambda l, j: (0, 0)),
            pl.BlockSpec((1, d, tn), lambda l, j: (l, 0, j)),
            pl.BlockSpec((1, 1, tn), lambda l, j: (l, 0, j)),
        ],
        out_specs=pl.BlockSpec((1, r, tn), lambda l, j: (l, 0, j)),
        out_shape=jax.ShapeDtypeStruct((depth, r, n), F32),
        compiler_params=_params(("arbitrary", "arbitrary")),
        name="modulation",
    )(cc, w_mod, b_mod.reshape(depth, 1, n))


def _stream_shape(xa):
    if isinstance(xa, tuple):
        ctx, x = xa
        return x.shape[0], ctx.shape[1] + x.shape[1], x.shape[2]
    return xa.shape


def _stream_specs(xa, tm, row0):
    if not isinstance(xa, tuple):
        return [pl.BlockSpec((1, tm, xa.shape[2]), lambda bi, i: (bi, i + row0, 0))], [xa]
    ctx, x = xa
    nct = ctx.shape[1] // tm
    return ([pl.BlockSpec((1, tm, ctx.shape[2]), lambda bi, i: (bi, jnp.minimum(i + row0, nct - 1), 0)),
             pl.BlockSpec((1, tm, x.shape[2]), lambda bi, i: (bi, jnp.maximum(i + row0 - nct, 0), 0))], [ctx, x])


def _stream_tile(refs, n_ctx_tiles, row0):
    if len(refs) == 1:
        return refs[0][0]
    return jnp.where(pl.program_id(1) + row0 < n_ctx_tiles, refs[0][0], refs[1][0])


_QA, _KA, _VA = (0, 512), (512, 640), (640, 768)
_QB, _KB, _VB = (768, 1152), (1152, 1536), (1536, 1920)
_QM, _KM, _VM = (1920, 2304), (2304, 2688), (2688, 3072)
_OM, _Z, _GM = (3072, 3456), (3456, 4480), (4480, 4608)
_W_COLS = 4608
_W_GROUPS = ((0, 768), (768, 1536), (1536, 2304), (2304, 3072), (3072, 4608))


def _rope(t, tab_ref, quarter):
    cos, sin_p, sin_m = tab_ref[0], tab_ref[1], tab_ref[2]
    outs = []
    for j in range(t.shape[1] // LANES):
        tj = t[:, j * LANES:(j + 1) * LANES]
        outs.append(tj * cos + pltpu.roll(tj, quarter, 1) * sin_p + pltpu.roll(tj, LANES - quarter, 1) * sin_m)
    return outs[0] if len(outs) == 1 else jnp.concatenate(outs, axis=1)


def _inproj_kernel(*refs, n_stream, n_ctx_tiles):
    (mod_ref, w_ref, ra_ref, rb_ref, qa_ref, ka_ref, va_ref, qb_ref, kb_ref, vb_ref,
     qm_ref, km_ref, vm_ref, om_ref, g_ref, gic_ref, gfc_ref, gir_ref, gfr_ref) = refs[n_stream:]
    x = _stream_tile(refs[:n_stream], n_ctx_tiles, 0)
    shift = mod_ref[0, 0, 0:1, :]
    scale = mod_ref[0, 0, 1:2, :]
    h = (x * (1.0 + scale) + shift).astype(BF16)

    group_dots = {}

    def proj(cols):
        lo, hi = next(g for g in _W_GROUPS if g[0] <= cols[0] and cols[1] <= g[1])
        if lo not in group_dots:
            group_dots[lo] = _dot(h, w_ref[:, lo:hi])
        return group_dots[lo][:, cols[0] - lo:cols[1] - lo]

    qa_ref[0] = (_rope(proj(_QA), ra_ref, HEAD_DIM // 4) * (HEAD_DIM ** -0.5)).astype(BF16)
    ka_ref[0] = _rope(proj(_KA), ra_ref, HEAD_DIM // 4).astype(BF16)
    va_ref[0] = proj(_VA).astype(BF16)
    qb_t = (_rope(proj(_QB), rb_ref, B_QK_DIM // 4) * (B_QK_DIM ** -0.5 * LOG2E)).T.astype(BF16)
    for p in range(qb_t.shape[0] // LANES):
        qb_ref[0, p, 0] = qb_t[p * LANES:(p + 1) * LANES]
    vb_t = proj(_VB).T
    ones_rows = jnp.where(lax.broadcasted_iota(jnp.int32, (VB_ROWS - HEAD_DIM, vb_t.shape[1]), 0) == 0, 1.0, 0.0)
    for hd in range(vb_t.shape[0] // HEAD_DIM):
        vb_ref[0, hd, 0] = jnp.concatenate([vb_t[hd * HEAD_DIM:(hd + 1) * HEAD_DIM], ones_rows], axis=0).astype(BF16)
    kb_ref[0] = _rope(proj(_KB), rb_ref, B_QK_DIM // 4).astype(BF16)
    km_ref[0] = (proj(_KM) * (HEAD_DIM ** -0.5)).astype(BF16)
    qm_t = proj(_QM).T.astype(BF16)
    om_t = _sigmoid(proj(_OM)).T.astype(BF16)
    vm_t = proj(_VM).T
    L = MLSTM_CHUNK
    ones_rows_c = jnp.where(lax.broadcasted_iota(jnp.int32, (VB_ROWS - HEAD_DIM, L), 0) == 0, 1.0, 0.0)
    for c in range(qm_t.shape[1] // L):
        cols = slice(c * L, (c + 1) * L)
        for p in range(qm_t.shape[0] // LANES):
            qm_ref[0, p, c] = qm_t[p * LANES:(p + 1) * LANES, cols]
            om_ref[0, p, c] = om_t[p * LANES:(p + 1) * LANES, cols]
        for hd in range(vm_t.shape[0] // HEAD_DIM):
            vm_ref[0, hd, c] = jnp.concatenate(
                [vm_t[hd * HEAD_DIM:(hd + 1) * HEAD_DIM, cols], ones_rows_c], axis=0).astype(BF16)
    z = proj(_Z)
    g_ref[0] = (z * _sigmoid(z)).astype(BF16)
    gm = proj(_GM)
    gm_t = gm.T
    for p in range(gic_ref.shape[1]):
        lo, hi = 2 * GATE_PAD * p, 2 * GATE_PAD * p + GATE_PAD
        gic_ref[0, p] = gm[:, lo:hi]
        gfc_ref[0, p] = gm[:, hi:hi + GATE_PAD]
        for c in range(gm_t.shape[1] // L):
            gir_ref[0, p, c] = gm_t[lo:hi, c * L:(c + 1) * L]
            gfr_ref[0, p, c] = gm_t[hi:hi + GATE_PAD, c * L:(c + 1) * L]


def _in_projection(xa, modsel, w, layer, rope_a, rope_b, n_ctx):
    b, s, d = _stream_shape(xa)
    tm = ROW_TILE
    x_specs, x_args = _stream_specs(xa, tm, 0)
    n_pairs = (_QB[1] - _QB[0]) // LANES
    n_heads = (_VB[1] - _VB[0]) // HEAD_DIM
    L = MLSTM_CHUNK
    widths = [512, 128, 128, (n_pairs, LANES, tm), 384, (n_heads, VB_ROWS, tm),
              (n_pairs, LANES, L), 384, (n_heads, VB_ROWS, L), (n_pairs, LANES, L), 1024]
    tposed = lambda g, r, tt: pl.BlockSpec((1, g, tm // tt, r, tt), lambda bi, i: (bi, 0, i, 0, 0))
    row = lambda n: tposed(*n) if isinstance(n, tuple) else pl.BlockSpec((1, tm, n), lambda bi, i: (bi, i, 0))
    out_shape = [jax.ShapeDtypeStruct((b, n[0], s // n[2], n[1], n[2]) if isinstance(n, tuple) else (b, s, n), BF16)
                 for n in widths]
    gate_c = pl.BlockSpec((1, n_pairs, tm, GATE_PAD), lambda bi, i: (bi, 0, i, 0))
    gate_r = pl.BlockSpec((1, n_pairs, tm // L, GATE_PAD, L), lambda bi, i: (bi, 0, i, 0, 0))
    out_shape += [jax.ShapeDtypeStruct((b, n_pairs, s, GATE_PAD), F32)] * 2
    out_shape += [jax.ShapeDtypeStruct((b, n_pairs, s // L, GATE_PAD, L), F32)] * 2
    return pl.pallas_call(
        functools.partial(_inproj_kernel, n_stream=len(x_args), n_ctx_tiles=n_ctx // tm),
        grid=(b, s // tm),
        in_specs=x_specs + [
            pl.BlockSpec((1, 1, 3, d), lambda bi, i: (bi, jnp.minimum(i // (n_ctx // tm), 1), 0, 0)),
            pl.BlockSpec((None, d, _W_COLS), lambda bi, i: (layer, 0, 0)),
            pl.BlockSpec((3, tm, LANES), lambda bi, i: (0, i, 0)),
            pl.BlockSpec((3, tm, LANES), lambda bi, i: (0, i, 0)),
        ],
        out_specs=[row(n) for n in widths] + [gate_c, gate_c, gate_r, gate_r],
        out_shape=out_shape,
        compiler_params=_params(("arbitrary", "arbitrary")),
        name="in_projection",
    )(*x_args, modsel, w, rope_a, rope_b)


def _attn_a_kernel(sink_ref, q_ref, k_ref, v_ref, o_ref, *, n_ctx, n_blocks):
    w = WINDOW
    s_len = n_blocks * w
    half = LANES // 2
    rows1 = lax.broadcasted_iota(jnp.int32, (2 * w, 1), 0)
    row = lax.broadcasted_iota(jnp.int32, (2 * w, 3 * w), 0) & (w - 1)
    col = lax.broadcasted_iota(jnp.int32, (2 * w, 3 * w), 1)
    lane = lax.broadcasted_iota(jnp.int32, (w, LANES), 1)
    k_ctx = k_ref[0, 0:n_ctx, :]
    v_ctx = v_ref[0, 0:n_ctx, :]

    def body(i, carry):
        r0 = pl.multiple_of(i * w, w)
        q = q_ref[0, pl.ds(r0, w), :]
        start = pl.multiple_of(jnp.clip(r0 - w, 0, s_len - 3 * w), w)
        k_loc = k_ref[0, pl.ds(start, 3 * w), :]
        v_loc = v_ref[0, pl.ds(start, 3 * w), :]
        kpos = col + start
        rel = kpos - r0 - row
        ok = (jnp.abs(rel) <= w) & (kpos >= jnp.where(r0 >= n_ctx, n_ctx, s_len))
        outs = []
        for g in range(2):
            qs = jnp.concatenate([q[:, (2 * g + j) * LANES:(2 * g + j + 1) * LANES] for j in range(2)], axis=0)
            sink = jnp.where(rows1 < w, sink_ref[2 * g], sink_ref[2 * g + 1])
            s_loc = jnp.where(ok, _dot_nt(qs, k_loc), -jnp.inf)
            s_ctx = _dot_nt(qs, k_ctx)
            m = jnp.maximum(jnp.maximum(jnp.max(s_loc, axis=1, keepdims=True),
                                        jnp.max(s_ctx, axis=1, keepdims=True)), sink)
            p_loc = jnp.exp(s_loc - m)
            p_ctx = jnp.exp(s_ctx - m)
            den = (jnp.sum(p_loc, axis=1, keepdims=True) + jnp.sum(p_ctx, axis=1, keepdims=True)
                   + jnp.exp(sink - m))
            o = (_dot(p_loc.astype(BF16), v_loc) + _dot(p_ctx.astype(BF16), v_ctx)) / den
            if g == 0:
                outs.append(jnp.where(lane < half, o[0:w], pltpu.roll(o[w:], half, 1)))
            else:
                outs.append(jnp.where(lane < half, pltpu.roll(o[0:w], half, 1), o[w:]))
        o_ref[0, pl.ds(r0, w), :] = jnp.concatenate(outs, axis=1).astype(o_ref.dtype)
        return carry

    lax.fori_loop(0, n_blocks, body, 0, unroll=2)


def _attn_a(sink, qa, ka, va, n_ctx):
    b, s, _ = qa.shape
    seq = lambda n: pl.BlockSpec((1, s, n), lambda bi: (bi, 0, 0))
    return pl.pallas_call(
        functools.partial(_attn_a_kernel, n_ctx=n_ctx, n_blocks=s // WINDOW),
        grid=(b,),
        in_specs=[pl.BlockSpec(memory_space=pltpu.SMEM), seq(4 * LANES), seq(LANES), seq(LANES)],
        out_specs=seq(2 * LANES),
        out_shape=jax.ShapeDtypeStruct((b, s, 2 * LANES), BF16),
        compiler_params=_params(("arbitrary",)),
        name="window_attention",
    )(sink, qa, ka, va)


def _attn_b_kernel(li_ref, lam_ref, ng_ref, qt_ref, k_ref, vt_ref, o_ref,
                   qs_scr, sa_scr, sb_scr, sc_scr, ea_scr, eb_scr, ec_scr, acc_scr, *, nt, q0, nq, kg, kgb):
    tq, tk = qt_ref.shape[-1], vt_ref.shape[-1]
    hv = HEAD_DIM
    q_cols = 4 * tq
    lam_init = li_ref[0]
    lv = lam_ref[...]
    lam = (jnp.exp(jnp.sum(lv[0:1] * lv[1:2], axis=1, keepdims=True))
           - jnp.exp(jnp.sum(lv[2:3] * lv[3:4], axis=1, keepdims=True)) + lam_init)

    u_r = lax.broadcasted_iota(jnp.int32, (LANES, LANES), 0) // B_QK_DIM
    u_c = lax.broadcasted_iota(jnp.int32, (LANES, LANES), 1) // B_QK_DIM
    unit_ones = jnp.where(u_r == u_c, 1.0, 0.0).astype(BF16)

    def key_norms(t, best):
        kt = k_ref[0, pl.ds(pl.multiple_of(t * tk, tk), tk), :].astype(F32)
        return jnp.maximum(best, jnp.max(_dot((kt * kt).astype(BF16), unit_ones), axis=0, keepdims=True))
    kmax2 = lax.fori_loop(0, nt, key_norms, jnp.zeros((1, LANES), F32))

    def scores(t, n, z):
        off = pl.multiple_of(t * tk, tk)
        return _dot(k_ref[0, pl.ds(off, n * tk), :], qs_scr[z])

    def values(p, t, n, hd):
        cols = slice(hd * 2 * tq, (hd + 1) * 2 * tq)
        pv = _dot(vt_ref[0, hd, t], p[0:tk, cols])
        for c in range(1, n):
            pv = pv + _dot(vt_ref[0, hd, t + c], p[c * tk:(c + 1) * tk, cols])
        return pv

    def finish(i, z):
        def head_out(hd):
            acc = acc_scr[z, hd]
            o = acc[0:hv] * (1.0 / acc[hv:hv + 1])
            od = o[:, 0:tq] - lam * o[:, tq:]
            ms = jnp.mean(od * od, axis=0, keepdims=True)
            return od * lax.rsqrt(ms + LN_EPS) * ng_ref[...]
        y_t = jnp.concatenate([head_out(0), head_out(1)], axis=0) * (1.0 - lam_init)
        o_ref[0, pl.ds(pl.multiple_of(i * tq, tq), tq), :] = y_t.T.astype(o_ref.dtype)

    def run_pipeline(fetch, consume, bufs, g, state):
        buf_a, buf_b, buf_c = bufs
        n_groups = (nt - 1) // g
        first = lambda j: 1 + g * j

        def body(tt, carry):
            state, tag_a = carry
            j0 = 2 * tt
            tag_b = fetch(first(j0 + 1), buf_b, g)
            state = consume(buf_a, tag_a, state, first(j0), g)
            tag_a = fetch(first(j0 + 2), buf_a, g)
            state = consume(buf_b, tag_b, state, first(j0 + 1), g)
            return state, tag_a

        tag_c = fetch(0, buf_c, 1)
        if n_groups == 0:
            return consume(buf_c, tag_c, state, 0, 1)
        tag_a = fetch(first(0), buf_a, g)
        state = consume(buf_c, tag_c, state, 0, 1)
        state, tag_a = lax.fori_loop(0, (n_groups - 1) // 2, body, (state, tag_a))
        if n_groups % 2 == 0:
            tag_b = fetch(first(n_groups - 1), buf_b, g)
            state = consume(buf_a, tag_a, state, first(n_groups - 2), g)
            return consume(buf_b, tag_b, state, first(n_groups - 1), g)
        return consume(buf_a, tag_a, state, first(n_groups - 1), g)

    def bounded(i, bound, z):
        def fetch(t, e_buf, n):
            e_buf[z] = jnp.exp2(scores(t, n, z) - bound).astype(BF16)
            return bound

        def consume(e_buf, tag, state, t, n):
            for hd in range(2):
                acc_scr[z, hd] = acc_scr[z, hd] + values(e_buf[z], t, n, hd)
            return state

        run_pipeline(fetch, consume, (ea_scr, eb_scr, ec_scr), kgb, bound)
        finish(i, z)

    def online(i):
        def fetch(t, s_scr, n):
            s_new = scores(t, n, 0)
            s_scr[...] = s_new
            return jnp.max(s_new, axis=0, keepdims=True)

        def consume(s_scr, mt, m, t, n):
            m_new = jnp.maximum(m, mt)
            alpha = jnp.exp2(m - m_new)
            p = jnp.exp2(s_scr[...] - m_new).astype(BF16)
            for hd in range(2):
                cols = slice(hd * 2 * tq, (hd + 1) * 2 * tq)
                acc_scr[0, hd] = alpha[:, cols] * acc_scr[0, hd] + values(p, t, n, hd)
            return m_new

        run_pipeline(fetch, consume, (sa_scr, sb_scr, sc_scr), kg, jnp.full((1, q_cols), -jnp.inf, F32))
        finish(i, 0)

    def prepare(i, z):
        qt = qt_ref[0, 0, q0 + i]
        unit = lax.broadcasted_iota(jnp.int32, qt.shape, 0) // B_QK_DIM
        zero = jnp.zeros_like(qt)
        qs_scr[z] = jnp.concatenate([jnp.where(unit == u, qt, zero) for u in range(4)], axis=1)
        acc_scr[z] = jnp.zeros(acc_scr.shape[1:], F32)
        qf = qt.astype(F32)
        qn2 = jnp.sum((qf * qf).reshape(4, B_QK_DIM, tq), axis=1)
        bound = jnp.concatenate(
            [jnp.sqrt(qn2[u:u + 1] * kmax2[:, u * B_QK_DIM:u * B_QK_DIM + 1]) for u in range(4)], axis=1)
        return bound * B_BOUND_MARGIN

    def one_tile(i):
        bound = prepare(i, 0)
        small = jnp.max(bound) <= B_BOUND_CAP
        pl.when(small)(lambda: bounded(i, bound, 0))
        pl.when(jnp.logical_not(small))(lambda: online(i))

    def tile_pair(j, carry):
        i0, i1 = 2 * j, 2 * j + 1
        b0, b1 = prepare(i0, 0), prepare(i1, 1)
        bounded(i0, b0, 0)
        bounded(i1, b1, 1)
        return carry

    def any_tile(i, carry):
        one_tile(i)
        return carry

    def query_norms(i, best):
        qf = qt_ref[0, 0, q0 + i].astype(F32)
        return jnp.maximum(best, jnp.max(jnp.sum((qf * qf).reshape(4, B_QK_DIM, tq), axis=1), axis=1, keepdims=True))
    qmax2 = lax.fori_loop(0, nq, query_norms, jnp.zeros((4, 1), F32))
    worst2 = functools.reduce(jnp.maximum, [qmax2[u:u + 1] * kmax2[:, u * B_QK_DIM:u * B_QK_DIM + 1]
                                            for u in range(4)])
    all_small = jnp.max(jnp.sqrt(worst2)) * B_BOUND_MARGIN <= B_BOUND_CAP

    @pl.when(all_small)
    def _():
        lax.fori_loop(0, nq // 2, tile_pair, 0)
        if nq % 2:
            bounded(nq - 1, prepare(nq - 1, 0), 0)

    @pl.when(jnp.logical_not(all_small))
    def _():
        lax.fori_loop(0, nq, any_tile, 0)


def _attn_b_call(lam_init, lam_vec, norm_g, qbt, kb, vbt, q0, nq, nt):
    b, pairs, n_tiles, _, tq = qbt.shape
    vrows, tk = vbt.shape[-2:]
    kg = B_KEY_GROUP if (nt - 1) % B_KEY_GROUP == 0 else 1
    kgb = B_KEY_GROUP_BOUNDED if (nt - 1) % B_KEY_GROUP_BOUNDED == 0 else 1
    score_bufs = lambda dt, g, *z: [pltpu.VMEM(z + (g * tk, 4 * tq), dt), pltpu.VMEM(z + (g * tk, 4 * tq), dt),
                                    pltpu.VMEM(z + (tk, 4 * tq), dt)]
    return pl.pallas_call(
        functools.partial(_attn_b_kernel, nt=nt, q0=q0, nq=nq, kg=kg, kgb=kgb),
        grid=(b, pairs),
        in_specs=[
            pl.BlockSpec(memory_space=pltpu.SMEM),
            pl.BlockSpec((4, B_QK_DIM), lambda bi, p: (0, 0)),
            pl.BlockSpec((HEAD_DIM, 1), lambda bi, p: (0, 0)),
            pl.BlockSpec((1, 1, n_tiles, LANES, tq), lambda bi, p: (bi, p, 0, 0, 0)),
            pl.BlockSpec((1, nt * tk, LANES), lambda bi, p: (bi, 0, p)),
            pl.BlockSpec((1, 2, nt, vrows, tk), lambda bi, p: (bi, p, 0, 0, 0)),
        ],
        out_specs=pl.BlockSpec((1, nq * tq, LANES), lambda bi, p: (bi, 0, p)),
        out_shape=jax.ShapeDtypeStruct((b, nq * tq, pairs * LANES), BF16),
        scratch_shapes=[pltpu.VMEM((2, LANES, 4 * tq), BF16)] + score_bufs(F32, kg) + score_bufs(BF16, kgb, 2)
                       + [pltpu.VMEM((2, 2, vrows, 2 * tq), F32)],
        compiler_params=_params(("arbitrary", "arbitrary")),
        name="diff_attention",
    )(lam_init, lam_vec, norm_g, qbt, kb, vbt)


def _attn_b(lam_init, lam_vec, norm_g, qbt, kb, vbt, n_ctx, with_ctx):
    n_tiles, tq = qbt.shape[2], qbt.shape[4]
    tk = vbt.shape[-1]
    n_ctx_tiles = n_ctx // tq
    y_lat = _attn_b_call(lam_init, lam_vec, norm_g, qbt, kb, vbt, n_ctx_tiles, n_tiles - n_ctx_tiles, n_tiles)
    y_ctx = _attn_b_call(lam_init, lam_vec, norm_g, qbt, kb, vbt, 0, n_ctx_tiles, n_ctx // tk) if with_ctx else None
    return y_lat, y_ctx


def _log_sigmoid(x):
    return jnp.minimum(x, 0.0) - jnp.log(1.0 + jnp.exp(-jnp.abs(x)))


def _mlstm_kernel(bir_ref, bic_ref, bfr_ref, bfc_ref, ng_ref, qt_ref, k_ref, vt_ref, gic_ref, gfc_ref, gir_ref,
                  gfr_ref, ogt_ref,
                  o_ref, hf_scr, hb_scr, st_scr, ir_scr, br_scr, gc_scr, *, n_ctx_chunks, n_chunks):
    L = MLSTM_CHUNK
    hv = HEAD_DIM
    r_i = lax.broadcasted_iota(jnp.int32, (L, L), 0)
    c_i = lax.broadcasted_iota(jnp.int32, (L, L), 1)
    upper = r_i <= c_i
    lower = r_i >= c_i
    t_up = jnp.where(upper, 1.0, 0.0).astype(BF16)
    t_low = jnp.where(lower, 1.0, 0.0).astype(BF16)
    row_q = lax.broadcasted_iota(jnp.int32, (LANES, L), 0)
    sel = [jnp.where(row_q < hv, 1.0, 0.0).astype(BF16), jnp.where(row_q < hv, 0.0, 1.0).astype(BF16)]
    fwd_r = lax.broadcasted_iota(jnp.int32, (GATE_PAD, L), 0) < 2
    fwd_c = lax.broadcasted_iota(jnp.int32, (L, GATE_PAD), 1) < 2
    st_scr[...] = jnp.zeros_like(st_scr)

    def gates(c, carry):
        r0 = pl.multiple_of(c * L, L)
        frow = _log_sigmoid(gfr_ref[0, 0, c] + bfr_ref[0])
        hi, lo = _split_bf16(frow)
        ir_scr[c] = gir_ref[0, 0, c] + bir_ref[0]
        br_scr[c] = jnp.where(fwd_r, _dot(hi, t_up) + _dot(lo, t_up), _dot(hi, t_low) + _dot(lo, t_low))
        fcol = _log_sigmoid(gfc_ref[0, 0, pl.ds(r0, L), :] + bfc_ref[0])
        hi, lo = _split_bf16(fcol)
        bcol = jnp.where(fwd_c, _dot(t_low, hi) + _dot(t_low, lo), _dot(t_up, hi) + _dot(t_up, lo))
        gc_scr[c] = gic_ref[0, 0, pl.ds(r0, L), :] + bic_ref[0] - bcol
        return carry
    lax.fori_loop(0, n_chunks, gates, 0, unroll=4)

    def chunk_step(direction, c, m_prev):
        r0 = pl.multiple_of(c * L, L)
        kc = k_ref[0, pl.ds(r0, L), :]
        qtc = qt_ref[0, 0, c]
        irow8, brow8, gcol8 = ir_scr[c], br_scr[c], gc_scr[c]
        tri = upper if direction == 0 else lower
        h_scr = hf_scr if direction == 0 else hb_scr
        qtm = [qtc * sel[a] for a in range(2)]
        s_both = _dot(kc, jnp.concatenate(qtm, axis=1))
        m_news, vws, decays = [], [], []
        for a in range(2):
            ci = 2 * direction + a
            b_row, i_row, g_col = brow8[ci:ci + 1], irow8[ci:ci + 1], gcol8[:, ci:ci + 1]
            log_d = jnp.where(tri, g_col + b_row, -jnp.inf)
            m_in = b_row + m_prev[a]
            m_t = jnp.maximum(m_in, jnp.max(log_d, axis=0, keepdims=True))
            w_in = jnp.exp(m_in - m_t)
            sd = (s_both[:, a * L:(a + 1) * L] * jnp.exp(log_d - m_t)).astype(BF16)
            qw = (qtm[a].astype(F32) * w_in).astype(BF16)
            vt = vt_ref[0, a, c]
            numden = _dot(jnp.concatenate([st_scr[ci].astype(BF16), vt], axis=1),
                          jnp.concatenate([qw, sd], axis=0))
            h_scr[c, a * hv:(a + 1) * hv, :] = (
                numden[0:hv] / jnp.maximum(jnp.abs(numden[hv:hv + 1]), jnp.exp(-m_t)))
            b_last = jnp.min(b_row, axis=1, keepdims=True)
            log_w = b_last - b_row + i_row
            m_new = jnp.maximum(b_last + m_prev[a], jnp.max(log_w, axis=1, keepdims=True))
            vws.append((vt.astype(F32) * jnp.exp(log_w - m_new)).astype(BF16))
            decays.append(jnp.exp(b_last + m_prev[a] - m_new))
            m_news.append(m_new)
        rows = vt_ref.shape[3]
        upd = _dot(jnp.concatenate(vws, axis=0), kc)
        for a in range(2):
            ci = 2 * direction + a
            st_scr[ci] = decays[a] * st_scr[ci] + upd[a * rows:(a + 1) * rows]
        return tuple(m_news)

    def body(j, carry):
        mf, mb = carry
        cb = jnp.where(j < n_ctx_chunks, n_ctx_chunks - 1 - j, n_chunks - 1 + n_ctx_chunks - j)
        return chunk_step(0, j, mf), chunk_step(1, cb, mb)

    z11 = jnp.zeros((1, 1), F32)
    lax.fori_loop(0, n_chunks, body, ((z11, z11), (z11, z11)), unroll=4)

    def finish(c, carry):
        r0 = pl.multiple_of(c * L, L)
        h = hf_scr[c] + hb_scr[c]
        outs = []
        for a in range(2):
            ha = h[a * hv:(a + 1) * hv]
            outs.append(ha * lax.rsqrt(jnp.mean(ha * ha, axis=0, keepdims=True) + LN_EPS))
        y_t = jnp.concatenate(outs, axis=0) * ng_ref[0] * ogt_ref[0, 0, c].astype(F32)
        o_ref[0, pl.ds(r0, L), :] = y_t.T.astype(o_ref.dtype)
        return carry
    lax.fori_loop(0, n_chunks, finish, 0, unroll=4)


def _mlstm(bias_i, bias_f, norm_g, qmt, km, vmt, gic, gfc, gir, gfr, ogt, n_ctx):
    b, pairs, nc, _, L = qmt.shape
    s = nc * L
    vrows = vmt.shape[3]
    seq = pl.BlockSpec((1, s, LANES), lambda bi, p: (bi, 0, p))
    pair_tiles = pl.BlockSpec((1, 1, nc, LANES, L), lambda bi, p: (bi, p, 0, 0, 0))
    gate_c = pl.BlockSpec((1, 1, s, GATE_PAD), lambda bi, p: (bi, p, 0, 0))
    gate_r = pl.BlockSpec((1, 1, nc, GATE_PAD, L), lambda bi, p: (bi, p, 0, 0, 0))
    bias_r = pl.BlockSpec((1, GATE_PAD, 1), lambda bi, p: (p, 0, 0))
    bias_c = pl.BlockSpec((1, 1, GATE_PAD), lambda bi, p: (p, 0, 0))
    return pl.pallas_call(
        functools.partial(_mlstm_kernel, n_ctx_chunks=n_ctx // L, n_chunks=nc),
        grid=(b, pairs),
        in_specs=[
            bias_r, bias_c, bias_r, bias_c,
            pl.BlockSpec((1, LANES, 1), lambda bi, p: (p, 0, 0)),
            pair_tiles, seq,
            pl.BlockSpec((1, 2, nc, vrows, L), lambda bi, p: (bi, p, 0, 0, 0)),
            gate_c, gate_c, gate_r, gate_r,
            pair_tiles,
        ],
        out_specs=seq,
        out_shape=jax.ShapeDtypeStruct((b, s, pairs * LANES), BF16),
        scratch_shapes=[pltpu.VMEM((nc, LANES, L), F32), pltpu.VMEM((nc, LANES, L), F32),
                        pltpu.VMEM((4, vrows, LANES), F32),
                        pltpu.VMEM((nc, GATE_PAD, L), F32), pltpu.VMEM((nc, GATE_PAD, L), F32),
                        pltpu.VMEM((nc, L, GATE_PAD), F32)],
        compiler_params=_params(("arbitrary", "arbitrary")),
        name="mlstm",
    )(bias_i[:, :, None], bias_i[:, None, :], bias_f[:, :, None], bias_f[:, None, :], norm_g[:, :, None],
      qmt, km, vmt, gic, gfc, gir, gfr, ogt)


def _out_kernel(*refs, alpha, n_ctx_tiles, n_stream, row0):
    x = _stream_tile(refs[:n_stream], n_ctx_tiles, row0)
    mod_ref, ya_ref, yb_ref = refs[n_stream:n_stream + 3]
    refs = refs[n_stream + 3:]
    ym_ref, g_ref, w_ref, lng_ref, lnb_ref, o_ref = refs[-6:]
    gate = mod_ref[0, 0, 2:3, :]
    g = g_ref[0]
    na, nb_ = ya_ref.shape[2], yb_ref.shape[2]
    yb = yb_ref[0]
    if len(refs) == 7:
        yb = jnp.where(pl.program_id(1) < n_ctx_tiles, refs[0][0], yb)
    mix_a = ya_ref[0] * g[:, :na]
    mix_b = yb * g[:, na:na + nb_]
    mix_m = ym_ref[0] * g[:, na + nb_:]
    y = _dot(jnp.concatenate([mix_a, mix_b, mix_m], axis=1), w_ref[...])
    r = alpha * x + gate * y
    mu = jnp.mean(r, axis=1, keepdims=True)
    d = r - mu
    var = jnp.mean(d * d, axis=1, keepdims=True)
    o_ref[0] = d * lax.rsqrt(var + LN_EPS) * lng_ref[...] + lnb_ref[...]


def _out_projection(xa, modsel, ya, yb_lat, yb_ctx, ym, g, w, layer, ln_g, ln_b, alpha, n_ctx_tiles):
    b, s, d = _stream_shape(xa)
    tm = ROW_TILE
    row0 = 0 if yb_ctx is not None else n_ctx_tiles
    row = lambda n: pl.BlockSpec((1, tm, n), lambda bi, i: (bi, i + row0, 0))
    vec = pl.BlockSpec((1, d), lambda bi, i: (0, 0))
    nb_ = yb_lat.shape[2]
    x_specs, x_args = _stream_specs(xa, tm, row0)
    in_specs = x_specs + [
        pl.BlockSpec((1, 1, 3, d), lambda bi, i: (bi, jnp.minimum((i + row0) // n_ctx_tiles, 1), 0, 0)),
        row(ya.shape[2]),
        pl.BlockSpec((1, tm, nb_), lambda bi, i: (bi, jnp.maximum(i + row0 - n_ctx_tiles, 0), 0)),
    ]
    args = x_args + [modsel, ya, yb_lat]
    if yb_ctx is not None:
        in_specs.append(pl.BlockSpec((1, tm, nb_), lambda bi, i: (bi, jnp.minimum(i, n_ctx_tiles - 1), 0)))
        args.append(yb_ctx)
    in_specs += [row(ym.shape[2]), row(d), pl.BlockSpec((None,) + w.shape[1:], lambda bi, i: (layer, 0, 0)),
                 vec, vec]
    args += [ym, g, w, ln_g, ln_b]
    return pl.pallas_call(
        functools.partial(_out_kernel, alpha=alpha, n_ctx_tiles=n_ctx_tiles, n_stream=len(x_args), row0=row0),
        grid=(b, s // tm - row0),
        in_specs=in_specs,
        out_specs=pl.BlockSpec((1, tm, d), lambda bi, i: (bi, i, 0)),
        out_shape=jax.ShapeDtypeStruct((b, s - row0 * tm, d), F32),
        compiler_params=_params(("arbitrary", "arbitrary")),
        name="out_projection",
    )(*args)


def _in_weight_columns(d_in):
    z = d_in
    src = {}
    off = 0
    for name, n in (("qa", 256), ("ka", 128), ("va", 128), ("za", 256), ("qb", 384), ("kb", 384), ("vb", 384),
                    ("zb", 384), ("qm", 384), ("km", 384), ("vm", 384), ("om", 384), ("zm", 384), ("gm", 24)):
        src[name] = np.arange(off, off + n)
        off += n
    assert off == d_in
    cols = []
    for hd in range(4):
        blk = np.full(LANES, z)
        gq = hd // 2
        blk[64 * gq:64 * gq + 64] = src["qa"][64 * hd:64 * hd + 64]
        cols.append(blk)
    for name in ("ka", "va", "qb", "kb", "vb", "qm", "km", "vm", "om", "za", "zb", "zm"):
        cols.append(src[name])
    heads = len(src["gm"]) // 4
    for p in range(heads // 2):
        for gate in range(2):
            cols.append(np.array([src["gm"][di * 2 * heads + gate * heads + 2 * p + a]
                                  for di in range(2) for a in range(2)] + [z] * (GATE_PAD - 4)))
    cols.append(np.full(LANES - 2 * GATE_PAD * (heads // 2), z))
    cols = np.concatenate(cols)
    assert cols.shape[0] == _W_COLS
    return cols


def _permute_columns(w, cols):
    n = w.shape[-1]
    pieces, start = [], 0
    breaks = np.flatnonzero(np.diff(cols) != 1) + 1
    for stop in list(breaks) + [len(cols)]:
        run = cols[start:stop]
        if len(run) >= 32:
            pieces.append(lax.slice_in_dim(w, int(run[0]), int(run[-1]) + 1, axis=w.ndim - 1))
        elif pieces and isinstance(pieces[-1], list):
            pieces[-1].extend(run)
        else:
            pieces.append(list(run))
        start = stop
    lo = min(min(p) for p in pieces if isinstance(p, list))
    tail = jnp.concatenate([lax.slice_in_dim(w, lo, n, axis=w.ndim - 1), jnp.zeros(w.shape[:-1] + (1,), w.dtype)], -1)
    pieces = [jnp.take(tail, np.array(p) - lo, axis=-1) if isinstance(p, list) else p for p in pieces]
    return jnp.concatenate(pieces, axis=-1)


def _rope_table(n_tokens, n_ctx, dim):
    f32 = np.float32
    rows = n_tokens // GRID_W
    rowp = np.broadcast_to(np.arange(rows, dtype=f32)[:, None], (rows, GRID_W)).reshape(-1)
    colp = np.broadcast_to(np.arange(GRID_W, dtype=f32)[None, :], (rows, GRID_W)).reshape(-1)
    n_freq = dim // 4
    inv = np.power(f32(ROPE_BASE), -np.arange(n_freq, dtype=f32) / f32(n_freq)).astype(f32)
    ar = rowp[:, None] * inv
    ac = colp[:, None] * inv
    ang = np.concatenate([ar, ar, ac, ac], -1).astype(f32)
    cos, sin = np.cos(ang), np.sin(ang)
    odd = (np.arange(dim) // n_freq) % 2 == 1
    sin_p = np.where(odd, sin, f32(0))
    sin_m = np.where(odd, f32(0), -sin)
    tab = np.stack([cos, sin_p, sin_m])
    ident = np.stack([np.ones((n_ctx, dim), f32), np.zeros((n_ctx, dim), f32), np.zeros((n_ctx, dim), f32)])
    tab = np.concatenate([ident, tab], axis=1)
    return jnp.asarray(np.tile(tab, (1, 1, LANES // dim)).astype(f32))


def kernel(x, c, ctx, c_ctx, w_mod, b_mod, w_in, attn_sink, diff_lambda, diff_norm_g, mlstm_i_bias,
           mlstm_f_bias, mlstm_norm_g, w_out, ln_g, ln_b):
    b, t, d = x.shape
    n_ctx = ctx.shape[1]
    depth = w_mod.shape[0]
    d_in = w_in.shape[2]
    assert n_ctx % ROW_TILE == 0 and t % ROW_TILE == 0 and t % GRID_W == 0, (n_ctx, t)
    assert (d, d_in, w_out.shape[1]) == (1024, 4248, 1024), "column layout below is written for d_model = 1024"
    alpha = (2 * depth) ** 0.25

    xa = (ctx, x)
    rope_a = _rope_table(t, n_ctx, HEAD_DIM)
    rope_b = _rope_table(t, n_ctx, B_QK_DIM)

    rows = -(-(b + 1) // 8) * 8
    cc = jnp.concatenate([c, c_ctx[None, :], jnp.zeros((rows - b - 1, d), F32)], axis=0)
    mod = _modulation(cc, w_mod, b_mod).reshape(depth, rows, 3, d)

    w_in_p = _permute_columns(w_in.astype(BF16), _in_weight_columns(d_in))
    w_out_b = w_out.astype(BF16)
    norm_b = diff_norm_g[:, :, None]

    n_pairs = mlstm_i_bias.shape[2] // 2
    heads = mlstm_i_bias.shape[2]
    chains = [(di, a) for di in range(2) for a in range(2)]
    bias_idx = np.array([[di * heads + 2 * p + a for di, a in chains] for p in range(n_pairs)])
    pad8 = lambda v: jnp.pad(v.reshape(-1)[bias_idx], ((0, 0), (0, GATE_PAD - 4)))

    for l in range(depth):
        with_ctx = l < depth - 1
        lam_init = 0.8 - 0.6 * math.exp(-0.3 * l)
        modsel = jnp.stack([jnp.broadcast_to(mod[l, b], (b, 3, d)), mod[l, :b]], axis=1)
        (qa, ka, va, qb, kb, vb, qm, km, vm, og, g, gic, gfc, gir, gfr) = _in_projection(
            xa, modsel, w_in_p, l, rope_a, rope_b, n_ctx)

        ya = _attn_a(attn_sink[l], qa, ka, va, n_ctx)
        yb, yb_ctx = _attn_b(jnp.full((1,), lam_init, F32), diff_lambda[l], norm_b[l], qb, kb, vb, n_ctx, with_ctx)

        ym = _mlstm(pad8(mlstm_i_bias[l]), pad8(mlstm_f_bias[l]), mlstm_norm_g[l].reshape(n_pairs, LANES),
                    qm, km, vm, gic, gfc, gir, gfr, og, n_ctx)

        xa = _out_projection(xa, modsel, ya, yb, yb_ctx, ym, g, w_out_b, l, ln_g[l][None, :], ln_b[l][None, :],
                             alpha, n_ctx // ROW_TILE)
    return xa
```
